```python
import jax, jax.numpy as jnp
from jax import lax
import numpy as np

D_MODEL = 1024
BATCH = 32
SEQ = 256
DEPTH = 1
DEC_BATCH = 8
DEC_SEQ = 4096
PAST_LEN = 512

GRID_W = 64
NA_HEADS = 8
NA_DIM = 64
NA_WIDTH = NA_HEADS * NA_DIM
NA_KH_MAX = 8
NA_KW = 16
RET_HEADS = 4
RET_DK = 128
RET_DV = 128
RET_WIDTH = RET_HEADS * RET_DK
CHUNK = 128
IN_COLS = 3 * NA_WIDTH + 4 * RET_WIDTH
N_EXPERTS = 32
TOP_K = 4
D_FF = 1024
SWIGLU_LIMIT = 7.0
SWIGLU_ALPHA = 1.702
MOE_BLOCK = 128
ROPE_BASE = 10000.0
LN_EPS = 1e-5
Q_BLOCK = 128
DEEPNORM_ALPHA = (2.0 * DEPTH) ** 0.25
DEEPNORM_BETA = (8.0 * DEPTH) ** -0.25

kernel_name = 'hybrid_na_retention_moe_diffusion_step'


def standardize(x):
    xf = x.astype(jnp.float32)
    mu = jnp.mean(xf, axis=-1, keepdims=True)
    var = jnp.mean(jnp.square(xf - mu), axis=-1, keepdims=True)
    return ((xf - mu) * lax.rsqrt(var + LN_EPS)).astype(x.dtype)


def layer_norm(x, g, b):
    return standardize(x) * g + b


def rms_norm(x):
    xf = x.astype(jnp.float32)
    return (xf * lax.rsqrt(jnp.mean(jnp.square(xf), axis=-1, keepdims=True) + LN_EPS)).astype(x.dtype)


def adaln(cond, w_ada, b_ada):
    m = jax.nn.silu(cond) @ w_ada + b_ada
    return jnp.split(m, 6, axis=-1)


def modulate(x, shift, scale):
    return x * (1.0 + scale[:, None, :]) + shift[:, None, :]


def project(h, w_in):
    B, T, _ = h.shape
    p = h @ w_in
    q_na, k_na, v_na, q_r, k_r, v_r, g_r = jnp.split(
        p, [NA_WIDTH, 2 * NA_WIDTH, 3 * NA_WIDTH, 3 * NA_WIDTH + RET_WIDTH,
            3 * NA_WIDTH + 2 * RET_WIDTH, 3 * NA_WIDTH + 3 * RET_WIDTH], axis=-1)
    na = lambda t: t.reshape(B, T, NA_HEADS, NA_DIM)
    q_r = q_r.reshape(B, T, RET_HEADS, RET_DK)
    k_r = k_r.reshape(B, T, RET_HEADS, RET_DK) * (RET_DK ** -0.5)
    v_r = v_r.reshape(B, T, RET_HEADS, RET_DV)
    return na(q_na), na(k_na), na(v_na), q_r, k_r, v_r, g_r


def axial_rope(x, grid_row, grid_col):
    half = x.shape[-1] // 2
    n_freq = half // 2
    inv_freq = ROPE_BASE ** (-jnp.arange(n_freq, dtype=jnp.float32) / n_freq)

    def rotate(xa, pos):
        ang = pos.astype(jnp.float32)[:, None] * inv_freq[None, :]
        cos = jnp.cos(ang)[None, :, None, :].astype(x.dtype)
        sin = jnp.sin(ang)[None, :, None, :].astype(x.dtype)
        x1, x2 = xa[..., :n_freq], xa[..., n_freq:]
        return jnp.concatenate([x1 * cos - x2 * sin, x1 * sin + x2 * cos], axis=-1)

    return jnp.concatenate([rotate(x[..., :half], grid_row), rotate(x[..., half:], grid_col)], axis=-1)


def retention_chunkwise(q, k, v, log_gamma, s0):
    B, T, H, dk = q.shape
    dv = v.shape[-1]
    n = T // CHUNK
    qc = q.reshape(B, n, CHUNK, H, dk)
    kc = k.reshape(B, n, CHUNK, H, dk)
    vc = v.reshape(B, n, CHUNK, H, dv)
    pos = jnp.arange(CHUNK, dtype=jnp.float32)
    diff = pos[:, None] - pos[None, :]
    d_in = jnp.where(diff >= 0, jnp.exp(jnp.maximum(diff, 0.0) * log_gamma[:, None, None]), 0.0).astype(q.dtype)
    xi = jnp.exp((pos + 1.0)[:, None] * log_gamma[None, :]).astype(q.dtype)
    zeta = jnp.exp((CHUNK - 1.0 - pos)[:, None] * log_gamma[None, :]).astype(q.dtype)
    g_chunk = jnp.exp(CHUNK * log_gamma).astype(s0.dtype)[None, :, None, None]
    scores = jnp.einsum('bnqhd,bnkhd->bnhqk', qc, kc) * d_in
    inner = jnp.einsum('bnhqk,bnkhe->bnqhe', scores, vc)
    kv = jnp.einsum('bnkhd,bnkhe->bnhde', kc, vc * zeta[:, :, None]).astype(s0.dtype)

    def carry_state(s, kv_i):
        return g_chunk * s + kv_i, s

    s_final, s_prev = lax.scan(carry_state, s0, jnp.moveaxis(kv, 1, 0))
    s_prev = jnp.moveaxis(s_prev, 0, 1).astype(q.dtype)
    cross = jnp.einsum('bnqhd,bnhde->bnqhe', qc * xi[:, :, None], s_prev)
    return (inner + cross).reshape(B, T, H, dv), s_final


def bidir_retention(q, k, v, lg_f, lg_b, s0_f, s0_b):
    y_f, s_f = retention_chunkwise(q, k, v, lg_f, s0_f)
    y_b, s_b = retention_chunkwise(q[:, ::-1], k[:, ::-1], v[:, ::-1], lg_b, s0_b)
    return y_f + y_b[:, ::-1], s_f, s_b


def context_attention(q, k, v):
    B, L, H, d = q.shape
    scale = d ** -0.5
    qb = jnp.moveaxis(q.reshape(B, L // Q_BLOCK, Q_BLOCK, H, d), 1, 0)

    def block(qi):
        s = jnp.einsum('bqhd,bkhd->bhqk', qi, k).astype(jnp.float32) * scale
        p = jax.nn.softmax(s, axis=-1).astype(v.dtype)
        return jnp.einsum('bhqk,bkhe->bqhe', p, v)

    return jnp.moveaxis(lax.map(block, qb), 0, 1).reshape(B, L, H, d)


def neighbourhood_attention(q, k, v, ck, cv, rpb):
    B, T, H, d = q.shape
    rows = T // GRID_W
    kh = min(NA_KH_MAX, rows)
    scale = d ** -0.5
    qg = q.reshape(B, rows, GRID_W, H, d)
    kg = k.reshape(B, rows, GRID_W, H, d)
    vg = v.reshape(B, rows, GRID_W, H, d)
    col = jnp.arange(GRID_W, dtype=jnp.int32)
    col_start = jnp.clip(col - NA_KW // 2, 0, GRID_W - NA_KW)
    col_idx = col_start[:, None] + jnp.arange(NA_KW, dtype=jnp.int32)[None, :]
    col_off = col_idx - col[:, None] + (NA_KW - 1)
    n_loc = kh * NA_KW

    def one_row(r):
        r_start = jnp.clip(r - kh // 2, 0, rows - kh)
        kr = lax.dynamic_slice_in_dim(kg, r_start, kh, axis=1)
        vr = lax.dynamic_slice_in_dim(vg, r_start, kh, axis=1)
        kw = jnp.take(kr, col_idx, axis=2)
        vw = jnp.take(vr, col_idx, axis=2)
        qr = lax.dynamic_index_in_dim(qg, r, axis=1, keepdims=False)
        row_off = r_start + jnp.arange(kh, dtype=jnp.int32) - r + (NA_KH_MAX - 1)
        bias = rpb[:, row_off][:, :, col_off]
        s_loc = jnp.einsum('bwhd,brwkhd->bhwrk', qr, kw) * scale + jnp.transpose(bias, (0, 2, 1, 3))[None]
        s_ctx = jnp.einsum('bwhd,bchd->bhwc', qr, ck) * scale
        s = jnp.concatenate([s_loc.reshape(B, H, GRID_W, n_loc), s_ctx], axis=-1).astype(jnp.float32)
        p = jax.nn.softmax(s, axis=-1).astype(v.dtype)
        p_loc = p[..., :n_loc].reshape(B, H, GRID_W, kh, NA_KW)
        return (jnp.einsum('bhwrk,brwkhe->bwhe', p_loc, vw)
                + jnp.einsum('bhwc,bche->bwhe', p[..., n_loc:], cv))

    out = lax.map(one_row, jnp.arange(rows, dtype=jnp.int32))
    return jnp.moveaxis(out, 0, 1).reshape(B, T, H, d)


def merge_groups(o_na, o_r, g_r, beta_na, beta_ret, w_out):
    B, T = o_na.shape[:2]
    na = rms_norm(o_na.reshape(B, T, NA_WIDTH)) * beta_na
    ret = standardize(o_r).reshape(B, T, RET_HEADS * RET_DV) * beta_ret * jax.nn.silu(g_r)
    return jnp.concatenate([na, ret], axis=-1) @ w_out


def moe_ffn(x, w_router, b_router, w_gate_up, b_gate_up, w_down, b_down):
    T = x.shape[0]
    logits = (x @ w_router + b_router).astype(jnp.float32)
    top_v, top_i = lax.top_k(logits, TOP_K)
    gates = jax.nn.softmax(top_v, axis=-1)
    n_pairs = T * TOP_K
    e_flat = top_i.reshape(n_pairs).astype(jnp.int32)
    tok_flat = jnp.repeat(jnp.arange(T, dtype=jnp.int32), TOP_K)
    g_flat = gates.reshape(n_pairs)
    order = jnp.argsort(e_flat)
    e_sorted, tok_sorted, g_sorted = e_flat[order], tok_flat[order], g_flat[order]
    counts = jnp.bincount(e_flat, length=N_EXPERTS).astype(jnp.int32)
    padded = (counts + MOE_BLOCK - 1) // MOE_BLOCK * MOE_BLOCK
    pad_end = jnp.cumsum(padded)
    pad_start = pad_end - padded
    start = jnp.cumsum(counts) - counts
    dest = pad_start[e_sorted] + jnp.arange(n_pairs, dtype=jnp.int32) - start[e_sorted]
    n_blocks = n_pairs // MOE_BLOCK + N_EXPERTS
    cap = n_blocks * MOE_BLOCK
    row_tok = jnp.zeros((cap,), jnp.int32).at[dest].set(tok_sorted)
    row_gate = jnp.zeros((cap,), jnp.float32).at[dest].set(g_sorted)
    block_start = jnp.arange(n_blocks, dtype=jnp.int32) * MOE_BLOCK
    block_expert = jnp.minimum(jnp.searchsorted(pad_end, block_start, side='right'), N_EXPERTS - 1)

    def expert_block(acc, blk):
        tok, g, e = blk
        xe = x[tok]
        hu = xe @ w_gate_up[e] + b_gate_up[e]
        x_glu = jnp.minimum(hu[:, ::2], SWIGLU_LIMIT)
        x_lin = jnp.clip(hu[:, 1::2], -SWIGLU_LIMIT, SWIGLU_LIMIT)
        act = x_glu * jax.nn.sigmoid(SWIGLU_ALPHA * x_glu) * (x_lin + 1.0)
        y = act @ w_down[e] + b_down[e]
        return acc.at[tok].add(y * g[:, None].astype(x.dtype)), None

    out, _ = lax.scan(expert_block, jnp.zeros_like(x),
                      (row_tok.reshape(n_blocks, MOE_BLOCK), row_gate.reshape(n_blocks, MOE_BLOCK), block_expert))
    return out


def setup_inputs(seed: int = 0) -> dict:
    key = jax.random.key(seed)
    ks = jax.random.split(key, 27)
    nrm = lambda k, shape, s: jax.random.normal(k, shape, jnp.float32) * s
    e = jnp.arange(RET_HEADS, dtype=jnp.float32) + 5.0
    decay_logit0 = jnp.log(2.0 ** e - 1.0)
    return {
        'x_prompt': nrm(ks[0], (BATCH, SEQ, D_MODEL), 1.0),
        'x_sample': nrm(ks[1], (DEC_BATCH, DEC_SEQ, D_MODEL), 1.0),
        'cache_na_k': nrm(ks[2], (DEC_BATCH, DEPTH, PAST_LEN, NA_HEADS, NA_DIM), 1.0),
        'cache_na_v': nrm(ks[3], (DEC_BATCH, DEPTH, PAST_LEN, NA_HEADS, NA_DIM), 1.0),
        'state_ret_fwd': nrm(ks[4], (DEC_BATCH, DEPTH, RET_HEADS, RET_DK, RET_DV), 0.5),
        'state_ret_bwd': nrm(ks[5], (DEC_BATCH, DEPTH, RET_HEADS, RET_DK, RET_DV), 0.5),
        'c': nrm(ks[6], (DEC_BATCH, D_MODEL), 1.0),
        'c_ctx': nrm(ks[7], (D_MODEL,), 1.0),
        'w_ada': nrm(ks[8], (DEPTH, D_MODEL, 6 * D_MODEL), D_MODEL ** -0.5),
        'b_ada': nrm(ks[9], (DEPTH, 6 * D_MODEL), 0.01),
        'w_in': nrm(ks[10], (DEPTH, D_MODEL, IN_COLS), D_MODEL ** -0.5),
        'rpb': nrm(ks[11], (DEPTH, NA_HEADS, 2 * NA_KH_MAX - 1, 2 * NA_KW - 1), 0.1),
        'ret_decay_fwd': decay_logit0[None, :] + nrm(ks[12], (DEPTH, RET_HEADS), 0.1),
        'ret_decay_bwd': decay_logit0[None, :] + nrm(ks[13], (DEPTH, RET_HEADS), 0.1),
        'beta_na': 1.0 + nrm(ks[14], (DEPTH, NA_WIDTH), 0.01),
        'beta_ret': 1.0 + nrm(ks[15], (DEPTH, RET_HEADS * RET_DV), 0.01),
        'w_out': nrm(ks[16], (DEPTH, NA_WIDTH + RET_HEADS * RET_DV, D_MODEL), D_MODEL ** -0.5 * DEEPNORM_BETA),
        'ln1_g': 1.0 + nrm(ks[17], (DEPTH, D_MODEL), 0.01),
        'ln1_b': nrm(ks[18], (DEPTH, D_MODEL), 0.01),
        'w_router': nrm(ks[19], (DEPTH, D_MODEL, N_EXPERTS), D_MODEL ** -0.5),
        'b_router': nrm(ks[20], (DEPTH, N_EXPERTS), 0.01),
        'w_gate_up': nrm(ks[21], (DEPTH, N_EXPERTS, D_MODEL, 2 * D_FF), D_MODEL ** -0.5),
        'b_gate_up': nrm(ks[22], (DEPTH, N_EXPERTS, 2 * D_FF), 0.01),
        'w_down': nrm(ks[23], (DEPTH, N_EXPERTS, D_FF, D_MODEL), D_FF ** -0.5 * DEEPNORM_BETA),
        'b_down': nrm(ks[24], (DEPTH, N_EXPERTS, D_MODEL), 0.01),
        'ln2_g': 1.0 + nrm(ks[25], (DEPTH, D_MODEL), 0.01),
        'ln2_b': nrm(ks[26], (DEPTH, D_MODEL), 0.01),
    }


def reference(x_prompt, x_sample, cache_na_k, cache_na_v, state_ret_fwd, state_ret_bwd, c, c_ctx,
              w_ada, b_ada, w_in, rpb, ret_decay_fwd, ret_decay_bwd, beta_na, beta_ret, w_out,
              ln1_g, ln1_b, w_router, b_router, w_gate_up, b_gate_up, w_down, b_down, ln2_g, ln2_b):
    alpha = DEEPNORM_ALPHA
    t_lat = x_sample.shape[1]
    t = jnp.arange(t_lat, dtype=jnp.int32)
    grid_row, grid_col = t // GRID_W, t % GRID_W
    yp, ys = x_prompt, x_sample
    new_k, new_v, new_sf, new_sb = [], [], [], []
    for l in range(DEPTH):
        lg_f = jax.nn.log_sigmoid(ret_decay_fwd[l].astype(jnp.float32))
        lg_b = jax.nn.log_sigmoid(ret_decay_bwd[l].astype(jnp.float32))
        sh_ac, sc_ac, gt_ac, sh_fc, sc_fc, gt_fc = adaln(c_ctx[None, :], w_ada[l], b_ada[l])
        sh_as, sc_as, gt_as, sh_fs, sc_fs, gt_fs = adaln(c, w_ada[l], b_ada[l])

        q_na, k_na, v_na, q_r, k_r, v_r, g_r = project(modulate(yp, sh_ac, sc_ac), w_in[l])
        o_na = context_attention(q_na, k_na, v_na)
        s0 = jnp.zeros((yp.shape[0], RET_HEADS, RET_DK, RET_DV), yp.dtype)
        o_r, s_f, s_b = bidir_retention(q_r, k_r, v_r, lg_f, lg_b, s0, s0)
        mix = merge_groups(o_na, o_r, g_r, beta_na[l], beta_ret[l], w_out[l])
        yp = layer_norm(alpha * yp + gt_ac[:, None, :] * mix, ln1_g[l], ln1_b[l])
        new_k.append(k_na)
        new_v.append(v_na)
        new_sf.append(s_f)
        new_sb.append(s_b)

        q_na, k_na, v_na, q_r, k_r, v_r, g_r = project(modulate(ys, sh_as, sc_as), w_in[l])
        o_na = neighbourhood_attention(q_na, k_na, v_na, cache_na_k[:, l], cache_na_v[:, l], rpb[l])
        q_r = axial_rope(q_r, grid_row, grid_col)
        k_r = axial_rope(k_r, grid_row, grid_col)
        o_r, _, _ = bidir_retention(q_r, k_r, v_r, lg_f, lg_b, state_ret_fwd[:, l], state_ret_bwd[:, l])
        mix = merge_groups(o_na, o_r, g_r, beta_na[l], beta_ret[l], w_out[l])
        ys = layer_norm(alpha * ys + gt_as[:, None, :] * mix, ln1_g[l], ln1_b[l])

        hp = modulate(yp, sh_fc, sc_fc).reshape(-1, D_MODEL)
        hs = modulate(ys, sh_fs, sc_fs).reshape(-1, D_MODEL)
        f = moe_ffn(jnp.concatenate([hp, hs], axis=0), w_router[l], b_router[l],
                    w_gate_up[l], b_gate_up[l], w_down[l], b_down[l])
        n_p = hp.shape[0]
        yp = layer_norm(alpha * yp + gt_fc[:, None, :] * f[:n_p].reshape(yp.shape), ln2_g[l], ln2_b[l])
        ys = layer_norm(alpha * ys + gt_fs[:, None, :] * f[n_p:].reshape(ys.shape), ln2_g[l], ln2_b[l])

    new_na_k = jnp.stack(new_k, axis=1)
    new_na_v = jnp.stack(new_v, axis=1)
    new_ret_fwd = jnp.stack(new_sf, axis=1)
    new_ret_bwd = jnp.stack(new_sb, axis=1)
    return (yp, ys, new_na_k, new_na_v, new_ret_fwd, new_ret_bwd)
```

```python
import functools

import jax
import jax.numpy as jnp
from jax import lax
from jax.experimental import pallas as pl
from jax.experimental.pallas import tpu as pltpu

F32 = jnp.float32
BF16 = jnp.bfloat16
I32 = jnp.int32

D_MODEL = 1024
GRID_W = 64
NA_HEADS = 8
NA_DIM = 64
NA_WIDTH = NA_HEADS * NA_DIM
NA_KH = 8
NA_KW = 16
RET_HEADS = 4
RET_DK = 128
RET_WIDTH = RET_HEADS * RET_DK
CHUNK = 128
N_EXPERTS = 32
TOP_K = 4
D_FF = 1024
SWIGLU_LIMIT = 7.0
SWIGLU_ALPHA = 1.702
ROPE_BASE = 10000.0
LN_EPS = 1e-5
DEPTH = 1
DEEPNORM_ALPHA = (2.0 * DEPTH) ** 0.25

LANES = 128
NEG_BIG = -1e30
TOKEN_BLOCK = 256
EXPERT_TILE = 256
VMEM_LIMIT = 56 * 1024 * 1024


def _params(*sem):
    return pltpu.CompilerParams(dimension_semantics=sem, vmem_limit_bytes=VMEM_LIMIT)


def _silu(x):
    return x / (1.0 + jnp.exp(-x))


def _standardize(x):
    mu = jnp.mean(x, axis=-1, keepdims=True)
    xc = x - mu
    var = jnp.mean(xc * xc, axis=-1, keepdims=True)
    return xc * lax.rsqrt(var + LN_EPS)


def _dot_nt(a, b):
    return lax.dot_general(a, b, (((1,), (1,)), ((), ())), preferred_element_type=F32)


def _dot_tn(a, b):
    return lax.dot_general(a, b, (((0,), (0,)), ((), ())), preferred_element_type=F32)


def _ada_body(c_ref, w_ref, b_ref, o_ref):
    s = _silu(c_ref[...])
    o_ref[...] = jnp.dot(s, w_ref[...], preferred_element_type=F32,
                         precision=lax.Precision.HIGHEST) + b_ref[...]


def _adaln(cond, w_ada, b_ada):
    r = cond.shape[0]
    n = w_ada.shape[1]
    tn = 1536
    return pl.pallas_call(
        _ada_body,
        grid=(n // tn,),
        in_specs=[pl.BlockSpec((r, D_MODEL), lambda j: (0, 0)),
                  pl.BlockSpec((D_MODEL, tn), lambda j: (0, j)),
                  pl.BlockSpec((1, tn), lambda j: (0, j))],
        out_specs=pl.BlockSpec((r, tn), lambda j: (0, j)),
        out_shape=jax.ShapeDtypeStruct((r, n), F32),
        compiler_params=_params("arbitrary"),
        name="adaln",
    )(cond, w_ada, b_ada.reshape(1, n))


def _proj_body(*refs, rope):
    if rope:
        (x_ref, sh_ref, sc_ref, w_ref, cos_ref, sa_ref, sb_ref,
         q_ref, k_ref, v_ref, qr_ref, kr_ref, vr_ref, g_ref) = refs
    else:
        (x_ref, sh_ref, sc_ref, w_ref,
         q_ref, k_ref, v_ref, qr_ref, kr_ref, vr_ref, g_ref) = refs
    h = (x_ref[...] * (1.0 + sc_ref[...]) + sh_ref[...]).astype(BF16)

    def cols(c):
        return jnp.dot(h, w_ref[:, c * 512:(c + 1) * 512], preferred_element_type=F32)

    q_ref[...] = cols(0).astype(q_ref.dtype)
    k_ref[...] = cols(1).astype(k_ref.dtype)
    v_ref[...] = cols(2).astype(v_ref.dtype)
    pq = cols(3)
    pk = cols(4) * (RET_DK ** -0.5)
    if rope:
        cs, sa, sb = cos_ref[...], sa_ref[...], sb_ref[...]
        for hd in range(RET_HEADS):
            sl = slice(hd * LANES, (hd + 1) * LANES)
            for p, o_ref in ((pq, qr_ref), (pk, kr_ref)):
                xs = p[:, sl]
                rot = xs * cs + pltpu.roll(xs, 96, 1) * sa + pltpu.roll(xs, 32, 1) * sb
                o_ref[:, sl] = rot.astype(o_ref.dtype)
    else:
        qr_ref[...] = pq.astype(qr_ref.dtype)
        kr_ref[...] = pk.astype(kr_ref.dtype)
    vr_ref[...] = cols(5).astype(vr_ref.dtype)
    g_ref[...] = cols(6)


def _rope_tables(t_len):
    t = jnp.arange(t_len, dtype=jnp.int32)
    pos_row = (t // GRID_W).astype(F32)[:, None]
    pos_col = (t % GRID_W).astype(F32)[:, None]
    lane = jnp.arange(LANES, dtype=jnp.int32)[None, :]
    n_freq = RET_DK // 4
    inv_freq = ROPE_BASE ** (-(lane % n_freq).astype(F32) / n_freq)
    ang = jnp.where(lane < RET_DK // 2, pos_row, pos_col) * inv_freq
    first = (lane % (2 * n_freq)) < n_freq
    cos = jnp.cos(ang)
    sin = jnp.sin(ang)
    return cos, jnp.where(first, -sin, 0.0), jnp.where(first, 0.0, sin)


def _project(x, shift, scale, mod_row, w_in, tm, kv_dtype, rope):
    b, t, _ = x.shape
    tok = lambda bi, ti: (bi, ti, 0)
    mod = lambda bi, ti: (mod_row(bi), 0, 0)
    in_specs = [pl.BlockSpec((None, tm, D_MODEL), tok),
                pl.BlockSpec((None, 1, D_MODEL), mod),
                pl.BlockSpec((None, 1, D_MODEL), mod),
                pl.BlockSpec(w_in.shape, lambda bi, ti: (0, 0))]
    args = [x, shift, scale, w_in]
    if rope:
        tab = lambda bi, ti: (ti, 0)
        in_specs += [pl.BlockSpec((tm, LANES), tab)] * 3
        args += list(_rope_tables(t))
    o512 = pl.BlockSpec((None, tm, 512), tok)
    shp = lambda dt: jax.ShapeDtypeStruct((b, t, 512), dt)
    return pl.pallas_call(
        functools.partial(_proj_body, rope=rope),
        grid=(b, t // tm),
        in_specs=in_specs,
        out_specs=[o512] * 7,
        out_shape=[shp(BF16), shp(kv_dtype), shp(kv_dtype), shp(BF16), shp(BF16), shp(BF16), shp(F32)],
        compiler_params=_params("parallel", "parallel"),
        name="in_proj",
    )(*args)


def _pair_masks(shape):
    lane = lax.broadcasted_iota(I32, shape, 1)
    return lane < NA_DIM, lane >= NA_DIM


def _ctx_attn_body(q_ref, k_ref, v_ref, o_ref):
    q2 = q_ref[...]
    k2 = k_ref[...].astype(BF16)
    v2 = v_ref[...].astype(BF16)
    outs = []
    for msk in _pair_masks(q2.shape):
        qm = jnp.where(msk, q2, jnp.zeros_like(q2))
        s = _dot_nt(qm, k2) * (NA_DIM ** -0.5)
        e = jnp.exp(s - jnp.max(s, axis=-1, keepdims=True))
        den = jnp.sum(e, axis=-1, keepdims=True)
        outs.append(jnp.dot(e.astype(BF16), v2, preferred_element_type=F32) / den)
    lo, _ = _pair_masks(outs[0].shape)
    o_ref[...] = jnp.where(lo, outs[0], outs[1]).astype(o_ref.dtype)


def _context_attention(q, k, v):
    b, t, _ = q.shape
    spec = pl.BlockSpec((None, t, LANES), lambda bi, hp: (bi, 0, hp))
    return pl.pallas_call(
        _ctx_attn_body,
        grid=(b, NA_WIDTH // LANES),
        in_specs=[spec] * 3,
        out_specs=spec,
        out_shape=jax.ShapeDtypeStruct((b, t, NA_WIDTH), BF16),
        compiler_params=_params("parallel", "parallel"),
        name="ctx_attn",
    )(q, k, v)


def _na_bias_body(rpb_ref, o_ref):
    h = pl.program_id(0)
    v = pl.program_id(1)
    shape = (GRID_W, LANES)
    w = lax.broadcasted_iota(I32, shape, 0)
    lane = lax.broadcasted_iota(I32, shape, 1)
    upper = lane >= GRID_W
    kc = jnp.where(upper, lane - GRID_W, lane)
    cdiff = kc - w + (NA_KW - 1)
    cstart = jnp.clip(w - NA_KW // 2, 0, GRID_W - NA_KW)
    inwin = (kc >= cstart) & (kc < cstart + NA_KW)
    n_ro = 2 * NA_KH - 1
    n_co = 2 * NA_KW - 1
    for j in range(NA_KH // 2):
        ro = 2 * j - v + (NA_KH - 1)
        base0 = (h * n_ro + ro) * n_co
        acc = jnp.zeros(shape, F32)
        for c in range(n_co):
            val = jnp.where(upper, rpb_ref[base0 + n_co + c], rpb_ref[base0 + c])
            acc = jnp.where(cdiff == c, val, acc)
        o_ref[:, j * LANES:(j + 1) * LANES] = jnp.where(inwin, acc, NEG_BIG)


def _na_bias(rpb):
    n = NA_KH * GRID_W
    return pl.pallas_call(
        _na_bias_body,
        grid=(NA_HEADS, NA_KH),
        in_specs=[pl.BlockSpec(memory_space=pltpu.SMEM)],
        out_specs=pl.BlockSpec((None, None, GRID_W, n), lambda h, v: (h, v, 0, 0)),
        out_shape=jax.ShapeDtypeStruct((NA_HEADS, NA_KH, GRID_W, n), F32),
        compiler_params=_params("parallel", "parallel"),
        name="na_bias",
    )(rpb.reshape(-1))


def _na_body(q_ref, k_ref, v_ref, ck_ref, cv_ref, bias_ref, o_ref, *, rows):
    ck = ck_ref[...].astype(BF16)
    cv = cv_ref[...].astype(BF16)
    lo, hi = _pair_masks((GRID_W, LANES))
    scale = NA_DIM ** -0.5
    n_loc = NA_KH * GRID_W

    def one_row(r, carry):
        r_start = jnp.clip(r - NA_KH // 2, 0, rows - NA_KH)
        vidx = r - r_start
        q2 = q_ref[pl.ds(pl.multiple_of(r * GRID_W, GRID_W), GRID_W), :]
        ksl = pl.ds(pl.multiple_of(r_start * GRID_W, GRID_W), n_loc)
        kl = k_ref[ksl, :]
        vl = v_ref[ksl, :]
        outs = []
        for hh, msk in enumerate((lo, hi)):
            qm = jnp.where(msk, q2, jnp.zeros_like(q2))
            s_loc = _dot_nt(qm, kl) * scale + bias_ref[hh, vidx]
            s_ctx = _dot_nt(qm, ck) * scale
            m = jnp.maximum(jnp.max(s_loc, axis=-1, keepdims=True), jnp.max(s_ctx, axis=-1, keepdims=True))
            e_loc = jnp.exp(s_loc - m)
            e_ctx = jnp.exp(s_ctx - m)
            den = jnp.sum(e_loc, axis=-1, keepdims=True) + jnp.sum(e_ctx, axis=-1, keepdims=True)
            o = (jnp.dot(e_loc.astype(BF16), vl, preferred_element_type=F32)
                 + jnp.dot(e_ctx.astype(BF16), cv, preferred_element_type=F32))
            outs.append(o / den)
        o_ref[pl.ds(pl.multiple_of(r * GRID_W, GRID_W), GRID_W), :] = (
            jnp.where(lo, outs[0], outs[1]).astype(o_ref.dtype))
        return carry

    lax.fori_loop(0, rows, one_row, 0)


def _neighbourhood_attention(q, k, v, ck, cv, bias):
    b, t, _ = q.shape
    lc = ck.shape[1]
    rows = t // GRID_W
    tok = pl.BlockSpec((None, t, LANES), lambda hp, bi: (bi, 0, hp))
    ctx = pl.BlockSpec((None, lc, LANES), lambda hp, bi: (bi, 0, hp))
    return pl.pallas_call(
        functools.partial(_na_body, rows=rows),
        grid=(NA_WIDTH // LANES, b),
        in_specs=[tok, tok, tok, ctx, ctx,
                  pl.BlockSpec((2, NA_KH, GRID_W, NA_KH * GRID_W), lambda hp, bi: (hp, 0, 0, 0))],
        out_specs=tok,
        out_shape=jax.ShapeDtypeStruct((b, t, NA_WIDTH), BF16),
        compiler_params=_params("parallel", "parallel"),
        name="na_attn",
    )(q, k, v, ck, cv, bias)


def _ret_body(q_ref, k_ref, v_ref, g_ref, df_ref, db_ref, beta_ref, s0f_ref, s0b_ref,
              o_ref, sf_ref, sb_ref, kvf_s, kvb_s, *, n_chunks):
    c_len = CHUNK
    lgf = -jnp.log1p(jnp.exp(-df_ref[...]))
    lgb = -jnp.log1p(jnp.exp(-db_ref[...]))
    shape = (c_len, c_len)
    i = lax.broadcasted_iota(I32, shape, 0).astype(F32)
    j = lax.broadcasted_iota(I32, shape, 1).astype(F32)
    dij = i - j
    d_comb = (jnp.where(dij >= 0, jnp.exp(jnp.maximum(dij, 0.0) * lgf), 0.0)
              + jnp.where(dij <= 0, jnp.exp(jnp.maximum(-dij, 0.0) * lgb), 0.0))
    xi_f = jnp.exp((i + 1.0) * lgf)
    xi_b = jnp.exp((c_len - i) * lgb)
    zeta_f = jnp.exp((c_len - 1.0 - i) * lgf)
    zeta_b = jnp.exp(i * lgb)
    g_f = jnp.exp(c_len * lgf)
    g_b = jnp.exp(c_len * lgb)

    def chunk(c):
        return pl.ds(pl.multiple_of(c * c_len, c_len), c_len)

    def kv_pass(c, carry):
        kc = k_ref[chunk(c), :]
        vc = v_ref[chunk(c), :].astype(F32)
        kvf_s[c] = _dot_tn(kc, (vc * zeta_f).astype(BF16))
        kvb_s[c] = _dot_tn(kc, (vc * zeta_b).astype(BF16))
        return carry

    lax.fori_loop(0, n_chunks, kv_pass, 0)

    def scan_f(c, s):
        kv = kvf_s[c]
        kvf_s[c] = s
        return g_f * s + kv

    def scan_b(ci, s):
        c = n_chunks - 1 - ci
        kv = kvb_s[c]
        kvb_s[c] = s
        return g_b * s + kv

    sf_ref[...] = lax.fori_loop(0, n_chunks, scan_f, s0f_ref[...])
    sb_ref[...] = lax.fori_loop(0, n_chunks, scan_b, s0b_ref[...])

    beta = beta_ref[...]

    def out_pass(c, carry):
        qc = q_ref[chunk(c), :]
        kc = k_ref[chunk(c), :]
        vc = v_ref[chunk(c), :]
        scores = _dot_nt(qc, kc) * d_comb
        y = jnp.dot(scores.astype(BF16), vc, preferred_element_type=F32)
        qf = qc.astype(F32)
        y += jnp.dot((qf * xi_f).astype(BF16), kvf_s[c].astype(BF16), preferred_element_type=F32)
        y += jnp.dot((qf * xi_b).astype(BF16), kvb_s[c].astype(BF16), preferred_element_type=F32)
        o_ref[chunk(c), :] = (_standardize(y) * beta * _silu(g_ref[chunk(c), :])).astype(o_ref.dtype)
        return carry

    lax.fori_loop(0, n_chunks, out_pass, 0)


def _retention(q, k, v, g, decay_f, decay_b, beta_ret, s0f, s0b):
    b, t, _ = q.shape
    n_chunks = t // CHUNK
    tok = pl.BlockSpec((None, t, LANES), lambda bi, h: (bi, 0, h))
    per_head = pl.BlockSpec((None, 1, LANES), lambda bi, h: (h, 0, 0))
    state = pl.BlockSpec((None, None, RET_DK, RET_DK), lambda bi, h: (bi, h, 0, 0))
    lanes = lambda a: jnp.broadcast_to(a.astype(F32)[:, None, None], (RET_HEADS, 1, LANES))
    st_shape = jax.ShapeDtypeStruct((b, RET_HEADS, RET_DK, RET_DK), F32)
    return pl.pallas_call(
        functools.partial(_ret_body, n_chunks=n_chunks),
        grid=(b, RET_HEADS),
        in_specs=[tok, tok, tok, tok, per_head, per_head, per_head, state, state],
        out_specs=[tok, state, state],
        out_shape=[jax.ShapeDtypeStruct((b, t, RET_WIDTH), BF16), st_shape, st_shape],
        scratch_shapes=[pltpu.VMEM((n_chunks, RET_DK, RET_DK), F32)] * 2,
        compiler_params=_params("parallel", "parallel"),
        name="retention",
    )(q, k, v, g, lanes(decay_f), lanes(decay_b), beta_ret.reshape(RET_HEADS, 1, LANES), s0f, s0b)


def _merge_body(x_ref, ona_ref, ret_ref, wna_ref, wret_ref, bna_ref, gate_ref, lg_ref, lb_ref, o_ref):
    o = ona_ref[...].astype(F32)
    na = o * lax.rsqrt(jnp.mean(o * o, axis=-1, keepdims=True) + LN_EPS) * bna_ref[...]
    mix = (jnp.dot(na.astype(BF16), wna_ref[...], preferred_element_type=F32)
           + jnp.dot(ret_ref[...], wret_ref[...], preferred_element_type=F32))
    y = DEEPNORM_ALPHA * x_ref[...] + gate_ref[...] * mix
    o_ref[...] = _standardize(y) * lg_ref[...] + lb_ref[...]


def _merge(x, o_na, ret, w_out, beta_na, gate, mod_row, ln_g, ln_b, tm):
    b, t, _ = x.shape
    tok = lambda bi, ti: (bi, ti, 0)
    row = pl.BlockSpec((1, D_MODEL), lambda bi, ti: (0, 0))
    half = pl.BlockSpec((None, tm, 512), tok)
    return pl.pallas_call(
        _merge_body,
        grid=(b, t // tm),
        in_specs=[pl.BlockSpec((None, tm, D_MODEL), tok), half, half,
                  pl.BlockSpec((512, D_MODEL), lambda bi, ti: (0, 0)),
                  pl.BlockSpec((512, D_MODEL), lambda bi, ti: (1, 0)),
                  pl.BlockSpec((1, 512), lambda bi, ti: (0, 0)),
                  pl.BlockSpec((None, 1, D_MODEL), lambda bi, ti: (mod_row(bi), 0, 0)),
                  row, row],
        out_specs=pl.BlockSpec((None, tm, D_MODEL), tok),
        out_shape=jax.ShapeDtypeStruct(x.shape, F32),
        compiler_params=_params("parallel", "parallel"),
        name="merge_ln1",
    )(x, o_na, ret, w_out, w_out, beta_na.reshape(1, 512), gate, ln_g.reshape(1, -1), ln_b.reshape(1, -1))


def _two_group_specs(n_p_blocks, blocks_per_batch_s, ctx_row):
    tok_p = lambda i: (jnp.minimum(i, n_p_blocks - 1), 0)
    tok_s = lambda i: (jnp.maximum(i - n_p_blocks, 0), 0)
    mod = lambda i: (jnp.where(i < n_p_blocks, ctx_row, jnp.maximum(i - n_p_blocks, 0) // blocks_per_batch_s), 0, 0)
    return tok_p, tok_s, mod


def _router_body(yp_ref, ys_ref, sh_ref, sc_ref, wr_ref, br_ref, e_ref, gt_ref, rk_ref, cnt_ref, carry_s,
                 *, n_p_blocks):
    i = pl.program_id(0)
    tm = yp_ref.shape[0]

    @pl.when(i == 0)
    def _():
        carry_s[...] = jnp.zeros_like(carry_s)

    y = jnp.where(i < n_p_blocks, yp_ref[...], ys_ref[...])
    h = y * (1.0 + sc_ref[...]) + sh_ref[...]
    logits = jnp.dot(h, wr_ref[...], preferred_element_type=F32,
                     precision=lax.Precision.HIGHEST) + br_ref[...]
    lane = lax.broadcasted_iota(I32, (tm, LANES), 1)
    work = jnp.where(lane < N_EXPERTS, logits, NEG_BIG)
    vals, idxs, hots = [], [], []
    for _ in range(TOP_K):
        mx = jnp.max(work, axis=-1, keepdims=True)
        idx = jnp.min(jnp.where(work == mx, lane, LANES), axis=-1, keepdims=True)
        hot = lane == idx
        vals.append(mx)
        idxs.append(idx)
        hots.append(hot)
        work = jnp.where(hot, -jnp.inf, work)
    exps = [jnp.exp(v - vals[0]) for v in vals]
    den = exps[0] + exps[1] + exps[2] + exps[3]
    sel = jnp.zeros((tm, LANES), F32)
    for hot in hots:
        sel = sel + hot.astype(F32)
    r_i = lax.broadcasted_iota(I32, (tm, tm), 0)
    c_i = lax.broadcasted_iota(I32, (tm, tm), 1)
    tri = (r_i > c_i).astype(BF16)
    before = jnp.dot(tri, sel.astype(BF16), preferred_element_type=F32) + carry_s[...]
    e_t = jnp.zeros((tm, LANES), I32)
    g_t = jnp.zeros((tm, LANES), F32)
    r_t = jnp.zeros((tm, LANES), I32)
    for kk in range(TOP_K):
        rank = jnp.sum(jnp.where(hots[kk], before, 0.0), axis=-1, keepdims=True).astype(I32)
        at = lane == kk
        e_t = jnp.where(at, idxs[kk], e_t)
        g_t = jnp.where(at, exps[kk] / den, g_t)
        r_t = jnp.where(at, rank, r_t)
    e_ref[...] = e_t[:, :TOP_K]
    gt_ref[...] = g_t[:, :TOP_K]
    rk_ref[...] = r_t[:, :TOP_K]
    carry_s[...] = carry_s[...] + jnp.sum(sel, axis=0, keepdims=True)
    cnt_ref[...] = carry_s[...]


def _router(yp, ys, shift, scale, ctx_row, s_blocks_per_batch, w_router, b_router, tm):
    n_p, n_s = yp.shape[0], ys.shape[0]
    n = n_p + n_s
    npb = n_p // tm
    tok_p, tok_s, mod = _two_group_specs(npb, s_blocks_per_batch, ctx_row)
    wr = jnp.zeros((D_MODEL, LANES), F32).at[:, :N_EXPERTS].set(w_router)
    br = jnp.zeros((1, LANES), F32).at[0, :N_EXPERTS].set(b_router)
    o4 = pl.BlockSpec((tm, TOP_K), lambda i: (i, 0))
    return pl.pallas_call(
        functools.partial(_router_body, n_p_blocks=npb),
        grid=(n // tm,),
        in_specs=[pl.BlockSpec((tm, D_MODEL), tok_p), pl.BlockSpec((tm, D_MODEL), tok_s),
                  pl.BlockSpec((None, 1, D_MODEL), mod), pl.BlockSpec((None, 1, D_MODEL), mod),
                  pl.BlockSpec((D_MODEL, LANES), lambda i: (0, 0)),
                  pl.BlockSpec((1, LANES), lambda i: (0, 0))],
        out_specs=[o4, o4, o4, pl.BlockSpec((1, LANES), lambda i: (0, 0))],
        out_shape=[jax.ShapeDtypeStruct((n, TOP_K), I32), jax.ShapeDtypeStruct((n, TOP_K), F32),
                   jax.ShapeDtypeStruct((n, TOP_K), I32), jax.ShapeDtypeStruct((1, LANES), F32)],
        scratch_shapes=[pltpu.VMEM((1, LANES), F32)],
        compiler_params=_params("arbitrary"),
        name="moe_router",
    )(yp, ys, shift, scale, wr, br)


def _dispatch_body(dest_ref, zlo_ref, zn_ref, yp_ref, ys_ref, sh_ref, sc_ref, xs_ref, hbuf, zbuf, sem,
                   *, n_p_blocks, n_blocks):
    i = pl.program_id(0)
    tm = hbuf.shape[0]
    y = jnp.where(i < n_p_blocks, yp_ref[...], ys_ref[...])
    hbuf[...] = y * (1.0 + sc_ref[...]) + sh_ref[...]

    def row_copy(r, d):
        return pltpu.make_async_copy(hbuf.at[pl.ds(r, 1)], xs_ref.at[pl.ds(d, 1)], sem)

    def issue(r, carry):
        for kk in range(TOP_K):
            row_copy(r, dest_ref[r * TOP_K + kk]).start()
        return carry

    lax.fori_loop(0, tm, issue, 0)

    def drain(r, carry):
        for kk in range(TOP_K):
            row_copy(0, 0).wait()
        return carry

    lax.fori_loop(0, tm, drain, 0)

    @pl.when(i == n_blocks - 1)
    def _():
        zbuf[...] = jnp.zeros_like(zbuf)

        def zero_rows(lo, nz):
            def copy(r):
                return pltpu.make_async_copy(zbuf.at[pl.ds(0, 1)], xs_ref.at[pl.ds(lo + r, 1)], sem)

            def z_issue(r, c2):
                copy(r).start()
                return c2

            def z_drain(r, c2):
                copy(0).wait()
                return c2

            lax.fori_loop(0, nz, z_issue, 0)
            lax.fori_loop(0, nz, z_drain, 0)

        def per_expert(e, carry):
            zero_rows(zlo_ref[e], zn_ref[e])
            return carry

        lax.fori_loop(0, N_EXPERTS, per_expert, 0)

        tail_lo = zlo_ref[N_EXPERTS]

        def tile_copy(t):
            start = pl.multiple_of(tail_lo + t * EXPERT_TILE, EXPERT_TILE)
            return pltpu.make_async_copy(zbuf, xs_ref.at[pl.ds(start, EXPERT_TILE)], sem)

        def t_issue(t, c2):
            tile_copy(t).start()
            return c2

        def t_drain(t, c2):
            tile_copy(0).wait()
            return c2

        n_tail = zn_ref[N_EXPERTS] // EXPERT_TILE
        lax.fori_loop(0, n_tail, t_issue, 0)
        lax.fori_loop(0, n_tail, t_drain, 0)


def _dispatch(yp, ys, shift, scale, ctx_row, s_blocks_per_batch, dest, zero_lo, zero_n, cap_rows, tm):
    n_p, n_s = yp.shape[0], ys.shape[0]
    n = n_p + n_s
    npb = n_p // tm
    nb = n // tm
    tok_p, tok_s, mod = _two_group_specs(npb, s_blocks_per_batch, ctx_row)
    smem_all = pl.BlockSpec(memory_space=pltpu.SMEM)
    return pl.pallas_call(
        functools.partial(_dispatch_body, n_p_blocks=npb, n_blocks=nb),
        grid=(nb,),
        in_specs=[pl.BlockSpec((tm * TOP_K,), lambda i: (i,), memory_space=pltpu.SMEM),
                  smem_all, smem_all,
                  pl.BlockSpec((tm, D_MODEL), tok_p), pl.BlockSpec((tm, D_MODEL), tok_s),
                  pl.BlockSpec((None, 1, D_MODEL), mod), pl.BlockSpec((None, 1, D_MODEL), mod)],
        out_specs=pl.BlockSpec(memory_space=pl.ANY),
        out_shape=jax.ShapeDtypeStruct((cap_rows, D_MODEL), F32),
        scratch_shapes=[pltpu.VMEM((tm, D_MODEL), F32), pltpu.VMEM((EXPERT_TILE, D_MODEL), F32),
                        pltpu.SemaphoreType.DMA(())],
        compiler_params=_params("arbitrary"),
        name="moe_dispatch",
    )(dest, zero_lo, zero_n, yp, ys, shift, scale)


def _expert_body(te_ref, nu_ref, x_ref, wgu_ref, bgu_ref, wd_ref, bd_ref, y_ref, act_s):
    j = pl.program_id(0)

    @pl.when(j < nu_ref[0])
    def _():
        x = x_ref[...].astype(BF16)
        for c in range(D_FF // LANES):
            sl = slice(c * 2 * LANES, (c + 1) * 2 * LANES)
            hu = jnp.dot(x, wgu_ref[:, sl], preferred_element_type=F32) + bgu_ref[:, sl]
            x_glu = jnp.minimum(hu[:, :LANES], SWIGLU_LIMIT)
            x_lin = jnp.clip(hu[:, LANES:], -SWIGLU_LIMIT, SWIGLU_LIMIT)
            act = x_glu * (1.0 / (1.0 + jnp.exp(-SWIGLU_ALPHA * x_glu))) * (x_lin + 1.0)
            act_s[:, c * LANES:(c + 1) * LANES] = act.astype(BF16)
        y_ref[...] = jnp.dot(act_s[...], wd_ref[...], preferred_element_type=F32) + bd_ref[...]

    @pl.when(j >= nu_ref[0])
    def _():
        y_ref[...] = jnp.zeros_like(y_ref)


def _experts(tile_expert, n_used, xs, wgu, bgu, wd, bd):
    cap_rows = xs.shape[0]
    n_tiles = cap_rows // EXPERT_TILE
    grid_spec = pltpu.PrefetchScalarGridSpec(
        num_scalar_prefetch=2,
        grid=(n_tiles,),
        in_specs=[pl.BlockSpec((EXPERT_TILE, D_MODEL), lambda j, te, nu: (jnp.minimum(j, nu[0] - 1), 0)),
                  pl.BlockSpec((None, D_MODEL, 2 * D_FF), lambda j, te, nu: (te[j], 0, 0)),
                  pl.BlockSpec((None, 1, 2 * D_FF), lambda j, te, nu: (te[j], 0, 0)),
                  pl.BlockSpec((None, D_FF, D_MODEL), lambda j, te, nu: (te[j], 0, 0)),
                  pl.BlockSpec((None, 1, D_MODEL), lambda j, te, nu: (te[j], 0, 0))],
        out_specs=pl.BlockSpec((EXPERT_TILE, D_MODEL), lambda j, te, nu: (j, 0)),
        scratch_shapes=[pltpu.VMEM((EXPERT_TILE, D_FF), BF16)],
    )
    return pl.pallas_call(
        _expert_body,
        grid_spec=grid_spec,
        out_shape=jax.ShapeDtypeStruct((cap_rows, D_MODEL), F32),
        compiler_params=_params("arbitrary"),
        name="moe_experts",
    )(tile_expert, n_used, xs, wgu, bgu, wd, bd)


def _combine_body(dest_ref, y1_ref, gt_ref, gate_ref, lg_ref, lb_ref, ys_ref, o_ref, buf, sem):
    tm = y1_ref.shape[0]

    def row_copy(kk, r, d):
        return pltpu.make_async_copy(ys_ref.at[pl.ds(d, 1)], buf.at[kk, pl.ds(r, 1)], sem)

    def issue(r, carry):
        for kk in range(TOP_K):
            row_copy(kk, r, dest_ref[r * TOP_K + kk]).start()
        return carry

    lax.fori_loop(0, tm, issue, 0)

    def drain(r, carry):
        for kk in range(TOP_K):
            row_copy(kk, 0, 0).wait()
        return carry

    lax.fori_loop(0, tm, drain, 0)

    gates = gt_ref[...]
    f = gates[:, 0:1] * buf[0]
    for kk in range(1, TOP_K):
        f = f + gates[:, kk:kk + 1] * buf[kk]
    y = DEEPNORM_ALPHA * y1_ref[...] + gate_ref[...] * f
    o_ref[...] = _standardize(y) * lg_ref[...] + lb_ref[...]


def _combine(y1, dest, gates, block0, gate_mod, mod_row, blocks_per_batch, y_sorted, ln_g, ln_b, tm):
    n_g = y1.shape[0]
    row = pl.BlockSpec((1, D_MODEL), lambda i: (0, 0))
    return pl.pallas_call(
        _combine_body,
        grid=(n_g // tm,),
        in_specs=[pl.BlockSpec((tm * TOP_K,), lambda i: (i + block0,), memory_space=pltpu.SMEM),
                  pl.BlockSpec((tm, D_MODEL), lambda i: (i, 0)),
                  pl.BlockSpec((tm, TOP_K), lambda i: (i + block0, 0)),
                  pl.BlockSpec((None, 1, D_MODEL), lambda i: (mod_row(i // blocks_per_batch), 0, 0)),
                  row, row,
                  pl.BlockSpec(memory_space=pl.ANY)],
        out_specs=pl.BlockSpec((tm, D_MODEL), lambda i: (i, 0)),
        out_shape=jax.ShapeDtypeStruct((n_g, D_MODEL), F32),
        scratch_shapes=[pltpu.VMEM((TOP_K, tm, D_MODEL), F32), pltpu.SemaphoreType.DMA(())],
        compiler_params=_params("arbitrary"),
        name="moe_combine_ln2",
    )(dest, y1, gates, gate_mod, ln_g.reshape(1, -1), ln_b.reshape(1, -1), y_sorted)


def _expert_weights(w_gate_up, b_gate_up, w_down):
    e = w_gate_up.shape[0]
    nblk = D_FF // LANES
    wgu = w_gate_up.reshape(e, D_MODEL, nblk, LANES, 2).transpose(0, 1, 2, 4, 3).reshape(e, D_MODEL, 2 * D_FF)
    bgu = b_gate_up.reshape(e, 1, nblk, LANES, 2).transpose(0, 1, 2, 4, 3).reshape(e, 1, 2 * D_FF)
    return wgu.astype(BF16), bgu, w_down.astype(BF16)


def _routing_plan(e_idx, rank, counts):
    n_pairs = e_idx.shape[0] * TOP_K
    cnt = counts[0, :N_EXPERTS].astype(I32)
    padded = (cnt + EXPERT_TILE - 1) // EXPERT_TILE * EXPERT_TILE
    pad_end = jnp.cumsum(padded)
    pad_start = pad_end - padded
    dest = (pad_start[e_idx] + rank).reshape(-1)
    n_tiles = n_pairs // EXPERT_TILE + N_EXPERTS
    n_used = pad_end[-1] // EXPERT_TILE
    tile_row = jnp.minimum(jnp.arange(n_tiles, dtype=I32), n_used - 1) * EXPERT_TILE
    tile_expert = jnp.sum((pad_end[None, :] <= tile_row[:, None]).astype(I32), axis=1)
    tile_expert = jnp.minimum(tile_expert, N_EXPERTS - 1)
    cap_rows = n_tiles * EXPERT_TILE
    zero_lo = jnp.concatenate([pad_start + cnt, pad_end[-1:]])
    zero_n = jnp.concatenate([padded - cnt, cap_rows - pad_end[-1:]])
    return dest, tile_expert, n_used.reshape(1), zero_lo, zero_n, cap_rows


def kernel(x_prompt, x_sample, cache_na_k, cache_na_v, state_ret_fwd, state_ret_bwd, c, c_ctx, w_ada, b_ada, w_in, rpb, ret_decay_fwd, ret_decay_bwd, beta_na, beta_ret, w_out, ln1_g, ln1_b, w_router, b_router, w_gate_up, b_gate_up, w_down, b_down, ln2_g, ln2_b):
    bp, tp, _ = x_prompt.shape
    bs, ts, _ = x_sample.shape
    assert w_ada.shape[0] == DEPTH == 1
    l = 0
    ctx_row = bs
    sample_row = lambda bi: bi
    prompt_row = lambda bi: ctx_row

    cond = jnp.concatenate([c, c_ctx[None, :], jnp.zeros((16 - bs - 1, D_MODEL), F32)], axis=0)
    m = _adaln(cond, w_ada[l], b_ada[l]).reshape(16, 6, 1, D_MODEL)
    sh_a, sc_a, gt_a, sh_f, sc_f, gt_f = (m[:, i] for i in range(6))

    w_in_b = w_in[l].astype(BF16)
    w_out_b = w_out[l].astype(BF16)

    q, k_p, v_p, qr, kr, vr, g = _project(x_prompt, sh_a, sc_a, prompt_row, w_in_b, tp, F32, rope=False)
    o_na = _context_attention(q, k_p, v_p)
    zeros_state = jnp.zeros((bp, RET_HEADS, RET_DK, RET_DK), F32)
    ret, s_f, s_b = _retention(qr, kr, vr, g, ret_decay_fwd[l], ret_decay_bwd[l], beta_ret[l],
                               zeros_state, zeros_state)
    yp1 = _merge(x_prompt, o_na, ret, w_out_b, beta_na[l], gt_a, prompt_row, ln1_g[l], ln1_b[l], tp)

    q, k_s, v_s, qr, kr, vr, g = _project(x_sample, sh_a, sc_a, sample_row, w_in_b, 512, BF16, rope=True)
    lc = cache_na_k.shape[2]
    o_na = _neighbourhood_attention(q, k_s, v_s, cache_na_k[:, l].reshape(bs, lc, NA_WIDTH),
                                    cache_na_v[:, l].reshape(bs, lc, NA_WIDTH), _na_bias(rpb[l]))
    ret, _, _ = _retention(qr, kr, vr, g, ret_decay_fwd[l], ret_decay_bwd[l], beta_ret[l],
                           state_ret_fwd[:, l], state_ret_bwd[:, l])
    ys1 = _merge(x_sample, o_na, ret, w_out_b, beta_na[l], gt_a, sample_row, ln1_g[l], ln1_b[l], 512)

    tm = TOKEN_BLOCK
    yp1f = yp1.reshape(bp * tp, D_MODEL)
    ys1f = ys1.reshape(bs * ts, D_MODEL)
    e_idx, gates, rank, counts = _router(yp1f, ys1f, sh_f, sc_f, ctx_row, ts // tm, w_router[l], b_router[l], tm)
    dest, tile_expert, n_used, zero_lo, zero_n, cap_rows = _routing_plan(e_idx, rank, counts)
    xs = _dispatch(yp1f, ys1f, sh_f, sc_f, ctx_row, ts // tm, dest, zero_lo, zero_n, cap_rows, tm)
    wgu, bgu, wd = _expert_weights(w_gate_up[l], b_gate_up[l], w_down[l])
    y_sorted = _experts(tile_expert, n_used, xs, wgu, bgu, wd, b_down[l][:, None, :])
    yp = _combine(yp1f, dest, gates, 0, gt_f, prompt_row, 1, y_sorted, ln2_g[l], ln2_b[l], tm)
    ys = _combine(ys1f, dest, gates, (bp * tp) // tm, gt_f, sample_row, ts // tm, y_sorted,
                  ln2_g[l], ln2_b[l], tm)

    return (yp.reshape(bp, tp, D_MODEL), ys.reshape(bs, ts, D_MODEL),
            k_p.reshape(bp, 1, tp, NA_HEADS, NA_DIM), v_p.reshape(bp, 1, tp, NA_HEADS, NA_DIM),
            s_f[:, None], s_b[:, None])
```

```python
import functools

import jax
import jax.numpy as jnp
from jax import lax
from jax.experimental import pallas as pl
from jax.experimental.pallas import tpu as pltpu

F32 = jnp.float32
BF16 = jnp.bfloat16
I32 = jnp.int32

D_MODEL = 1024
GRID_W = 64
NA_HEADS = 8
NA_DIM = 64
NA_WIDTH = NA_HEADS * NA_DIM
NA_KH = 8
NA_KW = 16
RET_HEADS = 4
RET_DK = 128
RET_WIDTH = RET_HEADS * RET_DK
CHUNK = 128
N_EXPERTS = 32
TOP_K = 4
D_FF = 1024
SWIGLU_LIMIT = 7.0
SWIGLU_ALPHA = 1.702
ROPE_BASE = 10000.0
LN_EPS = 1e-5
DEPTH = 1
DEEPNORM_ALPHA = (2.0 * DEPTH) ** 0.25

LANES = 128
NEG_BIG = -1e30
TOKEN_BLOCK = 256
EXPERT_TILE = 256
VMEM_LIMIT = 56 * 1024 * 1024
NA_ROW_UNROLL = 2
RET_CHUNK_UNROLL = 4
SUBLANES = 8


def _params(*sem):
    return pltpu.CompilerParams(dimension_semantics=sem, vmem_limit_bytes=VMEM_LIMIT)


def _silu(x):
    return x / (1.0 + jnp.exp(-x))


def _standardize(x):
    mu = jnp.mean(x, axis=-1, keepdims=True)
    xc = x - mu
    var = jnp.mean(xc * xc, axis=-1, keepdims=True)
    return xc * lax.rsqrt(var + LN_EPS)


def _dot_nt(a, b):
    return lax.dot_general(a, b, (((1,), (1,)), ((), ())), preferred_element_type=F32)


def _dot_tn(a, b):
    return lax.dot_general(a, b, (((0,), (0,)), ((), ())), preferred_element_type=F32)


def _ada_body(c_ref, w_ref, b_ref, o_ref):
    s = _silu(c_ref[...])
    o_ref[...] = jnp.dot(s, w_ref[...], preferred_element_type=F32,
                         precision=lax.Precision.HIGHEST) + b_ref[...]


def _adaln(cond, w_ada, b_ada):
    r = cond.shape[0]
    n = w_ada.shape[1]
    tn = 1536
    return pl.pallas_call(
        _ada_body,
        grid=(n // tn,),
        in_specs=[pl.BlockSpec((r, D_MODEL), lambda j: (0, 0)),
                  pl.BlockSpec((D_MODEL, tn), lambda j: (0, j)),
                  pl.BlockSpec((1, tn), lambda j: (0, j))],
        out_specs=pl.BlockSpec((r, tn), lambda j: (0, j)),
        out_shape=jax.ShapeDtypeStruct((r, n), F32),
        compiler_params=_params("arbitrary"),
        name="adaln",
    )(cond, w_ada, b_ada.reshape(1, n))


def _proj_body(*refs, rope):
    if rope:
        (x_ref, sh_ref, sc_ref, w_ref, cos_ref, sa_ref, sb_ref,
         q_ref, k_ref, v_ref, qr_ref, kr_ref, vr_ref, g_ref) = refs
    else:
        (x_ref, sh_ref, sc_ref, w_ref,
         q_ref, k_ref, v_ref, qr_ref, kr_ref, vr_ref, g_ref) = refs
    h = (x_ref[...] * (1.0 + sc_ref[...]) + sh_ref[...]).astype(BF16)

    def cols(c):
        return jnp.dot(h, w_ref[:, c * 512:(c + 1) * 512], preferred_element_type=F32)

    q_ref[...] = cols(0).astype(q_ref.dtype)
    k_ref[...] = cols(1).astype(k_ref.dtype)
    v_ref[...] = cols(2).astype(v_ref.dtype)
    pq = cols(3)
    pk = cols(4) * (RET_DK ** -0.5)
    if rope:
        cs, sa, sb = cos_ref[...], sa_ref[...], sb_ref[...]
        for hd in range(RET_HEADS):
            sl = slice(hd * LANES, (hd + 1) * LANES)
            for p, o_ref in ((pq, qr_ref), (pk, kr_ref)):
                xs = p[:, sl]
                rot = xs * cs + pltpu.roll(xs, 96, 1) * sa + pltpu.roll(xs, 32, 1) * sb
                o_ref[:, sl] = rot.astype(o_ref.dtype)
    else:
        qr_ref[...] = pq.astype(qr_ref.dtype)
        kr_ref[...] = pk.astype(kr_ref.dtype)
    vr_ref[...] = cols(5).astype(vr_ref.dtype)
    g_ref[...] = cols(6)


def _rope_tables(t_len):
    t = jnp.arange(t_len, dtype=jnp.int32)
    pos_row = (t // GRID_W).astype(F32)[:, None]
    pos_col = (t % GRID_W).astype(F32)[:, None]
    lane = jnp.arange(LANES, dtype=jnp.int32)[None, :]
    n_freq = RET_DK // 4
    inv_freq = ROPE_BASE ** (-(lane % n_freq).astype(F32) / n_freq)
    ang = jnp.where(lane < RET_DK // 2, pos_row, pos_col) * inv_freq
    first = (lane % (2 * n_freq)) < n_freq
    cos = jnp.cos(ang)
    sin = jnp.sin(ang)
    return cos, jnp.where(first, -sin, 0.0), jnp.where(first, 0.0, sin)


def _project(x, shift, scale, mod_row, w_in, tm, kv_dtype, rope):
    b, t, _ = x.shape
    tok = lambda bi, ti: (bi, ti, 0)
    mod = lambda bi, ti: (mod_row(bi), 0, 0)
    in_specs = [pl.BlockSpec((None, tm, D_MODEL), tok),
                pl.BlockSpec((None, 1, D_MODEL), mod),
                pl.BlockSpec((None, 1, D_MODEL), mod),
                pl.BlockSpec(w_in.shape, lambda bi, ti: (0, 0))]
    args = [x, shift, scale, w_in]
    if rope:
        tab = lambda bi, ti: (ti, 0)
        in_specs += [pl.BlockSpec((tm, LANES), tab)] * 3
        args += list(_rope_tables(t))
    o512 = pl.BlockSpec((None, tm, 512), tok)
    shp = lambda dt: jax.ShapeDtypeStruct((b, t, 512), dt)
    return pl.pallas_call(
        functools.partial(_proj_body, rope=rope),
        grid=(b, t // tm),
        in_specs=in_specs,
        out_specs=[o512] * 7,
        out_shape=[shp(BF16), shp(kv_dtype), shp(kv_dtype), shp(BF16), shp(BF16), shp(BF16), shp(F32)],
        compiler_params=_params("parallel", "parallel"),
        name="in_proj",
    )(*args)


def _pair_masks(shape):
    lane = lax.broadcasted_iota(I32, shape, 1)
    return lane < NA_DIM, lane >= NA_DIM


def _ctx_attn_body(q_ref, k_ref, v_ref, o_ref):
    q2 = q_ref[...]
    k2 = k_ref[...].astype(BF16)
    v2 = v_ref[...].astype(BF16)
    t = q2.shape[0]
    lo, hi = _pair_masks(q2.shape)
    zero = jnp.zeros_like(q2)
    qs = jnp.concatenate([jnp.where(lo, q2, zero), jnp.where(hi, q2, zero)], axis=0)
    s = _dot_nt(qs, k2) * (NA_DIM ** -0.5)
    e = jnp.exp(s - jnp.max(s, axis=-1, keepdims=True))
    den = jnp.sum(e, axis=-1, keepdims=True)
    o = jnp.dot(e.astype(BF16), v2, preferred_element_type=F32) / den
    o_ref[...] = jnp.where(lo, o[:t], o[t:]).astype(o_ref.dtype)


def _context_attention(q, k, v):
    b, t, _ = q.shape
    spec = pl.BlockSpec((None, t, LANES), lambda bi, hp: (bi, 0, hp))
    return pl.pallas_call(
        _ctx_attn_body,
        grid=(b, NA_WIDTH // LANES),
        in_specs=[spec] * 3,
        out_specs=spec,
        out_shape=jax.ShapeDtypeStruct((b, t, NA_WIDTH), BF16),
        compiler_params=_params("parallel", "parallel"),
        name="ctx_attn",
    )(q, k, v)


def _na_bias_body(rpb_ref, o_ref):
    h = pl.program_id(0)
    v = pl.program_id(1)
    shape = (GRID_W, LANES)
    w = lax.broadcasted_iota(I32, shape, 0)
    lane = lax.broadcasted_iota(I32, shape, 1)
    upper = lane >= GRID_W
    kc = jnp.where(upper, lane - GRID_W, lane)
    cdiff = kc - w + (NA_KW - 1)
    cstart = jnp.clip(w - NA_KW // 2, 0, GRID_W - NA_KW)
    inwin = (kc >= cstart) & (kc < cstart + NA_KW)
    n_ro = 2 * NA_KH - 1
    n_co = 2 * NA_KW - 1
    for j in range(NA_KH // 2):
        ro = 2 * j - v + (NA_KH - 1)
        base0 = (h * n_ro + ro) * n_co
        acc = jnp.zeros(shape, F32)
        for c in range(n_co):
            val = jnp.where(upper, rpb_ref[base0 + n_co + c], rpb_ref[base0 + c])
            acc = jnp.where(cdiff == c, val, acc)
        o_ref[:, j * LANES:(j + 1) * LANES] = jnp.where(inwin, acc, NEG_BIG)


def _na_bias(rpb):
    n = NA_KH * GRID_W
    return pl.pallas_call(
        _na_bias_body,
        grid=(NA_HEADS, NA_KH),
        in_specs=[pl.BlockSpec(memory_space=pltpu.SMEM)],
        out_specs=pl.BlockSpec((None, None, GRID_W, n), lambda h, v: (h, v, 0, 0)),
        out_shape=jax.ShapeDtypeStruct((NA_HEADS, NA_KH, GRID_W, n), F32),
        compiler_params=_params("parallel", "parallel"),
        name="na_bias",
    )(rpb.reshape(-1))


def _na_body(q_ref, k_ref, v_ref, ck_ref, cv_ref, bias_ref, o_ref, *, rows):
    ck = ck_ref[...].astype(BF16)
    cv = cv_ref[...].astype(BF16)
    lo, hi = _pair_masks((GRID_W, LANES))
    scale = NA_DIM ** -0.5
    n_loc = NA_KH * GRID_W

    def one_row(r, carry):
        r_start = jnp.clip(r - NA_KH // 2, 0, rows - NA_KH)
        vidx = r - r_start
        q2 = q_ref[pl.ds(pl.multiple_of(r * GRID_W, GRID_W), GRID_W), :]
        ksl = pl.ds(pl.multiple_of(r_start * GRID_W, GRID_W), n_loc)
        kl = k_ref[ksl, :]
        vl = v_ref[ksl, :]
        zero = jnp.zeros_like(q2)
        qs = jnp.concatenate([jnp.where(lo, q2, zero), jnp.where(hi, q2, zero)], axis=0)
        bias = jnp.concatenate([bias_ref[0, vidx], bias_ref[1, vidx]], axis=0)
        s_loc = _dot_nt(qs, kl) * scale + bias
        s_ctx = _dot_nt(qs, ck) * scale
        m = jnp.maximum(jnp.max(s_loc, axis=-1, keepdims=True), jnp.max(s_ctx, axis=-1, keepdims=True))
        e_loc = jnp.exp(s_loc - m)
        e_ctx = jnp.exp(s_ctx - m)
        den = jnp.sum(e_loc, axis=-1, keepdims=True) + jnp.sum(e_ctx, axis=-1, keepdims=True)
        o = (jnp.dot(e_loc.astype(BF16), vl, preferred_element_type=F32)
             + jnp.dot(e_ctx.astype(BF16), cv, preferred_element_type=F32)) / den
        o_ref[pl.ds(pl.multiple_of(r * GRID_W, GRID_W), GRID_W), :] = (
            jnp.where(lo, o[:GRID_W], o[GRID_W:]).astype(o_ref.dtype))
        return carry

    lax.fori_loop(0, rows, one_row, 0, unroll=NA_ROW_UNROLL)


def _neighbourhood_attention(q, k, v, ck, cv, bias):
    b, t, _ = q.shape
    lc = ck.shape[1]
    rows = t // GRID_W
    tok = pl.BlockSpec((None, t, LANES), lambda hp, bi: (bi, 0, hp))
    ctx = pl.BlockSpec((None, lc, LANES), lambda hp, bi: (bi, 0, hp))
    return pl.pallas_call(
        functools.partial(_na_body, rows=rows),
        grid=(NA_WIDTH // LANES, b),
        in_specs=[tok, tok, tok, ctx, ctx,
                  pl.BlockSpec((2, NA_KH, GRID_W, NA_KH * GRID_W), lambda hp, bi: (hp, 0, 0, 0))],
        out_specs=tok,
        out_shape=jax.ShapeDtypeStruct((b, t, NA_WIDTH), BF16),
        compiler_params=_params("parallel", "parallel"),
        name="na_attn",
    )(q, k, v, ck, cv, bias)


def _ret_body(q_ref, k_ref, v_ref, g_ref, df_ref, db_ref, beta_ref, s0f_ref, s0b_ref,
              o_ref, sf_ref, sb_ref, kvf_s, kvb_s, *, n_chunks):
    c_len = CHUNK
    lgf = -jnp.log1p(jnp.exp(-df_ref[...]))
    lgb = -jnp.log1p(jnp.exp(-db_ref[...]))
    shape = (c_len, c_len)
    i = lax.broadcasted_iota(I32, shape, 0).astype(F32)
    j = lax.broadcasted_iota(I32, shape, 1).astype(F32)
    dij = i - j
    d_comb = (jnp.where(dij >= 0, jnp.exp(jnp.maximum(dij, 0.0) * lgf), 0.0)
              + jnp.where(dij <= 0, jnp.exp(jnp.maximum(-dij, 0.0) * lgb), 0.0))
    xi_f = jnp.exp((i + 1.0) * lgf)
    xi_b = jnp.exp((c_len - i) * lgb)
    zeta_f = jnp.exp((c_len - 1.0 - i) * lgf)
    zeta_b = jnp.exp(i * lgb)
    g_f = jnp.exp(c_len * lgf)
    g_b = jnp.exp(c_len * lgb)

    def chunk(c):
        return pl.ds(pl.multiple_of(c * c_len, c_len), c_len)

    def kv_pass(c, carry):
        kc = k_ref[chunk(c), :]
        vc = v_ref[chunk(c), :].astype(F32)
        kvf_s[c] = _dot_tn(kc, (vc * zeta_f).astype(BF16))
        kvb_s[c] = _dot_tn(kc, (vc * zeta_b).astype(BF16))
        return carry

    lax.fori_loop(0, n_chunks, kv_pass, 0, unroll=RET_CHUNK_UNROLL)

    def scan_f(c, s):
        kv = kvf_s[c]
        kvf_s[c] = s
        return g_f * s + kv

    def scan_b(ci, s):
        c = n_chunks - 1 - ci
        kv = kvb_s[c]
        kvb_s[c] = s
        return g_b * s + kv

    sf_ref[...] = lax.fori_loop(0, n_chunks, scan_f, s0f_ref[...])
    sb_ref[...] = lax.fori_loop(0, n_chunks, scan_b, s0b_ref[...])

    beta = beta_ref[...]

    def out_pass(c, carry):
        qc = q_ref[chunk(c), :]
        kc = k_ref[chunk(c), :]
        vc = v_ref[chunk(c), :]
        scores = _dot_nt(qc, kc) * d_comb
        y = jnp.dot(scores.astype(BF16), vc, preferred_element_type=F32)
        qf = qc.astype(F32)
        y += jnp.dot((qf * xi_f).astype(BF16), kvf_s[c].astype(BF16), preferred_element_type=F32)
        y += jnp.dot((qf * xi_b).astype(BF16), kvb_s[c].astype(BF16), preferred_element_type=F32)
        o_ref[chunk(c), :] = (_standardize(y) * beta * _silu(g_ref[chunk(c), :])).astype(o_ref.dtype)
        return carry

    lax.fori_loop(0, n_chunks, out_pass, 0, unroll=RET_CHUNK_UNROLL)


def _retention(q, k, v, g, decay_f, decay_b, beta_ret, s0f, s0b):
    b, t, _ = q.shape
    n_chunks = t // CHUNK
    tok = pl.BlockSpec((None, t, LANES), lambda bi, h: (bi, 0, h))
    per_head = pl.BlockSpec((None, 1, LANES), lambda bi, h: (h, 0, 0))
    state = pl.BlockSpec((None, None, RET_DK, RET_DK), lambda bi, h: (bi, h, 0, 0))
    lanes = lambda a: jnp.broadcast_to(a.astype(F32)[:, None, None], (RET_HEADS, 1, LANES))
    st_shape = jax.ShapeDtypeStruct((b, RET_HEADS, RET_DK, RET_DK), F32)
    return pl.pallas_call(
        functools.partial(_ret_body, n_chunks=n_chunks),
        grid=(b, RET_HEADS),
        in_specs=[tok, tok, tok, tok, per_head, per_head, per_head, state, state],
        out_specs=[tok, state, state],
        out_shape=[jax.ShapeDtypeStruct((b, t, RET_WIDTH), BF16), st_shape, st_shape],
        scratch_shapes=[pltpu.VMEM((n_chunks, RET_DK, RET_DK), F32)] * 2,
        compiler_params=_params("parallel", "parallel"),
        name="retention",
    )(q, k, v, g, lanes(decay_f), lanes(decay_b), beta_ret.reshape(RET_HEADS, 1, LANES), s0f, s0b)


def _merge_body(x_ref, ona_ref, ret_ref, wna_ref, wret_ref, bna_ref, gate_ref, lg_ref, lb_ref, o_ref):
    o = ona_ref[...].astype(F32)
    na = o * lax.rsqrt(jnp.mean(o * o, axis=-1, keepdims=True) + LN_EPS) * bna_ref[...]
    mix = (jnp.dot(na.astype(BF16), wna_ref[...], preferred_element_type=F32)
           + jnp.dot(ret_ref[...], wret_ref[...], preferred_element_type=F32))
    y = DEEPNORM_ALPHA * x_ref[...] + gate_ref[...] * mix
    o_ref[...] = _standardize(y) * lg_ref[...] + lb_ref[...]


def _merge(x, o_na, ret, w_out, beta_na, gate, mod_row, ln_g, ln_b, tm):
    b, t, _ = x.shape
    tok = lambda bi, ti: (bi, ti, 0)
    row = pl.BlockSpec((1, D_MODEL), lambda bi, ti: (0, 0))
    half = pl.BlockSpec((None, tm, 512), tok)
    return pl.pallas_call(
        _merge_body,
        grid=(b, t // tm),
        in_specs=[pl.BlockSpec((None, tm, D_MODEL), tok), half, half,
                  pl.BlockSpec((512, D_MODEL), lambda bi, ti: (0, 0)),
                  pl.BlockSpec((512, D_MODEL), lambda bi, ti: (1, 0)),
                  pl.BlockSpec((1, 512), lambda bi, ti: (0, 0)),
                  pl.BlockSpec((None, 1, D_MODEL), lambda bi, ti: (mod_row(bi), 0, 0)),
                  row, row],
        out_specs=pl.BlockSpec((None, tm, D_MODEL), tok),
        out_shape=jax.ShapeDtypeStruct(x.shape, F32),
        compiler_params=_params("parallel", "parallel"),
        name="merge_ln1",
    )(x, o_na, ret, w_out, w_out, beta_na.reshape(1, 512), gate, ln_g.reshape(1, -1), ln_b.reshape(1, -1))


def _two_group_specs(n_p_blocks, blocks_per_batch_s, ctx_row):
    tok_p = lambda i: (jnp.minimum(i, n_p_blocks - 1), 0)
    tok_s = lambda i: (jnp.maximum(i - n_p_blocks, 0), 0)
    mod = lambda i: (jnp.where(i < n_p_blocks, ctx_row, jnp.maximum(i - n_p_blocks, 0) // blocks_per_batch_s), 0, 0)
    return tok_p, tok_s, mod


def _router_body(yp_ref, ys_ref, sh_ref, sc_ref, wh_ref, wl_ref, br_ref, e_ref, gt_ref, rk_ref, cnt_ref,
                 carry_s, *, n_p_blocks):
    i = pl.program_id(0)
    tm = yp_ref.shape[0]

    @pl.when(i == 0)
    def _():
        carry_s[...] = jnp.zeros_like(carry_s)

    y = jnp.where(i < n_p_blocks, yp_ref[...], ys_ref[...])
    h = y * (1.0 + sc_ref[...]) + sh_ref[...]
    h_hi = h.astype(BF16)
    h_lo = (h - h_hi.astype(F32)).astype(BF16)
    w_hi = wh_ref[...]
    lg = _dot_nt(jnp.concatenate([w_hi, wl_ref[...]], axis=0), h_hi)
    work = lg[:N_EXPERTS] + lg[N_EXPERTS:] + _dot_nt(w_hi, h_lo) + br_ref[...]
    eidx = lax.broadcasted_iota(I32, (N_EXPERTS, tm), 0).astype(F32)
    vals, idxs, hots = [], [], []
    for _ in range(TOP_K):
        mx = jnp.max(work, axis=0, keepdims=True)
        idx = jnp.min(jnp.where(work == mx, eidx, float(N_EXPERTS)), axis=0, keepdims=True)
        hot = eidx == idx
        vals.append(mx)
        idxs.append(idx)
        hots.append(hot)
        work = jnp.where(hot, -jnp.inf, work)
    exps = [jnp.exp(v - vals[0]) for v in vals]
    den = exps[0] + exps[1] + exps[2] + exps[3]
    sel = jnp.zeros((N_EXPERTS, tm), F32)
    for hot in hots:
        sel = sel + hot.astype(F32)
    r_i = lax.broadcasted_iota(I32, (tm, tm), 0)
    c_i = lax.broadcasted_iota(I32, (tm, tm), 1)
    earlier = (r_i < c_i).astype(BF16)
    carry = carry_s[...]
    before = (jnp.dot(sel.astype(BF16), earlier, preferred_element_type=F32)
              + jnp.concatenate([carry] * (tm // LANES), axis=1))
    ranks = [jnp.sum(jnp.where(hot, before, 0.0), axis=0, keepdims=True) for hot in hots]
    e_ref[...] = jnp.concatenate(idxs, axis=0).astype(I32)
    gt_ref[...] = jnp.concatenate([e / den for e in exps], axis=0)
    rk_ref[...] = jnp.concatenate(ranks, axis=0).astype(I32)
    carry = carry + jnp.sum(sel, axis=1, keepdims=True)
    carry_s[...] = carry
    cnt_ref[...] = carry


def _router(yp, ys, shift, scale, ctx_row, s_blocks_per_batch, w_router, b_router, tm):
    n_p, n_s = yp.shape[0], ys.shape[0]
    n = n_p + n_s
    npb = n_p // tm
    tok_p, tok_s, mod = _two_group_specs(npb, s_blocks_per_batch, ctx_row)
    w_t = w_router.T
    w_hi = w_t.astype(BF16)
    w_lo = (w_t - w_hi.astype(F32)).astype(BF16)
    br = jnp.broadcast_to(b_router[:, None], (N_EXPERTS, tm))
    o4 = pl.BlockSpec((TOP_K, tm), lambda i: (0, i))
    whole = lambda shape: pl.BlockSpec(shape, lambda i: (0, 0))
    return pl.pallas_call(
        functools.partial(_router_body, n_p_blocks=npb),
        grid=(n // tm,),
        in_specs=[pl.BlockSpec((tm, D_MODEL), tok_p), pl.BlockSpec((tm, D_MODEL), tok_s),
                  pl.BlockSpec((None, 1, D_MODEL), mod), pl.BlockSpec((None, 1, D_MODEL), mod),
                  whole((N_EXPERTS, D_MODEL)), whole((N_EXPERTS, D_MODEL)), whole((N_EXPERTS, tm))],
        out_specs=[o4, o4, o4, whole((N_EXPERTS, LANES))],
        out_shape=[jax.ShapeDtypeStruct((TOP_K, n), I32), jax.ShapeDtypeStruct((TOP_K, n), F32),
                   jax.ShapeDtypeStruct((TOP_K, n), I32), jax.ShapeDtypeStruct((N_EXPERTS, LANES), F32)],
        scratch_shapes=[pltpu.VMEM((N_EXPERTS, LANES), F32)],
        compiler_params=_params("arbitrary"),
        name="moe_router",
    )(yp, ys, shift, scale, w_hi, w_lo, br)


def _dispatch_body(dest_ref, zlo_ref, zn_ref, yp_ref, ys_ref, sh_ref, sc_ref, xs_ref, hbuf, zbuf, sem,
                   *, n_p_blocks, n_blocks):
    i = pl.program_id(0)
    tm = yp_ref.shape[0]
    y = jnp.where(i < n_p_blocks, yp_ref[...], ys_ref[...])
    hbuf[...] = (y * (1.0 + sc_ref[...]) + sh_ref[...]).reshape(hbuf.shape)

    def issue(g, carry):
        for u in range(SUBLANES):
            for kk in range(TOP_K):
                d = dest_ref[(g * SUBLANES + u) * TOP_K + kk]
                pltpu.make_async_copy(hbuf.at[g, pl.ds(u, 1)], xs_ref.at[pl.ds(d, 1)], sem).start(
                    priority=kk % 2)
        return carry

    lax.fori_loop(0, tm // SUBLANES, issue, 0)
    block = xs_ref.at[pl.ds(0, tm)]
    for kk in range(TOP_K):
        pltpu.make_async_copy(block, block, sem).wait()

    @pl.when(i == n_blocks - 1)
    def _():
        zbuf[...] = jnp.zeros_like(zbuf)

        def zero_rows(lo, nz):
            def copy(r):
                return pltpu.make_async_copy(zbuf.at[pl.ds(0, 1)], xs_ref.at[pl.ds(lo + r, 1)], sem)

            def z_issue(r, c2):
                copy(r).start()
                return c2

            def z_drain(r, c2):
                copy(0).wait()
                return c2

            lax.fori_loop(0, nz, z_issue, 0)
            lax.fori_loop(0, nz, z_drain, 0)

        def per_expert(e, carry):
            zero_rows(zlo_ref[e], zn_ref[e])
            return carry

        lax.fori_loop(0, N_EXPERTS, per_expert, 0)

        tail_lo = zlo_ref[N_EXPERTS]

        def tile_copy(t):
            start = pl.multiple_of(tail_lo + t * EXPERT_TILE, EXPERT_TILE)
            return pltpu.make_async_copy(zbuf, xs_ref.at[pl.ds(start, EXPERT_TILE)], sem)

        def t_issue(t, c2):
            tile_copy(t).start()
            return c2

        def t_drain(t, c2):
            tile_copy(0).wait()
            return c2

        n_tail = zn_ref[N_EXPERTS] // EXPERT_TILE
        lax.fori_loop(0, n_tail, t_issue, 0)
        lax.fori_loop(0, n_tail, t_drain, 0)


def _dispatch(yp, ys, shift, scale, ctx_row, s_blocks_per_batch, dest, zero_lo, zero_n, cap_rows, tm):
    n_p, n_s = yp.shape[0], ys.shape[0]
    n = n_p + n_s
    npb = n_p // tm
    nb = n // tm
    tok_p, tok_s, mod = _two_group_specs(npb, s_blocks_per_batch, ctx_row)
    smem_all = pl.BlockSpec(memory_space=pltpu.SMEM)
    return pl.pallas_call(
        functools.partial(_dispatch_body, n_p_blocks=npb, n_blocks=nb),
        grid=(nb,),
        in_specs=[pl.BlockSpec((tm * TOP_K,), lambda i: (i,), memory_space=pltpu.SMEM),
                  smem_all, smem_all,
                  pl.BlockSpec((tm, D_MODEL), tok_p), pl.BlockSpec((tm, D_MODEL), tok_s),
                  pl.BlockSpec((None, 1, D_MODEL), mod), pl.BlockSpec((None, 1, D_MODEL), mod)],
        out_specs=pl.BlockSpec(memory_space=pl.ANY),
        out_shape=jax.ShapeDtypeStruct((cap_rows, D_MODEL), F32),
        scratch_shapes=[pltpu.VMEM((tm // SUBLANES, SUBLANES, D_MODEL), F32),
                        pltpu.VMEM((EXPERT_TILE, D_MODEL), F32),
                        pltpu.SemaphoreType.DMA(())],
        compiler_params=_params("arbitrary"),
        name="moe_dispatch",
    )(dest, zero_lo, zero_n, yp, ys, shift, scale)


def _expert_body(te_ref, nu_ref, x_ref, wgu_ref, bgu_ref, wd_ref, bd_ref, y_ref, act_s):
    j = pl.program_id(0)

    @pl.when(j < nu_ref[0])
    def _():
        x = x_ref[...].astype(BF16)
        for c in range(D_FF // LANES):
            sl = slice(c * 2 * LANES, (c + 1) * 2 * LANES)
            hu = jnp.dot(x, wgu_ref[:, sl], preferred_element_type=F32) + bgu_ref[:, sl]
            x_glu = jnp.minimum(hu[:, :LANES], SWIGLU_LIMIT)
            x_lin = jnp.clip(hu[:, LANES:], -SWIGLU_LIMIT, SWIGLU_LIMIT)
            act = x_glu * (1.0 / (1.0 + jnp.exp(-SWIGLU_ALPHA * x_glu))) * (x_lin + 1.0)
            act_s[:, c * LANES:(c + 1) * LANES] = act.astype(BF16)
        y_ref[...] = jnp.dot(act_s[...], wd_ref[...], preferred_element_type=F32) + bd_ref[...]

    @pl.when(j >= nu_ref[0])
    def _():
        y_ref[...] = jnp.zeros_like(y_ref)


def _experts(tile_expert, n_used, xs, wgu, bgu, wd, bd):
    cap_rows = xs.shape[0]
    n_tiles = cap_rows // EXPERT_TILE
    grid_spec = pltpu.PrefetchScalarGridSpec(
        num_scalar_prefetch=2,
        grid=(n_tiles,),
        in_specs=[pl.BlockSpec((EXPERT_TILE, D_MODEL), lambda j, te, nu: (jnp.minimum(j, nu[0] - 1), 0)),
                  pl.BlockSpec((None, D_MODEL, 2 * D_FF), lambda j, te, nu: (te[j], 0, 0)),
                  pl.BlockSpec((None, 1, 2 * D_FF), lambda j, te, nu: (te[j], 0, 0)),
                  pl.BlockSpec((None, D_FF, D_MODEL), lambda j, te, nu: (te[j], 0, 0)),
                  pl.BlockSpec((None, 1, D_MODEL), lambda j, te, nu: (te[j], 0, 0))],
        out_specs=pl.BlockSpec((EXPERT_TILE, D_MODEL), lambda j, te, nu: (j, 0)),
        scratch_shapes=[pltpu.VMEM((EXPERT_TILE, D_FF), BF16)],
    )
    return pl.pallas_call(
        _expert_body,
        grid_spec=grid_spec,
        out_shape=jax.ShapeDtypeStruct((cap_rows, D_MODEL), F32),
        compiler_params=_params("arbitrary"),
        name="moe_experts",
    )(tile_expert, n_used, xs, wgu, bgu, wd, bd)


def _combine_body(dest_ref, y1_ref, gt_ref, gate_ref, lg_ref, lb_ref, ys_ref, o_ref, buf, sem):
    tm = y1_ref.shape[0]

    def issue(g, carry):
        for u in range(SUBLANES):
            for kk in range(TOP_K):
                d = dest_ref[(g * SUBLANES + u) * TOP_K + kk]
                pltpu.make_async_copy(ys_ref.at[pl.ds(d, 1)], buf.at[kk, g, pl.ds(u, 1)], sem).start(
                    priority=kk % 2)
        return carry

    lax.fori_loop(0, tm // SUBLANES, issue, 0)
    block = ys_ref.at[pl.ds(0, tm)]
    for kk in range(TOP_K):
        pltpu.make_async_copy(block, block, sem).wait()

    gates = gt_ref[...]
    f = gates[:, 0:1] * buf[0].reshape(tm, D_MODEL)
    for kk in range(1, TOP_K):
        f = f + gates[:, kk:kk + 1] * buf[kk].reshape(tm, D_MODEL)
    y = DEEPNORM_ALPHA * y1_ref[...] + gate_ref[...] * f
    o_ref[...] = _standardize(y) * lg_ref[...] + lb_ref[...]


def _combine(y1, dest, gates, block0, gate_mod, mod_row, blocks_per_batch, y_sorted, ln_g, ln_b, tm):
    n_g = y1.shape[0]
    row = pl.BlockSpec((1, D_MODEL), lambda i: (0, 0))
    return pl.pallas_call(
        _combine_body,
        grid=(n_g // tm,),
        in_specs=[pl.BlockSpec((tm * TOP_K,), lambda i: (i + block0,), memory_space=pltpu.SMEM),
                  pl.BlockSpec((tm, D_MODEL), lambda i: (i, 0)),
                  pl.BlockSpec((tm, TOP_K), lambda i: (i + block0, 0)),
                  pl.BlockSpec((None, 1, D_MODEL), lambda i: (mod_row(i // blocks_per_batch), 0, 0)),
                  row, row,
                  pl.BlockSpec(memory_space=pl.ANY)],
        out_specs=pl.BlockSpec((tm, D_MODEL), lambda i: (i, 0)),
        out_shape=jax.ShapeDtypeStruct((n_g, D_MODEL), F32),
        scratch_shapes=[pltpu.VMEM((TOP_K, tm // SUBLANES, SUBLANES, D_MODEL), F32),
                        pltpu.SemaphoreType.DMA(())],
        compiler_params=_params("arbitrary"),
        name="moe_combine_ln2",
    )(dest, y1, gates, gate_mod, ln_g.reshape(1, -1), ln_b.reshape(1, -1), y_sorted)


def _regroup_body(w_ref, o_ref):
    blk = 2 * LANES
    src = lax.broadcasted_iota(I32, (blk, blk), 0)
    dst = lax.broadcasted_iota(I32, (blk, blk), 1)
    perm = (src == jnp.where(dst < LANES, 2 * dst, 2 * (dst - LANES) + 1)).astype(BF16)
    for s in range(w_ref.shape[1] // blk):
        sl = slice(s * blk, (s + 1) * blk)
        o_ref[:, sl] = jnp.dot(w_ref[:, sl].astype(BF16), perm, preferred_element_type=F32).astype(BF16)


def _regroup_gate_up(w_gate_up):
    e, d, n = w_gate_up.shape
    tn = 512
    spec = pl.BlockSpec((None, d, tn), lambda ei, j: (ei, 0, j))
    return pl.pallas_call(
        _regroup_body,
        grid=(e, n // tn),
        in_specs=[spec],
        out_specs=spec,
        out_shape=jax.ShapeDtypeStruct((e, d, n), BF16),
        compiler_params=_params("parallel", "parallel"),
        name="regroup_gate_up",
    )(w_gate_up)


def _expert_weights(w_gate_up, b_gate_up, w_down):
    e = w_gate_up.shape[0]
    nblk = D_FF // LANES
    bgu = b_gate_up.reshape(e, 1, nblk, LANES, 2).transpose(0, 1, 2, 4, 3).reshape(e, 1, 2 * D_FF)
    return _regroup_gate_up(w_gate_up), bgu, w_down.astype(BF16)


def _routing_plan(e_idx, rank, counts):
    n_pairs = e_idx.shape[1] * TOP_K
    cnt = counts[:, 0].astype(I32)
    padded = (cnt + EXPERT_TILE - 1) // EXPERT_TILE * EXPERT_TILE
    pad_end = jnp.cumsum(padded)
    pad_start = pad_end - padded
    dest = (pad_start[e_idx] + rank).T.reshape(-1)
    n_tiles = n_pairs // EXPERT_TILE + N_EXPERTS
    n_used = pad_end[-1] // EXPERT_TILE
    tile_row = jnp.minimum(jnp.arange(n_tiles, dtype=I32), n_used - 1) * EXPERT_TILE
    tile_expert = jnp.sum((pad_end[None, :] <= tile_row[:, None]).astype(I32), axis=1)
    tile_expert = jnp.minimum(tile_expert, N_EXPERTS - 1)
    cap_rows = n_tiles * EXPERT_TILE
    zero_lo = jnp.concatenate([pad_start + cnt, pad_end[-1:]])
    zero_n = jnp.concatenate([padded - cnt, cap_rows - pad_end[-1:]])
    return dest, tile_expert, n_used.reshape(1), zero_lo, zero_n, cap_rows


def kernel(x_prompt, x_sample, cache_na_k, cache_na_v, state_ret_fwd, state_ret_bwd, c, c_ctx, w_ada, b_ada, w_in, rpb, ret_decay_fwd, ret_decay_bwd, beta_na, beta_ret, w_out, ln1_g, ln1_b, w_router, b_router, w_gate_up, b_gate_up, w_down, b_down, ln2_g, ln2_b):
    bp, tp, _ = x_prompt.shape
    bs, ts, _ = x_sample.shape
    assert w_ada.shape[0] == DEPTH == 1
    l = 0
    ctx_row = bs
    sample_row = lambda bi: bi
    prompt_row = lambda bi: ctx_row

    cond = jnp.concatenate([c, c_ctx[None, :], jnp.zeros((16 - bs - 1, D_MODEL), F32)], axis=0)
    m = _adaln(cond, w_ada[l], b_ada[l]).reshape(16, 6, 1, D_MODEL)
    sh_a, sc_a, gt_a, sh_f, sc_f, gt_f = (m[:, i] for i in range(6))

    w_in_b = w_in[l].astype(BF16)
    w_out_b = w_out[l].astype(BF16)

    q, k_p, v_p, qr, kr, vr, g = _project(x_prompt, sh_a, sc_a, prompt_row, w_in_b, tp, F32, rope=False)
    o_na = _context_attention(q, k_p, v_p)
    zeros_state = jnp.zeros((bp, RET_HEADS, RET_DK, RET_DK), F32)
    ret, s_f, s_b = _retention(qr, kr, vr, g, ret_decay_fwd[l], ret_decay_bwd[l], beta_ret[l],
                               zeros_state, zeros_state)
    yp1 = _merge(x_prompt, o_na, ret, w_out_b, beta_na[l], gt_a, prompt_row, ln1_g[l], ln1_b[l], tp)

    q, k_s, v_s, qr, kr, vr, g = _project(x_sample, sh_a, sc_a, sample_row, w_in_b, 512, BF16, rope=True)
    lc = cache_na_k.shape[2]
    o_na = _neighbourhood_attention(q, k_s, v_s, cache_na_k[:, l].reshape(bs, lc, NA_WIDTH),
                                    cache_na_v[:, l].reshape(bs, lc, NA_WIDTH), _na_bias(rpb[l]))
    ret, _, _ = _retention(qr, kr, vr, g, ret_decay_fwd[l], ret_decay_bwd[l], beta_ret[l],
                           state_ret_fwd[:, l], state_ret_bwd[:, l])
    ys1 = _merge(x_sample, o_na, ret, w_out_b, beta_na[l], gt_a, sample_row, ln1_g[l], ln1_b[l], 512)

    tm = TOKEN_BLOCK
    yp1f = yp1.reshape(bp * tp, D_MODEL)
    ys1f = ys1.reshape(bs * ts, D_MODEL)
    e_idx, gates, rank, counts = _router(yp1f, ys1f, sh_f, sc_f, ctx_row, ts // tm, w_router[l], b_router[l], tm)
    dest, tile_expert, n_used, zero_lo, zero_n, cap_rows = _routing_plan(e_idx, rank, counts)
    gates = gates.T
    xs = _dispatch(yp1f, ys1f, sh_f, sc_f, ctx_row, ts // tm, dest, zero_lo, zero_n, cap_rows, tm)
    wgu, bgu, wd = _expert_weights(w_gate_up[l], b_gate_up[l], w_down[l])
    y_sorted = _experts(tile_expert, n_used, xs, wgu, bgu, wd, b_down[l][:, None, :])
    yp = _combine(yp1f, dest, gates, 0, gt_f, prompt_row, 1, y_sorted, ln2_g[l], ln2_b[l], tm)
    ys = _combine(ys1f, dest, gates, (bp * tp) // tm, gt_f, sample_row, ts // tm, y_sorted,
                  ln2_g[l], ln2_b[l], tm)

    return (yp.reshape(bp, tp, D_MODEL), ys.reshape(bs, ts, D_MODEL),
            k_p.reshape(bp, 1, tp, NA_HEADS, NA_DIM), v_p.reshape(bp, 1, tp, NA_HEADS, NA_DIM),
            s_f[:, None], s_b[:, None])
```

```python
import functools

import jax
import jax.numpy as jnp
from jax import lax
from jax.experimental import pallas as pl
from jax.experimental.pallas import tpu as pltpu

F32 = jnp.float32
BF16 = jnp.bfloat16
I32 = jnp.int32

D_MODEL = 1024
GRID_W = 64
NA_HEADS = 8
NA_DIM = 64
NA_WIDTH = NA_HEADS * NA_DIM
NA_KH = 8
NA_KW = 16
RET_HEADS = 4
RET_DK = 128
RET_WIDTH = RET_HEADS * RET_DK
CHUNK = 128
N_EXPERTS = 32
TOP_K = 4
D_FF = 1024
SWIGLU_LIMIT = 7.0
SWIGLU_ALPHA = 1.702
ROPE_BASE = 10000.0
LN_EPS = 1e-5
DEPTH = 1
DEEPNORM_ALPHA = (2.0 * DEPTH) ** 0.25

LANES = 128
NEG_BIG = -1e30
TOKEN_BLOCK = 256
EXPERT_TILE = 512
VMEM_LIMIT = 56 * 1024 * 1024
NA_ROW_UNROLL = 4
RET_CHUNK_UNROLL = 4
SUBLANES = 8


def _params(*sem):
    return pltpu.CompilerParams(dimension_semantics=sem, vmem_limit_bytes=VMEM_LIMIT)


def _silu(x):
    return x / (1.0 + jnp.exp(-x))


def _standardize(x):
    mu = jnp.mean(x, axis=-1, keepdims=True)
    xc = x - mu
    var = jnp.mean(xc * xc, axis=-1, keepdims=True)
    return xc * lax.rsqrt(var + LN_EPS)


def _dot_nt(a, b):
    return lax.dot_general(a, b, (((1,), (1,)), ((), ())), preferred_element_type=F32)


def _dot_tn(a, b):
    return lax.dot_general(a, b, (((0,), (0,)), ((), ())), preferred_element_type=F32)


def _ada_body(c_ref, w_ref, b_ref, o_ref):
    s = _silu(c_ref[...])
    o_ref[...] = jnp.dot(s, w_ref[...], preferred_element_type=F32,
                         precision=lax.Precision.HIGHEST) + b_ref[...]


def _adaln(cond, w_ada, b_ada):
    r = cond.shape[0]
    n = w_ada.shape[1]
    tn = 1536
    return pl.pallas_call(
        _ada_body,
        grid=(n // tn,),
        in_specs=[pl.BlockSpec((r, D_MODEL), lambda j: (0, 0)),
                  pl.BlockSpec((D_MODEL, tn), lambda j: (0, j)),
                  pl.BlockSpec((1, tn), lambda j: (0, j))],
        out_specs=pl.BlockSpec((r, tn), lambda j: (0, j)),
        out_shape=jax.ShapeDtypeStruct((r, n), F32),
        compiler_params=_params("arbitrary"),
        name="adaln",
    )(cond, w_ada, b_ada.reshape(1, n))


def _proj_body(*refs, rope):
    if rope:
        (x_ref, sh_ref, sc_ref, w_ref, cos_ref, sa_ref, sb_ref,
         q_ref, k_ref, v_ref, qr_ref, kr_ref, vr_ref, g_ref) = refs
    else:
        (x_ref, sh_ref, sc_ref, w_ref,
         q_ref, k_ref, v_ref, qr_ref, kr_ref, vr_ref, g_ref) = refs
    h = (x_ref[...] * (1.0 + sc_ref[...]) + sh_ref[...]).astype(BF16)

    def cols(c):
        return jnp.dot(h, w_ref[:, c * 512:(c + 1) * 512], preferred_element_type=F32)

    q_ref[...] = cols(0).astype(q_ref.dtype)
    k_ref[...] = cols(1).astype(k_ref.dtype)
    v_ref[...] = cols(2).astype(v_ref.dtype)
    pq = cols(3)
    pk = cols(4) * (RET_DK ** -0.5)
    if rope:
        cs, sa, sb = cos_ref[...], sa_ref[...], sb_ref[...]
        for hd in range(RET_HEADS):
            sl = slice(hd * LANES, (hd + 1) * LANES)
            for p, o_ref in ((pq, qr_ref), (pk, kr_ref)):
                xs = p[:, sl]
                rot = xs * cs + pltpu.roll(xs, 96, 1) * sa + pltpu.roll(xs, 32, 1) * sb
                o_ref[:, sl] = rot.astype(o_ref.dtype)
    else:
        qr_ref[...] = pq.astype(qr_ref.dtype)
        kr_ref[...] = pk.astype(kr_ref.dtype)
    vr_ref[...] = cols(5).astype(vr_ref.dtype)
    g_ref[...] = cols(6)


def _rope_tables(t_len):
    t = jnp.arange(t_len, dtype=jnp.int32)
    pos_row = (t // GRID_W).astype(F32)[:, None]
    pos_col = (t % GRID_W).astype(F32)[:, None]
    lane = jnp.arange(LANES, dtype=jnp.int32)[None, :]
    n_freq = RET_DK // 4
    inv_freq = ROPE_BASE ** (-(lane % n_freq).astype(F32) / n_freq)
    ang = jnp.where(lane < RET_DK // 2, pos_row, pos_col) * inv_freq
    first = (lane % (2 * n_freq)) < n_freq
    cos = jnp.cos(ang)
    sin = jnp.sin(ang)
    return cos, jnp.where(first, -sin, 0.0), jnp.where(first, 0.0, sin)


def _project(x, shift, scale, mod_row, w_in, tm, kv_dtype, rope):
    b, t, _ = x.shape
    tok = lambda bi, ti: (bi, ti, 0)
    mod = lambda bi, ti: (mod_row(bi), 0, 0)
    in_specs = [pl.BlockSpec((None, tm, D_MODEL), tok),
                pl.BlockSpec((None, 1, D_MODEL), mod),
                pl.BlockSpec((None, 1, D_MODEL), mod),
                pl.BlockSpec(w_in.shape, lambda bi, ti: (0, 0))]
    args = [x, shift, scale, w_in]
    if rope:
        tab = lambda bi, ti: (ti, 0)
        in_specs += [pl.BlockSpec((tm, LANES), tab)] * 3
        args += list(_rope_tables(t))
    o512 = pl.BlockSpec((None, tm, 512), tok)
    shp = lambda dt: jax.ShapeDtypeStruct((b, t, 512), dt)
    return pl.pallas_call(
        functools.partial(_proj_body, rope=rope),
        grid=(b, t // tm),
        in_specs=in_specs,
        out_specs=[o512] * 7,
        out_shape=[shp(BF16), shp(kv_dtype), shp(kv_dtype), shp(BF16), shp(BF16), shp(BF16), shp(F32)],
        compiler_params=_params("parallel", "parallel"),
        name="in_proj",
    )(*args)


def _pair_masks(shape):
    lane = lax.broadcasted_iota(I32, shape, 1)
    return lane < NA_DIM, lane >= NA_DIM


def _ctx_attn_body(q_ref, k_ref, v_ref, o_ref):
    q2 = q_ref[...]
    k2 = k_ref[...].astype(BF16)
    v2 = v_ref[...].astype(BF16)
    t = q2.shape[0]
    lo, hi = _pair_masks(q2.shape)
    zero = jnp.zeros_like(q2)
    qs = jnp.concatenate([jnp.where(lo, q2, zero), jnp.where(hi, q2, zero)], axis=0)
    s = _dot_nt(qs, k2) * (NA_DIM ** -0.5)
    e = jnp.exp(s - jnp.max(s, axis=-1, keepdims=True))
    den = jnp.sum(e, axis=-1, keepdims=True)
    o = jnp.dot(e.astype(BF16), v2, preferred_element_type=F32) / den
    o_ref[...] = jnp.where(lo, o[:t], o[t:]).astype(o_ref.dtype)


def _context_attention(q, k, v):
    b, t, _ = q.shape
    spec = pl.BlockSpec((None, t, LANES), lambda bi, hp: (bi, 0, hp))
    return pl.pallas_call(
        _ctx_attn_body,
        grid=(b, NA_WIDTH // LANES),
        in_specs=[spec] * 3,
        out_specs=spec,
        out_shape=jax.ShapeDtypeStruct((b, t, NA_WIDTH), BF16),
        compiler_params=_params("parallel", "parallel"),
        name="ctx_attn",
    )(q, k, v)


def _na_bias_body(rpb_ref, o_ref):
    h = pl.program_id(0)
    v = pl.program_id(1)
    shape = (GRID_W, LANES)
    w = lax.broadcasted_iota(I32, shape, 0)
    lane = lax.broadcasted_iota(I32, shape, 1)
    upper = lane >= GRID_W
    kc = jnp.where(upper, lane - GRID_W, lane)
    cdiff = kc - w + (NA_KW - 1)
    cstart = jnp.clip(w - NA_KW // 2, 0, GRID_W - NA_KW)
    inwin = (kc >= cstart) & (kc < cstart + NA_KW)
    n_ro = 2 * NA_KH - 1
    n_co = 2 * NA_KW - 1
    for j in range(NA_KH // 2):
        ro = 2 * j - v + (NA_KH - 1)
        base0 = (h * n_ro + ro) * n_co
        acc = jnp.zeros(shape, F32)
        for c in range(n_co):
            val = jnp.where(upper, rpb_ref[base0 + n_co + c], rpb_ref[base0 + c])
            acc = jnp.where(cdiff == c, val, acc)
        o_ref[:, j * LANES:(j + 1) * LANES] = jnp.where(inwin, acc, NEG_BIG)


def _na_bias(rpb):
    n = NA_KH * GRID_W
    return pl.pallas_call(
        _na_bias_body,
        grid=(NA_HEADS, NA_KH),
        in_specs=[pl.BlockSpec(memory_space=pltpu.SMEM)],
        out_specs=pl.BlockSpec((None, None, GRID_W, n), lambda h, v: (h, v, 0, 0)),
        out_shape=jax.ShapeDtypeStruct((NA_HEADS, NA_KH, GRID_W, n), F32),
        compiler_params=_params("parallel", "parallel"),
        name="na_bias",
    )(rpb.reshape(-1))


def _na_body(q_ref, k_ref, v_ref, ck_ref, cv_ref, bias_ref, o_ref, *, rows):
    ck = ck_ref[...].astype(BF16)
    cv = cv_ref[...].astype(BF16)
    lo, hi = _pair_masks((GRID_W, LANES))
    scale = NA_DIM ** -0.5
    n_loc = NA_KH * GRID_W

    def one_row(r, carry):
        r_start = jnp.clip(r - NA_KH // 2, 0, rows - NA_KH)
        vidx = r - r_start
        q2 = q_ref[pl.ds(pl.multiple_of(r * GRID_W, GRID_W), GRID_W), :]
        ksl = pl.ds(pl.multiple_of(r_start * GRID_W, GRID_W), n_loc)
        kl = k_ref[ksl, :]
        vl = v_ref[ksl, :]
        zero = jnp.zeros_like(q2)
        qs = jnp.concatenate([jnp.where(lo, q2, zero), jnp.where(hi, q2, zero)], axis=0)
        bias = jnp.concatenate([bias_ref[0, vidx], bias_ref[1, vidx]], axis=0)
        s_loc = _dot_nt(qs, kl) * scale + bias
        s_ctx = _dot_nt(qs, ck) * scale
        m = jnp.maximum(jnp.max(s_loc, axis=-1, keepdims=True), jnp.max(s_ctx, axis=-1, keepdims=True))
        e_loc = jnp.exp(s_loc - m)
        e_ctx = jnp.exp(s_ctx - m)
        den = jnp.sum(e_loc, axis=-1, keepdims=True) + jnp.sum(e_ctx, axis=-1, keepdims=True)
        o = (jnp.dot(e_loc.astype(BF16), vl, preferred_element_type=F32)
             + jnp.dot(e_ctx.astype(BF16), cv, preferred_element_type=F32)) / den
        o_ref[pl.ds(pl.multiple_of(r * GRID_W, GRID_W), GRID_W), :] = (
            jnp.where(lo, o[:GRID_W], o[GRID_W:]).astype(o_ref.dtype))
        return carry

    lax.fori_loop(0, rows, one_row, 0, unroll=NA_ROW_UNROLL)


def _neighbourhood_attention(q, k, v, ck, cv, bias):
    b, t, _ = q.shape
    lc = ck.shape[1]
    rows = t // GRID_W
    tok = pl.BlockSpec((None, t, LANES), lambda hp, bi: (bi, 0, hp))
    ctx = pl.BlockSpec((None, lc, LANES), lambda hp, bi: (bi, 0, hp))
    return pl.pallas_call(
        functools.partial(_na_body, rows=rows),
        grid=(NA_WIDTH // LANES, b),
        in_specs=[tok, tok, tok, ctx, ctx,
                  pl.BlockSpec((2, NA_KH, GRID_W, NA_KH * GRID_W), lambda hp, bi: (hp, 0, 0, 0))],
        out_specs=tok,
        out_shape=jax.ShapeDtypeStruct((b, t, NA_WIDTH), BF16),
        compiler_params=_params("parallel", "parallel"),
        name="na_attn",
    )(q, k, v, ck, cv, bias)


def _ret_body(q_ref, k_ref, v_ref, g_ref, df_ref, db_ref, beta_ref, s0f_ref, s0b_ref,
              o_ref, sf_ref, sb_ref, kvf_s, kvb_s, *, n_chunks):
    c_len = CHUNK
    lgf = -jnp.log1p(jnp.exp(-df_ref[...]))
    lgb = -jnp.log1p(jnp.exp(-db_ref[...]))
    shape = (c_len, c_len)
    i = lax.broadcasted_iota(I32, shape, 0).astype(F32)
    j = lax.broadcasted_iota(I32, shape, 1).astype(F32)
    dij = i - j
    d_comb = (jnp.where(dij >= 0, jnp.exp(jnp.maximum(dij, 0.0) * lgf), 0.0)
              + jnp.where(dij <= 0, jnp.exp(jnp.maximum(-dij, 0.0) * lgb), 0.0))
    xi_f = jnp.exp((i + 1.0) * lgf)
    xi_b = jnp.exp((c_len - i) * lgb)
    zeta_f = jnp.exp((c_len - 1.0 - i) * lgf)
    zeta_b = jnp.exp(i * lgb)
    g_f = jnp.exp(c_len * lgf)
    g_b = jnp.exp(c_len * lgb)

    def chunk(c):
        return pl.ds(pl.multiple_of(c * c_len, c_len), c_len)

    def kv_pass(c, carry):
        kc = k_ref[chunk(c), :]
        vc = v_ref[chunk(c), :].astype(F32)
        kvf_s[c] = _dot_tn(kc, (vc * zeta_f).astype(BF16))
        kvb_s[c] = _dot_tn(kc, (vc * zeta_b).astype(BF16))
        return carry

    lax.fori_loop(0, n_chunks, kv_pass, 0, unroll=RET_CHUNK_UNROLL)

    def scan_f(c, s):
        kv = kvf_s[c]
        kvf_s[c] = s
        return g_f * s + kv

    def scan_b(ci, s):
        c = n_chunks - 1 - ci
        kv = kvb_s[c]
        kvb_s[c] = s
        return g_b * s + kv

    sf_ref[...] = lax.fori_loop(0, n_chunks, scan_f, s0f_ref[...])
    sb_ref[...] = lax.fori_loop(0, n_chunks, scan_b, s0b_ref[...])

    beta = beta_ref[...]

    def out_pass(c, carry):
        qc = q_ref[chunk(c), :]
        kc = k_ref[chunk(c), :]
        vc = v_ref[chunk(c), :]
        scores = _dot_nt(qc, kc) * d_comb
        y = jnp.dot(scores.astype(BF16), vc, preferred_element_type=F32)
        qf = qc.astype(F32)
        y += jnp.dot((qf * xi_f).astype(BF16), kvf_s[c].astype(BF16), preferred_element_type=F32)
        y += jnp.dot((qf * xi_b).astype(BF16), kvb_s[c].astype(BF16), preferred_element_type=F32)
        o_ref[chunk(c), :] = (_standardize(y) * beta * _silu(g_ref[chunk(c), :])).astype(o_ref.dtype)
        return carry

    lax.fori_loop(0, n_chunks, out_pass, 0, unroll=RET_CHUNK_UNROLL)


def _retention(q, k, v, g, decay_f, decay_b, beta_ret, s0f, s0b):
    b, t, _ = q.shape
    n_chunks = t // CHUNK
    tok = pl.BlockSpec((None, t, LANES), lambda bi, h: (bi, 0, h))
    per_head = pl.BlockSpec((None, 1, LANES), lambda bi, h: (h, 0, 0))
    state = pl.BlockSpec((None, None, RET_DK, RET_DK), lambda bi, h: (bi, h, 0, 0))
    lanes = lambda a: jnp.broadcast_to(a.astype(F32)[:, None, None], (RET_HEADS, 1, LANES))
    st_shape = jax.ShapeDtypeStruct((b, RET_HEADS, RET_DK, RET_DK), F32)
    return pl.pallas_call(
        functools.partial(_ret_body, n_chunks=n_chunks),
        grid=(b, RET_HEADS),
        in_specs=[tok, tok, tok, tok, per_head, per_head, per_head, state, state],
        out_specs=[tok, state, state],
        out_shape=[jax.ShapeDtypeStruct((b, t, RET_WIDTH), BF16), st_shape, st_shape],
        scratch_shapes=[pltpu.VMEM((n_chunks, RET_DK, RET_DK), F32)] * 2,
        compiler_params=_params("parallel", "parallel"),
        name="retention",
    )(q, k, v, g, lanes(decay_f), lanes(decay_b), beta_ret.reshape(RET_HEADS, 1, LANES), s0f, s0b)


def _merge_body(x_ref, ona_ref, ret_ref, wna_ref, wret_ref, bna_ref, gate_ref, lg_ref, lb_ref, o_ref):
    o = ona_ref[...].astype(F32)
    na = o * lax.rsqrt(jnp.mean(o * o, axis=-1, keepdims=True) + LN_EPS) * bna_ref[...]
    mix = (jnp.dot(na.astype(BF16), wna_ref[...], preferred_element_type=F32)
           + jnp.dot(ret_ref[...], wret_ref[...], preferred_element_type=F32))
    y = DEEPNORM_ALPHA * x_ref[...] + gate_ref[...] * mix
    o_ref[...] = _standardize(y) * lg_ref[...] + lb_ref[...]


def _merge(x, o_na, ret, w_out, beta_na, gate, mod_row, ln_g, ln_b, tm):
    b, t, _ = x.shape
    tok = lambda bi, ti: (bi, ti, 0)
    row = pl.BlockSpec((1, D_MODEL), lambda bi, ti: (0, 0))
    half = pl.BlockSpec((None, tm, 512), tok)
    return pl.pallas_call(
        _merge_body,
        grid=(b, t // tm),
        in_specs=[pl.BlockSpec((None, tm, D_MODEL), tok), half, half,
                  pl.BlockSpec((512, D_MODEL), lambda bi, ti: (0, 0)),
                  pl.BlockSpec((512, D_MODEL), lambda bi, ti: (1, 0)),
                  pl.BlockSpec((1, 512), lambda bi, ti: (0, 0)),
                  pl.BlockSpec((None, 1, D_MODEL), lambda bi, ti: (mod_row(bi), 0, 0)),
                  row, row],
        out_specs=pl.BlockSpec((None, tm, D_MODEL), tok),
        out_shape=jax.ShapeDtypeStruct(x.shape, F32),
        compiler_params=_params("parallel", "parallel"),
        name="merge_ln1",
    )(x, o_na, ret, w_out, w_out, beta_na.reshape(1, 512), gate, ln_g.reshape(1, -1), ln_b.reshape(1, -1))


def _two_group_specs(n_p_blocks, blocks_per_batch_s, ctx_row):
    tok_p = lambda i: (jnp.minimum(i, n_p_blocks - 1), 0)
    tok_s = lambda i: (jnp.maximum(i - n_p_blocks, 0), 0)
    mod = lambda i: (jnp.where(i < n_p_blocks, ctx_row, jnp.maximum(i - n_p_blocks, 0) // blocks_per_batch_s), 0, 0)
    return tok_p, tok_s, mod


def _router_body(yp_ref, ys_ref, sh_ref, sc_ref, wh_ref, wl_ref, br_ref, e_ref, gt_ref, rk_ref, cnt_ref,
                 carry_s, *, n_p_blocks):
    i = pl.program_id(0)
    tm = yp_ref.shape[0]

    @pl.when(i == 0)
    def _():
        carry_s[...] = jnp.zeros_like(carry_s)

    y = jnp.where(i < n_p_blocks, yp_ref[...], ys_ref[...])
    h = y * (1.0 + sc_ref[...]) + sh_ref[...]
    h_hi = h.astype(BF16)
    h_lo = (h - h_hi.astype(F32)).astype(BF16)
    w_hi = wh_ref[...]
    lg = _dot_nt(jnp.concatenate([w_hi, wl_ref[...]], axis=0), h_hi)
    work = lg[:N_EXPERTS] + lg[N_EXPERTS:] + _dot_nt(w_hi, h_lo) + br_ref[...]
    eidx = lax.broadcasted_iota(I32, (N_EXPERTS, tm), 0).astype(F32)
    vals, idxs, hots = [], [], []
    for _ in range(TOP_K):
        mx = jnp.max(work, axis=0, keepdims=True)
        idx = jnp.min(jnp.where(work == mx, eidx, float(N_EXPERTS)), axis=0, keepdims=True)
        hot = eidx == idx
        vals.append(mx)
        idxs.append(idx)
        hots.append(hot)
        work = jnp.where(hot, -jnp.inf, work)
    exps = [jnp.exp(v - vals[0]) for v in vals]
    den = exps[0] + exps[1] + exps[2] + exps[3]
    sel = jnp.zeros((N_EXPERTS, tm), F32)
    for hot in hots:
        sel = sel + hot.astype(F32)
    r_i = lax.broadcasted_iota(I32, (tm, tm), 0)
    c_i = lax.broadcasted_iota(I32, (tm, tm), 1)
    earlier = (r_i < c_i).astype(BF16)
    carry = carry_s[...]
    before = (jnp.dot(sel.astype(BF16), earlier, preferred_element_type=F32)
              + jnp.concatenate([carry] * (tm // LANES), axis=1))
    ranks = [jnp.sum(jnp.where(hot, before, 0.0), axis=0, keepdims=True) for hot in hots]
    e_ref[...] = jnp.concatenate(idxs, axis=0).astype(I32)
    gt_ref[...] = jnp.concatenate([e / den for e in exps], axis=0)
    rk_ref[...] = jnp.concatenate(ranks, axis=0).astype(I32)
    carry = carry + jnp.sum(sel, axis=1, keepdims=True)
    carry_s[...] = carry
    cnt_ref[...] = carry


def _router(yp, ys, shift, scale, ctx_row, s_blocks_per_batch, w_router, b_router, tm):
    n_p, n_s = yp.shape[0], ys.shape[0]
    n = n_p + n_s
    npb = n_p // tm
    tok_p, tok_s, mod = _two_group_specs(npb, s_blocks_per_batch, ctx_row)
    w_t = w_router.T
    w_hi = w_t.astype(BF16)
    w_lo = (w_t - w_hi.astype(F32)).astype(BF16)
    br = jnp.broadcast_to(b_router[:, None], (N_EXPERTS, tm))
    o4 = pl.BlockSpec((TOP_K, tm), lambda i: (0, i))
    whole = lambda shape: pl.BlockSpec(shape, lambda i: (0, 0))
    return pl.pallas_call(
        functools.partial(_router_body, n_p_blocks=npb),
        grid=(n // tm,),
        in_specs=[pl.BlockSpec((tm, D_MODEL), tok_p), pl.BlockSpec((tm, D_MODEL), tok_s),
                  pl.BlockSpec((None, 1, D_MODEL), mod), pl.BlockSpec((None, 1, D_MODEL), mod),
                  whole((N_EXPERTS, D_MODEL)), whole((N_EXPERTS, D_MODEL)), whole((N_EXPERTS, tm))],
        out_specs=[o4, o4, o4, whole((N_EXPERTS, LANES))],
        out_shape=[jax.ShapeDtypeStruct((TOP_K, n), I32), jax.ShapeDtypeStruct((TOP_K, n), F32),
                   jax.ShapeDtypeStruct((TOP_K, n), I32), jax.ShapeDtypeStruct((N_EXPERTS, LANES), F32)],
        scratch_shapes=[pltpu.VMEM((N_EXPERTS, LANES), F32)],
        compiler_params=_params("arbitrary"),
        name="moe_router",
    )(yp, ys, shift, scale, w_hi, w_lo, br)


def _dispatch_body(dest_ref, zlo_ref, zn_ref, yp_ref, ys_ref, sh_ref, sc_ref, xs_ref, hbuf, zbuf, sem,
                   *, n_p_blocks, n_blocks):
    i = pl.program_id(0)
    tm = yp_ref.shape[0]
    y = jnp.where(i < n_p_blocks, yp_ref[...], ys_ref[...])
    hbuf[...] = (y * (1.0 + sc_ref[...]) + sh_ref[...]).reshape(hbuf.shape)

    def issue(g, carry):
        for u in range(SUBLANES):
            for kk in range(TOP_K):
                d = dest_ref[(g * SUBLANES + u) * TOP_K + kk]
                pltpu.make_async_copy(hbuf.at[g, pl.ds(u, 1)], xs_ref.at[pl.ds(d, 1)], sem).start(
                    priority=kk % 2)
        return carry

    lax.fori_loop(0, tm // SUBLANES, issue, 0)
    block = xs_ref.at[pl.ds(0, tm)]
    for kk in range(TOP_K):
        pltpu.make_async_copy(block, block, sem).wait()

    @pl.when(i == n_blocks - 1)
    def _():
        zbuf[...] = jnp.zeros_like(zbuf)

        def zero_rows(lo, nz):
            def copy(r):
                return pltpu.make_async_copy(zbuf.at[pl.ds(0, 1)], xs_ref.at[pl.ds(lo + r, 1)], sem)

            def z_issue(r, c2):
                copy(r).start()
                return c2

            def z_drain(r, c2):
                copy(0).wait()
                return c2

            lax.fori_loop(0, nz, z_issue, 0)
            lax.fori_loop(0, nz, z_drain, 0)

        def per_expert(e, carry):
            zero_rows(zlo_ref[e], zn_ref[e])
            return carry

        lax.fori_loop(0, N_EXPERTS, per_expert, 0)

        tail_lo = zlo_ref[N_EXPERTS]

        def tile_copy(t):
            start = pl.multiple_of(tail_lo + t * EXPERT_TILE, EXPERT_TILE)
            return pltpu.make_async_copy(zbuf, xs_ref.at[pl.ds(start, EXPERT_TILE)], sem)

        def t_issue(t, c2):
            tile_copy(t).start()
            return c2

        def t_drain(t, c2):
            tile_copy(0).wait()
            return c2

        n_tail = zn_ref[N_EXPERTS] // EXPERT_TILE
        lax.fori_loop(0, n_tail, t_issue, 0)
        lax.fori_loop(0, n_tail, t_drain, 0)


def _dispatch(yp, ys, shift, scale, ctx_row, s_blocks_per_batch, dest, zero_lo, zero_n, cap_rows, tm):
    n_p, n_s = yp.shape[0], ys.shape[0]
    n = n_p + n_s
    npb = n_p // tm
    nb = n // tm
    tok_p, tok_s, mod = _two_group_specs(npb, s_blocks_per_batch, ctx_row)
    smem_all = pl.BlockSpec(memory_space=pltpu.SMEM)
    return pl.pallas_call(
        functools.partial(_dispatch_body, n_p_blocks=npb, n_blocks=nb),
        grid=(nb,),
        in_specs=[pl.BlockSpec((tm * TOP_K,), lambda i: (i,), memory_space=pltpu.SMEM),
                  smem_all, smem_all,
                  pl.BlockSpec((tm, D_MODEL), tok_p), pl.BlockSpec((tm, D_MODEL), tok_s),
                  pl.BlockSpec((None, 1, D_MODEL), mod), pl.BlockSpec((None, 1, D_MODEL), mod)],
        out_specs=pl.BlockSpec(memory_space=pl.ANY),
        out_shape=jax.ShapeDtypeStruct((cap_rows, D_MODEL), F32),
        scratch_shapes=[pltpu.VMEM((tm // SUBLANES, SUBLANES, D_MODEL), F32),
                        pltpu.VMEM((EXPERT_TILE, D_MODEL), F32),
                        pltpu.SemaphoreType.DMA(())],
        compiler_params=_params("arbitrary"),
        name="moe_dispatch",
    )(dest, zero_lo, zero_n, yp, ys, shift, scale)


def _expert_body(te_ref, nu_ref, x_ref, wgu_ref, bgu_ref, wd_ref, bd_ref, y_ref, act_s):
    j = pl.program_id(0)

    @pl.when(j < nu_ref[0])
    def _():
        x = x_ref[...].astype(BF16)
        for c in range(D_FF // LANES):
            sl = slice(c * 2 * LANES, (c + 1) * 2 * LANES)
            hu = jnp.dot(x, wgu_ref[:, sl], preferred_element_type=F32) + bgu_ref[:, sl]
            x_glu = jnp.minimum(hu[:, :LANES], SWIGLU_LIMIT)
            x_lin = jnp.clip(hu[:, LANES:], -SWIGLU_LIMIT, SWIGLU_LIMIT)
            act = x_glu * (1.0 / (1.0 + jnp.exp(-SWIGLU_ALPHA * x_glu))) * (x_lin + 1.0)
            act_s[:, c * LANES:(c + 1) * LANES] = act.astype(BF16)
        y_ref[...] = jnp.dot(act_s[...], wd_ref[...], preferred_element_type=F32) + bd_ref[...]

    @pl.when(j >= nu_ref[0])
    def _():
        y_ref[...] = jnp.zeros_like(y_ref)


def _experts(tile_expert, n_used, xs, wgu, bgu, wd, bd):
    cap_rows = xs.shape[0]
    n_tiles = cap_rows // EXPERT_TILE
    grid_spec = pltpu.PrefetchScalarGridSpec(
        num_scalar_prefetch=2,
        grid=(n_tiles,),
        in_specs=[pl.BlockSpec((EXPERT_TILE, D_MODEL), lambda j, te, nu: (jnp.minimum(j, nu[0] - 1), 0)),
                  pl.BlockSpec((None, D_MODEL, 2 * D_FF), lambda j, te, nu: (te[j], 0, 0)),
                  pl.BlockSpec((None, 1, 2 * D_FF), lambda j, te, nu: (te[j], 0, 0)),
                  pl.BlockSpec((None, D_FF, D_MODEL), lambda j, te, nu: (te[j], 0, 0)),
                  pl.BlockSpec((None, 1, D_MODEL), lambda j, te, nu: (te[j], 0, 0))],
        out_specs=pl.BlockSpec((EXPERT_TILE, D_MODEL), lambda j, te, nu: (j, 0)),
        scratch_shapes=[pltpu.VMEM((EXPERT_TILE, D_FF), BF16)],
    )
    return pl.pallas_call(
        _expert_body,
        grid_spec=grid_spec,
        out_shape=jax.ShapeDtypeStruct((cap_rows, D_MODEL), F32),
        compiler_params=_params("arbitrary"),
        name="moe_experts",
    )(tile_expert, n_used, xs, wgu, bgu, wd, bd)


def _combine_body(dest_ref, y1_ref, gt_ref, gate_ref, lg_ref, lb_ref, ys_ref, o_ref, buf, sem):
    tm = y1_ref.shape[0]

    def issue(g, carry):
        for u in range(SUBLANES):
            for kk in range(TOP_K):
                d = dest_ref[(g * SUBLANES + u) * TOP_K + kk]
                pltpu.make_async_copy(ys_ref.at[pl.ds(d, 1)], buf.at[kk, g, pl.ds(u, 1)], sem).start(
                    priority=kk % 2)
        return carry

    lax.fori_loop(0, tm // SUBLANES, issue, 0)
    block = ys_ref.at[pl.ds(0, tm)]
    for kk in range(TOP_K):
        pltpu.make_async_copy(block, block, sem).wait()

    gates = gt_ref[...]
    f = gates[:, 0:1] * buf[0].reshape(tm, D_MODEL)
    for kk in range(1, TOP_K):
        f = f + gates[:, kk:kk + 1] * buf[kk].reshape(tm, D_MODEL)
    y = DEEPNORM_ALPHA * y1_ref[...] + gate_ref[...] * f
    o_ref[...] = _standardize(y) * lg_ref[...] + lb_ref[...]


def _combine(y1, dest, gates, block0, gate_mod, mod_row, blocks_per_batch, y_sorted, ln_g, ln_b, tm):
    n_g = y1.shape[0]
    row = pl.BlockSpec((1, D_MODEL), lambda i: (0, 0))
    return pl.pallas_call(
        _combine_body,
        grid=(n_g // tm,),
        in_specs=[pl.BlockSpec((tm * TOP_K,), lambda i: (i + block0,), memory_space=pltpu.SMEM),
                  pl.BlockSpec((tm, D_MODEL), lambda i: (i, 0)),
                  pl.BlockSpec((tm, TOP_K), lambda i: (i + block0, 0)),
                  pl.BlockSpec((None, 1, D_MODEL), lambda i: (mod_row(i // blocks_per_batch), 0, 0)),
                  row, row,
                  pl.BlockSpec(memory_space=pl.ANY)],
        out_specs=pl.BlockSpec((tm, D_MODEL), lambda i: (i, 0)),
        out_shape=jax.ShapeDtypeStruct((n_g, D_MODEL), F32),
        scratch_shapes=[pltpu.VMEM((TOP_K, tm // SUBLANES, SUBLANES, D_MODEL), F32),
                        pltpu.SemaphoreType.DMA(())],
        compiler_params=_params("arbitrary"),
        name="moe_combine_ln2",
    )(dest, y1, gates, gate_mod, ln_g.reshape(1, -1), ln_b.reshape(1, -1), y_sorted)


def _regroup_body(w_ref, o_ref):
    blk = 2 * LANES
    src = lax.broadcasted_iota(I32, (blk, blk), 0)
    dst = lax.broadcasted_iota(I32, (blk, blk), 1)
    perm = (src == jnp.where(dst < LANES, 2 * dst, 2 * (dst - LANES) + 1)).astype(BF16)
    for s in range(w_ref.shape[1] // blk):
        sl = slice(s * blk, (s + 1) * blk)
        o_ref[:, sl] = jnp.dot(w_ref[:, sl].astype(BF16), perm, preferred_element_type=F32).astype(BF16)


def _regroup_gate_up(w_gate_up):
    e, d, n = w_gate_up.shape
    tn = 512
    spec = pl.BlockSpec((None, d, tn), lambda ei, j: (ei, 0, j))
    return pl.pallas_call(
        _regroup_body,
        grid=(e, n // tn),
        in_specs=[spec],
        out_specs=spec,
        out_shape=jax.ShapeDtypeStruct((e, d, n), BF16),
        compiler_params=_params("parallel", "parallel"),
        name="regroup_gate_up",
    )(w_gate_up)


def _expert_weights(w_gate_up, b_gate_up, w_down):
    e = w_gate_up.shape[0]
    nblk = D_FF // LANES
    bgu = b_gate_up.reshape(e, 1, nblk, LANES, 2).transpose(0, 1, 2, 4, 3).reshape(e, 1, 2 * D_FF)
    return _regroup_gate_up(w_gate_up), bgu, w_down.astype(BF16)


def _routing_plan(e_idx, rank, counts):
    n_pairs = e_idx.shape[1] * TOP_K
    cnt = counts[:, 0].astype(I32)
    padded = (cnt + EXPERT_TILE - 1) // EXPERT_TILE * EXPERT_TILE
    pad_end = jnp.cumsum(padded)
    pad_start = pad_end - padded
    hot = e_idx[:, :, None] == jnp.arange(N_EXPERTS, dtype=I32)
    start_of = jnp.sum(jnp.where(hot, pad_start, 0), axis=-1)
    dest = (start_of + rank).T.reshape(-1)
    n_tiles = n_pairs // EXPERT_TILE + N_EXPERTS
    n_used = pad_end[-1] // EXPERT_TILE
    tile_row = jnp.minimum(jnp.arange(n_tiles, dtype=I32), n_used - 1) * EXPERT_TILE
    tile_expert = jnp.sum((pad_end[None, :] <= tile_row[:, None]).astype(I32), axis=1)
    tile_expert = jnp.minimum(tile_expert, N_EXPERTS - 1)
    cap_rows = n_tiles * EXPERT_TILE
    zero_lo = jnp.concatenate([pad_start + cnt, pad_end[-1:]])
    zero_n = jnp.concatenate([padded - cnt, cap_rows - pad_end[-1:]])
    return dest, tile_expert, n_used.reshape(1), zero_lo, zero_n, cap_rows


def kernel(x_prompt, x_sample, cache_na_k, cache_na_v, state_ret_fwd, state_ret_bwd, c, c_ctx, w_ada, b_ada, w_in, rpb, ret_decay_fwd, ret_decay_bwd, beta_na, beta_ret, w_out, ln1_g, ln1_b, w_router, b_router, w_gate_up, b_gate_up, w_down, b_down, ln2_g, ln2_b):
    bp, tp, _ = x_prompt.shape
    bs, ts, _ = x_sample.shape
    assert w_ada.shape[0] == DEPTH == 1
    l = 0
    ctx_row = bs
    sample_row = lambda bi: bi
    prompt_row = lambda bi: ctx_row

    cond = jnp.concatenate([c, c_ctx[None, :], jnp.zeros((16 - bs - 1, D_MODEL), F32)], axis=0)
    m = _adaln(cond, w_ada[l], b_ada[l]).reshape(16, 6, 1, D_MODEL)
    sh_a, sc_a, gt_a, sh_f, sc_f, gt_f = (m[:, i] for i in range(6))

    w_in_b = w_in[l].astype(BF16)
    w_out_b = w_out[l].astype(BF16)

    q, k_p, v_p, qr, kr, vr, g = _project(x_prompt, sh_a, sc_a, prompt_row, w_in_b, tp, F32, rope=False)
    o_na = _context_attention(q, k_p, v_p)
    zeros_state = jnp.zeros((bp, RET_HEADS, RET_DK, RET_DK), F32)
    ret, s_f, s_b = _retention(qr, kr, vr, g, ret_decay_fwd[l], ret_decay_bwd[l], beta_ret[l],
                               zeros_state, zeros_state)
    yp1 = _merge(x_prompt, o_na, ret, w_out_b, beta_na[l], gt_a, prompt_row, ln1_g[l], ln1_b[l], tp)

    q, k_s, v_s, qr, kr, vr, g = _project(x_sample, sh_a, sc_a, sample_row, w_in_b, 512, BF16, rope=True)
    lc = cache_na_k.shape[2]
    o_na = _neighbourhood_attention(q, k_s, v_s, cache_na_k[:, l].reshape(bs, lc, NA_WIDTH),
                                    cache_na_v[:, l].reshape(bs, lc, NA_WIDTH), _na_bias(rpb[l]))
    ret, _, _ = _retention(qr, kr, vr, g, ret_decay_fwd[l], ret_decay_bwd[l], beta_ret[l],
                           state_ret_fwd[:, l], state_ret_bwd[:, l])
    ys1 = _merge(x_sample, o_na, ret, w_out_b, beta_na[l], gt_a, sample_row, ln1_g[l], ln1_b[l], 512)

    tm = TOKEN_BLOCK
    yp1f = yp1.reshape(bp * tp, D_MODEL)
    ys1f = ys1.reshape(bs * ts, D_MODEL)
    e_idx, gates, rank, counts = _router(yp1f, ys1f, sh_f, sc_f, ctx_row, ts // tm, w_router[l], b_router[l], tm)
    dest, tile_expert, n_used, zero_lo, zero_n, cap_rows = _routing_plan(e_idx, rank, counts)
    gates = gates.T
    xs = _dispatch(yp1f, ys1f, sh_f, sc_f, ctx_row, ts // tm, dest, zero_lo, zero_n, cap_rows, tm)
    wgu, bgu, wd = _expert_weights(w_gate_up[l], b_gate_up[l], w_down[l])
    y_sorted = _experts(tile_expert, n_used, xs, wgu, bgu, wd, b_down[l][:, None, :])
    yp = _combine(yp1f, dest, gates, 0, gt_f, prompt_row, 1, y_sorted, ln2_g[l], ln2_b[l], tm)
    ys = _combine(ys1f, dest, gates, (bp * tp) // tm, gt_f, sample_row, ts // tm, y_sorted,
                  ln2_g[l], ln2_b[l], tm)

    return (yp.reshape(bp, tp, D_MODEL), ys.reshape(bs, ts, D_MODEL),
            k_p.reshape(bp, 1, tp, NA_HEADS, NA_DIM), v_p.reshape(bp, 1, tp, NA_HEADS, NA_DIM),
            s_f[:, None], s_b[:, None])
```

```python
import functools

import jax
import jax.numpy as jnp
from jax import lax
from jax.experimental import pallas as pl
from jax.experimental.pallas import tpu as pltpu

F32 = jnp.float32
BF16 = jnp.bfloat16
I32 = jnp.int32

D_MODEL = 1024
GRID_W = 64
NA_HEADS = 8
NA_DIM = 64
NA_WIDTH = NA_HEADS * NA_DIM
NA_KH = 8
NA_KW = 16
RET_HEADS = 4
RET_DK = 128
RET_WIDTH = RET_HEADS * RET_DK
CHUNK = 128
N_EXPERTS = 32
TOP_K = 4
D_FF = 1024
SWIGLU_LIMIT = 7.0
SWIGLU_ALPHA = 1.702
ROPE_BASE = 10000.0
LN_EPS = 1e-5
DEPTH = 1
DEEPNORM_ALPHA = (2.0 * DEPTH) ** 0.25

LANES = 128
NEG_BIG = -1e30
TOKEN_BLOCK = 256
EXPERT_TILE = 512
VMEM_LIMIT = 56 * 1024 * 1024
NA_ROW_UNROLL = 4
RET_CHUNK_UNROLL = 4
SUBLANES = 8
RUN_ALIGN = SUBLANES
LOCAL_ROWS = TOKEN_BLOCK * TOP_K + N_EXPERTS * RUN_ALIGN
RUN_PIECES = tuple(TOKEN_BLOCK >> s for s in range(TOKEN_BLOCK.bit_length())
                   if (TOKEN_BLOCK >> s) >= RUN_ALIGN)


def _params(*sem):
    return pltpu.CompilerParams(dimension_semantics=sem, vmem_limit_bytes=VMEM_LIMIT)


def _silu(x):
    return x / (1.0 + jnp.exp(-x))


def _standardize(x):
    mu = jnp.mean(x, axis=-1, keepdims=True)
    xc = x - mu
    var = jnp.mean(xc * xc, axis=-1, keepdims=True)
    return xc * lax.rsqrt(var + LN_EPS)


def _dot_nt(a, b):
    return lax.dot_general(a, b, (((1,), (1,)), ((), ())), preferred_element_type=F32)


def _dot_tn(a, b):
    return lax.dot_general(a, b, (((0,), (0,)), ((), ())), preferred_element_type=F32)


def _ada_body(c_ref, w_ref, b_ref, o_ref):
    s = _silu(c_ref[...])
    o_ref[...] = jnp.dot(s, w_ref[...], preferred_element_type=F32,
                         precision=lax.Precision.HIGHEST) + b_ref[...]


def _adaln(cond, w_ada, b_ada):
    r = cond.shape[0]
    n = w_ada.shape[1]
    tn = 1536
    return pl.pallas_call(
        _ada_body,
        grid=(n // tn,),
        in_specs=[pl.BlockSpec((r, D_MODEL), lambda j: (0, 0)),
                  pl.BlockSpec((D_MODEL, tn), lambda j: (0, j)),
                  pl.BlockSpec((1, tn), lambda j: (0, j))],
        out_specs=pl.BlockSpec((r, tn), lambda j: (0, j)),
        out_shape=jax.ShapeDtypeStruct((r, n), F32),
        compiler_params=_params("arbitrary"),
        name="adaln",
    )(cond, w_ada, b_ada.reshape(1, n))


def _proj_body(*refs, rope):
    if rope:
        (x_ref, sh_ref, sc_ref, w_ref, cos_ref, sa_ref, sb_ref,
         q_ref, k_ref, v_ref, qr_ref, kr_ref, vr_ref, g_ref) = refs
    else:
        (x_ref, sh_ref, sc_ref, w_ref,
         q_ref, k_ref, v_ref, qr_ref, kr_ref, vr_ref, g_ref) = refs
    h = (x_ref[...] * (1.0 + sc_ref[...]) + sh_ref[...]).astype(BF16)

    def cols(c):
        return jnp.dot(h, w_ref[:, c * 512:(c + 1) * 512], preferred_element_type=F32)

    q_ref[...] = cols(0).astype(q_ref.dtype)
    k_ref[...] = cols(1).astype(k_ref.dtype)
    v_ref[...] = cols(2).astype(v_ref.dtype)
    pq = cols(3)
    pk = cols(4) * (RET_DK ** -0.5)
    if rope:
        cs, sa, sb = cos_ref[...], sa_ref[...], sb_ref[...]
        for hd in range(RET_HEADS):
            sl = slice(hd * LANES, (hd + 1) * LANES)
            for p, o_ref in ((pq, qr_ref), (pk, kr_ref)):
                xs = p[:, sl]
                rot = xs * cs + pltpu.roll(xs, 96, 1) * sa + pltpu.roll(xs, 32, 1) * sb
                o_ref[:, sl] = rot.astype(o_ref.dtype)
    else:
        qr_ref[...] = pq.astype(qr_ref.dtype)
        kr_ref[...] = pk.astype(kr_ref.dtype)
    vr_ref[...] = cols(5).astype(vr_ref.dtype)
    g_ref[...] = cols(6)


def _rope_tables(t_len):
    t = jnp.arange(t_len, dtype=jnp.int32)
    pos_row = (t // GRID_W).astype(F32)[:, None]
    pos_col = (t % GRID_W).astype(F32)[:, None]
    lane = jnp.arange(LANES, dtype=jnp.int32)[None, :]
    n_freq = RET_DK // 4
    inv_freq = ROPE_BASE ** (-(lane % n_freq).astype(F32) / n_freq)
    ang = jnp.where(lane < RET_DK // 2, pos_row, pos_col) * inv_freq
    first = (lane % (2 * n_freq)) < n_freq
    cos = jnp.cos(ang)
    sin = jnp.sin(ang)
    return cos, jnp.where(first, -sin, 0.0), jnp.where(first, 0.0, sin)


def _project(x, shift, scale, mod_row, w_in, tm, kv_dtype, rope):
    b, t, _ = x.shape
    tok = lambda bi, ti: (bi, ti, 0)
    mod = lambda bi, ti: (mod_row(bi), 0, 0)
    in_specs = [pl.BlockSpec((None, tm, D_MODEL), tok),
                pl.BlockSpec((None, 1, D_MODEL), mod),
                pl.BlockSpec((None, 1, D_MODEL), mod),
                pl.BlockSpec(w_in.shape, lambda bi, ti: (0, 0))]
    args = [x, shift, scale, w_in]
    if rope:
        tab = lambda bi, ti: (ti, 0)
        in_specs += [pl.BlockSpec((tm, LANES), tab)] * 3
        args += list(_rope_tables(t))
    o512 = pl.BlockSpec((None, tm, 512), tok)
    shp = lambda dt: jax.ShapeDtypeStruct((b, t, 512), dt)
    return pl.pallas_call(
        functools.partial(_proj_body, rope=rope),
        grid=(b, t // tm),
        in_specs=in_specs,
        out_specs=[o512] * 7,
        out_shape=[shp(BF16), shp(kv_dtype), shp(kv_dtype), shp(BF16), shp(BF16), shp(BF16), shp(F32)],
        compiler_params=_params("parallel", "parallel"),
        name="in_proj",
    )(*args)


def _pair_masks(shape):
    lane = lax.broadcasted_iota(I32, shape, 1)
    return lane < NA_DIM, lane >= NA_DIM


def _ctx_attn_body(q_ref, k_ref, v_ref, o_ref):
    q2 = q_ref[...]
    k2 = k_ref[...].astype(BF16)
    v2 = v_ref[...].astype(BF16)
    t = q2.shape[0]
    lo, hi = _pair_masks(q2.shape)
    zero = jnp.zeros_like(q2)
    qs = jnp.concatenate([jnp.where(lo, q2, zero), jnp.where(hi, q2, zero)], axis=0)
    s = _dot_nt(qs, k2) * (NA_DIM ** -0.5)
    e = jnp.exp(s - jnp.max(s, axis=-1, keepdims=True))
    den = jnp.sum(e, axis=-1, keepdims=True)
    o = jnp.dot(e.astype(BF16), v2, preferred_element_type=F32) / den
    o_ref[...] = jnp.where(lo, o[:t], o[t:]).astype(o_ref.dtype)


def _context_attention(q, k, v):
    b, t, _ = q.shape
    spec = pl.BlockSpec((None, t, LANES), lambda bi, hp: (bi, 0, hp))
    return pl.pallas_call(
        _ctx_attn_body,
        grid=(b, NA_WIDTH // LANES),
        in_specs=[spec] * 3,
        out_specs=spec,
        out_shape=jax.ShapeDtypeStruct((b, t, NA_WIDTH), BF16),
        compiler_params=_params("parallel", "parallel"),
        name="ctx_attn",
    )(q, k, v)


def _na_bias_body(rpb_ref, o_ref):
    h = pl.program_id(0)
    v = pl.program_id(1)
    shape = (GRID_W, LANES)
    w = lax.broadcasted_iota(I32, shape, 0)
    lane = lax.broadcasted_iota(I32, shape, 1)
    upper = lane >= GRID_W
    kc = jnp.where(upper, lane - GRID_W, lane)
    cdiff = kc - w + (NA_KW - 1)
    cstart = jnp.clip(w - NA_KW // 2, 0, GRID_W - NA_KW)
    inwin = (kc >= cstart) & (kc < cstart + NA_KW)
    n_ro = 2 * NA_KH - 1
    n_co = 2 * NA_KW - 1
    for j in range(NA_KH // 2):
        ro = 2 * j - v + (NA_KH - 1)
        base0 = (h * n_ro + ro) * n_co
        acc = jnp.zeros(shape, F32)
        for c in range(n_co):
            val = jnp.where(upper, rpb_ref[base0 + n_co + c], rpb_ref[base0 + c])
            acc = jnp.where(cdiff == c, val, acc)
        o_ref[:, j * LANES:(j + 1) * LANES] = jnp.where(inwin, acc, NEG_BIG)


def _na_bias(rpb):
    n = NA_KH * GRID_W
    return pl.pallas_call(
        _na_bias_body,
        grid=(NA_HEADS, NA_KH),
        in_specs=[pl.BlockSpec(memory_space=pltpu.SMEM)],
        out_specs=pl.BlockSpec((None, None, GRID_W, n), lambda h, v: (h, v, 0, 0)),
        out_shape=jax.ShapeDtypeStruct((NA_HEADS, NA_KH, GRID_W, n), F32),
        compiler_params=_params("parallel", "parallel"),
        name="na_bias",
    )(rpb.reshape(-1))


def _na_body(q_ref, k_ref, v_ref, ck_ref, cv_ref, bias_ref, o_ref, *, rows):
    ck = ck_ref[...].astype(BF16)
    cv = cv_ref[...].astype(BF16)
    lo, hi = _pair_masks((GRID_W, LANES))
    scale = NA_DIM ** -0.5
    n_loc = NA_KH * GRID_W

    def one_row(r, carry):
        r_start = jnp.clip(r - NA_KH // 2, 0, rows - NA_KH)
        vidx = r - r_start
        q2 = q_ref[pl.ds(pl.multiple_of(r * GRID_W, GRID_W), GRID_W), :]
        ksl = pl.ds(pl.multiple_of(r_start * GRID_W, GRID_W), n_loc)
        kl = k_ref[ksl, :]
        vl = v_ref[ksl, :]
        zero = jnp.zeros_like(q2)
        qs = jnp.concatenate([jnp.where(lo, q2, zero), jnp.where(hi, q2, zero)], axis=0)
        bias = jnp.concatenate([bias_ref[0, vidx], bias_ref[1, vidx]], axis=0)
        s_loc = _dot_nt(qs, kl) * scale + bias
        s_ctx = _dot_nt(qs, ck) * scale
        m = jnp.maximum(jnp.max(s_loc, axis=-1, keepdims=True), jnp.max(s_ctx, axis=-1, keepdims=True))
        e_loc = jnp.exp(s_loc - m)
        e_ctx = jnp.exp(s_ctx - m)
        den = jnp.sum(e_loc, axis=-1, keepdims=True) + jnp.sum(e_ctx, axis=-1, keepdims=True)
        o = (jnp.dot(e_loc.astype(BF16), vl, preferred_element_type=F32)
             + jnp.dot(e_ctx.astype(BF16), cv, preferred_element_type=F32)) / den
        o_ref[pl.ds(pl.multiple_of(r * GRID_W, GRID_W), GRID_W), :] = (
            jnp.where(lo, o[:GRID_W], o[GRID_W:]).astype(o_ref.dtype))
        return carry

    lax.fori_loop(0, rows, one_row, 0, unroll=NA_ROW_UNROLL)


def _neighbourhood_attention(q, k, v, ck, cv, bias):
    b, t, _ = q.shape
    lc = ck.shape[1]
    rows = t // GRID_W
    tok = pl.BlockSpec((None, t, LANES), lambda hp, bi: (bi, 0, hp))
    ctx = pl.BlockSpec((None, lc, LANES), lambda hp, bi: (bi, 0, hp))
    return pl.pallas_call(
        functools.partial(_na_body, rows=rows),
        grid=(NA_WIDTH // LANES, b),
        in_specs=[tok, tok, tok, ctx, ctx,
                  pl.BlockSpec((2, NA_KH, GRID_W, NA_KH * GRID_W), lambda hp, bi: (hp, 0, 0, 0))],
        out_specs=tok,
        out_shape=jax.ShapeDtypeStruct((b, t, NA_WIDTH), BF16),
        compiler_params=_params("parallel", "parallel"),
        name="na_attn",
    )(q, k, v, ck, cv, bias)


def _ret_body(q_ref, k_ref, v_ref, g_ref, df_ref, db_ref, beta_ref, s0f_ref, s0b_ref,
              o_ref, sf_ref, sb_ref, kvf_s, kvb_s, *, n_chunks):
    c_len = CHUNK
    lgf = -jnp.log1p(jnp.exp(-df_ref[...]))
    lgb = -jnp.log1p(jnp.exp(-db_ref[...]))
    shape = (c_len, c_len)
    i = lax.broadcasted_iota(I32, shape, 0).astype(F32)
    j = lax.broadcasted_iota(I32, shape, 1).astype(F32)
    dij = i - j
    d_comb = (jnp.where(dij >= 0, jnp.exp(jnp.maximum(dij, 0.0) * lgf), 0.0)
              + jnp.where(dij <= 0, jnp.exp(jnp.maximum(-dij, 0.0) * lgb), 0.0))
    xi_f = jnp.exp((i + 1.0) * lgf)
    xi_b = jnp.exp((c_len - i) * lgb)
    zeta_f = jnp.exp((c_len - 1.0 - i) * lgf)
    zeta_b = jnp.exp(i * lgb)
    g_f = jnp.exp(c_len * lgf)
    g_b = jnp.exp(c_len * lgb)

    def chunk(c):
        return pl.ds(pl.multiple_of(c * c_len, c_len), c_len)

    def kv_pass(c, carry):
        kc = k_ref[chunk(c), :]
        vc = v_ref[chunk(c), :].astype(F32)
        kvf_s[c] = _dot_tn(kc, (vc * zeta_f).astype(BF16))
        kvb_s[c] = _dot_tn(kc, (vc * zeta_b).astype(BF16))
        return carry

    lax.fori_loop(0, n_chunks, kv_pass, 0, unroll=RET_CHUNK_UNROLL)

    def scan_f(c, s):
        kv = kvf_s[c]
        kvf_s[c] = s
        return g_f * s + kv

    def scan_b(ci, s):
        c = n_chunks - 1 - ci
        kv = kvb_s[c]
        kvb_s[c] = s
        return g_b * s + kv

    sf_ref[...] = lax.fori_loop(0, n_chunks, scan_f, s0f_ref[...])
    sb_ref[...] = lax.fori_loop(0, n_chunks, scan_b, s0b_ref[...])

    beta = beta_ref[...]

    def out_pass(c, carry):
        qc = q_ref[chunk(c), :]
        kc = k_ref[chunk(c), :]
        vc = v_ref[chunk(c), :]
        scores = _dot_nt(qc, kc) * d_comb
        y = jnp.dot(scores.astype(BF16), vc, preferred_element_type=F32)
        qf = qc.astype(F32)
        y += jnp.dot((qf * xi_f).astype(BF16), kvf_s[c].astype(BF16), preferred_element_type=F32)
        y += jnp.dot((qf * xi_b).astype(BF16), kvb_s[c].astype(BF16), preferred_element_type=F32)
        o_ref[chunk(c), :] = (_standardize(y) * beta * _silu(g_ref[chunk(c), :])).astype(o_ref.dtype)
        return carry

    lax.fori_loop(0, n_chunks, out_pass, 0, unroll=RET_CHUNK_UNROLL)


def _retention(q, k, v, g, decay_f, decay_b, beta_ret, s0f, s0b):
    b, t, _ = q.shape
    n_chunks = t // CHUNK
    tok = pl.BlockSpec((None, t, LANES), lambda bi, h: (bi, 0, h))
    per_head = pl.BlockSpec((None, 1, LANES), lambda bi, h: (h, 0, 0))
    state = pl.BlockSpec((None, None, RET_DK, RET_DK), lambda bi, h: (bi, h, 0, 0))
    lanes = lambda a: jnp.broadcast_to(a.astype(F32)[:, None, None], (RET_HEADS, 1, LANES))
    st_shape = jax.ShapeDtypeStruct((b, RET_HEADS, RET_DK, RET_DK), F32)
    return pl.pallas_call(
        functools.partial(_ret_body, n_chunks=n_chunks),
        grid=(b, RET_HEADS),
        in_specs=[tok, tok, tok, tok, per_head, per_head, per_head, state, state],
        out_specs=[tok, state, state],
        out_shape=[jax.ShapeDtypeStruct((b, t, RET_WIDTH), BF16), st_shape, st_shape],
        scratch_shapes=[pltpu.VMEM((n_chunks, RET_DK, RET_DK), F32)] * 2,
        compiler_params=_params("parallel", "parallel"),
        name="retention",
    )(q, k, v, g, lanes(decay_f), lanes(decay_b), beta_ret.reshape(RET_HEADS, 1, LANES), s0f, s0b)


def _merge_body(x_ref, ona_ref, ret_ref, wna_ref, wret_ref, bna_ref, gate_ref, lg_ref, lb_ref, o_ref):
    o = ona_ref[...].astype(F32)
    na = o * lax.rsqrt(jnp.mean(o * o, axis=-1, keepdims=True) + LN_EPS) * bna_ref[...]
    mix = (jnp.dot(na.astype(BF16), wna_ref[...], preferred_element_type=F32)
           + jnp.dot(ret_ref[...], wret_ref[...], preferred_element_type=F32))
    y = DEEPNORM_ALPHA * x_ref[...] + gate_ref[...] * mix
    o_ref[...] = _standardize(y) * lg_ref[...] + lb_ref[...]


def _merge(x, o_na, ret, w_out, beta_na, gate, mod_row, ln_g, ln_b, tm):
    b, t, _ = x.shape
    tok = lambda bi, ti: (bi, ti, 0)
    row = pl.BlockSpec((1, D_MODEL), lambda bi, ti: (0, 0))
    half = pl.BlockSpec((None, tm, 512), tok)
    return pl.pallas_call(
        _merge_body,
        grid=(b, t // tm),
        in_specs=[pl.BlockSpec((None, tm, D_MODEL), tok), half, half,
                  pl.BlockSpec((512, D_MODEL), lambda bi, ti: (0, 0)),
                  pl.BlockSpec((512, D_MODEL), lambda bi, ti: (1, 0)),
                  pl.BlockSpec((1, 512), lambda bi, ti: (0, 0)),
                  pl.BlockSpec((None, 1, D_MODEL), lambda bi, ti: (mod_row(bi), 0, 0)),
                  row, row],
        out_specs=pl.BlockSpec((None, tm, D_MODEL), tok),
        out_shape=jax.ShapeDtypeStruct(x.shape, F32),
        compiler_params=_params("parallel", "parallel"),
        name="merge_ln1",
    )(x, o_na, ret, w_out, w_out, beta_na.reshape(1, 512), gate, ln_g.reshape(1, -1), ln_b.reshape(1, -1))


def _two_group_specs(n_p_blocks, blocks_per_batch_s, ctx_row):
    tok_p = lambda i: (jnp.minimum(i, n_p_blocks - 1), 0)
    tok_s = lambda i: (jnp.maximum(i - n_p_blocks, 0), 0)
    mod = lambda i: (jnp.where(i < n_p_blocks, ctx_row, jnp.maximum(i - n_p_blocks, 0) // blocks_per_batch_s), 0, 0)
    return tok_p, tok_s, mod


def _router_body(yp_ref, ys_ref, sh_ref, sc_ref, wh_ref, wl_ref, br_ref, p_ref, gt_ref, tab_ref, cnt_ref,
                 carry_s, *, n_p_blocks):
    i = pl.program_id(0)
    tm = yp_ref.shape[0]

    @pl.when(i == 0)
    def _():
        carry_s[...] = jnp.zeros_like(carry_s)

    y = jnp.where(i < n_p_blocks, yp_ref[...], ys_ref[...])
    h = y * (1.0 + sc_ref[...]) + sh_ref[...]
    h_hi = h.astype(BF16)
    h_lo = (h - h_hi.astype(F32)).astype(BF16)
    w_hi = wh_ref[...]
    lg = _dot_nt(jnp.concatenate([w_hi, wl_ref[...]], axis=0), h_hi)
    work = lg[:N_EXPERTS] + lg[N_EXPERTS:] + _dot_nt(w_hi, h_lo) + br_ref[...]
    eidx = lax.broadcasted_iota(I32, (N_EXPERTS, tm), 0).astype(F32)
    vals, idxs, hots = [], [], []
    for _ in range(TOP_K):
        mx = jnp.max(work, axis=0, keepdims=True)
        idx = jnp.min(jnp.where(work == mx, eidx, float(N_EXPERTS)), axis=0, keepdims=True)
        hot = eidx == idx
        vals.append(mx)
        idxs.append(idx)
        hots.append(hot)
        work = jnp.where(hot, -jnp.inf, work)
    exps = [jnp.exp(v - vals[0]) for v in vals]
    den = exps[0] + exps[1] + exps[2] + exps[3]
    sel = jnp.zeros((N_EXPERTS, tm), F32)
    for hot in hots:
        sel = sel + hot.astype(F32)
    r_i = lax.broadcasted_iota(I32, (tm, tm), 0)
    c_i = lax.broadcasted_iota(I32, (tm, tm), 1)
    earlier = (r_i < c_i).astype(BF16)
    before = jnp.dot(sel.astype(BF16), earlier, preferred_element_type=F32)
    n_e = jnp.sum(sel, axis=1, keepdims=True)
    n8 = jnp.floor((n_e + (RUN_ALIGN - 1.0)) * (1.0 / RUN_ALIGN)) * RUN_ALIGN + jnp.zeros((1, LANES), F32)
    e_r = lax.broadcasted_iota(I32, (N_EXPERTS, N_EXPERTS), 0)
    e_c = lax.broadcasted_iota(I32, (N_EXPERTS, N_EXPERTS), 1)
    off = jnp.dot((e_c < e_r).astype(BF16), n8.astype(BF16), preferred_element_type=F32)
    local = before + jnp.concatenate([off] * (tm // LANES), axis=1)
    rows = [jnp.sum(jnp.where(hot, local, 0.0), axis=0, keepdims=True) for hot in hots]
    p_ref[...] = jnp.concatenate(rows, axis=0).astype(I32)
    gt_ref[...] = jnp.concatenate([e / den for e in exps], axis=0)
    carry = carry_s[...]
    lane = lax.broadcasted_iota(I32, (N_EXPERTS, LANES), 1)
    tab_ref[...] = jnp.where(lane == 0, n8, jnp.where(lane == 1, carry, jnp.where(lane == 2, off, 0.0)))
    carry = carry + n8
    carry_s[...] = carry
    cnt_ref[...] = carry


def _router(yp, ys, shift, scale, ctx_row, s_blocks_per_batch, w_router, b_router, tm):
    n_p, n_s = yp.shape[0], ys.shape[0]
    n = n_p + n_s
    npb = n_p // tm
    tok_p, tok_s, mod = _two_group_specs(npb, s_blocks_per_batch, ctx_row)
    w_t = w_router.T
    w_hi = w_t.astype(BF16)
    w_lo = (w_t - w_hi.astype(F32)).astype(BF16)
    br = jnp.broadcast_to(b_router[:, None], (N_EXPERTS, tm))
    o4 = pl.BlockSpec((TOP_K, tm), lambda i: (0, i))
    whole = lambda shape: pl.BlockSpec(shape, lambda i: (0, 0))
    return pl.pallas_call(
        functools.partial(_router_body, n_p_blocks=npb),
        grid=(n // tm,),
        in_specs=[pl.BlockSpec((tm, D_MODEL), tok_p), pl.BlockSpec((tm, D_MODEL), tok_s),
                  pl.BlockSpec((None, 1, D_MODEL), mod), pl.BlockSpec((None, 1, D_MODEL), mod),
                  whole((N_EXPERTS, D_MODEL)), whole((N_EXPERTS, D_MODEL)), whole((N_EXPERTS, tm))],
        out_specs=[o4, o4, pl.BlockSpec((None, N_EXPERTS, LANES), lambda i: (i, 0, 0)),
                   whole((N_EXPERTS, LANES))],
        out_shape=[jax.ShapeDtypeStruct((TOP_K, n), I32), jax.ShapeDtypeStruct((TOP_K, n), F32),
                   jax.ShapeDtypeStruct((n // tm, N_EXPERTS, LANES), F32),
                   jax.ShapeDtypeStruct((N_EXPERTS, LANES), F32)],
        scratch_shapes=[pltpu.VMEM((N_EXPERTS, LANES), F32)],
        compiler_params=_params("arbitrary"),
        name="moe_router",
    )(yp, ys, shift, scale, w_hi, w_lo, br)


def _for_run_pieces(n_rows, visit):
    off = 0
    for size in RUN_PIECES:
        present = (n_rows & size) != 0
        visit(off, size, present)
        off = off + jnp.where(present, size, 0)


def _block_runs(blk, n_ref, loc_ref, glob_ref, local_buf, sorted_hbm, sem, to_hbm, wait):
    def per_expert(e, carry):
        idx = blk * N_EXPERTS + e
        loc0 = loc_ref[idx]
        glob0 = glob_ref[idx]

        def visit(off, size, present):
            loc = local_buf.at[pl.ds(pl.multiple_of(loc0 + off, RUN_ALIGN), size)]
            glob = sorted_hbm.at[pl.ds(pl.multiple_of(glob0 + off, RUN_ALIGN), size)]
            copy = pltpu.make_async_copy(loc, glob, sem) if to_hbm else pltpu.make_async_copy(glob, loc, sem)

            @pl.when(present)
            def _():
                if wait:
                    copy.wait()
                else:
                    copy.start()

        _for_run_pieces(n_ref[idx], visit)
        return carry

    lax.fori_loop(0, N_EXPERTS, per_expert, 0)


def _dispatch_body(n_ref, loc_ref, glob_ref, zlo_ref, zn_ref, p_ref, yp_ref, ys_ref, sh_ref, sc_ref, xs_ref,
                   xbuf, zbuf, sems, zsem, *, n_p_blocks, n_blocks):
    i = pl.program_id(0)
    slot = i % 2
    y = jnp.where(i < n_p_blocks, yp_ref[...], ys_ref[...])
    h = (y * (1.0 + sc_ref[...]) + sh_ref[...]).astype(BF16)
    rows = p_ref[...]
    r_iota = lax.broadcasted_iota(I32, (LOCAL_ROWS, rows.shape[1]), 0)
    place = r_iota == rows[0:1]
    for kk in range(1, TOP_K):
        place = place | (r_iota == rows[kk:kk + 1])
    xbuf[slot] = jnp.dot(jnp.where(place, 1.0, 0.0).astype(BF16), h, preferred_element_type=F32)

    runs = functools.partial(_block_runs, n_ref=n_ref, loc_ref=loc_ref, glob_ref=glob_ref,
                             sorted_hbm=xs_ref, to_hbm=True)
    runs(i, local_buf=xbuf.at[slot], sem=sems.at[slot], wait=False)

    @pl.when(i > 0)
    def _():
        runs(i - 1, local_buf=xbuf.at[1 - slot], sem=sems.at[1 - slot], wait=True)

    @pl.when(i == n_blocks - 1)
    def _():
        runs(i, local_buf=xbuf.at[slot], sem=sems.at[slot], wait=True)
        zbuf[...] = jnp.zeros_like(zbuf)
        sem = zsem

        def per_expert(e, carry):
            lo = zlo_ref[e]

            def visit(off, size, present):
                copy = pltpu.make_async_copy(
                    zbuf.at[pl.ds(0, size)], xs_ref.at[pl.ds(pl.multiple_of(lo + off, RUN_ALIGN), size)], sem)

                @pl.when(present)
                def _():
                    copy.start()
                    copy.wait()

            _for_run_pieces(zn_ref[e], visit)
            return carry

        lax.fori_loop(0, N_EXPERTS, per_expert, 0)

        tail_lo = zlo_ref[N_EXPERTS]

        def tile_copy(t):
            start = pl.multiple_of(tail_lo + t * EXPERT_TILE, EXPERT_TILE)
            return pltpu.make_async_copy(zbuf, xs_ref.at[pl.ds(start, EXPERT_TILE)], sem)

        def t_issue(t, c2):
            tile_copy(t).start()
            return c2

        def t_drain(t, c2):
            tile_copy(0).wait()
            return c2

        n_tail = zn_ref[N_EXPERTS] // EXPERT_TILE
        lax.fori_loop(0, n_tail, t_issue, 0)
        lax.fori_loop(0, n_tail, t_drain, 0)


def _dispatch(yp, ys, shift, scale, ctx_row, s_blocks_per_batch, local_rows, run_n, run_local, run_global,
              zero_lo, zero_n, cap_rows, tm):
    n_p, n_s = yp.shape[0], ys.shape[0]
    n = n_p + n_s
    npb = n_p // tm
    nb = n // tm
    tok_p, tok_s, mod = _two_group_specs(npb, s_blocks_per_batch, ctx_row)
    smem_all = pl.BlockSpec(memory_space=pltpu.SMEM)
    return pl.pallas_call(
        functools.partial(_dispatch_body, n_p_blocks=npb, n_blocks=nb),
        grid=(nb,),
        in_specs=[smem_all] * 5 + [
            pl.BlockSpec((TOP_K, tm), lambda i: (0, i)),
            pl.BlockSpec((tm, D_MODEL), tok_p), pl.BlockSpec((tm, D_MODEL), tok_s),
            pl.BlockSpec((None, 1, D_MODEL), mod), pl.BlockSpec((None, 1, D_MODEL), mod)],
        out_specs=pl.BlockSpec(memory_space=pl.ANY),
        out_shape=jax.ShapeDtypeStruct((cap_rows, D_MODEL), F32),
        scratch_shapes=[pltpu.VMEM((2, LOCAL_ROWS, D_MODEL), F32),
                        pltpu.VMEM((EXPERT_TILE, D_MODEL), F32),
                        pltpu.SemaphoreType.DMA((2,)), pltpu.SemaphoreType.DMA(())],
        compiler_params=_params("arbitrary"),
        name="moe_dispatch",
    )(run_n, run_local, run_global, zero_lo, zero_n, local_rows, yp, ys, shift, scale)


def _expert_body(te_ref, nu_ref, x_ref, wgu_ref, bgu_ref, wd_ref, bd_ref, y_ref, act_s):
    j = pl.program_id(0)

    @pl.when(j < nu_ref[0])
    def _():
        x = x_ref[...].astype(BF16)
        for c in range(D_FF // LANES):
            sl = slice(c * 2 * LANES, (c + 1) * 2 * LANES)
            hu = jnp.dot(x, wgu_ref[:, sl], preferred_element_type=F32) + bgu_ref[:, sl]
            x_glu = jnp.minimum(hu[:, :LANES], SWIGLU_LIMIT)
            x_lin = jnp.clip(hu[:, LANES:], -SWIGLU_LIMIT, SWIGLU_LIMIT)
            act = x_glu * (1.0 / (1.0 + jnp.exp(-SWIGLU_ALPHA * x_glu))) * (x_lin + 1.0)
            act_s[:, c * LANES:(c + 1) * LANES] = act.astype(BF16)
        y_ref[...] = jnp.dot(act_s[...], wd_ref[...], preferred_element_type=F32) + bd_ref[...]

    @pl.when(j >= nu_ref[0])
    def _():
        y_ref[...] = jnp.zeros_like(y_ref)


def _experts(tile_expert, n_used, xs, wgu, bgu, wd, bd):
    cap_rows = xs.shape[0]
    n_tiles = cap_rows // EXPERT_TILE
    grid_spec = pltpu.PrefetchScalarGridSpec(
        num_scalar_prefetch=2,
        grid=(n_tiles,),
        in_specs=[pl.BlockSpec((EXPERT_TILE, D_MODEL), lambda j, te, nu: (jnp.minimum(j, nu[0] - 1), 0)),
                  pl.BlockSpec((None, D_MODEL, 2 * D_FF), lambda j, te, nu: (te[j], 0, 0)),
                  pl.BlockSpec((None, 1, 2 * D_FF), lambda j, te, nu: (te[j], 0, 0)),
                  pl.BlockSpec((None, D_FF, D_MODEL), lambda j, te, nu: (te[j], 0, 0)),
                  pl.BlockSpec((None, 1, D_MODEL), lambda j, te, nu: (te[j], 0, 0))],
        out_specs=pl.BlockSpec((EXPERT_TILE, D_MODEL), lambda j, te, nu: (j, 0)),
        scratch_shapes=[pltpu.VMEM((EXPERT_TILE, D_FF), BF16)],
    )
    return pl.pallas_call(
        _expert_body,
        grid_spec=grid_spec,
        out_shape=jax.ShapeDtypeStruct((cap_rows, D_MODEL), F32),
        compiler_params=_params("arbitrary"),
        name="moe_experts",
    )(tile_expert, n_used, xs, wgu, bgu, wd, bd)


def _combine_body(n_ref, loc_ref, glob_ref, y1_ref, p_ref, gt_ref, gate_ref, lg_ref, lb_ref, ys_ref, o_ref,
                  ybuf, sems, *, block0, n_blocks):
    i = pl.program_id(0)
    slot = i % 2
    tm = y1_ref.shape[0]
    runs = functools.partial(_block_runs, n_ref=n_ref, loc_ref=loc_ref, glob_ref=glob_ref,
                             sorted_hbm=ys_ref, to_hbm=False)

    @pl.when(i == 0)
    def _():
        ybuf[...] = jnp.zeros_like(ybuf)
        runs(block0, local_buf=ybuf.at[0], sem=sems.at[0], wait=False)

    @pl.when(i + 1 < n_blocks)
    def _():
        runs(block0 + i + 1, local_buf=ybuf.at[1 - slot], sem=sems.at[1 - slot], wait=False)

    runs(block0 + i, local_buf=ybuf.at[slot], sem=sems.at[slot], wait=True)

    rows = p_ref[...]
    gates = gt_ref[...]
    c_iota = lax.broadcasted_iota(I32, (tm, LOCAL_ROWS), 1)
    weight = jnp.where(c_iota == rows[:, 0:1], gates[:, 0:1], 0.0)
    for kk in range(1, TOP_K):
        weight = weight + jnp.where(c_iota == rows[:, kk:kk + 1], gates[:, kk:kk + 1], 0.0)
    w_hi = weight.astype(BF16)
    w_lo = (weight - w_hi.astype(F32)).astype(BF16)
    yb = ybuf[slot].astype(BF16)
    f = (jnp.dot(w_hi, yb, preferred_element_type=F32) + jnp.dot(w_lo, yb, preferred_element_type=F32))
    y = DEEPNORM_ALPHA * y1_ref[...] + gate_ref[...] * f
    o_ref[...] = _standardize(y) * lg_ref[...] + lb_ref[...]


def _combine(y1, local_rows, gates, run_n, run_local, run_global, block0, gate_mod, mod_row, blocks_per_batch,
             y_sorted, ln_g, ln_b, tm):
    n_g = y1.shape[0]
    nb = n_g // tm
    row = pl.BlockSpec((1, D_MODEL), lambda i: (0, 0))
    smem_all = pl.BlockSpec(memory_space=pltpu.SMEM)
    pair = pl.BlockSpec((tm, TOP_K), lambda i: (i + block0, 0))
    return pl.pallas_call(
        functools.partial(_combine_body, block0=block0, n_blocks=nb),
        grid=(nb,),
        in_specs=[smem_all] * 3 + [
            pl.BlockSpec((tm, D_MODEL), lambda i: (i, 0)), pair, pair,
            pl.BlockSpec((None, 1, D_MODEL), lambda i: (mod_row(i // blocks_per_batch), 0, 0)),
            row, row,
            pl.BlockSpec(memory_space=pl.ANY)],
        out_specs=pl.BlockSpec((tm, D_MODEL), lambda i: (i, 0)),
        out_shape=jax.ShapeDtypeStruct((n_g, D_MODEL), F32),
        scratch_shapes=[pltpu.VMEM((2, LOCAL_ROWS, D_MODEL), F32), pltpu.SemaphoreType.DMA((2,))],
        compiler_params=_params("arbitrary"),
        name="moe_combine_ln2",
    )(run_n, run_local, run_global, y1, local_rows, gates, gate_mod,
      ln_g.reshape(1, -1), ln_b.reshape(1, -1), y_sorted)


def _regroup_body(w_ref, o_ref):
    blk = 2 * LANES
    src = lax.broadcasted_iota(I32, (blk, blk), 0)
    dst = lax.broadcasted_iota(I32, (blk, blk), 1)
    perm = (src == jnp.where(dst < LANES, 2 * dst, 2 * (dst - LANES) + 1)).astype(BF16)
    for s in range(w_ref.shape[1] // blk):
        sl = slice(s * blk, (s + 1) * blk)
        o_ref[:, sl] = jnp.dot(w_ref[:, sl].astype(BF16), perm, preferred_element_type=F32).astype(BF16)


def _regroup_gate_up(w_gate_up):
    e, d, n = w_gate_up.shape
    tn = 512
    spec = pl.BlockSpec((None, d, tn), lambda ei, j: (ei, 0, j))
    return pl.pallas_call(
        _regroup_body,
        grid=(e, n // tn),
        in_specs=[spec],
        out_specs=spec,
        out_shape=jax.ShapeDtypeStruct((e, d, n), BF16),
        compiler_params=_params("parallel", "parallel"),
        name="regroup_gate_up",
    )(w_gate_up)


def _expert_weights(w_gate_up, b_gate_up, w_down):
    e = w_gate_up.shape[0]
    nblk = D_FF // LANES
    bgu = b_gate_up.reshape(e, 1, nblk, LANES, 2).transpose(0, 1, 2, 4, 3).reshape(e, 1, 2 * D_FF)
    return _regroup_gate_up(w_gate_up), bgu, w_down.astype(BF16)


def _routing_plan(table, totals, n_pairs):
    nb = table.shape[0]
    run_n = table[:, :, 0].astype(I32)
    rows_before = table[:, :, 1].astype(I32)
    run_local = table[:, :, 2].astype(I32)
    cnt = totals[:, 0].astype(I32)
    padded = (cnt + EXPERT_TILE - 1) // EXPERT_TILE * EXPERT_TILE
    pad_end = jnp.cumsum(padded)
    pad_start = pad_end - padded
    run_global = pad_start[None, :] + rows_before
    rows_max = n_pairs + nb * N_EXPERTS * (RUN_ALIGN - 1) + N_EXPERTS * (EXPERT_TILE - 1)
    n_tiles = -(-rows_max // EXPERT_TILE)
    n_used = pad_end[-1] // EXPERT_TILE
    tile_row = jnp.minimum(jnp.arange(n_tiles, dtype=I32), n_used - 1) * EXPERT_TILE
    tile_expert = jnp.sum((pad_end[None, :] <= tile_row[:, None]).astype(I32), axis=1)
    tile_expert = jnp.minimum(tile_expert, N_EXPERTS - 1)
    cap_rows = n_tiles * EXPERT_TILE
    zero_lo = jnp.concatenate([pad_start + cnt, pad_end[-1:]])
    zero_n = jnp.concatenate([padded - cnt, cap_rows - pad_end[-1:]])
    runs = (run_n.reshape(-1), run_local.reshape(-1), run_global.reshape(-1))
    return runs, tile_expert, n_used.reshape(1), zero_lo, zero_n, cap_rows


def kernel(x_prompt, x_sample, cache_na_k, cache_na_v, state_ret_fwd, state_ret_bwd, c, c_ctx, w_ada, b_ada, w_in, rpb, ret_decay_fwd, ret_decay_bwd, beta_na, beta_ret, w_out, ln1_g, ln1_b, w_router, b_router, w_gate_up, b_gate_up, w_down, b_down, ln2_g, ln2_b):
    bp, tp, _ = x_prompt.shape
    bs, ts, _ = x_sample.shape
    assert w_ada.shape[0] == DEPTH == 1
    l = 0
    ctx_row = bs
    sample_row = lambda bi: bi
    prompt_row = lambda bi: ctx_row

    cond = jnp.concatenate([c, c_ctx[None, :], jnp.zeros((16 - bs - 1, D_MODEL), F32)], axis=0)
    m = _adaln(cond, w_ada[l], b_ada[l]).reshape(16, 6, 1, D_MODEL)
    sh_a, sc_a, gt_a, sh_f, sc_f, gt_f = (m[:, i] for i in range(6))

    w_in_b = w_in[l].astype(BF16)
    w_out_b = w_out[l].astype(BF16)

    q, k_p, v_p, qr, kr, vr, g = _project(x_prompt, sh_a, sc_a, prompt_row, w_in_b, tp, F32, rope=False)
    o_na = _context_attention(q, k_p, v_p)
    zeros_state = jnp.zeros((bp, RET_HEADS, RET_DK, RET_DK), F32)
    ret, s_f, s_b = _retention(qr, kr, vr, g, ret_decay_fwd[l], ret_decay_bwd[l], beta_ret[l],
                               zeros_state, zeros_state)
    yp1 = _merge(x_prompt, o_na, ret, w_out_b, beta_na[l], gt_a, prompt_row, ln1_g[l], ln1_b[l], tp)

    q, k_s, v_s, qr, kr, vr, g = _project(x_sample, sh_a, sc_a, sample_row, w_in_b, 512, BF16, rope=True)
    lc = cache_na_k.shape[2]
    o_na = _neighbourhood_attention(q, k_s, v_s, cache_na_k[:, l].reshape(bs, lc, NA_WIDTH),
                                    cache_na_v[:, l].reshape(bs, lc, NA_WIDTH), _na_bias(rpb[l]))
    ret, _, _ = _retention(qr, kr, vr, g, ret_decay_fwd[l], ret_decay_bwd[l], beta_ret[l],
                           state_ret_fwd[:, l], state_ret_bwd[:, l])
    ys1 = _merge(x_sample, o_na, ret, w_out_b, beta_na[l], gt_a, sample_row, ln1_g[l], ln1_b[l], 512)

    tm = TOKEN_BLOCK
    yp1f = yp1.reshape(bp * tp, D_MODEL)
    ys1f = ys1.reshape(bs * ts, D_MODEL)
    local_rows, gates, table, totals = _router(yp1f, ys1f, sh_f, sc_f, ctx_row, ts // tm,
                                               w_router[l], b_router[l], tm)
    runs, tile_expert, n_used, zero_lo, zero_n, cap_rows = _routing_plan(
        table, totals, (bp * tp + bs * ts) * TOP_K)
    xs = _dispatch(yp1f, ys1f, sh_f, sc_f, ctx_row, ts // tm, local_rows, *runs, zero_lo, zero_n, cap_rows, tm)
    wgu, bgu, wd = _expert_weights(w_gate_up[l], b_gate_up[l], w_down[l])
    y_sorted = _experts(tile_expert, n_used, xs, wgu, bgu, wd, b_down[l][:, None, :])
    rows_t, gates_t = local_rows.T, gates.T
    yp = _combine(yp1f, rows_t, gates_t, *runs, 0, gt_f, prompt_row, 1, y_sorted, ln2_g[l], ln2_b[l], tm)
    ys = _combine(ys1f, rows_t, gates_t, *runs, (bp * tp) // tm, gt_f, sample_row, ts // tm, y_sorted,
                  ln2_g[l], ln2_b[l], tm)

    return (yp.reshape(bp, tp, D_MODEL), ys.reshape(bs, ts, D_MODEL),
            k_p.reshape(bp, 1, tp, NA_HEADS, NA_DIM), v_p.reshape(bp, 1, tp, NA_HEADS, NA_DIM),
            s_f[:, None], s_b[:, None])
```

```python
import functools

import jax
import jax.numpy as jnp
from jax import lax
from jax.experimental import pallas as pl
from jax.experimental.pallas import tpu as pltpu

F32 = jnp.float32
BF16 = jnp.bfloat16
I32 = jnp.int32

D_MODEL = 1024
GRID_W = 64
NA_HEADS = 8
NA_DIM = 64
NA_WIDTH = NA_HEADS * NA_DIM
NA_KH = 8
NA_KW = 16
RET_HEADS = 4
RET_DK = 128
RET_WIDTH = RET_HEADS * RET_DK
CHUNK = 128
N_EXPERTS = 32
TOP_K = 4
D_FF = 1024
SWIGLU_LIMIT = 7.0
SWIGLU_ALPHA = 1.702
ROPE_BASE = 10000.0
LN_EPS = 1e-5
DEPTH = 1
DEEPNORM_ALPHA = (2.0 * DEPTH) ** 0.25

LANES = 128
NEG_BIG = -1e30
TOKEN_BLOCK = 256
EXPERT_TILE = 512
VMEM_LIMIT = 56 * 1024 * 1024
RET_CHUNK_UNROLL = 4
SUBLANES = 8
RUN_ALIGN = SUBLANES
LOCAL_ROWS = TOKEN_BLOCK * TOP_K + N_EXPERTS * RUN_ALIGN
RUN_CHUNK = 32
RUN_PIECES = tuple(TOKEN_BLOCK >> s for s in range(TOKEN_BLOCK.bit_length())
                   if (TOKEN_BLOCK >> s) >= RUN_ALIGN)


def _params(*sem):
    return pltpu.CompilerParams(dimension_semantics=sem, vmem_limit_bytes=VMEM_LIMIT)


def _silu(x):
    return x / (1.0 + jnp.exp(-x))


def _standardize(x):
    mu = jnp.mean(x, axis=-1, keepdims=True)
    xc = x - mu
    var = jnp.mean(xc * xc, axis=-1, keepdims=True)
    return xc * lax.rsqrt(var + LN_EPS)


def _dot_nt(a, b):
    return lax.dot_general(a, b, (((1,), (1,)), ((), ())), preferred_element_type=F32)


def _dot_tn(a, b):
    return lax.dot_general(a, b, (((0,), (0,)), ((), ())), preferred_element_type=F32)


def _ada_body(c_ref, w_ref, b_ref, o_ref):
    s = _silu(c_ref[...])
    o_ref[...] = jnp.dot(s, w_ref[...], preferred_element_type=F32,
                         precision=lax.Precision.HIGHEST) + b_ref[...]


def _adaln(cond, w_ada, b_ada):
    r = cond.shape[0]
    n = w_ada.shape[1]
    tn = 1536
    return pl.pallas_call(
        _ada_body,
        grid=(n // tn,),
        in_specs=[pl.BlockSpec((r, D_MODEL), lambda j: (0, 0)),
                  pl.BlockSpec((D_MODEL, tn), lambda j: (0, j)),
                  pl.BlockSpec((1, tn), lambda j: (0, j))],
        out_specs=pl.BlockSpec((r, tn), lambda j: (0, j)),
        out_shape=jax.ShapeDtypeStruct((r, n), F32),
        compiler_params=_params("arbitrary"),
        name="adaln",
    )(cond, w_ada, b_ada.reshape(1, n))


def _proj_body(*refs, rope):
    if rope:
        (x_ref, sh_ref, sc_ref, w_ref, cos_ref, sa_ref, sb_ref,
         q_ref, k_ref, v_ref, qr_ref, kr_ref, vr_ref, g_ref) = refs
    else:
        (x_ref, sh_ref, sc_ref, w_ref,
         q_ref, k_ref, v_ref, qr_ref, kr_ref, vr_ref, g_ref) = refs
    h = (x_ref[...] * (1.0 + sc_ref[...]) + sh_ref[...]).astype(BF16)

    def cols(c):
        return jnp.dot(h, w_ref[:, c * 512:(c + 1) * 512], preferred_element_type=F32)

    q_ref[...] = cols(0).astype(q_ref.dtype)
    k_ref[...] = cols(1).astype(k_ref.dtype)
    v_ref[...] = cols(2).astype(v_ref.dtype)
    pq = cols(3)
    pk = cols(4) * (RET_DK ** -0.5)
    if rope:
        cs, sa, sb = cos_ref[...], sa_ref[...], sb_ref[...]
        for hd in range(RET_HEADS):
            sl = slice(hd * LANES, (hd + 1) * LANES)
            for p, o_ref in ((pq, qr_ref), (pk, kr_ref)):
                xs = p[:, sl]
                rot = xs * cs + pltpu.roll(xs, 96, 1) * sa + pltpu.roll(xs, 32, 1) * sb
                o_ref[:, sl] = rot.astype(o_ref.dtype)
    else:
        qr_ref[...] = pq.astype(qr_ref.dtype)
        kr_ref[...] = pk.astype(kr_ref.dtype)
    vr_ref[...] = cols(5).astype(vr_ref.dtype)
    g_ref[...] = cols(6)


def _rope_tables(t_len):
    t = jnp.arange(t_len, dtype=jnp.int32)
    pos_row = (t // GRID_W).astype(F32)[:, None]
    pos_col = (t % GRID_W).astype(F32)[:, None]
    lane = jnp.arange(LANES, dtype=jnp.int32)[None, :]
    n_freq = RET_DK // 4
    inv_freq = ROPE_BASE ** (-(lane % n_freq).astype(F32) / n_freq)
    ang = jnp.where(lane < RET_DK // 2, pos_row, pos_col) * inv_freq
    first = (lane % (2 * n_freq)) < n_freq
    cos = jnp.cos(ang)
    sin = jnp.sin(ang)
    return cos, jnp.where(first, -sin, 0.0), jnp.where(first, 0.0, sin)


def _project(x, shift, scale, mod_row, w_in, tm, kv_dtype, rope):
    b, t, _ = x.shape
    tok = lambda bi, ti: (bi, ti, 0)
    mod = lambda bi, ti: (mod_row(bi), 0, 0)
    in_specs = [pl.BlockSpec((None, tm, D_MODEL), tok),
                pl.BlockSpec((None, 1, D_MODEL), mod),
                pl.BlockSpec((None, 1, D_MODEL), mod),
                pl.BlockSpec(w_in.shape, lambda bi, ti: (0, 0))]
    args = [x, shift, scale, w_in]
    if rope:
        tab = lambda bi, ti: (ti, 0)
        in_specs += [pl.BlockSpec((tm, LANES), tab)] * 3
        args += list(_rope_tables(t))
    o512 = pl.BlockSpec((None, tm, 512), tok)
    shp = lambda dt: jax.ShapeDtypeStruct((b, t, 512), dt)
    return pl.pallas_call(
        functools.partial(_proj_body, rope=rope),
        grid=(b, t // tm),
        in_specs=in_specs,
        out_specs=[o512] * 7,
        out_shape=[shp(BF16), shp(kv_dtype), shp(kv_dtype), shp(BF16), shp(BF16), shp(BF16), shp(F32)],
        compiler_params=_params("parallel", "parallel"),
        name="in_proj",
    )(*args)


def _pair_masks(shape):
    lane = lax.broadcasted_iota(I32, shape, 1)
    return lane < NA_DIM, lane >= NA_DIM


def _ctx_attn_body(q_ref, k_ref, v_ref, o_ref):
    q2 = q_ref[...]
    k2 = k_ref[...].astype(BF16)
    v2 = v_ref[...].astype(BF16)
    t = q2.shape[0]
    lo, hi = _pair_masks(q2.shape)
    zero = jnp.zeros_like(q2)
    qs = jnp.concatenate([jnp.where(lo, q2, zero), jnp.where(hi, q2, zero)], axis=0)
    s = _dot_nt(qs, k2) * (NA_DIM ** -0.5)
    e = jnp.exp(s - jnp.max(s, axis=-1, keepdims=True))
    den = jnp.sum(e, axis=-1, keepdims=True)
    o = jnp.dot(e.astype(BF16), v2, preferred_element_type=F32) / den
    o_ref[...] = jnp.where(lo, o[:t], o[t:]).astype(o_ref.dtype)


def _context_attention(q, k, v):
    b, t, _ = q.shape
    spec = pl.BlockSpec((None, t, LANES), lambda bi, hp: (bi, 0, hp))
    return pl.pallas_call(
        _ctx_attn_body,
        grid=(b, NA_WIDTH // LANES),
        in_specs=[spec] * 3,
        out_specs=spec,
        out_shape=jax.ShapeDtypeStruct((b, t, NA_WIDTH), BF16),
        compiler_params=_params("parallel", "parallel"),
        name="ctx_attn",
    )(q, k, v)


def _na_bias_body(rpb_ref, o_ref):
    h = pl.program_id(0)
    v = pl.program_id(1)
    shape = (GRID_W, LANES)
    w = lax.broadcasted_iota(I32, shape, 0)
    lane = lax.broadcasted_iota(I32, shape, 1)
    upper = lane >= GRID_W
    kc = jnp.where(upper, lane - GRID_W, lane)
    cdiff = kc - w + (NA_KW - 1)
    cstart = jnp.clip(w - NA_KW // 2, 0, GRID_W - NA_KW)
    inwin = (kc >= cstart) & (kc < cstart + NA_KW)
    n_ro = 2 * NA_KH - 1
    n_co = 2 * NA_KW - 1
    for j in range(NA_KH // 2):
        ro = 2 * j - v + (NA_KH - 1)
        base0 = (h * n_ro + ro) * n_co
        acc = jnp.zeros(shape, F32)
        for c in range(n_co):
            val = jnp.where(upper, rpb_ref[base0 + n_co + c], rpb_ref[base0 + c])
            acc = jnp.where(cdiff == c, val, acc)
        o_ref[:, j * LANES:(j + 1) * LANES] = jnp.where(inwin, acc, NEG_BIG)


def _na_bias(rpb):
    n = NA_KH * GRID_W
    return pl.pallas_call(
        _na_bias_body,
        grid=(NA_HEADS, NA_KH),
        in_specs=[pl.BlockSpec(memory_space=pltpu.SMEM)],
        out_specs=pl.BlockSpec((None, None, GRID_W, n), lambda h, v: (h, v, 0, 0)),
        out_shape=jax.ShapeDtypeStruct((NA_HEADS, NA_KH, GRID_W, n), F32),
        compiler_params=_params("parallel", "parallel"),
        name="na_bias",
    )(rpb.reshape(-1))


def _na_body(q_ref, k_ref, v_ref, ck_ref, cv_ref, bias_ref, o_ref, s_buf, e_buf, d_buf, *, rows):
    ck = ck_ref[...].astype(BF16)
    cv = cv_ref[...].astype(BF16)
    lo, hi = _pair_masks((GRID_W, LANES))
    scale = NA_DIM ** -0.5
    n_loc = NA_KH * GRID_W

    def window(r):
        r = jnp.clip(r, 0, rows - 1)
        r_start = jnp.clip(r - NA_KH // 2, 0, rows - NA_KH)
        return (r - r_start, pl.ds(pl.multiple_of(r * GRID_W, GRID_W), GRID_W),
                pl.ds(pl.multiple_of(r_start * GRID_W, GRID_W), n_loc))

    def scores(r, slot):
        vidx, qsl, ksl = window(r)
        q2 = q_ref[qsl, :]
        zero = jnp.zeros_like(q2)
        q2 = q2 * scale
        qs = jnp.concatenate([jnp.where(lo, q2, zero), jnp.where(hi, q2, zero)], axis=0)
        bias = jnp.concatenate([bias_ref[0, vidx], bias_ref[1, vidx]], axis=0)
        s_buf[slot, :, :n_loc] = _dot_nt(qs, k_ref[ksl, :]) + bias
        s_buf[slot, :, n_loc:] = _dot_nt(qs, ck)

    def numerators(slot):
        s = s_buf[slot]
        e = jnp.exp(s - jnp.max(s, axis=-1, keepdims=True))
        d_buf[slot] = jnp.broadcast_to(jnp.sum(e, axis=-1, keepdims=True), d_buf.shape[1:])
        e_buf[slot] = e.astype(BF16)

    def output(r, slot):
        _, qsl, ksl = window(r)
        o = (jnp.dot(e_buf[slot, :, :n_loc], v_ref[ksl, :], preferred_element_type=F32)
             + jnp.dot(e_buf[slot, :, n_loc:], cv, preferred_element_type=F32)) / d_buf[slot]
        o_ref[qsl, :] = jnp.where(lo, o[:GRID_W], o[GRID_W:]).astype(o_ref.dtype)

    s_buf[...] = jnp.zeros_like(s_buf)
    e_buf[...] = jnp.zeros_like(e_buf)
    d_buf[...] = jnp.ones_like(d_buf)

    def two_steps(j, carry):
        r = 2 * j
        scores(r, 0)
        numerators(1)
        output(r - 2, 0)
        scores(r + 1, 1)
        numerators(0)
        output(r - 1, 1)
        return carry

    lax.fori_loop(0, rows // 2 + 1, two_steps, 0)


def _neighbourhood_attention(q, k, v, ck, cv, bias):
    b, t, _ = q.shape
    lc = ck.shape[1]
    rows = t // GRID_W
    tok = pl.BlockSpec((None, t, LANES), lambda hp, bi: (bi, 0, hp))
    ctx = pl.BlockSpec((None, lc, LANES), lambda hp, bi: (bi, 0, hp))
    return pl.pallas_call(
        functools.partial(_na_body, rows=rows),
        grid=(NA_WIDTH // LANES, b),
        in_specs=[tok, tok, tok, ctx, ctx,
                  pl.BlockSpec((2, NA_KH, GRID_W, NA_KH * GRID_W), lambda hp, bi: (hp, 0, 0, 0))],
        out_specs=tok,
        out_shape=jax.ShapeDtypeStruct((b, t, NA_WIDTH), BF16),
        scratch_shapes=[pltpu.VMEM((2, 2 * GRID_W, NA_KH * GRID_W + lc), F32),
                        pltpu.VMEM((2, 2 * GRID_W, NA_KH * GRID_W + lc), BF16),
                        pltpu.VMEM((2, 2 * GRID_W, LANES), F32)],
        compiler_params=_params("parallel", "parallel"),
        name="na_attn",
    )(q, k, v, ck, cv, bias)


def _ret_body(q_ref, k_ref, v_ref, g_ref, df_ref, db_ref, beta_ref, s0f_ref, s0b_ref,
              o_ref, sf_ref, sb_ref, kvf_s, kvb_s, *, n_chunks):
    c_len = CHUNK
    lgf = -jnp.log1p(jnp.exp(-df_ref[...]))
    lgb = -jnp.log1p(jnp.exp(-db_ref[...]))
    shape = (c_len, c_len)
    i = lax.broadcasted_iota(I32, shape, 0).astype(F32)
    j = lax.broadcasted_iota(I32, shape, 1).astype(F32)
    dij = i - j
    d_comb = (jnp.where(dij >= 0, jnp.exp(jnp.maximum(dij, 0.0) * lgf), 0.0)
              + jnp.where(dij <= 0, jnp.exp(jnp.maximum(-dij, 0.0) * lgb), 0.0))
    xi_f = jnp.exp((i + 1.0) * lgf)
    xi_b = jnp.exp((c_len - i) * lgb)
    zeta_f = jnp.exp((c_len - 1.0 - i) * lgf)
    zeta_b = jnp.exp(i * lgb)
    g_f = jnp.exp(c_len * lgf)
    g_b = jnp.exp(c_len * lgb)

    def chunk(c):
        return pl.ds(pl.multiple_of(c * c_len, c_len), c_len)

    def kv_pass(c, carry):
        kc = k_ref[chunk(c), :]
        vc = v_ref[chunk(c), :].astype(F32)
        kvf_s[c] = _dot_tn(kc, (vc * zeta_f).astype(BF16))
        kvb_s[c] = _dot_tn(kc, (vc * zeta_b).astype(BF16))
        return carry

    lax.fori_loop(0, n_chunks, kv_pass, 0, unroll=RET_CHUNK_UNROLL)

    def scan_f(c, s):
        kv = kvf_s[c]
        kvf_s[c] = s
        return g_f * s + kv

    def scan_b(ci, s):
        c = n_chunks - 1 - ci
        kv = kvb_s[c]
        kvb_s[c] = s
        return g_b * s + kv

    sf_ref[...] = lax.fori_loop(0, n_chunks, scan_f, s0f_ref[...])
    sb_ref[...] = lax.fori_loop(0, n_chunks, scan_b, s0b_ref[...])

    beta = beta_ref[...]

    def out_pass(c, carry):
        qc = q_ref[chunk(c), :]
        kc = k_ref[chunk(c), :]
        vc = v_ref[chunk(c), :]
        scores = _dot_nt(qc, kc) * d_comb
        y = jnp.dot(scores.astype(BF16), vc, preferred_element_type=F32)
        qf = qc.astype(F32)
        y += jnp.dot((qf * xi_f).astype(BF16), kvf_s[c].astype(BF16), preferred_element_type=F32)
        y += jnp.dot((qf * xi_b).astype(BF16), kvb_s[c].astype(BF16), preferred_element_type=F32)
        o_ref[chunk(c), :] = (_standardize(y) * beta * _silu(g_ref[chunk(c), :])).astype(o_ref.dtype)
        return carry

    lax.fori_loop(0, n_chunks, out_pass, 0, unroll=RET_CHUNK_UNROLL)


def _retention(q, k, v, g, decay_f, decay_b, beta_ret, s0f, s0b):
    b, t, _ = q.shape
    n_chunks = t // CHUNK
    tok = pl.BlockSpec((None, t, LANES), lambda bi, h: (bi, 0, h))
    per_head = pl.BlockSpec((None, 1, LANES), lambda bi, h: (h, 0, 0))
    state = pl.BlockSpec((None, None, RET_DK, RET_DK), lambda bi, h: (bi, h, 0, 0))
    lanes = lambda a: jnp.broadcast_to(a.astype(F32)[:, None, None], (RET_HEADS, 1, LANES))
    st_shape = jax.ShapeDtypeStruct((b, RET_HEADS, RET_DK, RET_DK), F32)
    return pl.pallas_call(
        functools.partial(_ret_body, n_chunks=n_chunks),
        grid=(b, RET_HEADS),
        in_specs=[tok, tok, tok, tok, per_head, per_head, per_head, state, state],
        out_specs=[tok, state, state],
        out_shape=[jax.ShapeDtypeStruct((b, t, RET_WIDTH), BF16), st_shape, st_shape],
        scratch_shapes=[pltpu.VMEM((n_chunks, RET_DK, RET_DK), F32)] * 2,
        compiler_params=_params("parallel", "parallel"),
        name="retention",
    )(q, k, v, g, lanes(decay_f), lanes(decay_b), beta_ret.reshape(RET_HEADS, 1, LANES), s0f, s0b)


def _merge_body(x_ref, ona_ref, ret_ref, wna_ref, wret_ref, bna_ref, gate_ref, lg_ref, lb_ref, o_ref):
    o = ona_ref[...].astype(F32)
    na = o * lax.rsqrt(jnp.mean(o * o, axis=-1, keepdims=True) + LN_EPS) * bna_ref[...]
    mix = (jnp.dot(na.astype(BF16), wna_ref[...], preferred_element_type=F32)
           + jnp.dot(ret_ref[...], wret_ref[...], preferred_element_type=F32))
    y = DEEPNORM_ALPHA * x_ref[...] + gate_ref[...] * mix
    o_ref[...] = _standardize(y) * lg_ref[...] + lb_ref[...]


def _merge(x, o_na, ret, w_out, beta_na, gate, mod_row, ln_g, ln_b, tm):
    b, t, _ = x.shape
    tok = lambda bi, ti: (bi, ti, 0)
    row = pl.BlockSpec((1, D_MODEL), lambda bi, ti: (0, 0))
    half = pl.BlockSpec((None, tm, 512), tok)
    return pl.pallas_call(
        _merge_body,
        grid=(b, t // tm),
        in_specs=[pl.BlockSpec((None, tm, D_MODEL), tok), half, half,
                  pl.BlockSpec((512, D_MODEL), lambda bi, ti: (0, 0)),
                  pl.BlockSpec((512, D_MODEL), lambda bi, ti: (1, 0)),
                  pl.BlockSpec((1, 512), lambda bi, ti: (0, 0)),
                  pl.BlockSpec((None, 1, D_MODEL), lambda bi, ti: (mod_row(bi), 0, 0)),
                  row, row],
        out_specs=pl.BlockSpec((None, tm, D_MODEL), tok),
        out_shape=jax.ShapeDtypeStruct(x.shape, F32),
        compiler_params=_params("parallel", "parallel"),
        name="merge_ln1",
    )(x, o_na, ret, w_out, w_out, beta_na.reshape(1, 512), gate, ln_g.reshape(1, -1), ln_b.reshape(1, -1))


def _two_group_specs(n_p_blocks, blocks_per_batch_s, ctx_row):
    tok_p = lambda i: (jnp.minimum(i, n_p_blocks - 1), 0)
    tok_s = lambda i: (jnp.maximum(i - n_p_blocks, 0), 0)
    mod = lambda i: (jnp.where(i < n_p_blocks, ctx_row, jnp.maximum(i - n_p_blocks, 0) // blocks_per_batch_s), 0, 0)
    return tok_p, tok_s, mod


def _router_body(yp_ref, ys_ref, sh_ref, sc_ref, wh_ref, wl_ref, br_ref, p_ref, gt_ref, tab_ref, cnt_ref,
                 carry_s, *, n_p_blocks):
    i = pl.program_id(0)
    tm = yp_ref.shape[0]

    @pl.when(i == 0)
    def _():
        carry_s[...] = jnp.zeros_like(carry_s)

    y = jnp.where(i < n_p_blocks, yp_ref[...], ys_ref[...])
    h = y * (1.0 + sc_ref[...]) + sh_ref[...]
    h_hi = h.astype(BF16)
    h_lo = (h - h_hi.astype(F32)).astype(BF16)
    w_hi = wh_ref[...]
    lg = _dot_nt(jnp.concatenate([w_hi, wl_ref[...]], axis=0), h_hi)
    work = lg[:N_EXPERTS] + lg[N_EXPERTS:] + _dot_nt(w_hi, h_lo) + br_ref[...]
    eidx = lax.broadcasted_iota(I32, (N_EXPERTS, tm), 0).astype(F32)
    vals, idxs, hots = [], [], []
    for _ in range(TOP_K):
        mx = jnp.max(work, axis=0, keepdims=True)
        idx = jnp.min(jnp.where(work == mx, eidx, float(N_EXPERTS)), axis=0, keepdims=True)
        hot = eidx == idx
        vals.append(mx)
        idxs.append(idx)
        hots.append(hot)
        work = jnp.where(hot, -jnp.inf, work)
    exps = [jnp.exp(v - vals[0]) for v in vals]
    den = exps[0] + exps[1] + exps[2] + exps[3]
    sel = jnp.zeros((N_EXPERTS, tm), F32)
    for hot in hots:
        sel = sel + hot.astype(F32)
    r_i = lax.broadcasted_iota(I32, (tm, tm), 0)
    c_i = lax.broadcasted_iota(I32, (tm, tm), 1)
    earlier = (r_i < c_i).astype(BF16)
    before = jnp.dot(sel.astype(BF16), earlier, preferred_element_type=F32)
    n_e = jnp.sum(sel, axis=1, keepdims=True)
    n8 = jnp.floor((n_e + (RUN_ALIGN - 1.0)) * (1.0 / RUN_ALIGN)) * RUN_ALIGN + jnp.zeros((1, LANES), F32)
    e_r = lax.broadcasted_iota(I32, (N_EXPERTS, N_EXPERTS), 0)
    e_c = lax.broadcasted_iota(I32, (N_EXPERTS, N_EXPERTS), 1)
    off = jnp.dot((e_c < e_r).astype(BF16), n8.astype(BF16), preferred_element_type=F32)
    local = before + jnp.concatenate([off] * (tm // LANES), axis=1)
    rows = [jnp.sum(jnp.where(hot, local, 0.0), axis=0, keepdims=True) for hot in hots]
    p_ref[...] = jnp.concatenate(rows, axis=0).astype(I32)
    gt_ref[...] = jnp.concatenate([e / den for e in exps], axis=0)
    carry = carry_s[...]
    lane = lax.broadcasted_iota(I32, (N_EXPERTS, LANES), 1)
    tab_ref[...] = jnp.where(lane == 0, n8, jnp.where(lane == 1, carry, jnp.where(lane == 2, off, 0.0)))
    carry = carry + n8
    carry_s[...] = carry
    cnt_ref[...] = carry


def _router(yp, ys, shift, scale, ctx_row, s_blocks_per_batch, w_router, b_router, tm):
    n_p, n_s = yp.shape[0], ys.shape[0]
    n = n_p + n_s
    npb = n_p // tm
    tok_p, tok_s, mod = _two_group_specs(npb, s_blocks_per_batch, ctx_row)
    w_t = w_router.T
    w_hi = w_t.astype(BF16)
    w_lo = (w_t - w_hi.astype(F32)).astype(BF16)
    br = jnp.broadcast_to(b_router[:, None], (N_EXPERTS, tm))
    o4 = pl.BlockSpec((TOP_K, tm), lambda i: (0, i))
    whole = lambda shape: pl.BlockSpec(shape, lambda i: (0, 0))
    return pl.pallas_call(
        functools.partial(_router_body, n_p_blocks=npb),
        grid=(n // tm,),
        in_specs=[pl.BlockSpec((tm, D_MODEL), tok_p), pl.BlockSpec((tm, D_MODEL), tok_s),
                  pl.BlockSpec((None, 1, D_MODEL), mod), pl.BlockSpec((None, 1, D_MODEL), mod),
                  whole((N_EXPERTS, D_MODEL)), whole((N_EXPERTS, D_MODEL)), whole((N_EXPERTS, tm))],
        out_specs=[o4, o4, pl.BlockSpec((None, N_EXPERTS, LANES), lambda i: (i, 0, 0)),
                   whole((N_EXPERTS, LANES))],
        out_shape=[jax.ShapeDtypeStruct((TOP_K, n), I32), jax.ShapeDtypeStruct((TOP_K, n), F32),
                   jax.ShapeDtypeStruct((n // tm, N_EXPERTS, LANES), F32),
                   jax.ShapeDtypeStruct((N_EXPERTS, LANES), F32)],
        scratch_shapes=[pltpu.VMEM((N_EXPERTS, LANES), F32)],
        compiler_params=_params("arbitrary"),
        name="moe_router",
    )(yp, ys, shift, scale, w_hi, w_lo, br)


def _for_run_pieces(n_rows, visit):
    off = 0
    for size in RUN_PIECES:
        present = (n_rows & size) != 0
        visit(off, size, present)
        off = off + jnp.where(present, size, 0)


def _block_runs(blk, n_ref, loc_ref, glob_ref, local_buf, sorted_hbm, sem, to_hbm, wait):
    def act(off_local, off_global, size):
        loc = local_buf.at[pl.ds(pl.multiple_of(off_local, RUN_ALIGN), size)]
        glob = sorted_hbm.at[pl.ds(pl.multiple_of(off_global, RUN_ALIGN), size)]
        copy = pltpu.make_async_copy(loc, glob, sem) if to_hbm else pltpu.make_async_copy(glob, loc, sem)
        if wait:
            copy.wait()
        else:
            copy.start()

    def per_expert(e, carry):
        idx = blk * N_EXPERTS + e
        n = n_ref[idx]
        loc0 = loc_ref[idx]
        glob0 = glob_ref[idx]
        n_chunks = lax.shift_right_logical(n, RUN_CHUNK.bit_length() - 1)

        def chunk(j, c2):
            act(loc0 + j * RUN_CHUNK, glob0 + j * RUN_CHUNK, RUN_CHUNK)
            return c2

        lax.fori_loop(0, n_chunks, chunk, 0)
        off = n_chunks * RUN_CHUNK
        size = RUN_CHUNK // 2
        while size >= RUN_ALIGN:
            present = (n & size) != 0

            @pl.when(present)
            def _():
                act(loc0 + off, glob0 + off, size)

            off = off + jnp.where(present, size, 0)
            size //= 2
        return carry

    lax.fori_loop(0, N_EXPERTS, per_expert, 0)


def _dispatch_body(n_ref, loc_ref, glob_ref, zlo_ref, zn_ref, p_ref, yp_ref, ys_ref, sh_ref, sc_ref, xs_ref,
                   xbuf, zbuf, sems, zsem, *, n_p_blocks, n_blocks):
    i = pl.program_id(0)
    slot = i % 2
    y = jnp.where(i < n_p_blocks, yp_ref[...], ys_ref[...])
    h = (y * (1.0 + sc_ref[...]) + sh_ref[...]).astype(BF16)
    rows = p_ref[...]
    r_iota = lax.broadcasted_iota(I32, (LOCAL_ROWS, rows.shape[1]), 0)
    place = r_iota == rows[0:1]
    for kk in range(1, TOP_K):
        place = place | (r_iota == rows[kk:kk + 1])
    xbuf[slot] = jnp.dot(jnp.where(place, 1.0, 0.0).astype(BF16), h, preferred_element_type=F32)

    runs = functools.partial(_block_runs, n_ref=n_ref, loc_ref=loc_ref, glob_ref=glob_ref,
                             sorted_hbm=xs_ref, to_hbm=True)
    runs(i, local_buf=xbuf.at[slot], sem=sems.at[slot], wait=False)

    @pl.when(i > 0)
    def _():
        runs(i - 1, local_buf=xbuf.at[1 - slot], sem=sems.at[1 - slot], wait=True)

    @pl.when(i == n_blocks - 1)
    def _():
        runs(i, local_buf=xbuf.at[slot], sem=sems.at[slot], wait=True)
        zbuf[...] = jnp.zeros_like(zbuf)
        sem = zsem

        def per_expert(e, carry):
            lo = zlo_ref[e]

            def visit(off, size, present):
                copy = pltpu.make_async_copy(
                    zbuf.at[pl.ds(0, size)], xs_ref.at[pl.ds(pl.multiple_of(lo + off, RUN_ALIGN), size)], sem)

                @pl.when(present)
                def _():
                    copy.start()
                    copy.wait()

            _for_run_pieces(zn_ref[e], visit)
            return carry

        lax.fori_loop(0, N_EXPERTS, per_expert, 0)

        tail_lo = zlo_ref[N_EXPERTS]

        def tile_copy(t):
            start = pl.multiple_of(tail_lo + t * EXPERT_TILE, EXPERT_TILE)
            return pltpu.make_async_copy(zbuf, xs_ref.at[pl.ds(start, EXPERT_TILE)], sem)

        def t_issue(t, c2):
            tile_copy(t).start()
            return c2

        def t_drain(t, c2):
            tile_copy(0).wait()
            return c2

        n_tail = zn_ref[N_EXPERTS] // EXPERT_TILE
        lax.fori_loop(0, n_tail, t_issue, 0)
        lax.fori_loop(0, n_tail, t_drain, 0)


def _dispatch(yp, ys, shift, scale, ctx_row, s_blocks_per_batch, local_rows, run_n, run_local, run_global,
              zero_lo, zero_n, cap_rows, tm):
    n_p, n_s = yp.shape[0], ys.shape[0]
    n = n_p + n_s
    npb = n_p // tm
    nb = n // tm
    tok_p, tok_s, mod = _two_group_specs(npb, s_blocks_per_batch, ctx_row)
    smem_all = pl.BlockSpec(memory_space=pltpu.SMEM)
    return pl.pallas_call(
        functools.partial(_dispatch_body, n_p_blocks=npb, n_blocks=nb),
        grid=(nb,),
        in_specs=[smem_all] * 5 + [
            pl.BlockSpec((TOP_K, tm), lambda i: (0, i)),
            pl.BlockSpec((tm, D_MODEL), tok_p), pl.BlockSpec((tm, D_MODEL), tok_s),
            pl.BlockSpec((None, 1, D_MODEL), mod), pl.BlockSpec((None, 1, D_MODEL), mod)],
        out_specs=pl.BlockSpec(memory_space=pl.ANY),
        out_shape=jax.ShapeDtypeStruct((cap_rows, D_MODEL), F32),
        scratch_shapes=[pltpu.VMEM((2, LOCAL_ROWS, D_MODEL), F32),
                        pltpu.VMEM((EXPERT_TILE, D_MODEL), F32),
                        pltpu.SemaphoreType.DMA((2,)), pltpu.SemaphoreType.DMA(())],
        compiler_params=_params("arbitrary"),
        name="moe_dispatch",
    )(run_n, run_local, run_global, zero_lo, zero_n, local_rows, yp, ys, shift, scale)


def _expert_body(te_ref, nu_ref, x_ref, wgu_ref, bgu_ref, wd_ref, bd_ref, y_ref, act_s):
    j = pl.program_id(0)

    @pl.when(j < nu_ref[0])
    def _():
        x = x_ref[...].astype(BF16)
        for c in range(D_FF // LANES):
            sl = slice(c * 2 * LANES, (c + 1) * 2 * LANES)
            hu = jnp.dot(x, wgu_ref[:, sl], preferred_element_type=F32) + bgu_ref[:, sl]
            x_glu = jnp.minimum(hu[:, :LANES], SWIGLU_LIMIT)
            x_lin = jnp.clip(hu[:, LANES:], -SWIGLU_LIMIT, SWIGLU_LIMIT)
            act = x_glu * (1.0 / (1.0 + jnp.exp(-SWIGLU_ALPHA * x_glu))) * (x_lin + 1.0)
            act_s[:, c * LANES:(c + 1) * LANES] = act.astype(BF16)
        y_ref[...] = jnp.dot(act_s[...], wd_ref[...], preferred_element_type=F32) + bd_ref[...]

    @pl.when(j >= nu_ref[0])
    def _():
        y_ref[...] = jnp.zeros_like(y_ref)


def _experts(tile_expert, n_used, xs, wgu, bgu, wd, bd):
    cap_rows = xs.shape[0]
    n_tiles = cap_rows // EXPERT_TILE
    grid_spec = pltpu.PrefetchScalarGridSpec(
        num_scalar_prefetch=2,
        grid=(n_tiles,),
        in_specs=[pl.BlockSpec((EXPERT_TILE, D_MODEL), lambda j, te, nu: (jnp.minimum(j, nu[0] - 1), 0)),
                  pl.BlockSpec((None, D_MODEL, 2 * D_FF), lambda j, te, nu: (te[j], 0, 0)),
                  pl.BlockSpec((None, 1, 2 * D_FF), lambda j, te, nu: (te[j], 0, 0)),
                  pl.BlockSpec((None, D_FF, D_MODEL), lambda j, te, nu: (te[j], 0, 0)),
                  pl.BlockSpec((None, 1, D_MODEL), lambda j, te, nu: (te[j], 0, 0))],
        out_specs=pl.BlockSpec((EXPERT_TILE, D_MODEL), lambda j, te, nu: (j, 0)),
        scratch_shapes=[pltpu.VMEM((EXPERT_TILE, D_FF), BF16)],
    )
    return pl.pallas_call(
        _expert_body,
        grid_spec=grid_spec,
        out_shape=jax.ShapeDtypeStruct((cap_rows, D_MODEL), F32),
        compiler_params=_params("arbitrary"),
        name="moe_experts",
    )(tile_expert, n_used, xs, wgu, bgu, wd, bd)


def _combine_body(n_ref, loc_ref, glob_ref, y1_ref, p_ref, gt_ref, gate_ref, lg_ref, lb_ref, ys_ref, o_ref,
                  ybuf, sems, *, block0, n_blocks):
    i = pl.program_id(0)
    slot = i % 2
    tm = y1_ref.shape[0]
    runs = functools.partial(_block_runs, n_ref=n_ref, loc_ref=loc_ref, glob_ref=glob_ref,
                             sorted_hbm=ys_ref, to_hbm=False)

    @pl.when(i == 0)
    def _():
        ybuf[...] = jnp.zeros_like(ybuf)
        runs(block0, local_buf=ybuf.at[0], sem=sems.at[0], wait=False)

    @pl.when(i + 1 < n_blocks)
    def _():
        runs(block0 + i + 1, local_buf=ybuf.at[1 - slot], sem=sems.at[1 - slot], wait=False)

    runs(block0 + i, local_buf=ybuf.at[slot], sem=sems.at[slot], wait=True)

    rows = p_ref[...]
    gates = gt_ref[...]
    c_iota = lax.broadcasted_iota(I32, (tm, LOCAL_ROWS), 1)
    weight = jnp.where(c_iota == rows[:, 0:1], gates[:, 0:1], 0.0)
    for kk in range(1, TOP_K):
        weight = weight + jnp.where(c_iota == rows[:, kk:kk + 1], gates[:, kk:kk + 1], 0.0)
    w_hi = weight.astype(BF16)
    w_lo = (weight - w_hi.astype(F32)).astype(BF16)
    yb = ybuf[slot].astype(BF16)
    f = (jnp.dot(w_hi, yb, preferred_element_type=F32) + jnp.dot(w_lo, yb, preferred_element_type=F32))
    y = DEEPNORM_ALPHA * y1_ref[...] + gate_ref[...] * f
    o_ref[...] = _standardize(y) * lg_ref[...] + lb_ref[...]


def _combine(y1, local_rows, gates, run_n, run_local, run_global, block0, gate_mod, mod_row, blocks_per_batch,
             y_sorted, ln_g, ln_b, tm):
    n_g = y1.shape[0]
    nb = n_g // tm
    row = pl.BlockSpec((1, D_MODEL), lambda i: (0, 0))
    smem_all = pl.BlockSpec(memory_space=pltpu.SMEM)
    pair = pl.BlockSpec((tm, TOP_K), lambda i: (i + block0, 0))
    return pl.pallas_call(
        functools.partial(_combine_body, block0=block0, n_blocks=nb),
        grid=(nb,),
        in_specs=[smem_all] * 3 + [
            pl.BlockSpec((tm, D_MODEL), lambda i: (i, 0)), pair, pair,
            pl.BlockSpec((None, 1, D_MODEL), lambda i: (mod_row(i // blocks_per_batch), 0, 0)),
            row, row,
            pl.BlockSpec(memory_space=pl.ANY)],
        out_specs=pl.BlockSpec((tm, D_MODEL), lambda i: (i, 0)),
        out_shape=jax.ShapeDtypeStruct((n_g, D_MODEL), F32),
        scratch_shapes=[pltpu.VMEM((2, LOCAL_ROWS, D_MODEL), F32), pltpu.SemaphoreType.DMA((2,))],
        compiler_params=_params("arbitrary"),
        name="moe_combine_ln2",
    )(run_n, run_local, run_global, y1, local_rows, gates, gate_mod,
      ln_g.reshape(1, -1), ln_b.reshape(1, -1), y_sorted)


def _regroup_body(w_ref, o_ref):
    blk = 2 * LANES
    src = lax.broadcasted_iota(I32, (blk, blk), 0)
    dst = lax.broadcasted_iota(I32, (blk, blk), 1)
    perm = (src == jnp.where(dst < LANES, 2 * dst, 2 * (dst - LANES) + 1)).astype(BF16)
    for s in range(w_ref.shape[1] // blk):
        sl = slice(s * blk, (s + 1) * blk)
        o_ref[:, sl] = jnp.dot(w_ref[:, sl].astype(BF16), perm, preferred_element_type=F32).astype(BF16)


def _regroup_gate_up(w_gate_up):
    e, d, n = w_gate_up.shape
    tn = 512
    spec = pl.BlockSpec((None, d, tn), lambda ei, j: (ei, 0, j))
    return pl.pallas_call(
        _regroup_body,
        grid=(e, n // tn),
        in_specs=[spec],
        out_specs=spec,
        out_shape=jax.ShapeDtypeStruct((e, d, n), BF16),
        compiler_params=_params("parallel", "parallel"),
        name="regroup_gate_up",
    )(w_gate_up)


def _expert_weights(w_gate_up, b_gate_up, w_down):
    e = w_gate_up.shape[0]
    nblk = D_FF // LANES
    bgu = b_gate_up.reshape(e, 1, nblk, LANES, 2).transpose(0, 1, 2, 4, 3).reshape(e, 1, 2 * D_FF)
    return _regroup_gate_up(w_gate_up), bgu, w_down.astype(BF16)


def _routing_plan(table, totals, n_pairs):
    nb = table.shape[0]
    run_n = table[:, :, 0].astype(I32)
    rows_before = table[:, :, 1].astype(I32)
    run_local = table[:, :, 2].astype(I32)
    cnt = totals[:, 0].astype(I32)
    padded = (cnt + EXPERT_TILE - 1) // EXPERT_TILE * EXPERT_TILE
    pad_end = jnp.cumsum(padded)
    pad_start = pad_end - padded
    run_global = pad_start[None, :] + rows_before
    rows_max = n_pairs + nb * N_EXPERTS * (RUN_ALIGN - 1) + N_EXPERTS * (EXPERT_TILE - 1)
    n_tiles = -(-rows_max // EXPERT_TILE)
    n_used = pad_end[-1] // EXPERT_TILE
    tile_row = jnp.minimum(jnp.arange(n_tiles, dtype=I32), n_used - 1) * EXPERT_TILE
    tile_expert = jnp.sum((pad_end[None, :] <= tile_row[:, None]).astype(I32), axis=1)
    tile_expert = jnp.minimum(tile_expert, N_EXPERTS - 1)
    cap_rows = n_tiles * EXPERT_TILE
    zero_lo = jnp.concatenate([pad_start + cnt, pad_end[-1:]])
    zero_n = jnp.concatenate([padded - cnt, cap_rows - pad_end[-1:]])
    runs = (run_n.reshape(-1), run_local.reshape(-1), run_global.reshape(-1))
    return runs, tile_expert, n_used.reshape(1), zero_lo, zero_n, cap_rows


def kernel(x_prompt, x_sample, cache_na_k, cache_na_v, state_ret_fwd, state_ret_bwd, c, c_ctx, w_ada, b_ada, w_in, rpb, ret_decay_fwd, ret_decay_bwd, beta_na, beta_ret, w_out, ln1_g, ln1_b, w_router, b_router, w_gate_up, b_gate_up, w_down, b_down, ln2_g, ln2_b):
    bp, tp, _ = x_prompt.shape
    bs, ts, _ = x_sample.shape
    assert w_ada.shape[0] == DEPTH == 1
    l = 0
    ctx_row = bs
    sample_row = lambda bi: bi
    prompt_row = lambda bi: ctx_row

    cond = jnp.concatenate([c, c_ctx[None, :], jnp.zeros((16 - bs - 1, D_MODEL), F32)], axis=0)
    m = _adaln(cond, w_ada[l], b_ada[l]).reshape(16, 6, 1, D_MODEL)
    sh_a, sc_a, gt_a, sh_f, sc_f, gt_f = (m[:, i] for i in range(6))

    w_in_b = w_in[l].astype(BF16)
    w_out_b = w_out[l].astype(BF16)

    q, k_p, v_p, qr, kr, vr, g = _project(x_prompt, sh_a, sc_a, prompt_row, w_in_b, tp, F32, rope=False)
    o_na = _context_attention(q, k_p, v_p)
    zeros_state = jnp.zeros((bp, RET_HEADS, RET_DK, RET_DK), F32)
    ret, s_f, s_b = _retention(qr, kr, vr, g, ret_decay_fwd[l], ret_decay_bwd[l], beta_ret[l],
                               zeros_state, zeros_state)
    yp1 = _merge(x_prompt, o_na, ret, w_out_b, beta_na[l], gt_a, prompt_row, ln1_g[l], ln1_b[l], tp)

    q, k_s, v_s, qr, kr, vr, g = _project(x_sample, sh_a, sc_a, sample_row, w_in_b, 512, BF16, rope=True)
    lc = cache_na_k.shape[2]
    o_na = _neighbourhood_attention(q, k_s, v_s, cache_na_k[:, l].reshape(bs, lc, NA_WIDTH),
                                    cache_na_v[:, l].reshape(bs, lc, NA_WIDTH), _na_bias(rpb[l]))
    ret, _, _ = _retention(qr, kr, vr, g, ret_decay_fwd[l], ret_decay_bwd[l], beta_ret[l],
                           state_ret_fwd[:, l], state_ret_bwd[:, l])
    ys1 = _merge(x_sample, o_na, ret, w_out_b, beta_na[l], gt_a, sample_row, ln1_g[l], ln1_b[l], 512)

    tm = TOKEN_BLOCK
    yp1f = yp1.reshape(bp * tp, D_MODEL)
    ys1f = ys1.reshape(bs * ts, D_MODEL)
    local_rows, gates, table, totals = _router(yp1f, ys1f, sh_f, sc_f, ctx_row, ts // tm,
                                               w_router[l], b_router[l], tm)
    runs, tile_expert, n_used, zero_lo, zero_n, cap_rows = _routing_plan(
        table, totals, (bp * tp + bs * ts) * TOP_K)
    xs = _dispatch(yp1f, ys1f, sh_f, sc_f, ctx_row, ts // tm, local_rows, *runs, zero_lo, zero_n, cap_rows, tm)
    wgu, bgu, wd = _expert_weights(w_gate_up[l], b_gate_up[l], w_down[l])
    y_sorted = _experts(tile_expert, n_used, xs, wgu, bgu, wd, b_down[l][:, None, :])
    rows_t, gates_t = local_rows.T, gates.T
    yp = _combine(yp1f, rows_t, gates_t, *runs, 0, gt_f, prompt_row, 1, y_sorted, ln2_g[l], ln2_b[l], tm)
    ys = _combine(ys1f, rows_t, gates_t, *runs, (bp * tp) // tm, gt_f, sample_row, ts // tm, y_sorted,
                  ln2_g[l], ln2_b[l], tm)

    return (yp.reshape(bp, tp, D_MODEL), ys.reshape(bs, ts, D_MODEL),
            k_p.reshape(bp, 1, tp, NA_HEADS, NA_DIM), v_p.reshape(bp, 1, tp, NA_HEADS, NA_DIM),
            s_f[:, None], s_b[:, None])
```

```python
import functools

import jax
import jax.numpy as jnp
from jax import lax
from jax.experimental import pallas as pl
from jax.experimental.pallas import tpu as pltpu

F32 = jnp.float32
BF16 = jnp.bfloat16
I32 = jnp.int32

D_MODEL = 1024
GRID_W = 64
NA_HEADS = 8
NA_DIM = 64
NA_WIDTH = NA_HEADS * NA_DIM
NA_KH = 8
NA_KW = 16
RET_HEADS = 4
RET_DK = 128
RET_WIDTH = RET_HEADS * RET_DK
CHUNK = 128
N_EXPERTS = 32
TOP_K = 4
D_FF = 1024
SWIGLU_LIMIT = 7.0
SWIGLU_ALPHA = 1.702
ROPE_BASE = 10000.0
LN_EPS = 1e-5
DEPTH = 1
DEEPNORM_ALPHA = (2.0 * DEPTH) ** 0.25

LANES = 128
NEG_BIG = -1e30
TOKEN_BLOCK = 256
EXPERT_TILE = 512
VMEM_LIMIT = 56 * 1024 * 1024
RET_CHUNK_UNROLL = 8
SUBLANES = 8
RUN_ALIGN = SUBLANES
LOCAL_ROWS = TOKEN_BLOCK * TOP_K + N_EXPERTS * RUN_ALIGN
RUN_CHUNK = 32
RUN_PIECES = tuple(TOKEN_BLOCK >> s for s in range(TOKEN_BLOCK.bit_length())
                   if (TOKEN_BLOCK >> s) >= RUN_ALIGN)


def _params(*sem):
    return pltpu.CompilerParams(dimension_semantics=sem, vmem_limit_bytes=VMEM_LIMIT)


def _silu(x):
    return x / (1.0 + jnp.exp(-x))


def _standardize(x):
    mu = jnp.mean(x, axis=-1, keepdims=True)
    xc = x - mu
    var = jnp.mean(xc * xc, axis=-1, keepdims=True)
    return xc * lax.rsqrt(var + LN_EPS)


def _dot_nt(a, b):
    return lax.dot_general(a, b, (((1,), (1,)), ((), ())), preferred_element_type=F32)


def _dot_tn(a, b):
    return lax.dot_general(a, b, (((0,), (0,)), ((), ())), preferred_element_type=F32)


def _ada_body(c_ref, w_ref, b_ref, o_ref):
    s = _silu(c_ref[...])
    o_ref[...] = jnp.dot(s, w_ref[...], preferred_element_type=F32,
                         precision=lax.Precision.HIGHEST) + b_ref[...]


def _adaln(cond, w_ada, b_ada):
    r = cond.shape[0]
    n = w_ada.shape[1]
    tn = 1536
    return pl.pallas_call(
        _ada_body,
        grid=(n // tn,),
        in_specs=[pl.BlockSpec((r, D_MODEL), lambda j: (0, 0)),
                  pl.BlockSpec((D_MODEL, tn), lambda j: (0, j)),
                  pl.BlockSpec((1, tn), lambda j: (0, j))],
        out_specs=pl.BlockSpec((r, tn), lambda j: (0, j)),
        out_shape=jax.ShapeDtypeStruct((r, n), F32),
        compiler_params=_params("arbitrary"),
        name="adaln",
    )(cond, w_ada, b_ada.reshape(1, n))


def _proj_body(*refs, rope):
    if rope:
        (x_ref, sh_ref, sc_ref, w_ref, cos_ref, sa_ref, sb_ref,
         q_ref, k_ref, v_ref, qr_ref, kr_ref, vr_ref, g_ref) = refs
    else:
        (x_ref, sh_ref, sc_ref, w_ref,
         q_ref, k_ref, v_ref, qr_ref, kr_ref, vr_ref, g_ref) = refs
    h = (x_ref[...] * (1.0 + sc_ref[...]) + sh_ref[...]).astype(BF16)

    def cols(c):
        return jnp.dot(h, w_ref[:, c * 512:(c + 1) * 512], preferred_element_type=F32)

    q_ref[...] = cols(0).astype(q_ref.dtype)
    k_ref[...] = cols(1).astype(k_ref.dtype)
    v_ref[...] = cols(2).astype(v_ref.dtype)
    pq = cols(3)
    pk = cols(4) * (RET_DK ** -0.5)
    if rope:
        cs, sa, sb = cos_ref[...], sa_ref[...], sb_ref[...]
        for hd in range(RET_HEADS):
            sl = slice(hd * LANES, (hd + 1) * LANES)
            for p, o_ref in ((pq, qr_ref), (pk, kr_ref)):
                xs = p[:, sl]
                rot = xs * cs + pltpu.roll(xs, 96, 1) * sa + pltpu.roll(xs, 32, 1) * sb
                o_ref[:, sl] = rot.astype(o_ref.dtype)
    else:
        qr_ref[...] = pq.astype(qr_ref.dtype)
        kr_ref[...] = pk.astype(kr_ref.dtype)
    vr_ref[...] = cols(5).astype(vr_ref.dtype)
    g_ref[...] = cols(6)


def _rope_tables(t_len):
    t = jnp.arange(t_len, dtype=jnp.int32)
    pos_row = (t // GRID_W).astype(F32)[:, None]
    pos_col = (t % GRID_W).astype(F32)[:, None]
    lane = jnp.arange(LANES, dtype=jnp.int32)[None, :]
    n_freq = RET_DK // 4
    inv_freq = ROPE_BASE ** (-(lane % n_freq).astype(F32) / n_freq)
    ang = jnp.where(lane < RET_DK // 2, pos_row, pos_col) * inv_freq
    first = (lane % (2 * n_freq)) < n_freq
    cos = jnp.cos(ang)
    sin = jnp.sin(ang)
    return cos, jnp.where(first, -sin, 0.0), jnp.where(first, 0.0, sin)


def _project(x, shift, scale, mod_row, w_in, tm, kv_dtype, rope):
    b, t, _ = x.shape
    tok = lambda bi, ti: (bi, ti, 0)
    mod = lambda bi, ti: (mod_row(bi), 0, 0)
    in_specs = [pl.BlockSpec((None, tm, D_MODEL), tok),
                pl.BlockSpec((None, 1, D_MODEL), mod),
                pl.BlockSpec((None, 1, D_MODEL), mod),
                pl.BlockSpec(w_in.shape, lambda bi, ti: (0, 0))]
    args = [x, shift, scale, w_in]
    if rope:
        tab = lambda bi, ti: (ti, 0)
        in_specs += [pl.BlockSpec((tm, LANES), tab)] * 3
        args += list(_rope_tables(t))
    o512 = pl.BlockSpec((None, tm, 512), tok)
    shp = lambda dt: jax.ShapeDtypeStruct((b, t, 512), dt)
    return pl.pallas_call(
        functools.partial(_proj_body, rope=rope),
        grid=(b, t // tm),
        in_specs=in_specs,
        out_specs=[o512] * 7,
        out_shape=[shp(BF16), shp(kv_dtype), shp(kv_dtype), shp(BF16), shp(BF16), shp(BF16), shp(F32)],
        compiler_params=_params("parallel", "parallel"),
        name="in_proj",
    )(*args)


def _pair_masks(shape):
    lane = lax.broadcasted_iota(I32, shape, 1)
    return lane < NA_DIM, lane >= NA_DIM


def _ctx_attn_body(q_ref, k_ref, v_ref, o_ref):
    q2 = q_ref[...]
    k2 = k_ref[...].astype(BF16)
    v2 = v_ref[...].astype(BF16)
    t = q2.shape[0]
    lo, hi = _pair_masks(q2.shape)
    zero = jnp.zeros_like(q2)
    qs = jnp.concatenate([jnp.where(lo, q2, zero), jnp.where(hi, q2, zero)], axis=0)
    s = _dot_nt(qs, k2) * (NA_DIM ** -0.5)
    e = jnp.exp(s - jnp.max(s, axis=-1, keepdims=True))
    den = jnp.sum(e, axis=-1, keepdims=True)
    o = jnp.dot(e.astype(BF16), v2, preferred_element_type=F32) / den
    o_ref[...] = jnp.where(lo, o[:t], o[t:]).astype(o_ref.dtype)


def _context_attention(q, k, v):
    b, t, _ = q.shape
    spec = pl.BlockSpec((None, t, LANES), lambda bi, hp: (bi, 0, hp))
    return pl.pallas_call(
        _ctx_attn_body,
        grid=(b, NA_WIDTH // LANES),
        in_specs=[spec] * 3,
        out_specs=spec,
        out_shape=jax.ShapeDtypeStruct((b, t, NA_WIDTH), BF16),
        compiler_params=_params("parallel", "parallel"),
        name="ctx_attn",
    )(q, k, v)


def _na_bias_body(rpb_ref, o_ref):
    h = pl.program_id(0)
    v = pl.program_id(1)
    shape = (GRID_W, LANES)
    w = lax.broadcasted_iota(I32, shape, 0)
    lane = lax.broadcasted_iota(I32, shape, 1)
    upper = lane >= GRID_W
    kc = jnp.where(upper, lane - GRID_W, lane)
    cdiff = kc - w + (NA_KW - 1)
    cstart = jnp.clip(w - NA_KW // 2, 0, GRID_W - NA_KW)
    inwin = (kc >= cstart) & (kc < cstart + NA_KW)
    n_ro = 2 * NA_KH - 1
    n_co = 2 * NA_KW - 1
    for j in range(NA_KH // 2):
        ro = 2 * j - v + (NA_KH - 1)
        base0 = (h * n_ro + ro) * n_co
        acc = jnp.zeros(shape, F32)
        for c in range(n_co):
            val = jnp.where(upper, rpb_ref[base0 + n_co + c], rpb_ref[base0 + c])
            acc = jnp.where(cdiff == c, val, acc)
        o_ref[:, j * LANES:(j + 1) * LANES] = jnp.where(inwin, acc, NEG_BIG)


def _na_bias(rpb):
    n = NA_KH * GRID_W
    return pl.pallas_call(
        _na_bias_body,
        grid=(NA_HEADS, NA_KH),
        in_specs=[pl.BlockSpec(memory_space=pltpu.SMEM)],
        out_specs=pl.BlockSpec((None, None, GRID_W, n), lambda h, v: (h, v, 0, 0)),
        out_shape=jax.ShapeDtypeStruct((NA_HEADS, NA_KH, GRID_W, n), F32),
        compiler_params=_params("parallel", "parallel"),
        name="na_bias",
    )(rpb.reshape(-1))


def _na_body(q_ref, k_ref, v_ref, ck_ref, cv_ref, bias_ref, o_ref, s_buf, e_buf, d_buf, *, rows):
    ck = ck_ref[...].astype(BF16)
    cv = cv_ref[...].astype(BF16)
    lo, hi = _pair_masks((GRID_W, LANES))
    scale = NA_DIM ** -0.5
    n_loc = NA_KH * GRID_W

    def window(r):
        r = jnp.clip(r, 0, rows - 1)
        r_start = jnp.clip(r - NA_KH // 2, 0, rows - NA_KH)
        return (r - r_start, pl.ds(pl.multiple_of(r * GRID_W, GRID_W), GRID_W),
                pl.ds(pl.multiple_of(r_start * GRID_W, GRID_W), n_loc))

    def scores(r, slot):
        vidx, qsl, ksl = window(r)
        q2 = q_ref[qsl, :]
        zero = jnp.zeros_like(q2)
        q2 = q2 * scale
        qs = jnp.concatenate([jnp.where(lo, q2, zero), jnp.where(hi, q2, zero)], axis=0)
        bias = jnp.concatenate([bias_ref[0, vidx], bias_ref[1, vidx]], axis=0)
        s_buf[slot, :, :n_loc] = _dot_nt(qs, k_ref[ksl, :]) + bias
        s_buf[slot, :, n_loc:] = _dot_nt(qs, ck)

    def numerators(slot):
        s = s_buf[slot]
        e = jnp.exp(s - jnp.max(s, axis=-1, keepdims=True))
        d_buf[slot] = jnp.broadcast_to(jnp.sum(e, axis=-1, keepdims=True), d_buf.shape[1:])
        e_buf[slot] = e.astype(BF16)

    def output(r, slot):
        _, qsl, ksl = window(r)
        o = (jnp.dot(e_buf[slot, :, :n_loc], v_ref[ksl, :], preferred_element_type=F32)
             + jnp.dot(e_buf[slot, :, n_loc:], cv, preferred_element_type=F32)) / d_buf[slot]
        o_ref[qsl, :] = jnp.where(lo, o[:GRID_W], o[GRID_W:]).astype(o_ref.dtype)

    s_buf[...] = jnp.zeros_like(s_buf)
    e_buf[...] = jnp.zeros_like(e_buf)
    d_buf[...] = jnp.ones_like(d_buf)

    def two_steps(j, carry):
        r = 2 * j
        scores(r, 0)
        numerators(1)
        output(r - 2, 0)
        scores(r + 1, 1)
        numerators(0)
        output(r - 1, 1)
        return carry

    lax.fori_loop(0, rows // 2 + 1, two_steps, 0)


def _neighbourhood_attention(q, k, v, ck, cv, bias):
    b, t, _ = q.shape
    lc = ck.shape[1]
    rows = t // GRID_W
    tok = pl.BlockSpec((None, t, LANES), lambda hp, bi: (bi, 0, hp))
    ctx = pl.BlockSpec((None, lc, LANES), lambda hp, bi: (bi, 0, hp))
    return pl.pallas_call(
        functools.partial(_na_body, rows=rows),
        grid=(NA_WIDTH // LANES, b),
        in_specs=[tok, tok, tok, ctx, ctx,
                  pl.BlockSpec((2, NA_KH, GRID_W, NA_KH * GRID_W), lambda hp, bi: (hp, 0, 0, 0))],
        out_specs=tok,
        out_shape=jax.ShapeDtypeStruct((b, t, NA_WIDTH), BF16),
        scratch_shapes=[pltpu.VMEM((2, 2 * GRID_W, NA_KH * GRID_W + lc), F32),
                        pltpu.VMEM((2, 2 * GRID_W, NA_KH * GRID_W + lc), BF16),
                        pltpu.VMEM((2, 2 * GRID_W, LANES), F32)],
        compiler_params=_params("parallel", "parallel"),
        name="na_attn",
    )(q, k, v, ck, cv, bias)


def _ret_body(q_ref, k_ref, v_ref, g_ref, df_ref, db_ref, beta_ref, s0f_ref, s0b_ref,
              o_ref, sf_ref, sb_ref, kvf_s, kvb_s, *, n_chunks):
    c_len = CHUNK
    lgf = -jnp.log1p(jnp.exp(-df_ref[...]))
    lgb = -jnp.log1p(jnp.exp(-db_ref[...]))
    shape = (c_len, c_len)
    i = lax.broadcasted_iota(I32, shape, 0).astype(F32)
    j = lax.broadcasted_iota(I32, shape, 1).astype(F32)
    dij = i - j
    d_comb = (jnp.where(dij >= 0, jnp.exp(jnp.maximum(dij, 0.0) * lgf), 0.0)
              + jnp.where(dij <= 0, jnp.exp(jnp.maximum(-dij, 0.0) * lgb), 0.0))
    xi_f = jnp.exp((i + 1.0) * lgf)
    xi_b = jnp.exp((c_len - i) * lgb)
    zeta_f = jnp.exp((c_len - 1.0 - i) * lgf)
    zeta_b = jnp.exp(i * lgb)
    g_f = jnp.exp(c_len * lgf)
    g_b = jnp.exp(c_len * lgb)

    def chunk(c):
        return pl.ds(pl.multiple_of(c * c_len, c_len), c_len)

    def kv_pass(c, carry):
        kc = k_ref[chunk(c), :]
        vc = v_ref[chunk(c), :].astype(F32)
        kvf_s[c] = _dot_tn(kc, (vc * zeta_f).astype(BF16))
        kvb_s[c] = _dot_tn(kc, (vc * zeta_b).astype(BF16))
        return carry

    lax.fori_loop(0, n_chunks, kv_pass, 0, unroll=RET_CHUNK_UNROLL)

    def scan_f(c, s):
        kv = kvf_s[c]
        kvf_s[c] = s
        return g_f * s + kv

    def scan_b(ci, s):
        c = n_chunks - 1 - ci
        kv = kvb_s[c]
        kvb_s[c] = s
        return g_b * s + kv

    sf_ref[...] = lax.fori_loop(0, n_chunks, scan_f, s0f_ref[...])
    sb_ref[...] = lax.fori_loop(0, n_chunks, scan_b, s0b_ref[...])

    beta = beta_ref[...]

    def out_pass(c, carry):
        qc = q_ref[chunk(c), :]
        kc = k_ref[chunk(c), :]
        vc = v_ref[chunk(c), :]
        scores = _dot_nt(qc, kc) * d_comb
        y = jnp.dot(scores.astype(BF16), vc, preferred_element_type=F32)
        qf = qc.astype(F32)
        y += jnp.dot((qf * xi_f).astype(BF16), kvf_s[c].astype(BF16), preferred_element_type=F32)
        y += jnp.dot((qf * xi_b).astype(BF16), kvb_s[c].astype(BF16), preferred_element_type=F32)
        o_ref[chunk(c), :] = (_standardize(y) * beta * _silu(g_ref[chunk(c), :])).astype(o_ref.dtype)
        return carry

    lax.fori_loop(0, n_chunks, out_pass, 0, unroll=RET_CHUNK_UNROLL)


def _retention(q, k, v, g, decay_f, decay_b, beta_ret, s0f, s0b):
    b, t, _ = q.shape
    n_chunks = t // CHUNK
    tok = pl.BlockSpec((None, t, LANES), lambda bi, h: (bi, 0, h))
    per_head = pl.BlockSpec((None, 1, LANES), lambda bi, h: (h, 0, 0))
    state = pl.BlockSpec((None, None, RET_DK, RET_DK), lambda bi, h: (bi, h, 0, 0))
    lanes = lambda a: jnp.broadcast_to(a.astype(F32)[:, None, None], (RET_HEADS, 1, LANES))
    st_shape = jax.ShapeDtypeStruct((b, RET_HEADS, RET_DK, RET_DK), F32)
    return pl.pallas_call(
        functools.partial(_ret_body, n_chunks=n_chunks),
        grid=(b, RET_HEADS),
        in_specs=[tok, tok, tok, tok, per_head, per_head, per_head, state, state],
        out_specs=[tok, state, state],
        out_shape=[jax.ShapeDtypeStruct((b, t, RET_WIDTH), BF16), st_shape, st_shape],
        scratch_shapes=[pltpu.VMEM((n_chunks, RET_DK, RET_DK), F32)] * 2,
        compiler_params=_params("parallel", "parallel"),
        name="retention",
    )(q, k, v, g, lanes(decay_f), lanes(decay_b), beta_ret.reshape(RET_HEADS, 1, LANES), s0f, s0b)


def _merge_body(x_ref, ona_ref, ret_ref, wna_ref, wret_ref, bna_ref, gate_ref, lg_ref, lb_ref, o_ref):
    o = ona_ref[...].astype(F32)
    na = o * lax.rsqrt(jnp.mean(o * o, axis=-1, keepdims=True) + LN_EPS) * bna_ref[...]
    mix = (jnp.dot(na.astype(BF16), wna_ref[...], preferred_element_type=F32)
           + jnp.dot(ret_ref[...], wret_ref[...], preferred_element_type=F32))
    y = DEEPNORM_ALPHA * x_ref[...] + gate_ref[...] * mix
    o_ref[...] = _standardize(y) * lg_ref[...] + lb_ref[...]


def _merge(x, o_na, ret, w_out, beta_na, gate, mod_row, ln_g, ln_b, tm):
    b, t, _ = x.shape
    tok = lambda bi, ti: (bi, ti, 0)
    row = pl.BlockSpec((1, D_MODEL), lambda bi, ti: (0, 0))
    half = pl.BlockSpec((None, tm, 512), tok)
    return pl.pallas_call(
        _merge_body,
        grid=(b, t // tm),
        in_specs=[pl.BlockSpec((None, tm, D_MODEL), tok), half, half,
                  pl.BlockSpec((512, D_MODEL), lambda bi, ti: (0, 0)),
                  pl.BlockSpec((512, D_MODEL), lambda bi, ti: (1, 0)),
                  pl.BlockSpec((1, 512), lambda bi, ti: (0, 0)),
                  pl.BlockSpec((None, 1, D_MODEL), lambda bi, ti: (mod_row(bi), 0, 0)),
                  row, row],
        out_specs=pl.BlockSpec((None, tm, D_MODEL), tok),
        out_shape=jax.ShapeDtypeStruct(x.shape, F32),
        compiler_params=_params("parallel", "parallel"),
        name="merge_ln1",
    )(x, o_na, ret, w_out, w_out, beta_na.reshape(1, 512), gate, ln_g.reshape(1, -1), ln_b.reshape(1, -1))


def _two_group_specs(n_p_blocks, blocks_per_batch_s, ctx_row):
    tok_p = lambda i: (jnp.minimum(i, n_p_blocks - 1), 0)
    tok_s = lambda i: (jnp.maximum(i - n_p_blocks, 0), 0)
    mod = lambda i: (jnp.where(i < n_p_blocks, ctx_row, jnp.maximum(i - n_p_blocks, 0) // blocks_per_batch_s), 0, 0)
    return tok_p, tok_s, mod


def _router_body(yp_ref, ys_ref, sh_ref, sc_ref, wh_ref, wl_ref, br_ref, p_ref, gt_ref, tab_ref, cnt_ref,
                 carry_s, *, n_p_blocks):
    i = pl.program_id(0)
    tm = yp_ref.shape[0]

    @pl.when(i == 0)
    def _():
        carry_s[...] = jnp.zeros_like(carry_s)

    y = jnp.where(i < n_p_blocks, yp_ref[...], ys_ref[...])
    h = y * (1.0 + sc_ref[...]) + sh_ref[...]
    h_hi = h.astype(BF16)
    h_lo = (h - h_hi.astype(F32)).astype(BF16)
    w_hi = wh_ref[...]
    lg = _dot_nt(jnp.concatenate([w_hi, wl_ref[...]], axis=0), h_hi)
    work = lg[:N_EXPERTS] + lg[N_EXPERTS:] + _dot_nt(w_hi, h_lo) + br_ref[...]
    eidx = lax.broadcasted_iota(I32, (N_EXPERTS, tm), 0).astype(F32)
    vals, idxs, hots = [], [], []
    for _ in range(TOP_K):
        mx = jnp.max(work, axis=0, keepdims=True)
        idx = jnp.min(jnp.where(work == mx, eidx, float(N_EXPERTS)), axis=0, keepdims=True)
        hot = eidx == idx
        vals.append(mx)
        idxs.append(idx)
        hots.append(hot)
        work = jnp.where(hot, -jnp.inf, work)
    exps = [jnp.exp(v - vals[0]) for v in vals]
    den = exps[0] + exps[1] + exps[2] + exps[3]
    sel = jnp.zeros((N_EXPERTS, tm), F32)
    for hot in hots:
        sel = sel + hot.astype(F32)
    r_i = lax.broadcasted_iota(I32, (tm, tm), 0)
    c_i = lax.broadcasted_iota(I32, (tm, tm), 1)
    earlier = (r_i < c_i).astype(BF16)
    before = jnp.dot(sel.astype(BF16), earlier, preferred_element_type=F32)
    n_e = jnp.sum(sel, axis=1, keepdims=True)
    n8 = jnp.floor((n_e + (RUN_ALIGN - 1.0)) * (1.0 / RUN_ALIGN)) * RUN_ALIGN + jnp.zeros((1, LANES), F32)
    e_r = lax.broadcasted_iota(I32, (N_EXPERTS, N_EXPERTS), 0)
    e_c = lax.broadcasted_iota(I32, (N_EXPERTS, N_EXPERTS), 1)
    off = jnp.dot((e_c < e_r).astype(BF16), n8.astype(BF16), preferred_element_type=F32)
    local = before + jnp.concatenate([off] * (tm // LANES), axis=1)
    rows = [jnp.sum(jnp.where(hot, local, 0.0), axis=0, keepdims=True) for hot in hots]
    p_ref[...] = jnp.concatenate(rows, axis=0).astype(I32)
    gt_ref[...] = jnp.concatenate([e / den for e in exps], axis=0)
    carry = carry_s[...]
    lane = lax.broadcasted_iota(I32, (N_EXPERTS, LANES), 1)
    tab_ref[...] = jnp.where(lane == 0, n8, jnp.where(lane == 1, carry, jnp.where(lane == 2, off, 0.0)))
    carry = carry + n8
    carry_s[...] = carry
    cnt_ref[...] = carry


def _router(yp, ys, shift, scale, ctx_row, s_blocks_per_batch, w_router, b_router, tm):
    n_p, n_s = yp.shape[0], ys.shape[0]
    n = n_p + n_s
    npb = n_p // tm
    tok_p, tok_s, mod = _two_group_specs(npb, s_blocks_per_batch, ctx_row)
    w_t = w_router.T
    w_hi = w_t.astype(BF16)
    w_lo = (w_t - w_hi.astype(F32)).astype(BF16)
    br = jnp.broadcast_to(b_router[:, None], (N_EXPERTS, tm))
    o4 = pl.BlockSpec((TOP_K, tm), lambda i: (0, i))
    whole = lambda shape: pl.BlockSpec(shape, lambda i: (0, 0))
    return pl.pallas_call(
        functools.partial(_router_body, n_p_blocks=npb),
        grid=(n // tm,),
        in_specs=[pl.BlockSpec((tm, D_MODEL), tok_p), pl.BlockSpec((tm, D_MODEL), tok_s),
                  pl.BlockSpec((None, 1, D_MODEL), mod), pl.BlockSpec((None, 1, D_MODEL), mod),
                  whole((N_EXPERTS, D_MODEL)), whole((N_EXPERTS, D_MODEL)), whole((N_EXPERTS, tm))],
        out_specs=[o4, o4, pl.BlockSpec((None, N_EXPERTS, LANES), lambda i: (i, 0, 0)),
                   whole((N_EXPERTS, LANES))],
        out_shape=[jax.ShapeDtypeStruct((TOP_K, n), I32), jax.ShapeDtypeStruct((TOP_K, n), F32),
                   jax.ShapeDtypeStruct((n // tm, N_EXPERTS, LANES), F32),
                   jax.ShapeDtypeStruct((N_EXPERTS, LANES), F32)],
        scratch_shapes=[pltpu.VMEM((N_EXPERTS, LANES), F32)],
        compiler_params=_params("arbitrary"),
        name="moe_router",
    )(yp, ys, shift, scale, w_hi, w_lo, br)


def _for_run_pieces(n_rows, visit):
    off = 0
    for size in RUN_PIECES:
        present = (n_rows & size) != 0
        visit(off, size, present)
        off = off + jnp.where(present, size, 0)


def _block_runs(blk, n_ref, loc_ref, glob_ref, local_buf, sorted_hbm, sem, to_hbm, wait):
    def act(off_local, off_global, size):
        loc = local_buf.at[pl.ds(pl.multiple_of(off_local, RUN_ALIGN), size)]
        glob = sorted_hbm.at[pl.ds(pl.multiple_of(off_global, RUN_ALIGN), size)]
        copy = pltpu.make_async_copy(loc, glob, sem) if to_hbm else pltpu.make_async_copy(glob, loc, sem)
        if wait:
            copy.wait()
        else:
            copy.start()

    def per_expert(e, carry):
        idx = blk * N_EXPERTS + e
        n = n_ref[idx]
        loc0 = loc_ref[idx]
        glob0 = glob_ref[idx]
        n_chunks = lax.shift_right_logical(n, RUN_CHUNK.bit_length() - 1)

        def chunk(j, c2):
            act(loc0 + j * RUN_CHUNK, glob0 + j * RUN_CHUNK, RUN_CHUNK)
            return c2

        lax.fori_loop(0, n_chunks, chunk, 0)
        off = n_chunks * RUN_CHUNK
        size = RUN_CHUNK // 2
        while size >= RUN_ALIGN:
            present = (n & size) != 0

            @pl.when(present)
            def _():
                act(loc0 + off, glob0 + off, size)

            off = off + jnp.where(present, size, 0)
            size //= 2
        return carry

    lax.fori_loop(0, N_EXPERTS, per_expert, 0)


def _dispatch_body(n_ref, loc_ref, glob_ref, zlo_ref, zn_ref, p_ref, yp_ref, ys_ref, sh_ref, sc_ref, xs_ref,
                   xbuf, zbuf, sems, zsem, *, n_p_blocks, n_blocks):
    i = pl.program_id(0)
    slot = i % 2
    y = jnp.where(i < n_p_blocks, yp_ref[...], ys_ref[...])
    h = (y * (1.0 + sc_ref[...]) + sh_ref[...]).astype(BF16)
    rows = p_ref[...]
    r_iota = lax.broadcasted_iota(I32, (LOCAL_ROWS, rows.shape[1]), 0)
    place = r_iota == rows[0:1]
    for kk in range(1, TOP_K):
        place = place | (r_iota == rows[kk:kk + 1])
    xbuf[slot] = jnp.dot(jnp.where(place, 1.0, 0.0).astype(BF16), h, preferred_element_type=F32)

    runs = functools.partial(_block_runs, n_ref=n_ref, loc_ref=loc_ref, glob_ref=glob_ref,
                             sorted_hbm=xs_ref, to_hbm=True)
    runs(i, local_buf=xbuf.at[slot], sem=sems.at[slot], wait=False)

    @pl.when(i > 0)
    def _():
        runs(i - 1, local_buf=xbuf.at[1 - slot], sem=sems.at[1 - slot], wait=True)

    @pl.when(i == n_blocks - 1)
    def _():
        runs(i, local_buf=xbuf.at[slot], sem=sems.at[slot], wait=True)
        zbuf[...] = jnp.zeros_like(zbuf)
        sem = zsem

        def per_expert(e, carry):
            lo = zlo_ref[e]

            def visit(off, size, present):
                copy = pltpu.make_async_copy(
                    zbuf.at[pl.ds(0, size)], xs_ref.at[pl.ds(pl.multiple_of(lo + off, RUN_ALIGN), size)], sem)

                @pl.when(present)
                def _():
                    copy.start()
                    copy.wait()

            _for_run_pieces(zn_ref[e], visit)
            return carry

        lax.fori_loop(0, N_EXPERTS, per_expert, 0)

        tail_lo = zlo_ref[N_EXPERTS]

        def tile_copy(t):
            start = pl.multiple_of(tail_lo + t * EXPERT_TILE, EXPERT_TILE)
            return pltpu.make_async_copy(zbuf, xs_ref.at[pl.ds(start, EXPERT_TILE)], sem)

        def t_issue(t, c2):
            tile_copy(t).start()
            return c2

        def t_drain(t, c2):
            tile_copy(0).wait()
            return c2

        n_tail = zn_ref[N_EXPERTS] // EXPERT_TILE
        lax.fori_loop(0, n_tail, t_issue, 0)
        lax.fori_loop(0, n_tail, t_drain, 0)


def _dispatch(yp, ys, shift, scale, ctx_row, s_blocks_per_batch, local_rows, run_n, run_local, run_global,
              zero_lo, zero_n, cap_rows, tm):
    n_p, n_s = yp.shape[0], ys.shape[0]
    n = n_p + n_s
    npb = n_p // tm
    nb = n // tm
    tok_p, tok_s, mod = _two_group_specs(npb, s_blocks_per_batch, ctx_row)
    smem_all = pl.BlockSpec(memory_space=pltpu.SMEM)
    return pl.pallas_call(
        functools.partial(_dispatch_body, n_p_blocks=npb, n_blocks=nb),
        grid=(nb,),
        in_specs=[smem_all] * 5 + [
            pl.BlockSpec((TOP_K, tm), lambda i: (0, i)),
            pl.BlockSpec((tm, D_MODEL), tok_p), pl.BlockSpec((tm, D_MODEL), tok_s),
            pl.BlockSpec((None, 1, D_MODEL), mod), pl.BlockSpec((None, 1, D_MODEL), mod)],
        out_specs=pl.BlockSpec(memory_space=pl.ANY),
        out_shape=jax.ShapeDtypeStruct((cap_rows, D_MODEL), F32),
        scratch_shapes=[pltpu.VMEM((2, LOCAL_ROWS, D_MODEL), F32),
                        pltpu.VMEM((EXPERT_TILE, D_MODEL), F32),
                        pltpu.SemaphoreType.DMA((2,)), pltpu.SemaphoreType.DMA(())],
        compiler_params=_params("arbitrary"),
        name="moe_dispatch",
    )(run_n, run_local, run_global, zero_lo, zero_n, local_rows, yp, ys, shift, scale)


def _expert_body(te_ref, first_ref, nu_ref, x_ref, wgu_ref, bgu_ref, wd_ref, bd_ref, y_ref, act_s, wgu_s, wd_s):
    j = pl.program_id(0)

    @pl.when(first_ref[j] == 1)
    def _():
        blk = 2 * LANES
        src = lax.broadcasted_iota(I32, (blk, blk), 0)
        dst = lax.broadcasted_iota(I32, (blk, blk), 1)
        perm = (src == jnp.where(dst < LANES, 2 * dst, 2 * (dst - LANES) + 1)).astype(F32).astype(BF16)
        for s in range(2 * D_FF // blk):
            sl = slice(s * blk, (s + 1) * blk)
            wgu_s[:, sl] = jnp.dot(wgu_ref[:, sl].astype(BF16), perm, preferred_element_type=F32).astype(BF16)
        wd_s[...] = wd_ref[...].astype(BF16)

    @pl.when(j < nu_ref[0])
    def _():
        x = x_ref[...].astype(BF16)
        for c in range(D_FF // LANES):
            sl = slice(c * 2 * LANES, (c + 1) * 2 * LANES)
            hu = jnp.dot(x, wgu_s[:, sl], preferred_element_type=F32) + bgu_ref[:, sl]
            x_glu = jnp.minimum(hu[:, :LANES], SWIGLU_LIMIT)
            x_lin = jnp.clip(hu[:, LANES:], -SWIGLU_LIMIT, SWIGLU_LIMIT)
            act = x_glu * (1.0 / (1.0 + jnp.exp(-SWIGLU_ALPHA * x_glu))) * (x_lin + 1.0)
            act_s[:, c * LANES:(c + 1) * LANES] = act.astype(BF16)
        y_ref[...] = jnp.dot(act_s[...], wd_s[...], preferred_element_type=F32) + bd_ref[...]

    @pl.when(j >= nu_ref[0])
    def _():
        y_ref[...] = jnp.zeros_like(y_ref)


def _experts(tile_expert, tile_first, n_used, xs, w_gate_up, bgu, w_down, bd):
    cap_rows = xs.shape[0]
    n_tiles = cap_rows // EXPERT_TILE
    of_expert = lambda j, te, first, nu: (te[j], 0, 0)
    grid_spec = pltpu.PrefetchScalarGridSpec(
        num_scalar_prefetch=3,
        grid=(n_tiles,),
        in_specs=[pl.BlockSpec((EXPERT_TILE, D_MODEL), lambda j, te, first, nu: (jnp.minimum(j, nu[0] - 1), 0)),
                  pl.BlockSpec((None, D_MODEL, 2 * D_FF), of_expert),
                  pl.BlockSpec((None, 1, 2 * D_FF), of_expert),
                  pl.BlockSpec((None, D_FF, D_MODEL), of_expert),
                  pl.BlockSpec((None, 1, D_MODEL), of_expert)],
        out_specs=pl.BlockSpec((EXPERT_TILE, D_MODEL), lambda j, te, first, nu: (j, 0)),
        scratch_shapes=[pltpu.VMEM((EXPERT_TILE, D_FF), BF16),
                        pltpu.VMEM((D_MODEL, 2 * D_FF), BF16), pltpu.VMEM((D_FF, D_MODEL), BF16)],
    )
    return pl.pallas_call(
        _expert_body,
        grid_spec=grid_spec,
        out_shape=jax.ShapeDtypeStruct((cap_rows, D_MODEL), F32),
        compiler_params=_params("arbitrary"),
        name="moe_experts",
    )(tile_expert, tile_first, n_used, xs, w_gate_up, bgu, w_down, bd)


def _combine_body(n_ref, loc_ref, glob_ref, y1_ref, p_ref, gt_ref, gate_ref, lg_ref, lb_ref, ys_ref, o_ref,
                  ybuf, sems, *, block0, n_blocks):
    i = pl.program_id(0)
    slot = i % 2
    tm = y1_ref.shape[0]
    runs = functools.partial(_block_runs, n_ref=n_ref, loc_ref=loc_ref, glob_ref=glob_ref,
                             sorted_hbm=ys_ref, to_hbm=False)

    @pl.when(i == 0)
    def _():
        ybuf[...] = jnp.zeros_like(ybuf)
        runs(block0, local_buf=ybuf.at[0], sem=sems.at[0], wait=False)

    @pl.when(i + 1 < n_blocks)
    def _():
        runs(block0 + i + 1, local_buf=ybuf.at[1 - slot], sem=sems.at[1 - slot], wait=False)

    runs(block0 + i, local_buf=ybuf.at[slot], sem=sems.at[slot], wait=True)

    rows = p_ref[...]
    gates = gt_ref[...]
    c_iota = lax.broadcasted_iota(I32, (tm, LOCAL_ROWS), 1)
    weight = jnp.where(c_iota == rows[:, 0:1], gates[:, 0:1], 0.0)
    for kk in range(1, TOP_K):
        weight = weight + jnp.where(c_iota == rows[:, kk:kk + 1], gates[:, kk:kk + 1], 0.0)
    w_hi = weight.astype(BF16)
    w_lo = (weight - w_hi.astype(F32)).astype(BF16)
    yb = ybuf[slot].astype(BF16)
    f = (jnp.dot(w_hi, yb, preferred_element_type=F32) + jnp.dot(w_lo, yb, preferred_element_type=F32))
    y = DEEPNORM_ALPHA * y1_ref[...] + gate_ref[...] * f
    o_ref[...] = _standardize(y) * lg_ref[...] + lb_ref[...]


def _combine(y1, local_rows, gates, run_n, run_local, run_global, block0, gate_mod, mod_row, blocks_per_batch,
             y_sorted, ln_g, ln_b, tm):
    n_g = y1.shape[0]
    nb = n_g // tm
    row = pl.BlockSpec((1, D_MODEL), lambda i: (0, 0))
    smem_all = pl.BlockSpec(memory_space=pltpu.SMEM)
    pair = pl.BlockSpec((tm, TOP_K), lambda i: (i + block0, 0))
    return pl.pallas_call(
        functools.partial(_combine_body, block0=block0, n_blocks=nb),
        grid=(nb,),
        in_specs=[smem_all] * 3 + [
            pl.BlockSpec((tm, D_MODEL), lambda i: (i, 0)), pair, pair,
            pl.BlockSpec((None, 1, D_MODEL), lambda i: (mod_row(i // blocks_per_batch), 0, 0)),
            row, row,
            pl.BlockSpec(memory_space=pl.ANY)],
        out_specs=pl.BlockSpec((tm, D_MODEL), lambda i: (i, 0)),
        out_shape=jax.ShapeDtypeStruct((n_g, D_MODEL), F32),
        scratch_shapes=[pltpu.VMEM((2, LOCAL_ROWS, D_MODEL), F32), pltpu.SemaphoreType.DMA((2,))],
        compiler_params=_params("arbitrary"),
        name="moe_combine_ln2",
    )(run_n, run_local, run_global, y1, local_rows, gates, gate_mod,
      ln_g.reshape(1, -1), ln_b.reshape(1, -1), y_sorted)


def _regroup_gate_up_bias(b_gate_up):
    e = b_gate_up.shape[0]
    nblk = D_FF // LANES
    return b_gate_up.reshape(e, 1, nblk, LANES, 2).transpose(0, 1, 2, 4, 3).reshape(e, 1, 2 * D_FF)


def _routing_plan(table, totals, n_pairs):
    nb = table.shape[0]
    run_n = table[:, :, 0].astype(I32)
    rows_before = table[:, :, 1].astype(I32)
    run_local = table[:, :, 2].astype(I32)
    cnt = totals[:, 0].astype(I32)
    padded = (cnt + EXPERT_TILE - 1) // EXPERT_TILE * EXPERT_TILE
    pad_end = jnp.cumsum(padded)
    pad_start = pad_end - padded
    run_global = pad_start[None, :] + rows_before
    rows_max = n_pairs + nb * N_EXPERTS * (RUN_ALIGN - 1) + N_EXPERTS * (EXPERT_TILE - 1)
    n_tiles = -(-rows_max // EXPERT_TILE)
    n_used = pad_end[-1] // EXPERT_TILE
    tile_row = jnp.minimum(jnp.arange(n_tiles, dtype=I32), n_used - 1) * EXPERT_TILE
    tile_expert = jnp.sum((pad_end[None, :] <= tile_row[:, None]).astype(I32), axis=1)
    tile_expert = jnp.minimum(tile_expert, N_EXPERTS - 1)
    tile_id = jnp.arange(n_tiles, dtype=I32)
    changed = jnp.concatenate([jnp.ones((1,), bool), tile_expert[1:] != tile_expert[:-1]])
    tile_first = (changed & (tile_id < n_used)).astype(I32)
    cap_rows = n_tiles * EXPERT_TILE
    zero_lo = jnp.concatenate([pad_start + cnt, pad_end[-1:]])
    zero_n = jnp.concatenate([padded - cnt, cap_rows - pad_end[-1:]])
    runs = (run_n.reshape(-1), run_local.reshape(-1), run_global.reshape(-1))
    return runs, (tile_expert, tile_first, n_used.reshape(1)), zero_lo, zero_n, cap_rows


def kernel(x_prompt, x_sample, cache_na_k, cache_na_v, state_ret_fwd, state_ret_bwd, c, c_ctx, w_ada, b_ada, w_in, rpb, ret_decay_fwd, ret_decay_bwd, beta_na, beta_ret, w_out, ln1_g, ln1_b, w_router, b_router, w_gate_up, b_gate_up, w_down, b_down, ln2_g, ln2_b):
    bp, tp, _ = x_prompt.shape
    bs, ts, _ = x_sample.shape
    assert w_ada.shape[0] == DEPTH == 1
    l = 0
    ctx_row = bs
    sample_row = lambda bi: bi
    prompt_row = lambda bi: ctx_row

    cond = jnp.concatenate([c, c_ctx[None, :], jnp.zeros((16 - bs - 1, D_MODEL), F32)], axis=0)
    m = _adaln(cond, w_ada[l], b_ada[l]).reshape(16, 6, 1, D_MODEL)
    sh_a, sc_a, gt_a, sh_f, sc_f, gt_f = (m[:, i] for i in range(6))

    w_in_b = w_in[l].astype(BF16)
    w_out_b = w_out[l].astype(BF16)

    q, k_p, v_p, qr, kr, vr, g = _project(x_prompt, sh_a, sc_a, prompt_row, w_in_b, tp, F32, rope=False)
    o_na = _context_attention(q, k_p, v_p)
    zeros_state = jnp.zeros((bp, RET_HEADS, RET_DK, RET_DK), F32)
    ret, s_f, s_b = _retention(qr, kr, vr, g, ret_decay_fwd[l], ret_decay_bwd[l], beta_ret[l],
                               zeros_state, zeros_state)
    yp1 = _merge(x_prompt, o_na, ret, w_out_b, beta_na[l], gt_a, prompt_row, ln1_g[l], ln1_b[l], tp)

    q, k_s, v_s, qr, kr, vr, g = _project(x_sample, sh_a, sc_a, sample_row, w_in_b, 512, BF16, rope=True)
    lc = cache_na_k.shape[2]
    o_na = _neighbourhood_attention(q, k_s, v_s, cache_na_k[:, l].reshape(bs, lc, NA_WIDTH),
                                    cache_na_v[:, l].reshape(bs, lc, NA_WIDTH), _na_bias(rpb[l]))
    ret, _, _ = _retention(qr, kr, vr, g, ret_decay_fwd[l], ret_decay_bwd[l], beta_ret[l],
                           state_ret_fwd[:, l], state_ret_bwd[:, l])
    ys1 = _merge(x_sample, o_na, ret, w_out_b, beta_na[l], gt_a, sample_row, ln1_g[l], ln1_b[l], 512)

    tm = TOKEN_BLOCK
    yp1f = yp1.reshape(bp * tp, D_MODEL)
    ys1f = ys1.reshape(bs * ts, D_MODEL)
    local_rows, gates, table, totals = _router(yp1f, ys1f, sh_f, sc_f, ctx_row, ts // tm,
                                               w_router[l], b_router[l], tm)
    runs, tiles, zero_lo, zero_n, cap_rows = _routing_plan(table, totals, (bp * tp + bs * ts) * TOP_K)
    xs = _dispatch(yp1f, ys1f, sh_f, sc_f, ctx_row, ts // tm, local_rows, *runs, zero_lo, zero_n, cap_rows, tm)
    y_sorted = _experts(*tiles, xs, w_gate_up[l], _regroup_gate_up_bias(b_gate_up[l]), w_down[l],
                        b_down[l][:, None, :])
    rows_t, gates_t = local_rows.T, gates.T
    yp = _combine(yp1f, rows_t, gates_t, *runs, 0, gt_f, prompt_row, 1, y_sorted, ln2_g[l], ln2_b[l], tm)
    ys = _combine(ys1f, rows_t, gates_t, *runs, (bp * tp) // tm, gt_f, sample_row, ts // tm, y_sorted,
                  ln2_g[l], ln2_b[l], tm)

    return (yp.reshape(bp, tp, D_MODEL), ys.reshape(bs, ts, D_MODEL),
            k_p.reshape(bp, 1, tp, NA_HEADS, NA_DIM), v_p.reshape(bp, 1, tp, NA_HEADS, NA_DIM),
            s_f[:, None], s_b[:, None])
```

```python
import functools

import jax
import jax.numpy as jnp
from jax import lax
from jax.experimental import pallas as pl
from jax.experimental.pallas import tpu as pltpu

F32 = jnp.float32
BF16 = jnp.bfloat16
I32 = jnp.int32

D_MODEL = 1024
GRID_W = 64
NA_HEADS = 8
NA_DIM = 64
NA_WIDTH = NA_HEADS * NA_DIM
NA_KH = 8
NA_KW = 16
RET_HEADS = 4
RET_DK = 128
RET_WIDTH = RET_HEADS * RET_DK
CHUNK = 128
N_EXPERTS = 32
TOP_K = 4
D_FF = 1024
SWIGLU_LIMIT = 7.0
SWIGLU_ALPHA = 1.702
ROPE_BASE = 10000.0
LN_EPS = 1e-5
DEPTH = 1
DEEPNORM_ALPHA = (2.0 * DEPTH) ** 0.25

LANES = 128
NEG_BIG = -1e30
TOKEN_BLOCK = 256
EXPERT_TILE = 512
VMEM_LIMIT = 56 * 1024 * 1024
RET_CHUNK_UNROLL = 8
SUBLANES = 8
RUN_ALIGN = SUBLANES
LOCAL_ROWS = TOKEN_BLOCK * TOP_K + N_EXPERTS * RUN_ALIGN
LOCAL_UNITS = LOCAL_ROWS // RUN_ALIGN
UNIT_TABLE = 256
RUN_PIECES = tuple(TOKEN_BLOCK >> s for s in range(TOKEN_BLOCK.bit_length())
                   if (TOKEN_BLOCK >> s) >= RUN_ALIGN)


def _params(*sem):
    return pltpu.CompilerParams(dimension_semantics=sem, vmem_limit_bytes=VMEM_LIMIT)


def _silu(x):
    return x / (1.0 + jnp.exp(-x))


def _standardize(x):
    mu = jnp.mean(x, axis=-1, keepdims=True)
    xc = x - mu
    var = jnp.mean(xc * xc, axis=-1, keepdims=True)
    return xc * lax.rsqrt(var + LN_EPS)


def _dot_nt(a, b):
    return lax.dot_general(a, b, (((1,), (1,)), ((), ())), preferred_element_type=F32)


def _dot_tn(a, b):
    return lax.dot_general(a, b, (((0,), (0,)), ((), ())), preferred_element_type=F32)


def _ada_body(c_ref, w_ref, b_ref, o_ref):
    s = _silu(c_ref[...])
    o_ref[...] = jnp.dot(s, w_ref[...], preferred_element_type=F32,
                         precision=lax.Precision.HIGHEST) + b_ref[...]


def _adaln(cond, w_ada, b_ada):
    r = cond.shape[0]
    n = w_ada.shape[1]
    tn = 1536
    return pl.pallas_call(
        _ada_body,
        grid=(n // tn,),
        in_specs=[pl.BlockSpec((r, D_MODEL), lambda j: (0, 0)),
                  pl.BlockSpec((D_MODEL, tn), lambda j: (0, j)),
                  pl.BlockSpec((1, tn), lambda j: (0, j))],
        out_specs=pl.BlockSpec((r, tn), lambda j: (0, j)),
        out_shape=jax.ShapeDtypeStruct((r, n), F32),
        compiler_params=_params("arbitrary"),
        name="adaln",
    )(cond, w_ada, b_ada.reshape(1, n))


def _proj_body(*refs, rope):
    if rope:
        (x_ref, sh_ref, sc_ref, w_ref, cos_ref, sa_ref, sb_ref,
         q_ref, k_ref, v_ref, qr_ref, kr_ref, vr_ref, g_ref) = refs
    else:
        (x_ref, sh_ref, sc_ref, w_ref,
         q_ref, k_ref, v_ref, qr_ref, kr_ref, vr_ref, g_ref) = refs
    h = (x_ref[...] * (1.0 + sc_ref[...]) + sh_ref[...]).astype(BF16)

    def cols(c):
        return jnp.dot(h, w_ref[:, c * 512:(c + 1) * 512], preferred_element_type=F32)

    q_ref[...] = cols(0).astype(q_ref.dtype)
    k_ref[...] = cols(1).astype(k_ref.dtype)
    v_ref[...] = cols(2).astype(v_ref.dtype)
    pq = cols(3)
    pk = cols(4) * (RET_DK ** -0.5)
    if rope:
        cs, sa, sb = cos_ref[...], sa_ref[...], sb_ref[...]
        for hd in range(RET_HEADS):
            sl = slice(hd * LANES, (hd + 1) * LANES)
            for p, o_ref in ((pq, qr_ref), (pk, kr_ref)):
                xs = p[:, sl]
                rot = xs * cs + pltpu.roll(xs, 96, 1) * sa + pltpu.roll(xs, 32, 1) * sb
                o_ref[:, sl] = rot.astype(o_ref.dtype)
    else:
        qr_ref[...] = pq.astype(qr_ref.dtype)
        kr_ref[...] = pk.astype(kr_ref.dtype)
    vr_ref[...] = cols(5).astype(vr_ref.dtype)
    g_ref[...] = cols(6)


def _rope_tables(t_len):
    t = jnp.arange(t_len, dtype=jnp.int32)
    pos_row = (t // GRID_W).astype(F32)[:, None]
    pos_col = (t % GRID_W).astype(F32)[:, None]
    lane = jnp.arange(LANES, dtype=jnp.int32)[None, :]
    n_freq = RET_DK // 4
    inv_freq = ROPE_BASE ** (-(lane % n_freq).astype(F32) / n_freq)
    ang = jnp.where(lane < RET_DK // 2, pos_row, pos_col) * inv_freq
    first = (lane % (2 * n_freq)) < n_freq
    cos = jnp.cos(ang)
    sin = jnp.sin(ang)
    return cos, jnp.where(first, -sin, 0.0), jnp.where(first, 0.0, sin)


def _project(x, shift, scale, mod_row, w_in, tm, kv_dtype, rope):
    b, t, _ = x.shape
    tok = lambda bi, ti: (bi, ti, 0)
    mod = lambda bi, ti: (mod_row(bi), 0, 0)
    in_specs = [pl.BlockSpec((None, tm, D_MODEL), tok),
                pl.BlockSpec((None, 1, D_MODEL), mod),
                pl.BlockSpec((None, 1, D_MODEL), mod),
                pl.BlockSpec(w_in.shape, lambda bi, ti: (0, 0))]
    args = [x, shift, scale, w_in]
    if rope:
        tab = lambda bi, ti: (ti, 0)
        in_specs += [pl.BlockSpec((tm, LANES), tab)] * 3
        args += list(_rope_tables(t))
    o512 = pl.BlockSpec((None, tm, 512), tok)
    shp = lambda dt: jax.ShapeDtypeStruct((b, t, 512), dt)
    return pl.pallas_call(
        functools.partial(_proj_body, rope=rope),
        grid=(b, t // tm),
        in_specs=in_specs,
        out_specs=[o512] * 7,
        out_shape=[shp(BF16), shp(kv_dtype), shp(kv_dtype), shp(BF16), shp(BF16), shp(BF16), shp(F32)],
        compiler_params=_params("parallel", "parallel"),
        name="in_proj",
    )(*args)


def _pair_masks(shape):
    lane = lax.broadcasted_iota(I32, shape, 1)
    return lane < NA_DIM, lane >= NA_DIM


def _ctx_attn_body(q_ref, k_ref, v_ref, o_ref):
    q2 = q_ref[...]
    k2 = k_ref[...].astype(BF16)
    v2 = v_ref[...].astype(BF16)
    t = q2.shape[0]
    lo, hi = _pair_masks(q2.shape)
    zero = jnp.zeros_like(q2)
    qs = jnp.concatenate([jnp.where(lo, q2, zero), jnp.where(hi, q2, zero)], axis=0)
    s = _dot_nt(qs, k2) * (NA_DIM ** -0.5)
    e = jnp.exp(s - jnp.max(s, axis=-1, keepdims=True))
    den = jnp.sum(e, axis=-1, keepdims=True)
    o = jnp.dot(e.astype(BF16), v2, preferred_element_type=F32) / den
    o_ref[...] = jnp.where(lo, o[:t], o[t:]).astype(o_ref.dtype)


def _context_attention(q, k, v):
    b, t, _ = q.shape
    spec = pl.BlockSpec((None, t, LANES), lambda bi, hp: (bi, 0, hp))
    return pl.pallas_call(
        _ctx_attn_body,
        grid=(b, NA_WIDTH // LANES),
        in_specs=[spec] * 3,
        out_specs=spec,
        out_shape=jax.ShapeDtypeStruct((b, t, NA_WIDTH), BF16),
        compiler_params=_params("parallel", "parallel"),
        name="ctx_attn",
    )(q, k, v)


def _na_bias_body(rpb_ref, o_ref):
    h = pl.program_id(0)
    v = pl.program_id(1)
    shape = (GRID_W, LANES)
    w = lax.broadcasted_iota(I32, shape, 0)
    lane = lax.broadcasted_iota(I32, shape, 1)
    upper = lane >= GRID_W
    kc = jnp.where(upper, lane - GRID_W, lane)
    cdiff = kc - w + (NA_KW - 1)
    cstart = jnp.clip(w - NA_KW // 2, 0, GRID_W - NA_KW)
    inwin = (kc >= cstart) & (kc < cstart + NA_KW)
    n_ro = 2 * NA_KH - 1
    n_co = 2 * NA_KW - 1
    for j in range(NA_KH // 2):
        ro = 2 * j - v + (NA_KH - 1)
        base0 = (h * n_ro + ro) * n_co
        acc = jnp.zeros(shape, F32)
        for c in range(n_co):
            val = jnp.where(upper, rpb_ref[base0 + n_co + c], rpb_ref[base0 + c])
            acc = jnp.where(cdiff == c, val, acc)
        o_ref[:, j * LANES:(j + 1) * LANES] = jnp.where(inwin, acc, NEG_BIG)


def _na_bias(rpb):
    n = NA_KH * GRID_W
    return pl.pallas_call(
        _na_bias_body,
        grid=(NA_HEADS, NA_KH),
        in_specs=[pl.BlockSpec(memory_space=pltpu.SMEM)],
        out_specs=pl.BlockSpec((None, None, GRID_W, n), lambda h, v: (h, v, 0, 0)),
        out_shape=jax.ShapeDtypeStruct((NA_HEADS, NA_KH, GRID_W, n), F32),
        compiler_params=_params("parallel", "parallel"),
        name="na_bias",
    )(rpb.reshape(-1))


def _na_body(q_ref, k_ref, v_ref, ck_ref, cv_ref, bias_ref, o_ref, s_buf, e_buf, d_buf, *, rows):
    ck = ck_ref[...].astype(BF16)
    cv = cv_ref[...].astype(BF16)
    lo, hi = _pair_masks((GRID_W, LANES))
    scale = NA_DIM ** -0.5
    n_loc = NA_KH * GRID_W

    def window(r):
        r = jnp.clip(r, 0, rows - 1)
        r_start = jnp.clip(r - NA_KH // 2, 0, rows - NA_KH)
        return (r - r_start, pl.ds(pl.multiple_of(r * GRID_W, GRID_W), GRID_W),
                pl.ds(pl.multiple_of(r_start * GRID_W, GRID_W), n_loc))

    def scores(r, slot):
        vidx, qsl, ksl = window(r)
        q2 = q_ref[qsl, :]
        zero = jnp.zeros_like(q2)
        q2 = q2 * scale
        qs = jnp.concatenate([jnp.where(lo, q2, zero), jnp.where(hi, q2, zero)], axis=0)
        bias = jnp.concatenate([bias_ref[0, vidx], bias_ref[1, vidx]], axis=0)
        s_buf[slot, :, :n_loc] = _dot_nt(qs, k_ref[ksl, :]) + bias
        s_buf[slot, :, n_loc:] = _dot_nt(qs, ck)

    def numerators(slot):
        s = s_buf[slot]
        e = jnp.exp(s - jnp.max(s, axis=-1, keepdims=True))
        d_buf[slot] = jnp.broadcast_to(jnp.sum(e, axis=-1, keepdims=True), d_buf.shape[1:])
        e_buf[slot] = e.astype(BF16)

    def output(r, slot):
        _, qsl, ksl = window(r)
        o = (jnp.dot(e_buf[slot, :, :n_loc], v_ref[ksl, :], preferred_element_type=F32)
             + jnp.dot(e_buf[slot, :, n_loc:], cv, preferred_element_type=F32)) / d_buf[slot]
        o_ref[qsl, :] = jnp.where(lo, o[:GRID_W], o[GRID_W:]).astype(o_ref.dtype)

    s_buf[...] = jnp.zeros_like(s_buf)
    e_buf[...] = jnp.zeros_like(e_buf)
    d_buf[...] = jnp.ones_like(d_buf)

    def two_steps(j, carry):
        r = 2 * j
        scores(r, 0)
        numerators(1)
        output(r - 2, 0)
        scores(r + 1, 1)
        numerators(0)
        output(r - 1, 1)
        return carry

    lax.fori_loop(0, rows // 2 + 1, two_steps, 0)


def _neighbourhood_attention(q, k, v, ck, cv, bias):
    b, t, _ = q.shape
    lc = ck.shape[1]
    rows = t // GRID_W
    tok = pl.BlockSpec((None, t, LANES), lambda hp, bi: (bi, 0, hp))
    ctx = pl.BlockSpec((None, lc, LANES), lambda hp, bi: (bi, 0, hp))
    return pl.pallas_call(
        functools.partial(_na_body, rows=rows),
        grid=(NA_WIDTH // LANES, b),
        in_specs=[tok, tok, tok, ctx, ctx,
                  pl.BlockSpec((2, NA_KH, GRID_W, NA_KH * GRID_W), lambda hp, bi: (hp, 0, 0, 0))],
        out_specs=tok,
        out_shape=jax.ShapeDtypeStruct((b, t, NA_WIDTH), BF16),
        scratch_shapes=[pltpu.VMEM((2, 2 * GRID_W, NA_KH * GRID_W + lc), F32),
                        pltpu.VMEM((2, 2 * GRID_W, NA_KH * GRID_W + lc), BF16),
                        pltpu.VMEM((2, 2 * GRID_W, LANES), F32)],
        compiler_params=_params("parallel", "parallel"),
        name="na_attn",
    )(q, k, v, ck, cv, bias)


def _ret_body(q_ref, k_ref, v_ref, g_ref, df_ref, db_ref, beta_ref, s0f_ref, s0b_ref,
              o_ref, sf_ref, sb_ref, kvf_s, kvb_s, *, n_chunks):
    c_len = CHUNK
    lgf = -jnp.log1p(jnp.exp(-df_ref[...]))
    lgb = -jnp.log1p(jnp.exp(-db_ref[...]))
    shape = (c_len, c_len)
    i = lax.broadcasted_iota(I32, shape, 0).astype(F32)
    j = lax.broadcasted_iota(I32, shape, 1).astype(F32)
    dij = i - j
    d_comb = (jnp.where(dij >= 0, jnp.exp(jnp.maximum(dij, 0.0) * lgf), 0.0)
              + jnp.where(dij <= 0, jnp.exp(jnp.maximum(-dij, 0.0) * lgb), 0.0))
    xi_f = jnp.exp((i + 1.0) * lgf)
    xi_b = jnp.exp((c_len - i) * lgb)
    zeta_f = jnp.exp((c_len - 1.0 - i) * lgf)
    zeta_b = jnp.exp(i * lgb)
    g_f = jnp.exp(c_len * lgf)
    g_b = jnp.exp(c_len * lgb)

    def chunk(c):
        return pl.ds(pl.multiple_of(c * c_len, c_len), c_len)

    def kv_pass(c, carry):
        kc = k_ref[chunk(c), :]
        vc = v_ref[chunk(c), :].astype(F32)
        kvf_s[c] = _dot_tn(kc, (vc * zeta_f).astype(BF16))
        kvb_s[c] = _dot_tn(kc, (vc * zeta_b).astype(BF16))
        return carry

    lax.fori_loop(0, n_chunks, kv_pass, 0, unroll=RET_CHUNK_UNROLL)

    def scan_f(c, s):
        kv = kvf_s[c]
        kvf_s[c] = s
        return g_f * s + kv

    def scan_b(ci, s):
        c = n_chunks - 1 - ci
        kv = kvb_s[c]
        kvb_s[c] = s
        return g_b * s + kv

    sf_ref[...] = lax.fori_loop(0, n_chunks, scan_f, s0f_ref[...])
    sb_ref[...] = lax.fori_loop(0, n_chunks, scan_b, s0b_ref[...])

    beta = beta_ref[...]

    def out_pass(c, carry):
        qc = q_ref[chunk(c), :]
        kc = k_ref[chunk(c), :]
        vc = v_ref[chunk(c), :]
        scores = _dot_nt(qc, kc) * d_comb
        y = jnp.dot(scores.astype(BF16), vc, preferred_element_type=F32)
        qf = qc.astype(F32)
        y += jnp.dot((qf * xi_f).astype(BF16), kvf_s[c].astype(BF16), preferred_element_type=F32)
        y += jnp.dot((qf * xi_b).astype(BF16), kvb_s[c].astype(BF16), preferred_element_type=F32)
        o_ref[chunk(c), :] = (_standardize(y) * beta * _silu(g_ref[chunk(c), :])).astype(o_ref.dtype)
        return carry

    lax.fori_loop(0, n_chunks, out_pass, 0, unroll=RET_CHUNK_UNROLL)


def _retention(q, k, v, g, decay_f, decay_b, beta_ret, s0f, s0b):
    b, t, _ = q.shape
    n_chunks = t // CHUNK
    tok = pl.BlockSpec((None, t, LANES), lambda bi, h: (bi, 0, h))
    per_head = pl.BlockSpec((None, 1, LANES), lambda bi, h: (h, 0, 0))
    state = pl.BlockSpec((None, None, RET_DK, RET_DK), lambda bi, h: (bi, h, 0, 0))
    lanes = lambda a: jnp.broadcast_to(a.astype(F32)[:, None, None], (RET_HEADS, 1, LANES))
    st_shape = jax.ShapeDtypeStruct((b, RET_HEADS, RET_DK, RET_DK), F32)
    return pl.pallas_call(
        functools.partial(_ret_body, n_chunks=n_chunks),
        grid=(b, RET_HEADS),
        in_specs=[tok, tok, tok, tok, per_head, per_head, per_head, state, state],
        out_specs=[tok, state, state],
        out_shape=[jax.ShapeDtypeStruct((b, t, RET_WIDTH), BF16), st_shape, st_shape],
        scratch_shapes=[pltpu.VMEM((n_chunks, RET_DK, RET_DK), F32)] * 2,
        compiler_params=_params("parallel", "parallel"),
        name="retention",
    )(q, k, v, g, lanes(decay_f), lanes(decay_b), beta_ret.reshape(RET_HEADS, 1, LANES), s0f, s0b)


def _merge_body(x_ref, ona_ref, ret_ref, wna_ref, wret_ref, bna_ref, gate_ref, lg_ref, lb_ref, o_ref):
    o = ona_ref[...].astype(F32)
    na = o * lax.rsqrt(jnp.mean(o * o, axis=-1, keepdims=True) + LN_EPS) * bna_ref[...]
    mix = (jnp.dot(na.astype(BF16), wna_ref[...], preferred_element_type=F32)
           + jnp.dot(ret_ref[...], wret_ref[...], preferred_element_type=F32))
    y = DEEPNORM_ALPHA * x_ref[...] + gate_ref[...] * mix
    o_ref[...] = _standardize(y) * lg_ref[...] + lb_ref[...]


def _merge(x, o_na, ret, w_out, beta_na, gate, mod_row, ln_g, ln_b, tm):
    b, t, _ = x.shape
    tok = lambda bi, ti: (bi, ti, 0)
    row = pl.BlockSpec((1, D_MODEL), lambda bi, ti: (0, 0))
    half = pl.BlockSpec((None, tm, 512), tok)
    return pl.pallas_call(
        _merge_body,
        grid=(b, t // tm),
        in_specs=[pl.BlockSpec((None, tm, D_MODEL), tok), half, half,
                  pl.BlockSpec((512, D_MODEL), lambda bi, ti: (0, 0)),
                  pl.BlockSpec((512, D_MODEL), lambda bi, ti: (1, 0)),
                  pl.BlockSpec((1, 512), lambda bi, ti: (0, 0)),
                  pl.BlockSpec((None, 1, D_MODEL), lambda bi, ti: (mod_row(bi), 0, 0)),
                  row, row],
        out_specs=pl.BlockSpec((None, tm, D_MODEL), tok),
        out_shape=jax.ShapeDtypeStruct(x.shape, F32),
        compiler_params=_params("parallel", "parallel"),
        name="merge_ln1",
    )(x, o_na, ret, w_out, w_out, beta_na.reshape(1, 512), gate, ln_g.reshape(1, -1), ln_b.reshape(1, -1))


def _two_group_specs(n_p_blocks, blocks_per_batch_s, ctx_row):
    tok_p = lambda i: (jnp.minimum(i, n_p_blocks - 1), 0)
    tok_s = lambda i: (jnp.maximum(i - n_p_blocks, 0), 0)
    mod = lambda i: (jnp.where(i < n_p_blocks, ctx_row, jnp.maximum(i - n_p_blocks, 0) // blocks_per_batch_s), 0, 0)
    return tok_p, tok_s, mod


def _router_body(yp_ref, ys_ref, sh_ref, sc_ref, wh_ref, wl_ref, br_ref, p_ref, gt_ref, tab_ref, cnt_ref,
                 carry_s, *, n_p_blocks):
    i = pl.program_id(0)
    tm = yp_ref.shape[0]

    @pl.when(i == 0)
    def _():
        carry_s[...] = jnp.zeros_like(carry_s)

    y = jnp.where(i < n_p_blocks, yp_ref[...], ys_ref[...])
    h = y * (1.0 + sc_ref[...]) + sh_ref[...]
    h_hi = h.astype(BF16)
    h_lo = (h - h_hi.astype(F32)).astype(BF16)
    w_hi = wh_ref[...]
    lg = _dot_nt(jnp.concatenate([w_hi, wl_ref[...]], axis=0), h_hi)
    work = lg[:N_EXPERTS] + lg[N_EXPERTS:] + _dot_nt(w_hi, h_lo) + br_ref[...]
    eidx = lax.broadcasted_iota(I32, (N_EXPERTS, tm), 0).astype(F32)
    vals, idxs, hots = [], [], []
    for _ in range(TOP_K):
        mx = jnp.max(work, axis=0, keepdims=True)
        idx = jnp.min(jnp.where(work == mx, eidx, float(N_EXPERTS)), axis=0, keepdims=True)
        hot = eidx == idx
        vals.append(mx)
        idxs.append(idx)
        hots.append(hot)
        work = jnp.where(hot, -jnp.inf, work)
    exps = [jnp.exp(v - vals[0]) for v in vals]
    den = exps[0] + exps[1] + exps[2] + exps[3]
    sel = jnp.zeros((N_EXPERTS, tm), F32)
    for hot in hots:
        sel = sel + hot.astype(F32)
    r_i = lax.broadcasted_iota(I32, (tm, tm), 0)
    c_i = lax.broadcasted_iota(I32, (tm, tm), 1)
    earlier = (r_i < c_i).astype(BF16)
    before = jnp.dot(sel.astype(BF16), earlier, preferred_element_type=F32)
    n_e = jnp.sum(sel, axis=1, keepdims=True)
    n8 = jnp.floor((n_e + (RUN_ALIGN - 1.0)) * (1.0 / RUN_ALIGN)) * RUN_ALIGN + jnp.zeros((1, LANES), F32)
    e_r = lax.broadcasted_iota(I32, (N_EXPERTS, N_EXPERTS), 0)
    e_c = lax.broadcasted_iota(I32, (N_EXPERTS, N_EXPERTS), 1)
    off = jnp.dot((e_c < e_r).astype(BF16), n8.astype(BF16), preferred_element_type=F32)
    local = before + jnp.concatenate([off] * (tm // LANES), axis=1)
    rows = [jnp.sum(jnp.where(hot, local, 0.0), axis=0, keepdims=True) for hot in hots]
    p_ref[...] = jnp.concatenate(rows, axis=0).astype(I32)
    gt_ref[...] = jnp.concatenate([e / den for e in exps], axis=0)
    carry = carry_s[...]
    lane = lax.broadcasted_iota(I32, (N_EXPERTS, LANES), 1)
    tab_ref[...] = jnp.where(lane == 0, n8, jnp.where(lane == 1, carry, jnp.where(lane == 2, off, 0.0)))
    carry = carry + n8
    carry_s[...] = carry
    cnt_ref[...] = carry


def _router(yp, ys, shift, scale, ctx_row, s_blocks_per_batch, w_router, b_router, tm):
    n_p, n_s = yp.shape[0], ys.shape[0]
    n = n_p + n_s
    npb = n_p // tm
    tok_p, tok_s, mod = _two_group_specs(npb, s_blocks_per_batch, ctx_row)
    w_t = w_router.T
    w_hi = w_t.astype(BF16)
    w_lo = (w_t - w_hi.astype(F32)).astype(BF16)
    br = jnp.broadcast_to(b_router[:, None], (N_EXPERTS, tm))
    o4 = pl.BlockSpec((TOP_K, tm), lambda i: (0, i))
    whole = lambda shape: pl.BlockSpec(shape, lambda i: (0, 0))
    return pl.pallas_call(
        functools.partial(_router_body, n_p_blocks=npb),
        grid=(n // tm,),
        in_specs=[pl.BlockSpec((tm, D_MODEL), tok_p), pl.BlockSpec((tm, D_MODEL), tok_s),
                  pl.BlockSpec((None, 1, D_MODEL), mod), pl.BlockSpec((None, 1, D_MODEL), mod),
                  whole((N_EXPERTS, D_MODEL)), whole((N_EXPERTS, D_MODEL)), whole((N_EXPERTS, tm))],
        out_specs=[o4, o4, pl.BlockSpec((None, N_EXPERTS, LANES), lambda i: (i, 0, 0)),
                   whole((N_EXPERTS, LANES))],
        out_shape=[jax.ShapeDtypeStruct((TOP_K, n), I32), jax.ShapeDtypeStruct((TOP_K, n), F32),
                   jax.ShapeDtypeStruct((n // tm, N_EXPERTS, LANES), F32),
                   jax.ShapeDtypeStruct((N_EXPERTS, LANES), F32)],
        scratch_shapes=[pltpu.VMEM((N_EXPERTS, LANES), F32)],
        compiler_params=_params("arbitrary"),
        name="moe_router",
    )(yp, ys, shift, scale, w_hi, w_lo, br)


def _for_run_pieces(n_rows, visit):
    off = 0
    for size in RUN_PIECES:
        present = (n_rows & size) != 0
        visit(off, size, present)
        off = off + jnp.where(present, size, 0)


def _start_unit_copies(unit_ref, local_buf, sorted_hbm, sem, to_hbm):
    def one(u, carry):
        loc = local_buf.at[pl.ds(pl.multiple_of(u * RUN_ALIGN, RUN_ALIGN), RUN_ALIGN)]
        glob = sorted_hbm.at[pl.ds(pl.multiple_of(unit_ref[u], RUN_ALIGN), RUN_ALIGN)]
        copy = pltpu.make_async_copy(loc, glob, sem) if to_hbm else pltpu.make_async_copy(glob, loc, sem)
        copy.start()
        return carry

    lax.fori_loop(0, LOCAL_UNITS, one, 0, unroll=8)


def _wait_unit_copies(local_buf, sorted_hbm, sem):
    whole = sorted_hbm.at[pl.ds(0, LOCAL_ROWS)]
    pltpu.make_async_copy(whole, local_buf, sem).wait()


def _dispatch_body(unit_ref, zlo_ref, zn_ref, p_ref, yp_ref, ys_ref, sh_ref, sc_ref, xs_ref,
                   xbuf, zbuf, sems, zsem, *, n_p_blocks, n_blocks):
    i = pl.program_id(0)
    slot = i % 2
    y = jnp.where(i < n_p_blocks, yp_ref[...], ys_ref[...])
    h = (y * (1.0 + sc_ref[...]) + sh_ref[...]).astype(BF16)
    rows = p_ref[...]
    r_iota = lax.broadcasted_iota(I32, (LOCAL_ROWS, rows.shape[1]), 0)
    place = r_iota == rows[0:1]
    for kk in range(1, TOP_K):
        place = place | (r_iota == rows[kk:kk + 1])
    xbuf[slot] = jnp.dot(jnp.where(place, 1.0, 0.0).astype(BF16), h, preferred_element_type=F32)

    _start_unit_copies(unit_ref, xbuf.at[slot], xs_ref, sems.at[slot], to_hbm=True)

    @pl.when(i > 0)
    def _():
        _wait_unit_copies(xbuf.at[1 - slot], xs_ref, sems.at[1 - slot])

    @pl.when(i == n_blocks - 1)
    def _():
        _wait_unit_copies(xbuf.at[slot], xs_ref, sems.at[slot])
        zbuf[...] = jnp.zeros_like(zbuf)
        sem = zsem

        def per_expert(e, carry):
            lo = zlo_ref[e]

            def visit(off, size, present):
                copy = pltpu.make_async_copy(
                    zbuf.at[pl.ds(0, size)], xs_ref.at[pl.ds(pl.multiple_of(lo + off, RUN_ALIGN), size)], sem)

                @pl.when(present)
                def _():
                    copy.start()
                    copy.wait()

            _for_run_pieces(zn_ref[e], visit)
            return carry

        lax.fori_loop(0, N_EXPERTS, per_expert, 0)

        tail_lo = zlo_ref[N_EXPERTS]

        def tile_copy(t):
            start = pl.multiple_of(tail_lo + t * EXPERT_TILE, EXPERT_TILE)
            return pltpu.make_async_copy(zbuf, xs_ref.at[pl.ds(start, EXPERT_TILE)], sem)

        def t_issue(t, c2):
            tile_copy(t).start()
            return c2

        def t_drain(t, c2):
            tile_copy(0).wait()
            return c2

        n_tail = zn_ref[N_EXPERTS] // EXPERT_TILE
        lax.fori_loop(0, n_tail, t_issue, 0)
        lax.fori_loop(0, n_tail, t_drain, 0)


def _dispatch(yp, ys, shift, scale, ctx_row, s_blocks_per_batch, local_rows, unit_dst, zero_lo, zero_n,
              total_rows, tm):
    n_p, n_s = yp.shape[0], ys.shape[0]
    n = n_p + n_s
    npb = n_p // tm
    nb = n // tm
    tok_p, tok_s, mod = _two_group_specs(npb, s_blocks_per_batch, ctx_row)
    smem_all = pl.BlockSpec(memory_space=pltpu.SMEM)
    return pl.pallas_call(
        functools.partial(_dispatch_body, n_p_blocks=npb, n_blocks=nb),
        grid=(nb,),
        in_specs=[pl.BlockSpec((UNIT_TABLE,), lambda i: (i,), memory_space=pltpu.SMEM), smem_all, smem_all,
                  pl.BlockSpec((TOP_K, tm), lambda i: (0, i)),
                  pl.BlockSpec((tm, D_MODEL), tok_p), pl.BlockSpec((tm, D_MODEL), tok_s),
                  pl.BlockSpec((None, 1, D_MODEL), mod), pl.BlockSpec((None, 1, D_MODEL), mod)],
        out_specs=pl.BlockSpec(memory_space=pl.ANY),
        out_shape=jax.ShapeDtypeStruct((total_rows, D_MODEL), F32),
        scratch_shapes=[pltpu.VMEM((2, LOCAL_ROWS, D_MODEL), F32),
                        pltpu.VMEM((EXPERT_TILE, D_MODEL), F32),
                        pltpu.SemaphoreType.DMA((2,)), pltpu.SemaphoreType.DMA(())],
        compiler_params=_params("arbitrary"),
        name="moe_dispatch",
    )(unit_dst, zero_lo, zero_n, local_rows, yp, ys, shift, scale)


def _expert_body(te_ref, first_ref, nu_ref, x_ref, wgu_ref, bgu_ref, wd_ref, bd_ref, y_ref, act_s, wgu_s, wd_s):
    j = pl.program_id(0)

    @pl.when(first_ref[j] == 1)
    def _():
        blk = 2 * LANES
        src = lax.broadcasted_iota(I32, (blk, blk), 0)
        dst = lax.broadcasted_iota(I32, (blk, blk), 1)
        perm = (src == jnp.where(dst < LANES, 2 * dst, 2 * (dst - LANES) + 1)).astype(F32).astype(BF16)
        for s in range(2 * D_FF // blk):
            sl = slice(s * blk, (s + 1) * blk)
            wgu_s[:, sl] = jnp.dot(wgu_ref[:, sl].astype(BF16), perm, preferred_element_type=F32).astype(BF16)
        wd_s[...] = wd_ref[...].astype(BF16)

    @pl.when(j < nu_ref[0])
    def _():
        x = x_ref[...].astype(BF16)
        for c in range(D_FF // LANES):
            sl = slice(c * 2 * LANES, (c + 1) * 2 * LANES)
            hu = jnp.dot(x, wgu_s[:, sl], preferred_element_type=F32) + bgu_ref[:, sl]
            x_glu = jnp.minimum(hu[:, :LANES], SWIGLU_LIMIT)
            x_lin = jnp.clip(hu[:, LANES:], -SWIGLU_LIMIT, SWIGLU_LIMIT)
            act = x_glu * (1.0 / (1.0 + jnp.exp(-SWIGLU_ALPHA * x_glu))) * (x_lin + 1.0)
            act_s[:, c * LANES:(c + 1) * LANES] = act.astype(BF16)
        y_ref[...] = jnp.dot(act_s[...], wd_s[...], preferred_element_type=F32) + bd_ref[...]

    @pl.when(j >= nu_ref[0])
    def _():
        y_ref[...] = jnp.zeros_like(y_ref)


def _experts(tile_expert, tile_first, n_used, xs, cap_rows, w_gate_up, bgu, w_down, bd):
    n_tiles = cap_rows // EXPERT_TILE
    of_expert = lambda j, te, first, nu: (te[j], 0, 0)
    grid_spec = pltpu.PrefetchScalarGridSpec(
        num_scalar_prefetch=3,
        grid=(n_tiles,),
        in_specs=[pl.BlockSpec((EXPERT_TILE, D_MODEL), lambda j, te, first, nu: (jnp.minimum(j, nu[0] - 1), 0)),
                  pl.BlockSpec((None, D_MODEL, 2 * D_FF), of_expert),
                  pl.BlockSpec((None, 1, 2 * D_FF), of_expert),
                  pl.BlockSpec((None, D_FF, D_MODEL), of_expert),
                  pl.BlockSpec((None, 1, D_MODEL), of_expert)],
        out_specs=pl.BlockSpec((EXPERT_TILE, D_MODEL), lambda j, te, first, nu: (j, 0)),
        scratch_shapes=[pltpu.VMEM((EXPERT_TILE, D_FF), BF16),
                        pltpu.VMEM((D_MODEL, 2 * D_FF), BF16), pltpu.VMEM((D_FF, D_MODEL), BF16)],
    )
    return pl.pallas_call(
        _expert_body,
        grid_spec=grid_spec,
        out_shape=jax.ShapeDtypeStruct((cap_rows, D_MODEL), F32),
        compiler_params=_params("arbitrary"),
        name="moe_experts",
    )(tile_expert, tile_first, n_used, xs, w_gate_up, bgu, w_down, bd)


def _combine_body(unit_ref, next_unit_ref, y1_ref, p_ref, gt_ref, gate_ref, lg_ref, lb_ref, ys_ref, o_ref,
                  ybuf, sems, *, n_blocks):
    i = pl.program_id(0)
    slot = i % 2
    tm = y1_ref.shape[0]

    @pl.when(i == 0)
    def _():
        _start_unit_copies(unit_ref, ybuf.at[0], ys_ref, sems.at[0], to_hbm=False)

    @pl.when(i + 1 < n_blocks)
    def _():
        _start_unit_copies(next_unit_ref, ybuf.at[1 - slot], ys_ref, sems.at[1 - slot], to_hbm=False)

    _wait_unit_copies(ybuf.at[slot], ys_ref, sems.at[slot])

    rows = p_ref[...]
    gates = gt_ref[...]
    c_iota = lax.broadcasted_iota(I32, (tm, LOCAL_ROWS), 1)
    weight = jnp.where(c_iota == rows[:, 0:1], gates[:, 0:1], 0.0)
    for kk in range(1, TOP_K):
        weight = weight + jnp.where(c_iota == rows[:, kk:kk + 1], gates[:, kk:kk + 1], 0.0)
    w_hi = weight.astype(BF16)
    w_lo = (weight - w_hi.astype(F32)).astype(BF16)
    yb = ybuf[slot].astype(BF16)
    f = (jnp.dot(w_hi, yb, preferred_element_type=F32) + jnp.dot(w_lo, yb, preferred_element_type=F32))
    y = DEEPNORM_ALPHA * y1_ref[...] + gate_ref[...] * f
    o_ref[...] = _standardize(y) * lg_ref[...] + lb_ref[...]


def _combine(y1, local_rows, gates, unit_src, block0, gate_mod, mod_row, blocks_per_batch,
             y_sorted, ln_g, ln_b, tm):
    n_g = y1.shape[0]
    nb = n_g // tm
    row = pl.BlockSpec((1, D_MODEL), lambda i: (0, 0))
    pair = pl.BlockSpec((tm, TOP_K), lambda i: (i + block0, 0))
    units = lambda ahead: pl.BlockSpec(
        (UNIT_TABLE,), lambda i: (jnp.minimum(i + ahead, nb - 1) + block0,), memory_space=pltpu.SMEM)
    return pl.pallas_call(
        functools.partial(_combine_body, n_blocks=nb),
        grid=(nb,),
        in_specs=[units(0), units(1),
                  pl.BlockSpec((tm, D_MODEL), lambda i: (i, 0)), pair, pair,
                  pl.BlockSpec((None, 1, D_MODEL), lambda i: (mod_row(i // blocks_per_batch), 0, 0)),
                  row, row,
                  pl.BlockSpec(memory_space=pl.ANY)],
        out_specs=pl.BlockSpec((tm, D_MODEL), lambda i: (i, 0)),
        out_shape=jax.ShapeDtypeStruct((n_g, D_MODEL), F32),
        scratch_shapes=[pltpu.VMEM((2, LOCAL_ROWS, D_MODEL), F32), pltpu.SemaphoreType.DMA((2,))],
        compiler_params=_params("arbitrary"),
        name="moe_combine_ln2",
    )(unit_src, unit_src, y1, local_rows, gates, gate_mod,
      ln_g.reshape(1, -1), ln_b.reshape(1, -1), y_sorted)


def _regroup_gate_up_bias(b_gate_up):
    e = b_gate_up.shape[0]
    nblk = D_FF // LANES
    return b_gate_up.reshape(e, 1, nblk, LANES, 2).transpose(0, 1, 2, 4, 3).reshape(e, 1, 2 * D_FF)


def _routing_plan(table, totals, n_pairs):
    nb = table.shape[0]
    run_n = table[:, :, 0].astype(I32)
    rows_before = table[:, :, 1].astype(I32)
    run_local = table[:, :, 2].astype(I32)
    cnt = totals[:, 0].astype(I32)
    padded = (cnt + EXPERT_TILE - 1) // EXPERT_TILE * EXPERT_TILE
    pad_end = jnp.cumsum(padded)
    pad_start = pad_end - padded
    run_global = pad_start[None, :] + rows_before
    rows_max = n_pairs + nb * N_EXPERTS * (RUN_ALIGN - 1) + N_EXPERTS * (EXPERT_TILE - 1)
    n_tiles = -(-rows_max // EXPERT_TILE)
    n_used = pad_end[-1] // EXPERT_TILE
    tile_row = jnp.minimum(jnp.arange(n_tiles, dtype=I32), n_used - 1) * EXPERT_TILE
    tile_expert = jnp.sum((pad_end[None, :] <= tile_row[:, None]).astype(I32), axis=1)
    tile_expert = jnp.minimum(tile_expert, N_EXPERTS - 1)
    tile_id = jnp.arange(n_tiles, dtype=I32)
    changed = jnp.concatenate([jnp.ones((1,), bool), tile_expert[1:] != tile_expert[:-1]])
    tile_first = (changed & (tile_id < n_used)).astype(I32)
    cap_rows = n_tiles * EXPERT_TILE
    spare_rows = -(-LOCAL_ROWS // EXPERT_TILE) * EXPERT_TILE
    total_rows = cap_rows + 2 * spare_rows
    unit_row = jnp.arange(LOCAL_UNITS, dtype=I32) * RUN_ALIGN
    run_end = run_local + run_n
    owner = jnp.sum((run_end[:, None, :] <= unit_row[None, :, None]).astype(I32), axis=-1)
    own_hot = owner[:, :, None] == jnp.arange(N_EXPERTS, dtype=I32)
    shift = jnp.sum(jnp.where(own_hot, (run_global - run_local)[:, None, :], 0), axis=-1)
    used = owner < N_EXPERTS
    sorted_row = unit_row[None, :] + shift
    slot = (jnp.arange(nb, dtype=I32) % 2)[:, None]
    unit_dst = jnp.where(used, sorted_row, cap_rows + slot * spare_rows + unit_row[None, :])
    unit_src = jnp.where(used, sorted_row, unit_row[None, :])
    table_of = lambda a: jnp.pad(a, ((0, 0), (0, UNIT_TABLE - LOCAL_UNITS))).reshape(-1)
    zero_lo = jnp.concatenate([pad_start + cnt, pad_end[-1:]])
    zero_n = jnp.concatenate([padded - cnt, total_rows - pad_end[-1:]])
    return ((table_of(unit_dst), table_of(unit_src)), (tile_expert, tile_first, n_used.reshape(1)),
            zero_lo, zero_n, cap_rows, total_rows)


def kernel(x_prompt, x_sample, cache_na_k, cache_na_v, state_ret_fwd, state_ret_bwd, c, c_ctx, w_ada, b_ada, w_in, rpb, ret_decay_fwd, ret_decay_bwd, beta_na, beta_ret, w_out, ln1_g, ln1_b, w_router, b_router, w_gate_up, b_gate_up, w_down, b_down, ln2_g, ln2_b):
    bp, tp, _ = x_prompt.shape
    bs, ts, _ = x_sample.shape
    assert w_ada.shape[0] == DEPTH == 1
    l = 0
    ctx_row = bs
    sample_row = lambda bi: bi
    prompt_row = lambda bi: ctx_row

    cond = jnp.concatenate([c, c_ctx[None, :], jnp.zeros((16 - bs - 1, D_MODEL), F32)], axis=0)
    m = _adaln(cond, w_ada[l], b_ada[l]).reshape(16, 6, 1, D_MODEL)
    sh_a, sc_a, gt_a, sh_f, sc_f, gt_f = (m[:, i] for i in range(6))

    w_in_b = w_in[l].astype(BF16)
    w_out_b = w_out[l].astype(BF16)

    q, k_p, v_p, qr, kr, vr, g = _project(x_prompt, sh_a, sc_a, prompt_row, w_in_b, tp, F32, rope=False)
    o_na = _context_attention(q, k_p, v_p)
    zeros_state = jnp.zeros((bp, RET_HEADS, RET_DK, RET_DK), F32)
    ret, s_f, s_b = _retention(qr, kr, vr, g, ret_decay_fwd[l], ret_decay_bwd[l], beta_ret[l],
                               zeros_state, zeros_state)
    yp1 = _merge(x_prompt, o_na, ret, w_out_b, beta_na[l], gt_a, prompt_row, ln1_g[l], ln1_b[l], tp)

    q, k_s, v_s, qr, kr, vr, g = _project(x_sample, sh_a, sc_a, sample_row, w_in_b, 512, BF16, rope=True)
    lc = cache_na_k.shape[2]
    o_na = _neighbourhood_attention(q, k_s, v_s, cache_na_k[:, l].reshape(bs, lc, NA_WIDTH),
                                    cache_na_v[:, l].reshape(bs, lc, NA_WIDTH), _na_bias(rpb[l]))
    ret, _, _ = _retention(qr, kr, vr, g, ret_decay_fwd[l], ret_decay_bwd[l], beta_ret[l],
                           state_ret_fwd[:, l], state_ret_bwd[:, l])
    ys1 = _merge(x_sample, o_na, ret, w_out_b, beta_na[l], gt_a, sample_row, ln1_g[l], ln1_b[l], 512)

    tm = TOKEN_BLOCK
    yp1f = yp1.reshape(bp * tp, D_MODEL)
    ys1f = ys1.reshape(bs * ts, D_MODEL)
    local_rows, gates, table, totals = _router(yp1f, ys1f, sh_f, sc_f, ctx_row, ts // tm,
                                               w_router[l], b_router[l], tm)
    (unit_dst, unit_src), tiles, zero_lo, zero_n, cap_rows, total_rows = _routing_plan(
        table, totals, (bp * tp + bs * ts) * TOP_K)
    xs = _dispatch(yp1f, ys1f, sh_f, sc_f, ctx_row, ts // tm, local_rows, unit_dst, zero_lo, zero_n,
                   total_rows, tm)
    y_sorted = _experts(*tiles, xs, cap_rows, w_gate_up[l], _regroup_gate_up_bias(b_gate_up[l]), w_down[l],
                        b_down[l][:, None, :])
    rows_t, gates_t = local_rows.T, gates.T
    yp = _combine(yp1f, rows_t, gates_t, unit_src, 0, gt_f, prompt_row, 1, y_sorted, ln2_g[l], ln2_b[l], tm)
    ys = _combine(ys1f, rows_t, gates_t, unit_src, (bp * tp) // tm, gt_f, sample_row, ts // tm, y_sorted,
                  ln2_g[l], ln2_b[l], tm)

    return (yp.reshape(bp, tp, D_MODEL), ys.reshape(bs, ts, D_MODEL),
            k_p.reshape(bp, 1, tp, NA_HEADS, NA_DIM), v_p.reshape(bp, 1, tp, NA_HEADS, NA_DIM),
            s_f[:, None], s_b[:, None])
```

```python
import functools

import jax
import jax.numpy as jnp
from jax import lax
from jax.experimental import pallas as pl
from jax.experimental.pallas import tpu as pltpu

F32 = jnp.float32
BF16 = jnp.bfloat16
I32 = jnp.int32
U32 = jnp.uint32

D_MODEL = 1024
GRID_W = 64
NA_HEADS = 8
NA_DIM = 64
NA_WIDTH = NA_HEADS * NA_DIM
NA_KH = 8
NA_KW = 16
RET_HEADS = 4
RET_DK = 128
RET_WIDTH = RET_HEADS * RET_DK
CHUNK = 128
N_EXPERTS = 32
TOP_K = 4
D_FF = 1024
SWIGLU_LIMIT = 7.0
SWIGLU_ALPHA = 1.702
ROPE_BASE = 10000.0
LN_EPS = 1e-5
DEPTH = 1
DEEPNORM_ALPHA = (2.0 * DEPTH) ** 0.25

LANES = 128
NEG_BIG = -1e30
TOKEN_BLOCK = 256
EXPERT_TILE = 512
VMEM_LIMIT = 56 * 1024 * 1024
RET_CHUNK_UNROLL = 8
SUBLANES = 8
RUN_ALIGN = SUBLANES
LOCAL_ROWS = TOKEN_BLOCK * TOP_K + N_EXPERTS * RUN_ALIGN
LOCAL_UNITS = LOCAL_ROWS // RUN_ALIGN
UNIT_TABLE = 256
RUN_PIECES = tuple(TOKEN_BLOCK >> s for s in range(TOKEN_BLOCK.bit_length())
                   if (TOKEN_BLOCK >> s) >= RUN_ALIGN)


def _params(*sem):
    return pltpu.CompilerParams(dimension_semantics=sem, vmem_limit_bytes=VMEM_LIMIT)


def _silu(x):
    return x / (1.0 + jnp.exp(-x))


def _standardize(x):
    mu = jnp.mean(x, axis=-1, keepdims=True)
    xc = x - mu
    var = jnp.mean(xc * xc, axis=-1, keepdims=True)
    return xc * lax.rsqrt(var + LN_EPS)


def _pack_bf16_pairs(x):
    bits = lax.bitcast_convert_type(x.astype(BF16).astype(F32), U32)
    half = x.shape[1] // 2
    return bits[:, :half] | lax.shift_right_logical(bits[:, half:], jnp.uint32(16))


def _unpack_bf16_pairs(p):
    hi = lax.bitcast_convert_type(p & jnp.uint32(0xFFFF0000), F32)
    lo = lax.bitcast_convert_type(lax.shift_left(p, jnp.uint32(16)), F32)
    return jnp.concatenate([hi, lo], axis=1).astype(BF16)


def _dot_nt(a, b):
    return lax.dot_general(a, b, (((1,), (1,)), ((), ())), preferred_element_type=F32)


def _dot_tn(a, b):
    return lax.dot_general(a, b, (((0,), (0,)), ((), ())), preferred_element_type=F32)


def _ada_body(c_ref, w_ref, b_ref, o_ref):
    s = _silu(c_ref[...])
    o_ref[...] = jnp.dot(s, w_ref[...], preferred_element_type=F32,
                         precision=lax.Precision.HIGHEST) + b_ref[...]


def _adaln(cond, w_ada, b_ada):
    r = cond.shape[0]
    n = w_ada.shape[1]
    tn = 1536
    return pl.pallas_call(
        _ada_body,
        grid=(n // tn,),
        in_specs=[pl.BlockSpec((r, D_MODEL), lambda j: (0, 0)),
                  pl.BlockSpec((D_MODEL, tn), lambda j: (0, j)),
                  pl.BlockSpec((1, tn), lambda j: (0, j))],
        out_specs=pl.BlockSpec((r, tn), lambda j: (0, j)),
        out_shape=jax.ShapeDtypeStruct((r, n), F32),
        compiler_params=_params("arbitrary"),
        name="adaln",
    )(cond, w_ada, b_ada.reshape(1, n))


def _proj_body(*refs, rope):
    if rope:
        (x_ref, sh_ref, sc_ref, w_ref, cos_ref, sa_ref, sb_ref,
         q_ref, k_ref, v_ref, qr_ref, kr_ref, vr_ref, g_ref) = refs
    else:
        (x_ref, sh_ref, sc_ref, w_ref,
         q_ref, k_ref, v_ref, qr_ref, kr_ref, vr_ref, g_ref) = refs
    h = (x_ref[...] * (1.0 + sc_ref[...]) + sh_ref[...]).astype(BF16)

    def cols(c):
        return jnp.dot(h, w_ref[:, c * 512:(c + 1) * 512], preferred_element_type=F32)

    q_ref[...] = cols(0).astype(q_ref.dtype)
    k_ref[...] = cols(1).astype(k_ref.dtype)
    v_ref[...] = cols(2).astype(v_ref.dtype)
    pq = cols(3)
    pk = cols(4) * (RET_DK ** -0.5)
    if rope:
        cs, sa, sb = cos_ref[...], sa_ref[...], sb_ref[...]
        for hd in range(RET_HEADS):
            sl = slice(hd * LANES, (hd + 1) * LANES)
            for p, o_ref in ((pq, qr_ref), (pk, kr_ref)):
                xs = p[:, sl]
                rot = xs * cs + pltpu.roll(xs, 96, 1) * sa + pltpu.roll(xs, 32, 1) * sb
                o_ref[:, sl] = rot.astype(o_ref.dtype)
    else:
        qr_ref[...] = pq.astype(qr_ref.dtype)
        kr_ref[...] = pk.astype(kr_ref.dtype)
    vr_ref[...] = cols(5).astype(vr_ref.dtype)
    g_ref[...] = cols(6)


def _rope_tables(t_len):
    t = jnp.arange(t_len, dtype=jnp.int32)
    pos_row = (t // GRID_W).astype(F32)[:, None]
    pos_col = (t % GRID_W).astype(F32)[:, None]
    lane = jnp.arange(LANES, dtype=jnp.int32)[None, :]
    n_freq = RET_DK // 4
    inv_freq = ROPE_BASE ** (-(lane % n_freq).astype(F32) / n_freq)
    ang = jnp.where(lane < RET_DK // 2, pos_row, pos_col) * inv_freq
    first = (lane % (2 * n_freq)) < n_freq
    cos = jnp.cos(ang)
    sin = jnp.sin(ang)
    return cos, jnp.where(first, -sin, 0.0), jnp.where(first, 0.0, sin)


def _project(x, shift, scale, mod_row, w_in, tm, kv_dtype, rope):
    b, t, _ = x.shape
    tok = lambda bi, ti: (bi, ti, 0)
    mod = lambda bi, ti: (mod_row(bi), 0, 0)
    in_specs = [pl.BlockSpec((None, tm, D_MODEL), tok),
                pl.BlockSpec((None, 1, D_MODEL), mod),
                pl.BlockSpec((None, 1, D_MODEL), mod),
                pl.BlockSpec(w_in.shape, lambda bi, ti: (0, 0))]
    args = [x, shift, scale, w_in]
    if rope:
        tab = lambda bi, ti: (ti, 0)
        in_specs += [pl.BlockSpec((tm, LANES), tab)] * 3
        args += list(_rope_tables(t))
    o512 = pl.BlockSpec((None, tm, 512), tok)
    shp = lambda dt: jax.ShapeDtypeStruct((b, t, 512), dt)
    return pl.pallas_call(
        functools.partial(_proj_body, rope=rope),
        grid=(b, t // tm),
        in_specs=in_specs,
        out_specs=[o512] * 7,
        out_shape=[shp(BF16), shp(kv_dtype), shp(kv_dtype), shp(BF16), shp(BF16), shp(BF16), shp(F32)],
        compiler_params=_params("parallel", "parallel"),
        name="in_proj",
    )(*args)


def _pair_masks(shape):
    lane = lax.broadcasted_iota(I32, shape, 1)
    return lane < NA_DIM, lane >= NA_DIM


def _ctx_attn_body(q_ref, k_ref, v_ref, o_ref):
    q2 = q_ref[...]
    k2 = k_ref[...].astype(BF16)
    v2 = v_ref[...].astype(BF16)
    t = q2.shape[0]
    lo, hi = _pair_masks(q2.shape)
    zero = jnp.zeros_like(q2)
    qs = jnp.concatenate([jnp.where(lo, q2, zero), jnp.where(hi, q2, zero)], axis=0)
    s = _dot_nt(qs, k2) * (NA_DIM ** -0.5)
    e = jnp.exp(s - jnp.max(s, axis=-1, keepdims=True))
    den = jnp.sum(e, axis=-1, keepdims=True)
    o = jnp.dot(e.astype(BF16), v2, preferred_element_type=F32) / den
    o_ref[...] = jnp.where(lo, o[:t], o[t:]).astype(o_ref.dtype)


def _context_attention(q, k, v):
    b, t, _ = q.shape
    spec = pl.BlockSpec((None, t, LANES), lambda bi, hp: (bi, 0, hp))
    return pl.pallas_call(
        _ctx_attn_body,
        grid=(b, NA_WIDTH // LANES),
        in_specs=[spec] * 3,
        out_specs=spec,
        out_shape=jax.ShapeDtypeStruct((b, t, NA_WIDTH), BF16),
        compiler_params=_params("parallel", "parallel"),
        name="ctx_attn",
    )(q, k, v)


def _na_bias_body(rpb_ref, o_ref):
    h = pl.program_id(0)
    v = pl.program_id(1)
    shape = (GRID_W, LANES)
    w = lax.broadcasted_iota(I32, shape, 0)
    lane = lax.broadcasted_iota(I32, shape, 1)
    upper = lane >= GRID_W
    kc = jnp.where(upper, lane - GRID_W, lane)
    cdiff = kc - w + (NA_KW - 1)
    cstart = jnp.clip(w - NA_KW // 2, 0, GRID_W - NA_KW)
    inwin = (kc >= cstart) & (kc < cstart + NA_KW)
    n_ro = 2 * NA_KH - 1
    n_co = 2 * NA_KW - 1
    for j in range(NA_KH // 2):
        ro = 2 * j - v + (NA_KH - 1)
        base0 = (h * n_ro + ro) * n_co
        acc = jnp.zeros(shape, F32)
        for c in range(n_co):
            val = jnp.where(upper, rpb_ref[base0 + n_co + c], rpb_ref[base0 + c])
            acc = jnp.where(cdiff == c, val, acc)
        o_ref[:, j * LANES:(j + 1) * LANES] = jnp.where(inwin, acc, NEG_BIG)


def _na_bias(rpb):
    n = NA_KH * GRID_W
    return pl.pallas_call(
        _na_bias_body,
        grid=(NA_HEADS, NA_KH),
        in_specs=[pl.BlockSpec(memory_space=pltpu.SMEM)],
        out_specs=pl.BlockSpec((None, None, GRID_W, n), lambda h, v: (h, v, 0, 0)),
        out_shape=jax.ShapeDtypeStruct((NA_HEADS, NA_KH, GRID_W, n), F32),
        compiler_params=_params("parallel", "parallel"),
        name="na_bias",
    )(rpb.reshape(-1))


def _na_body(q_ref, k_ref, v_ref, ck_ref, cv_ref, bias_ref, o_ref, s_buf, e_buf, d_buf, *, rows):
    ck = ck_ref[...].astype(BF16)
    cv = cv_ref[...].astype(BF16)
    lo, hi = _pair_masks((GRID_W, LANES))
    scale = NA_DIM ** -0.5
    n_loc = NA_KH * GRID_W

    def window(r):
        r = jnp.clip(r, 0, rows - 1)
        r_start = jnp.clip(r - NA_KH // 2, 0, rows - NA_KH)
        return (r - r_start, pl.ds(pl.multiple_of(r * GRID_W, GRID_W), GRID_W),
                pl.ds(pl.multiple_of(r_start * GRID_W, GRID_W), n_loc))

    def scores(r, slot):
        vidx, qsl, ksl = window(r)
        q2 = q_ref[qsl, :]
        zero = jnp.zeros_like(q2)
        q2 = q2 * scale
        qs = jnp.concatenate([jnp.where(lo, q2, zero), jnp.where(hi, q2, zero)], axis=0)
        bias = jnp.concatenate([bias_ref[0, vidx], bias_ref[1, vidx]], axis=0)
        s_buf[slot, :, :n_loc] = _dot_nt(qs, k_ref[ksl, :]) + bias
        s_buf[slot, :, n_loc:] = _dot_nt(qs, ck)

    def numerators(slot):
        s = s_buf[slot]
        e = jnp.exp(s - jnp.max(s, axis=-1, keepdims=True))
        d_buf[slot] = jnp.broadcast_to(jnp.sum(e, axis=-1, keepdims=True), d_buf.shape[1:])
        e_buf[slot] = e.astype(BF16)

    def output(r, slot):
        _, qsl, ksl = window(r)
        o = (jnp.dot(e_buf[slot, :, :n_loc], v_ref[ksl, :], preferred_element_type=F32)
             + jnp.dot(e_buf[slot, :, n_loc:], cv, preferred_element_type=F32)) / d_buf[slot]
        o_ref[qsl, :] = jnp.where(lo, o[:GRID_W], o[GRID_W:]).astype(o_ref.dtype)

    s_buf[...] = jnp.zeros_like(s_buf)
    e_buf[...] = jnp.zeros_like(e_buf)
    d_buf[...] = jnp.ones_like(d_buf)

    def two_steps(j, carry):
        r = 2 * j
        scores(r, 0)
        numerators(1)
        output(r - 2, 0)
        scores(r + 1, 1)
        numerators(0)
        output(r - 1, 1)
        return carry

    lax.fori_loop(0, rows // 2 + 1, two_steps, 0, unroll=3)


def _neighbourhood_attention(q, k, v, ck, cv, bias):
    b, t, _ = q.shape
    lc = ck.shape[1]
    rows = t // GRID_W
    tok = pl.BlockSpec((None, t, LANES), lambda hp, bi: (bi, 0, hp))
    ctx = pl.BlockSpec((None, lc, LANES), lambda hp, bi: (bi, 0, hp))
    return pl.pallas_call(
        functools.partial(_na_body, rows=rows),
        grid=(NA_WIDTH // LANES, b),
        in_specs=[tok, tok, tok, ctx, ctx,
                  pl.BlockSpec((2, NA_KH, GRID_W, NA_KH * GRID_W), lambda hp, bi: (hp, 0, 0, 0))],
        out_specs=tok,
        out_shape=jax.ShapeDtypeStruct((b, t, NA_WIDTH), BF16),
        scratch_shapes=[pltpu.VMEM((2, 2 * GRID_W, NA_KH * GRID_W + lc), F32),
                        pltpu.VMEM((2, 2 * GRID_W, NA_KH * GRID_W + lc), BF16),
                        pltpu.VMEM((2, 2 * GRID_W, LANES), F32)],
        compiler_params=_params("parallel", "parallel"),
        name="na_attn",
    )(q, k, v, ck, cv, bias)


def _ret_body(q_ref, k_ref, v_ref, g_ref, df_ref, db_ref, beta_ref, s0f_ref, s0b_ref,
              o_ref, sf_ref, sb_ref, kvf_s, kvb_s, *, n_chunks):
    c_len = CHUNK
    lgf = -jnp.log1p(jnp.exp(-df_ref[...]))
    lgb = -jnp.log1p(jnp.exp(-db_ref[...]))
    shape = (c_len, c_len)
    i = lax.broadcasted_iota(I32, shape, 0).astype(F32)
    j = lax.broadcasted_iota(I32, shape, 1).astype(F32)
    dij = i - j
    d_comb = (jnp.where(dij >= 0, jnp.exp(jnp.maximum(dij, 0.0) * lgf), 0.0)
              + jnp.where(dij <= 0, jnp.exp(jnp.maximum(-dij, 0.0) * lgb), 0.0))
    xi_f = jnp.exp((i + 1.0) * lgf)
    xi_b = jnp.exp((c_len - i) * lgb)
    zeta_f = jnp.exp((c_len - 1.0 - i) * lgf)
    zeta_b = jnp.exp(i * lgb)
    g_f = jnp.exp(c_len * lgf)
    g_b = jnp.exp(c_len * lgb)

    def chunk(c):
        return pl.ds(pl.multiple_of(c * c_len, c_len), c_len)

    def kv_pass(c, carry):
        kc = k_ref[chunk(c), :]
        vc = v_ref[chunk(c), :].astype(F32)
        kvf_s[c] = _dot_tn(kc, (vc * zeta_f).astype(BF16))
        kvb_s[c] = _dot_tn(kc, (vc * zeta_b).astype(BF16))
        return carry

    lax.fori_loop(0, n_chunks, kv_pass, 0, unroll=RET_CHUNK_UNROLL)

    def scan_f(c, s):
        kv = kvf_s[c]
        kvf_s[c] = s
        return g_f * s + kv

    def scan_b(ci, s):
        c = n_chunks - 1 - ci
        kv = kvb_s[c]
        kvb_s[c] = s
        return g_b * s + kv

    sf_ref[...] = lax.fori_loop(0, n_chunks, scan_f, s0f_ref[...])
    sb_ref[...] = lax.fori_loop(0, n_chunks, scan_b, s0b_ref[...])

    beta = beta_ref[...]

    def out_pass(c, carry):
        qc = q_ref[chunk(c), :]
        kc = k_ref[chunk(c), :]
        vc = v_ref[chunk(c), :]
        scores = _dot_nt(qc, kc) * d_comb
        y = jnp.dot(scores.astype(BF16), vc, preferred_element_type=F32)
        qf = qc.astype(F32)
        y += jnp.dot((qf * xi_f).astype(BF16), kvf_s[c].astype(BF16), preferred_element_type=F32)
        y += jnp.dot((qf * xi_b).astype(BF16), kvb_s[c].astype(BF16), preferred_element_type=F32)
        o_ref[chunk(c), :] = (_standardize(y) * beta * _silu(g_ref[chunk(c), :])).astype(o_ref.dtype)
        return carry

    lax.fori_loop(0, n_chunks, out_pass, 0, unroll=RET_CHUNK_UNROLL)


def _retention(q, k, v, g, decay_f, decay_b, beta_ret, s0f, s0b):
    b, t, _ = q.shape
    n_chunks = t // CHUNK
    tok = pl.BlockSpec((None, t, LANES), lambda bi, h: (bi, 0, h))
    per_head = pl.BlockSpec((None, 1, LANES), lambda bi, h: (h, 0, 0))
    state = pl.BlockSpec((None, None, RET_DK, RET_DK), lambda bi, h: (bi, h, 0, 0))
    lanes = lambda a: jnp.broadcast_to(a.astype(F32)[:, None, None], (RET_HEADS, 1, LANES))
    st_shape = jax.ShapeDtypeStruct((b, RET_HEADS, RET_DK, RET_DK), F32)
    return pl.pallas_call(
        functools.partial(_ret_body, n_chunks=n_chunks),
        grid=(b, RET_HEADS),
        in_specs=[tok, tok, tok, tok, per_head, per_head, per_head, state, state],
        out_specs=[tok, state, state],
        out_shape=[jax.ShapeDtypeStruct((b, t, RET_WIDTH), BF16), st_shape, st_shape],
        scratch_shapes=[pltpu.VMEM((n_chunks, RET_DK, RET_DK), F32)] * 2,
        compiler_params=_params("parallel", "parallel"),
        name="retention",
    )(q, k, v, g, lanes(decay_f), lanes(decay_b), beta_ret.reshape(RET_HEADS, 1, LANES), s0f, s0b)


def _merge_body(x_ref, ona_ref, ret_ref, wna_ref, wret_ref, bna_ref, gate_ref, lg_ref, lb_ref, o_ref):
    o = ona_ref[...].astype(F32)
    na = o * lax.rsqrt(jnp.mean(o * o, axis=-1, keepdims=True) + LN_EPS) * bna_ref[...]
    mix = (jnp.dot(na.astype(BF16), wna_ref[...], preferred_element_type=F32)
           + jnp.dot(ret_ref[...], wret_ref[...], preferred_element_type=F32))
    y = DEEPNORM_ALPHA * x_ref[...] + gate_ref[...] * mix
    o_ref[...] = _standardize(y) * lg_ref[...] + lb_ref[...]


def _merge(x, o_na, ret, w_out, beta_na, gate, mod_row, ln_g, ln_b, tm):
    b, t, _ = x.shape
    tok = lambda bi, ti: (bi, ti, 0)
    row = pl.BlockSpec((1, D_MODEL), lambda bi, ti: (0, 0))
    half = pl.BlockSpec((None, tm, 512), tok)
    return pl.pallas_call(
        _merge_body,
        grid=(b, t // tm),
        in_specs=[pl.BlockSpec((None, tm, D_MODEL), tok), half, half,
                  pl.BlockSpec((512, D_MODEL), lambda bi, ti: (0, 0)),
                  pl.BlockSpec((512, D_MODEL), lambda bi, ti: (1, 0)),
                  pl.BlockSpec((1, 512), lambda bi, ti: (0, 0)),
                  pl.BlockSpec((None, 1, D_MODEL), lambda bi, ti: (mod_row(bi), 0, 0)),
                  row, row],
        out_specs=pl.BlockSpec((None, tm, D_MODEL), tok),
        out_shape=jax.ShapeDtypeStruct(x.shape, F32),
        compiler_params=_params("parallel", "parallel"),
        name="merge_ln1",
    )(x, o_na, ret, w_out, w_out, beta_na.reshape(1, 512), gate, ln_g.reshape(1, -1), ln_b.reshape(1, -1))


def _two_group_specs(n_p_blocks, blocks_per_batch_s, ctx_row):
    tok_p = lambda i: (jnp.minimum(i, n_p_blocks - 1), 0)
    tok_s = lambda i: (jnp.maximum(i - n_p_blocks, 0), 0)
    mod = lambda i: (jnp.where(i < n_p_blocks, ctx_row, jnp.maximum(i - n_p_blocks, 0) // blocks_per_batch_s), 0, 0)
    return tok_p, tok_s, mod


def _router_body(yp_ref, ys_ref, sh_ref, sc_ref, wh_ref, wl_ref, br_ref, p_ref, gt_ref, tab_ref, cnt_ref,
                 carry_s, *, n_p_blocks):
    i = pl.program_id(0)
    tm = yp_ref.shape[0]

    @pl.when(i == 0)
    def _():
        carry_s[...] = jnp.zeros_like(carry_s)

    y = jnp.where(i < n_p_blocks, yp_ref[...], ys_ref[...])
    h = y * (1.0 + sc_ref[...]) + sh_ref[...]
    h_hi = h.astype(BF16)
    h_lo = (h - h_hi.astype(F32)).astype(BF16)
    w_hi = wh_ref[...]
    lg = _dot_nt(jnp.concatenate([w_hi, wl_ref[...]], axis=0), h_hi)
    work = lg[:N_EXPERTS] + lg[N_EXPERTS:] + _dot_nt(w_hi, h_lo) + br_ref[...]
    eidx = lax.broadcasted_iota(I32, (N_EXPERTS, tm), 0).astype(F32)
    vals, idxs, hots = [], [], []
    for _ in range(TOP_K):
        mx = jnp.max(work, axis=0, keepdims=True)
        idx = jnp.min(jnp.where(work == mx, eidx, float(N_EXPERTS)), axis=0, keepdims=True)
        hot = eidx == idx
        vals.append(mx)
        idxs.append(idx)
        hots.append(hot)
        work = jnp.where(hot, -jnp.inf, work)
    exps = [jnp.exp(v - vals[0]) for v in vals]
    den = exps[0] + exps[1] + exps[2] + exps[3]
    sel = jnp.zeros((N_EXPERTS, tm), F32)
    for hot in hots:
        sel = sel + hot.astype(F32)
    r_i = lax.broadcasted_iota(I32, (tm, tm), 0)
    c_i = lax.broadcasted_iota(I32, (tm, tm), 1)
    earlier = (r_i < c_i).astype(BF16)
    before = jnp.dot(sel.astype(BF16), earlier, preferred_element_type=F32)
    n_e = jnp.sum(sel, axis=1, keepdims=True)
    n8 = jnp.floor((n_e + (RUN_ALIGN - 1.0)) * (1.0 / RUN_ALIGN)) * RUN_ALIGN + jnp.zeros((1, LANES), F32)
    e_r = lax.broadcasted_iota(I32, (N_EXPERTS, N_EXPERTS), 0)
    e_c = lax.broadcasted_iota(I32, (N_EXPERTS, N_EXPERTS), 1)
    off = jnp.dot((e_c < e_r).astype(BF16), n8.astype(BF16), preferred_element_type=F32)
    local = before + jnp.concatenate([off] * (tm // LANES), axis=1)
    rows = [jnp.sum(jnp.where(hot, local, 0.0), axis=0, keepdims=True) for hot in hots]
    p_ref[...] = jnp.concatenate(rows, axis=0).astype(I32)
    gt_ref[...] = jnp.concatenate([e / den for e in exps], axis=0)
    carry = carry_s[...]
    lane = lax.broadcasted_iota(I32, (N_EXPERTS, LANES), 1)
    tab_ref[...] = jnp.where(lane == 0, n8, jnp.where(lane == 1, carry, jnp.where(lane == 2, off, 0.0)))
    carry = carry + n8
    carry_s[...] = carry
    cnt_ref[...] = carry


def _router(yp, ys, shift, scale, ctx_row, s_blocks_per_batch, w_router, b_router, tm):
    n_p, n_s = yp.shape[0], ys.shape[0]
    n = n_p + n_s
    npb = n_p // tm
    tok_p, tok_s, mod = _two_group_specs(npb, s_blocks_per_batch, ctx_row)
    w_t = w_router.T
    w_hi = w_t.astype(BF16)
    w_lo = (w_t - w_hi.astype(F32)).astype(BF16)
    br = jnp.broadcast_to(b_router[:, None], (N_EXPERTS, tm))
    o4 = pl.BlockSpec((TOP_K, tm), lambda i: (0, i))
    whole = lambda shape: pl.BlockSpec(shape, lambda i: (0, 0))
    return pl.pallas_call(
        functools.partial(_router_body, n_p_blocks=npb),
        grid=(n // tm,),
        in_specs=[pl.BlockSpec((tm, D_MODEL), tok_p), pl.BlockSpec((tm, D_MODEL), tok_s),
                  pl.BlockSpec((None, 1, D_MODEL), mod), pl.BlockSpec((None, 1, D_MODEL), mod),
                  whole((N_EXPERTS, D_MODEL)), whole((N_EXPERTS, D_MODEL)), whole((N_EXPERTS, tm))],
        out_specs=[o4, o4, pl.BlockSpec((None, N_EXPERTS, LANES), lambda i: (i, 0, 0)),
                   whole((N_EXPERTS, LANES))],
        out_shape=[jax.ShapeDtypeStruct((TOP_K, n), I32), jax.ShapeDtypeStruct((TOP_K, n), F32),
                   jax.ShapeDtypeStruct((n // tm, N_EXPERTS, LANES), F32),
                   jax.ShapeDtypeStruct((N_EXPERTS, LANES), F32)],
        scratch_shapes=[pltpu.VMEM((N_EXPERTS, LANES), F32)],
        compiler_params=_params("arbitrary"),
        name="moe_router",
    )(yp, ys, shift, scale, w_hi, w_lo, br)


def _for_run_pieces(n_rows, visit):
    off = 0
    for size in RUN_PIECES:
        present = (n_rows & size) != 0
        visit(off, size, present)
        off = off + jnp.where(present, size, 0)


def _start_unit_copies(unit_ref, local_buf, sorted_hbm, sem, to_hbm):
    def one(u, carry):
        loc = local_buf.at[pl.ds(pl.multiple_of(u * RUN_ALIGN, RUN_ALIGN), RUN_ALIGN)]
        glob = sorted_hbm.at[pl.ds(pl.multiple_of(unit_ref[u], RUN_ALIGN), RUN_ALIGN)]
        copy = pltpu.make_async_copy(loc, glob, sem) if to_hbm else pltpu.make_async_copy(glob, loc, sem)
        copy.start()
        return carry

    lax.fori_loop(0, LOCAL_UNITS, one, 0, unroll=8)


def _wait_unit_copies(local_buf, sorted_hbm, sem):
    whole = sorted_hbm.at[pl.ds(0, LOCAL_ROWS)]
    pltpu.make_async_copy(whole, local_buf, sem).wait()


def _dispatch_body(unit_ref, zlo_ref, zn_ref, p_ref, yp_ref, ys_ref, sh_ref, sc_ref, xs_ref,
                   xbuf, zbuf, sems, zsem, *, n_p_blocks, n_blocks):
    i = pl.program_id(0)
    slot = i % 2
    y = jnp.where(i < n_p_blocks, yp_ref[...], ys_ref[...])
    h = (y * (1.0 + sc_ref[...]) + sh_ref[...]).astype(BF16)
    rows = p_ref[...]
    r_iota = lax.broadcasted_iota(I32, (LOCAL_ROWS, rows.shape[1]), 0)
    place = r_iota == rows[0:1]
    for kk in range(1, TOP_K):
        place = place | (r_iota == rows[kk:kk + 1])
    xbuf[slot] = _pack_bf16_pairs(
        jnp.dot(jnp.where(place, 1.0, 0.0).astype(BF16), h, preferred_element_type=F32))

    _start_unit_copies(unit_ref, xbuf.at[slot], xs_ref, sems.at[slot], to_hbm=True)

    @pl.when(i > 0)
    def _():
        _wait_unit_copies(xbuf.at[1 - slot], xs_ref, sems.at[1 - slot])

    @pl.when(i == n_blocks - 1)
    def _():
        _wait_unit_copies(xbuf.at[slot], xs_ref, sems.at[slot])
        zbuf[...] = jnp.zeros_like(zbuf)
        sem = zsem

        def per_expert(e, carry):
            lo = zlo_ref[e]

            def visit(off, size, present):
                copy = pltpu.make_async_copy(
                    zbuf.at[pl.ds(0, size)], xs_ref.at[pl.ds(pl.multiple_of(lo + off, RUN_ALIGN), size)], sem)

                @pl.when(present)
                def _():
                    copy.start()
                    copy.wait()

            _for_run_pieces(zn_ref[e], visit)
            return carry

        lax.fori_loop(0, N_EXPERTS, per_expert, 0)

        tail_lo = zlo_ref[N_EXPERTS]

        def tile_copy(t):
            start = pl.multiple_of(tail_lo + t * EXPERT_TILE, EXPERT_TILE)
            return pltpu.make_async_copy(zbuf, xs_ref.at[pl.ds(start, EXPERT_TILE)], sem)

        def t_issue(t, c2):
            tile_copy(t).start()
            return c2

        def t_drain(t, c2):
            tile_copy(0).wait()
            return c2

        n_tail = zn_ref[N_EXPERTS] // EXPERT_TILE
        lax.fori_loop(0, n_tail, t_issue, 0)
        lax.fori_loop(0, n_tail, t_drain, 0)


def _dispatch(yp, ys, shift, scale, ctx_row, s_blocks_per_batch, local_rows, unit_dst, zero_lo, zero_n,
              total_rows, tm):
    n_p, n_s = yp.shape[0], ys.shape[0]
    n = n_p + n_s
    npb = n_p // tm
    nb = n // tm
    tok_p, tok_s, mod = _two_group_specs(npb, s_blocks_per_batch, ctx_row)
    smem_all = pl.BlockSpec(memory_space=pltpu.SMEM)
    return pl.pallas_call(
        functools.partial(_dispatch_body, n_p_blocks=npb, n_blocks=nb),
        grid=(nb,),
        in_specs=[pl.BlockSpec((UNIT_TABLE,), lambda i: (i,), memory_space=pltpu.SMEM), smem_all, smem_all,
                  pl.BlockSpec((TOP_K, tm), lambda i: (0, i)),
                  pl.BlockSpec((tm, D_MODEL), tok_p), pl.BlockSpec((tm, D_MODEL), tok_s),
                  pl.BlockSpec((None, 1, D_MODEL), mod), pl.BlockSpec((None, 1, D_MODEL), mod)],
        out_specs=pl.BlockSpec(memory_space=pl.ANY),
        out_shape=jax.ShapeDtypeStruct((total_rows, D_MODEL // 2), U32),
        scratch_shapes=[pltpu.VMEM((2, LOCAL_ROWS, D_MODEL // 2), U32),
                        pltpu.VMEM((EXPERT_TILE, D_MODEL // 2), U32),
                        pltpu.SemaphoreType.DMA((2,)), pltpu.SemaphoreType.DMA(())],
        compiler_params=_params("arbitrary"),
        name="moe_dispatch",
    )(unit_dst, zero_lo, zero_n, local_rows, yp, ys, shift, scale)


def _expert_body(te_ref, first_ref, nu_ref, x_ref, wgu_ref, bgu_ref, wd_ref, bd_ref, y_ref, act_s, wgu_s, wd_s):
    j = pl.program_id(0)

    @pl.when(first_ref[j] == 1)
    def _():
        blk = 2 * LANES
        src = lax.broadcasted_iota(I32, (blk, blk), 0)
        dst = lax.broadcasted_iota(I32, (blk, blk), 1)
        perm = (src == jnp.where(dst < LANES, 2 * dst, 2 * (dst - LANES) + 1)).astype(F32).astype(BF16)
        for s in range(2 * D_FF // blk):
            sl = slice(s * blk, (s + 1) * blk)
            wgu_s[:, sl] = jnp.dot(wgu_ref[:, sl].astype(BF16), perm, preferred_element_type=F32).astype(BF16)
        wd_s[...] = wd_ref[...].astype(BF16)

    @pl.when(j < nu_ref[0])
    def _():
        x = _unpack_bf16_pairs(x_ref[...])
        for c in range(D_FF // LANES):
            sl = slice(c * 2 * LANES, (c + 1) * 2 * LANES)
            hu = jnp.dot(x, wgu_s[:, sl], preferred_element_type=F32) + bgu_ref[:, sl]
            x_glu = jnp.minimum(hu[:, :LANES], SWIGLU_LIMIT)
            x_lin = jnp.clip(hu[:, LANES:], -SWIGLU_LIMIT, SWIGLU_LIMIT)
            act = x_glu * (1.0 / (1.0 + jnp.exp(-SWIGLU_ALPHA * x_glu))) * (x_lin + 1.0)
            act_s[:, c * LANES:(c + 1) * LANES] = act.astype(BF16)
        y_ref[...] = _pack_bf16_pairs(jnp.dot(act_s[...], wd_s[...], preferred_element_type=F32) + bd_ref[...])

    @pl.when(j >= nu_ref[0])
    def _():
        y_ref[...] = jnp.zeros_like(y_ref)


def _experts(tile_expert, tile_first, n_used, xs, cap_rows, w_gate_up, bgu, w_down, bd):
    n_tiles = cap_rows // EXPERT_TILE
    of_expert = lambda j, te, first, nu: (te[j], 0, 0)
    grid_spec = pltpu.PrefetchScalarGridSpec(
        num_scalar_prefetch=3,
        grid=(n_tiles,),
        in_specs=[pl.BlockSpec((EXPERT_TILE, D_MODEL // 2),
                               lambda j, te, first, nu: (jnp.minimum(j, nu[0] - 1), 0)),
                  pl.BlockSpec((None, D_MODEL, 2 * D_FF), of_expert),
                  pl.BlockSpec((None, 1, 2 * D_FF), of_expert),
                  pl.BlockSpec((None, D_FF, D_MODEL), of_expert),
                  pl.BlockSpec((None, 1, D_MODEL), of_expert)],
        out_specs=pl.BlockSpec((EXPERT_TILE, D_MODEL // 2), lambda j, te, first, nu: (j, 0)),
        scratch_shapes=[pltpu.VMEM((EXPERT_TILE, D_FF), BF16),
                        pltpu.VMEM((D_MODEL, 2 * D_FF), BF16), pltpu.VMEM((D_FF, D_MODEL), BF16)],
    )
    return pl.pallas_call(
        _expert_body,
        grid_spec=grid_spec,
        out_shape=jax.ShapeDtypeStruct((cap_rows, D_MODEL // 2), U32),
        compiler_params=_params("arbitrary"),
        name="moe_experts",
    )(tile_expert, tile_first, n_used, xs, w_gate_up, bgu, w_down, bd)


def _combine_body(unit_ref, next_unit_ref, y1_ref, p_ref, gt_ref, gate_ref, lg_ref, lb_ref, ys_ref, o_ref,
                  ybuf, sems, *, n_blocks):
    i = pl.program_id(0)
    slot = i % 2
    tm = y1_ref.shape[0]

    @pl.when(i == 0)
    def _():
        _start_unit_copies(unit_ref, ybuf.at[0], ys_ref, sems.at[0], to_hbm=False)

    @pl.when(i + 1 < n_blocks)
    def _():
        _start_unit_copies(next_unit_ref, ybuf.at[1 - slot], ys_ref, sems.at[1 - slot], to_hbm=False)

    _wait_unit_copies(ybuf.at[slot], ys_ref, sems.at[slot])

    rows = p_ref[...]
    gates = gt_ref[...]
    c_iota = lax.broadcasted_iota(I32, (tm, LOCAL_ROWS), 1)
    weight = jnp.where(c_iota == rows[:, 0:1], gates[:, 0:1], 0.0)
    for kk in range(1, TOP_K):
        weight = weight + jnp.where(c_iota == rows[:, kk:kk + 1], gates[:, kk:kk + 1], 0.0)
    w_hi = weight.astype(BF16)
    w_lo = (weight - w_hi.astype(F32)).astype(BF16)
    yb = _unpack_bf16_pairs(ybuf[slot])
    f = (jnp.dot(w_hi, yb, preferred_element_type=F32) + jnp.dot(w_lo, yb, preferred_element_type=F32))
    y = DEEPNORM_ALPHA * y1_ref[...] + gate_ref[...] * f
    o_ref[...] = _standardize(y) * lg_ref[...] + lb_ref[...]


def _combine(y1, local_rows, gates, unit_src, block0, gate_mod, mod_row, blocks_per_batch,
             y_sorted, ln_g, ln_b, tm):
    n_g = y1.shape[0]
    nb = n_g // tm
    row = pl.BlockSpec((1, D_MODEL), lambda i: (0, 0))
    pair = pl.BlockSpec((tm, TOP_K), lambda i: (i + block0, 0))
    units = lambda ahead: pl.BlockSpec(
        (UNIT_TABLE,), lambda i: (jnp.minimum(i + ahead, nb - 1) + block0,), memory_space=pltpu.SMEM)
    return pl.pallas_call(
        functools.partial(_combine_body, n_blocks=nb),
        grid=(nb,),
        in_specs=[units(0), units(1),
                  pl.BlockSpec((tm, D_MODEL), lambda i: (i, 0)), pair, pair,
                  pl.BlockSpec((None, 1, D_MODEL), lambda i: (mod_row(i // blocks_per_batch), 0, 0)),
                  row, row,
                  pl.BlockSpec(memory_space=pl.ANY)],
        out_specs=pl.BlockSpec((tm, D_MODEL), lambda i: (i, 0)),
        out_shape=jax.ShapeDtypeStruct((n_g, D_MODEL), F32),
        scratch_shapes=[pltpu.VMEM((2, LOCAL_ROWS, D_MODEL // 2), U32), pltpu.SemaphoreType.DMA((2,))],
        compiler_params=_params("arbitrary"),
        name="moe_combine_ln2",
    )(unit_src, unit_src, y1, local_rows, gates, gate_mod,
      ln_g.reshape(1, -1), ln_b.reshape(1, -1), y_sorted)


def _regroup_gate_up_bias(b_gate_up):
    e = b_gate_up.shape[0]
    nblk = D_FF // LANES
    return b_gate_up.reshape(e, 1, nblk, LANES, 2).transpose(0, 1, 2, 4, 3).reshape(e, 1, 2 * D_FF)


def _routing_plan(table, totals, n_pairs):
    nb = table.shape[0]
    run_n = table[:, :, 0].astype(I32)
    rows_before = table[:, :, 1].astype(I32)
    run_local = table[:, :, 2].astype(I32)
    cnt = totals[:, 0].astype(I32)
    padded = (cnt + EXPERT_TILE - 1) // EXPERT_TILE * EXPERT_TILE
    pad_end = jnp.cumsum(padded)
    pad_start = pad_end - padded
    run_global = pad_start[None, :] + rows_before
    rows_max = n_pairs + nb * N_EXPERTS * (RUN_ALIGN - 1) + N_EXPERTS * (EXPERT_TILE - 1)
    n_tiles = -(-rows_max // EXPERT_TILE)
    n_used = pad_end[-1] // EXPERT_TILE
    tile_row = jnp.minimum(jnp.arange(n_tiles, dtype=I32), n_used - 1) * EXPERT_TILE
    tile_expert = jnp.sum((pad_end[None, :] <= tile_row[:, None]).astype(I32), axis=1)
    tile_expert = jnp.minimum(tile_expert, N_EXPERTS - 1)
    tile_id = jnp.arange(n_tiles, dtype=I32)
    changed = jnp.concatenate([jnp.ones((1,), bool), tile_expert[1:] != tile_expert[:-1]])
    tile_first = (changed & (tile_id < n_used)).astype(I32)
    cap_rows = n_tiles * EXPERT_TILE
    spare_rows = -(-LOCAL_ROWS // EXPERT_TILE) * EXPERT_TILE
    total_rows = cap_rows + 2 * spare_rows
    unit_row = jnp.arange(LOCAL_UNITS, dtype=I32) * RUN_ALIGN
    run_end = run_local + run_n
    owner = jnp.sum((run_end[:, None, :] <= unit_row[None, :, None]).astype(I32), axis=-1)
    own_hot = owner[:, :, None] == jnp.arange(N_EXPERTS, dtype=I32)
    shift = jnp.sum(jnp.where(own_hot, (run_global - run_local)[:, None, :], 0), axis=-1)
    used = owner < N_EXPERTS
    sorted_row = unit_row[None, :] + shift
    slot = (jnp.arange(nb, dtype=I32) % 2)[:, None]
    unit_dst = jnp.where(used, sorted_row, cap_rows + slot * spare_rows + unit_row[None, :])
    unit_src = jnp.where(used, sorted_row, unit_row[None, :])
    table_of = lambda a: jnp.pad(a, ((0, 0), (0, UNIT_TABLE - LOCAL_UNITS))).reshape(-1)
    zero_lo = jnp.concatenate([pad_start + cnt, pad_end[-1:]])
    zero_n = jnp.concatenate([padded - cnt, total_rows - pad_end[-1:]])
    return ((table_of(unit_dst), table_of(unit_src)), (tile_expert, tile_first, n_used.reshape(1)),
            zero_lo, zero_n, cap_rows, total_rows)


def kernel(x_prompt, x_sample, cache_na_k, cache_na_v, state_ret_fwd, state_ret_bwd, c, c_ctx, w_ada, b_ada, w_in, rpb, ret_decay_fwd, ret_decay_bwd, beta_na, beta_ret, w_out, ln1_g, ln1_b, w_router, b_router, w_gate_up, b_gate_up, w_down, b_down, ln2_g, ln2_b):
    bp, tp, _ = x_prompt.shape
    bs, ts, _ = x_sample.shape
    assert w_ada.shape[0] == DEPTH == 1
    l = 0
    ctx_row = bs
    sample_row = lambda bi: bi
    prompt_row = lambda bi: ctx_row

    cond = jnp.concatenate([c, c_ctx[None, :], jnp.zeros((16 - bs - 1, D_MODEL), F32)], axis=0)
    m = _adaln(cond, w_ada[l], b_ada[l]).reshape(16, 6, 1, D_MODEL)
    sh_a, sc_a, gt_a, sh_f, sc_f, gt_f = (m[:, i] for i in range(6))

    w_in_b = w_in[l].astype(BF16)
    w_out_b = w_out[l].astype(BF16)

    q, k_p, v_p, qr, kr, vr, g = _project(x_prompt, sh_a, sc_a, prompt_row, w_in_b, tp, F32, rope=False)
    o_na = _context_attention(q, k_p, v_p)
    zeros_state = jnp.zeros((bp, RET_HEADS, RET_DK, RET_DK), F32)
    ret, s_f, s_b = _retention(qr, kr, vr, g, ret_decay_fwd[l], ret_decay_bwd[l], beta_ret[l],
                               zeros_state, zeros_state)
    yp1 = _merge(x_prompt, o_na, ret, w_out_b, beta_na[l], gt_a, prompt_row, ln1_g[l], ln1_b[l], tp)

    q, k_s, v_s, qr, kr, vr, g = _project(x_sample, sh_a, sc_a, sample_row, w_in_b, 512, BF16, rope=True)
    lc = cache_na_k.shape[2]
    o_na = _neighbourhood_attention(q, k_s, v_s, cache_na_k[:, l].reshape(bs, lc, NA_WIDTH),
                                    cache_na_v[:, l].reshape(bs, lc, NA_WIDTH), _na_bias(rpb[l]))
    ret, _, _ = _retention(qr, kr, vr, g, ret_decay_fwd[l], ret_decay_bwd[l], beta_ret[l],
                           state_ret_fwd[:, l], state_ret_bwd[:, l])
    ys1 = _merge(x_sample, o_na, ret, w_out_b, beta_na[l], gt_a, sample_row, ln1_g[l], ln1_b[l], 512)

    tm = TOKEN_BLOCK
    yp1f = yp1.reshape(bp * tp, D_MODEL)
    ys1f = ys1.reshape(bs * ts, D_MODEL)
    local_rows, gates, table, totals = _router(yp1f, ys1f, sh_f, sc_f, ctx_row, ts // tm,
                                               w_router[l], b_router[l], tm)
    (unit_dst, unit_src), tiles, zero_lo, zero_n, cap_rows, total_rows = _routing_plan(
        table, totals, (bp * tp + bs * ts) * TOP_K)
    xs = _dispatch(yp1f, ys1f, sh_f, sc_f, ctx_row, ts // tm, local_rows, unit_dst, zero_lo, zero_n,
                   total_rows, tm)
    y_sorted = _experts(*tiles, xs, cap_rows, w_gate_up[l], _regroup_gate_up_bias(b_gate_up[l]), w_down[l],
                        b_down[l][:, None, :])
    rows_t, gates_t = local_rows.T, gates.T
    yp = _combine(yp1f, rows_t, gates_t, unit_src, 0, gt_f, prompt_row, 1, y_sorted, ln2_g[l], ln2_b[l], tm)
    ys = _combine(ys1f, rows_t, gates_t, unit_src, (bp * tp) // tm, gt_f, sample_row, ts // tm, y_sorted,
                  ln2_g[l], ln2_b[l], tm)

    return (yp.reshape(bp, tp, D_MODEL), ys.reshape(bs, ts, D_MODEL),
            k_p.reshape(bp, 1, tp, NA_HEADS, NA_DIM), v_p.reshape(bp, 1, tp, NA_HEADS, NA_DIM),
            s_f[:, None], s_b[:, None])
```

```python
import functools

import jax
import jax.numpy as jnp
from jax import lax
from jax.experimental import pallas as pl
from jax.experimental.pallas import tpu as pltpu

F32 = jnp.float32
BF16 = jnp.bfloat16
I32 = jnp.int32
U32 = jnp.uint32

D_MODEL = 1024
GRID_W = 64
NA_HEADS = 8
NA_DIM = 64
NA_WIDTH = NA_HEADS * NA_DIM
NA_KH = 8
NA_KW = 16
RET_HEADS = 4
RET_DK = 128
RET_WIDTH = RET_HEADS * RET_DK
CHUNK = 128
N_EXPERTS = 32
TOP_K = 4
D_FF = 1024
SWIGLU_LIMIT = 7.0
SWIGLU_ALPHA = 1.702
ROPE_BASE = 10000.0
LN_EPS = 1e-5
DEPTH = 1
DEEPNORM_ALPHA = (2.0 * DEPTH) ** 0.25

LANES = 128
NEG_BIG = -1e30
TOKEN_BLOCK = 256
EXPERT_TILE = 512
VMEM_LIMIT = 56 * 1024 * 1024
RET_CHUNK_UNROLL = 8
SUBLANES = 8
RUN_ALIGN = SUBLANES
LOCAL_ROWS = TOKEN_BLOCK * TOP_K + N_EXPERTS * RUN_ALIGN
LOCAL_UNITS = LOCAL_ROWS // RUN_ALIGN
UNIT_TABLE = 256
RUN_PIECES = tuple(TOKEN_BLOCK >> s for s in range(TOKEN_BLOCK.bit_length())
                   if (TOKEN_BLOCK >> s) >= RUN_ALIGN)


def _params(*sem):
    return pltpu.CompilerParams(dimension_semantics=sem, vmem_limit_bytes=VMEM_LIMIT)


def _silu(x):
    return x / (1.0 + jnp.exp(-x))


def _standardize(x):
    mu = jnp.mean(x, axis=-1, keepdims=True)
    xc = x - mu
    var = jnp.mean(xc * xc, axis=-1, keepdims=True)
    return xc * lax.rsqrt(var + LN_EPS)


def _pack_bf16_pairs(x):
    bits = lax.bitcast_convert_type(x.astype(BF16).astype(F32), U32)
    half = x.shape[1] // 2
    return bits[:, :half] | lax.shift_right_logical(bits[:, half:], jnp.uint32(16))


def _unpack_bf16_pairs(p):
    hi = lax.bitcast_convert_type(p & jnp.uint32(0xFFFF0000), F32)
    lo = lax.bitcast_convert_type(lax.shift_left(p, jnp.uint32(16)), F32)
    return jnp.concatenate([hi, lo], axis=1).astype(BF16)


def _dot_nt(a, b):
    return lax.dot_general(a, b, (((1,), (1,)), ((), ())), preferred_element_type=F32)


def _dot_tn(a, b):
    return lax.dot_general(a, b, (((0,), (0,)), ((), ())), preferred_element_type=F32)


def _ada_body(c_ref, w_ref, b_ref, o_ref):
    s = _silu(c_ref[...])
    o_ref[...] = jnp.dot(s, w_ref[...], preferred_element_type=F32,
                         precision=lax.Precision.HIGHEST) + b_ref[...]


def _adaln(cond, w_ada, b_ada):
    r = cond.shape[0]
    n = w_ada.shape[1]
    tn = 1536
    return pl.pallas_call(
        _ada_body,
        grid=(n // tn,),
        in_specs=[pl.BlockSpec((r, D_MODEL), lambda j: (0, 0)),
                  pl.BlockSpec((D_MODEL, tn), lambda j: (0, j)),
                  pl.BlockSpec((1, tn), lambda j: (0, j))],
        out_specs=pl.BlockSpec((r, tn), lambda j: (0, j)),
        out_shape=jax.ShapeDtypeStruct((r, n), F32),
        compiler_params=_params("arbitrary"),
        name="adaln",
    )(cond, w_ada, b_ada.reshape(1, n))


def _proj_body(*refs, rope):
    if rope:
        (x_ref, sh_ref, sc_ref, w_ref, cos_ref, sa_ref, sb_ref,
         q_ref, k_ref, v_ref, qr_ref, kr_ref, vr_ref, g_ref) = refs
    else:
        (x_ref, sh_ref, sc_ref, w_ref,
         q_ref, k_ref, v_ref, qr_ref, kr_ref, vr_ref, g_ref) = refs
    h = (x_ref[...] * (1.0 + sc_ref[...]) + sh_ref[...]).astype(BF16)

    def cols(c):
        return jnp.dot(h, w_ref[:, c * 512:(c + 1) * 512], preferred_element_type=F32)

    q_ref[...] = cols(0).astype(q_ref.dtype)
    k_ref[...] = cols(1).astype(k_ref.dtype)
    v_ref[...] = cols(2).astype(v_ref.dtype)
    pq = cols(3)
    pk = cols(4) * (RET_DK ** -0.5)
    if rope:
        cs, sa, sb = cos_ref[...], sa_ref[...], sb_ref[...]
        for hd in range(RET_HEADS):
            sl = slice(hd * LANES, (hd + 1) * LANES)
            for p, o_ref in ((pq, qr_ref), (pk, kr_ref)):
                xs = p[:, sl]
                rot = xs * cs + pltpu.roll(xs, 96, 1) * sa + pltpu.roll(xs, 32, 1) * sb
                o_ref[:, sl] = rot.astype(o_ref.dtype)
    else:
        qr_ref[...] = pq.astype(qr_ref.dtype)
        kr_ref[...] = pk.astype(kr_ref.dtype)
    vr_ref[...] = cols(5).astype(vr_ref.dtype)
    g_ref[...] = cols(6)


def _rope_tables(t_len):
    t = jnp.arange(t_len, dtype=jnp.int32)
    pos_row = (t // GRID_W).astype(F32)[:, None]
    pos_col = (t % GRID_W).astype(F32)[:, None]
    lane = jnp.arange(LANES, dtype=jnp.int32)[None, :]
    n_freq = RET_DK // 4
    inv_freq = ROPE_BASE ** (-(lane % n_freq).astype(F32) / n_freq)
    ang = jnp.where(lane < RET_DK // 2, pos_row, pos_col) * inv_freq
    first = (lane % (2 * n_freq)) < n_freq
    cos = jnp.cos(ang)
    sin = jnp.sin(ang)
    return cos, jnp.where(first, -sin, 0.0), jnp.where(first, 0.0, sin)


def _project(x, shift, scale, mod_row, w_in, tm, kv_dtype, rope):
    b, t, _ = x.shape
    tok = lambda bi, ti: (bi, ti, 0)
    mod = lambda bi, ti: (mod_row(bi), 0, 0)
    in_specs = [pl.BlockSpec((None, tm, D_MODEL), tok),
                pl.BlockSpec((None, 1, D_MODEL), mod),
                pl.BlockSpec((None, 1, D_MODEL), mod),
                pl.BlockSpec(w_in.shape, lambda bi, ti: (0, 0))]
    args = [x, shift, scale, w_in]
    if rope:
        tab = lambda bi, ti: (ti, 0)
        in_specs += [pl.BlockSpec((tm, LANES), tab)] * 3
        args += list(_rope_tables(t))
    o512 = pl.BlockSpec((None, tm, 512), tok)
    shp = lambda dt: jax.ShapeDtypeStruct((b, t, 512), dt)
    return pl.pallas_call(
        functools.partial(_proj_body, rope=rope),
        grid=(b, t // tm),
        in_specs=in_specs,
        out_specs=[o512] * 7,
        out_shape=[shp(BF16), shp(kv_dtype), shp(kv_dtype), shp(BF16), shp(BF16), shp(BF16), shp(F32)],
        compiler_params=_params("parallel", "parallel"),
        name="in_proj",
    )(*args)


def _pair_masks(shape):
    lane = lax.broadcasted_iota(I32, shape, 1)
    return lane < NA_DIM, lane >= NA_DIM


def _ctx_attn_body(q_ref, k_ref, v_ref, o_ref):
    q2 = q_ref[...]
    k2 = k_ref[...].astype(BF16)
    v2 = v_ref[...].astype(BF16)
    t = q2.shape[0]
    lo, hi = _pair_masks(q2.shape)
    zero = jnp.zeros_like(q2)
    qs = jnp.concatenate([jnp.where(lo, q2, zero), jnp.where(hi, q2, zero)], axis=0)
    s = _dot_nt(qs, k2) * (NA_DIM ** -0.5)
    e = jnp.exp(s - jnp.max(s, axis=-1, keepdims=True))
    den = jnp.sum(e, axis=-1, keepdims=True)
    o = jnp.dot(e.astype(BF16), v2, preferred_element_type=F32) / den
    o_ref[...] = jnp.where(lo, o[:t], o[t:]).astype(o_ref.dtype)


def _context_attention(q, k, v):
    b, t, _ = q.shape
    spec = pl.BlockSpec((None, t, LANES), lambda bi, hp: (bi, 0, hp))
    return pl.pallas_call(
        _ctx_attn_body,
        grid=(b, NA_WIDTH // LANES),
        in_specs=[spec] * 3,
        out_specs=spec,
        out_shape=jax.ShapeDtypeStruct((b, t, NA_WIDTH), BF16),
        compiler_params=_params("parallel", "parallel"),
        name="ctx_attn",
    )(q, k, v)


def _na_bias_body(rpb_ref, o_ref):
    h = pl.program_id(0)
    v = pl.program_id(1)
    shape = (GRID_W, LANES)
    w = lax.broadcasted_iota(I32, shape, 0)
    lane = lax.broadcasted_iota(I32, shape, 1)
    upper = lane >= GRID_W
    kc = jnp.where(upper, lane - GRID_W, lane)
    cdiff = kc - w + (NA_KW - 1)
    cstart = jnp.clip(w - NA_KW // 2, 0, GRID_W - NA_KW)
    inwin = (kc >= cstart) & (kc < cstart + NA_KW)
    n_ro = 2 * NA_KH - 1
    n_co = 2 * NA_KW - 1
    for j in range(NA_KH // 2):
        ro = 2 * j - v + (NA_KH - 1)
        base0 = (h * n_ro + ro) * n_co
        acc = jnp.zeros(shape, F32)
        for c in range(n_co):
            val = jnp.where(upper, rpb_ref[base0 + n_co + c], rpb_ref[base0 + c])
            acc = jnp.where(cdiff == c, val, acc)
        o_ref[:, j * LANES:(j + 1) * LANES] = jnp.where(inwin, acc, NEG_BIG)


def _na_bias(rpb):
    n = NA_KH * GRID_W
    return pl.pallas_call(
        _na_bias_body,
        grid=(NA_HEADS, NA_KH),
        in_specs=[pl.BlockSpec(memory_space=pltpu.SMEM)],
        out_specs=pl.BlockSpec((None, None, GRID_W, n), lambda h, v: (h, v, 0, 0)),
        out_shape=jax.ShapeDtypeStruct((NA_HEADS, NA_KH, GRID_W, n), F32),
        compiler_params=_params("parallel", "parallel"),
        name="na_bias",
    )(rpb.reshape(-1))


def _na_body(q_ref, k_ref, v_ref, ck_ref, cv_ref, bias_ref, o_ref, s_buf, e_buf, d_buf, *, rows):
    ck = ck_ref[...].astype(BF16)
    cv = cv_ref[...].astype(BF16)
    lo, hi = _pair_masks((GRID_W, LANES))
    scale = NA_DIM ** -0.5
    n_loc = NA_KH * GRID_W

    def window(r):
        r = jnp.clip(r, 0, rows - 1)
        r_start = jnp.clip(r - NA_KH // 2, 0, rows - NA_KH)
        return (r - r_start, pl.ds(pl.multiple_of(r * GRID_W, GRID_W), GRID_W),
                pl.ds(pl.multiple_of(r_start * GRID_W, GRID_W), n_loc))

    def scores(r, slot):
        vidx, qsl, ksl = window(r)
        q2 = q_ref[qsl, :]
        zero = jnp.zeros_like(q2)
        q2 = q2 * scale
        qs = jnp.concatenate([jnp.where(lo, q2, zero), jnp.where(hi, q2, zero)], axis=0)
        bias = jnp.concatenate([bias_ref[0, vidx], bias_ref[1, vidx]], axis=0)
        s_buf[slot, :, :n_loc] = _dot_nt(qs, k_ref[ksl, :]) + bias
        s_buf[slot, :, n_loc:] = _dot_nt(qs, ck)

    def numerators(slot):
        s = s_buf[slot]
        e = jnp.exp(s - jnp.max(s, axis=-1, keepdims=True))
        d_buf[slot] = jnp.broadcast_to(jnp.sum(e, axis=-1, keepdims=True), d_buf.shape[1:])
        e_buf[slot] = e.astype(BF16)

    def output(r, slot):
        _, qsl, ksl = window(r)
        o = (jnp.dot(e_buf[slot, :, :n_loc], v_ref[ksl, :], preferred_element_type=F32)
             + jnp.dot(e_buf[slot, :, n_loc:], cv, preferred_element_type=F32)) / d_buf[slot]
        o_ref[qsl, :] = jnp.where(lo, o[:GRID_W], o[GRID_W:]).astype(o_ref.dtype)

    s_buf[...] = jnp.zeros_like(s_buf)
    e_buf[...] = jnp.zeros_like(e_buf)
    d_buf[...] = jnp.ones_like(d_buf)

    def two_steps(j, carry):
        r = 2 * j
        scores(r, 0)
        numerators(1)
        output(r - 2, 0)
        scores(r + 1, 1)
        numerators(0)
        output(r - 1, 1)
        return carry

    lax.fori_loop(0, rows // 2 + 1, two_steps, 0, unroll=3)


def _neighbourhood_attention(q, k, v, ck, cv, bias):
    b, t, _ = q.shape
    lc = ck.shape[1]
    rows = t // GRID_W
    tok = pl.BlockSpec((None, t, LANES), lambda hp, bi: (bi, 0, hp))
    ctx = pl.BlockSpec((None, lc, LANES), lambda hp, bi: (bi, 0, hp))
    return pl.pallas_call(
        functools.partial(_na_body, rows=rows),
        grid=(NA_WIDTH // LANES, b),
        in_specs=[tok, tok, tok, ctx, ctx,
                  pl.BlockSpec((2, NA_KH, GRID_W, NA_KH * GRID_W), lambda hp, bi: (hp, 0, 0, 0))],
        out_specs=tok,
        out_shape=jax.ShapeDtypeStruct((b, t, NA_WIDTH), BF16),
        scratch_shapes=[pltpu.VMEM((2, 2 * GRID_W, NA_KH * GRID_W + lc), F32),
                        pltpu.VMEM((2, 2 * GRID_W, NA_KH * GRID_W + lc), BF16),
                        pltpu.VMEM((2, 2 * GRID_W, LANES), F32)],
        compiler_params=_params("parallel", "parallel"),
        name="na_attn",
    )(q, k, v, ck, cv, bias)


def _ret_body(q_ref, k_ref, v_ref, g_ref, df_ref, db_ref, beta_ref, s0f_ref, s0b_ref,
              o_ref, sf_ref, sb_ref, kvf_s, kvb_s, *, n_chunks):
    c_len = CHUNK
    lgf = -jnp.log1p(jnp.exp(-df_ref[...]))
    lgb = -jnp.log1p(jnp.exp(-db_ref[...]))
    shape = (c_len, c_len)
    i = lax.broadcasted_iota(I32, shape, 0).astype(F32)
    j = lax.broadcasted_iota(I32, shape, 1).astype(F32)
    dij = i - j
    d_comb = (jnp.where(dij >= 0, jnp.exp(jnp.maximum(dij, 0.0) * lgf), 0.0)
              + jnp.where(dij <= 0, jnp.exp(jnp.maximum(-dij, 0.0) * lgb), 0.0))
    xi_f = jnp.exp((i + 1.0) * lgf)
    xi_b = jnp.exp((c_len - i) * lgb)
    zeta_f = jnp.exp((c_len - 1.0 - i) * lgf)
    zeta_b = jnp.exp(i * lgb)
    g_f = jnp.exp(c_len * lgf)
    g_b = jnp.exp(c_len * lgb)

    def chunk(c):
        return pl.ds(pl.multiple_of(c * c_len, c_len), c_len)

    def kv_pass(c, carry):
        kc = k_ref[chunk(c), :]
        vc = v_ref[chunk(c), :].astype(F32)
        kvf_s[c] = _dot_tn(kc, (vc * zeta_f).astype(BF16))
        kvb_s[c] = _dot_tn(kc, (vc * zeta_b).astype(BF16))
        return carry

    lax.fori_loop(0, n_chunks, kv_pass, 0, unroll=RET_CHUNK_UNROLL)

    def scan_f(c, s):
        kv = kvf_s[c]
        kvf_s[c] = s
        return g_f * s + kv

    def scan_b(ci, s):
        c = n_chunks - 1 - ci
        kv = kvb_s[c]
        kvb_s[c] = s
        return g_b * s + kv

    sf_ref[...] = lax.fori_loop(0, n_chunks, scan_f, s0f_ref[...])
    sb_ref[...] = lax.fori_loop(0, n_chunks, scan_b, s0b_ref[...])

    beta = beta_ref[...]

    def out_pass(c, carry):
        qc = q_ref[chunk(c), :]
        kc = k_ref[chunk(c), :]
        vc = v_ref[chunk(c), :]
        scores = _dot_nt(qc, kc) * d_comb
        y = jnp.dot(scores.astype(BF16), vc, preferred_element_type=F32)
        qf = qc.astype(F32)
        y += jnp.dot((qf * xi_f).astype(BF16), kvf_s[c].astype(BF16), preferred_element_type=F32)
        y += jnp.dot((qf * xi_b).astype(BF16), kvb_s[c].astype(BF16), preferred_element_type=F32)
        o_ref[chunk(c), :] = (_standardize(y) * beta * _silu(g_ref[chunk(c), :])).astype(o_ref.dtype)
        return carry

    lax.fori_loop(0, n_chunks, out_pass, 0, unroll=RET_CHUNK_UNROLL)


def _retention(q, k, v, g, decay_f, decay_b, beta_ret, s0f, s0b):
    b, t, _ = q.shape
    n_chunks = t // CHUNK
    tok = pl.BlockSpec((None, t, LANES), lambda bi, h: (bi, 0, h))
    per_head = pl.BlockSpec((None, 1, LANES), lambda bi, h: (h, 0, 0))
    state = pl.BlockSpec((None, None, RET_DK, RET_DK), lambda bi, h: (bi, h, 0, 0))
    lanes = lambda a: jnp.broadcast_to(a.astype(F32)[:, None, None], (RET_HEADS, 1, LANES))
    st_shape = jax.ShapeDtypeStruct((b, RET_HEADS, RET_DK, RET_DK), F32)
    return pl.pallas_call(
        functools.partial(_ret_body, n_chunks=n_chunks),
        grid=(b, RET_HEADS),
        in_specs=[tok, tok, tok, tok, per_head, per_head, per_head, state, state],
        out_specs=[tok, state, state],
        out_shape=[jax.ShapeDtypeStruct((b, t, RET_WIDTH), BF16), st_shape, st_shape],
        scratch_shapes=[pltpu.VMEM((n_chunks, RET_DK, RET_DK), F32)] * 2,
        compiler_params=_params("parallel", "parallel"),
        name="retention",
    )(q, k, v, g, lanes(decay_f), lanes(decay_b), beta_ret.reshape(RET_HEADS, 1, LANES), s0f, s0b)


def _merge_body(x_ref, ona_ref, ret_ref, wna_ref, wret_ref, bna_ref, gate_ref, lg_ref, lb_ref,
                shf_ref, scf_ref, wh_ref, wl_ref, br_ref, carry_in_ref,
                o_ref, p_ref, gt_ref, tab_ref, cnt_ref, carry_s):
    @pl.when((pl.program_id(0) == 0) & (pl.program_id(1) == 0))
    def _():
        carry_s[...] = carry_in_ref[...]

    o = ona_ref[...].astype(F32)
    na = o * lax.rsqrt(jnp.mean(o * o, axis=-1, keepdims=True) + LN_EPS) * bna_ref[...]
    mix = (jnp.dot(na.astype(BF16), wna_ref[...], preferred_element_type=F32)
           + jnp.dot(ret_ref[...], wret_ref[...], preferred_element_type=F32))
    y = DEEPNORM_ALPHA * x_ref[...] + gate_ref[...] * mix
    y = _standardize(y) * lg_ref[...] + lb_ref[...]
    o_ref[...] = y

    h = y * (1.0 + scf_ref[...]) + shf_ref[...]
    w_hi, w_lo, bias = wh_ref[...], wl_ref[...], br_ref[...]
    carry = carry_s[...]
    for s in range(y.shape[0] // TOKEN_BLOCK):
        sl = slice(s * TOKEN_BLOCK, (s + 1) * TOKEN_BLOCK)
        rows, gates, table, carry = _route_block(h[sl], w_hi, w_lo, bias, carry)
        p_ref[:, sl] = rows
        gt_ref[:, sl] = gates
        tab_ref[s] = table
    carry_s[...] = carry
    cnt_ref[...] = carry


def _merge_and_route(x, o_na, ret, w_out, beta_na, gate, shift_f, scale_f, mod_row, ln_g, ln_b,
                     w_hi, w_lo, b_router, carry_in, tm):
    b, t, _ = x.shape
    nt = t // tm
    sub = tm // TOKEN_BLOCK
    tok = lambda bi, ti: (bi, ti, 0)
    flat = lambda bi, ti: (0, bi * nt + ti)
    mod = lambda bi, ti: (mod_row(bi), 0, 0)
    whole = lambda shape: pl.BlockSpec(shape, lambda bi, ti: (0,) * len(shape))
    half = pl.BlockSpec((None, tm, 512), tok)
    mod_spec = pl.BlockSpec((None, 1, D_MODEL), mod)
    return pl.pallas_call(
        _merge_body,
        grid=(b, nt),
        in_specs=[pl.BlockSpec((None, tm, D_MODEL), tok), half, half,
                  pl.BlockSpec((512, D_MODEL), lambda bi, ti: (0, 0)),
                  pl.BlockSpec((512, D_MODEL), lambda bi, ti: (1, 0)),
                  whole((1, 512)), mod_spec, whole((1, D_MODEL)), whole((1, D_MODEL)),
                  mod_spec, mod_spec,
                  whole((N_EXPERTS, D_MODEL)), whole((N_EXPERTS, D_MODEL)), whole((N_EXPERTS, TOKEN_BLOCK)),
                  whole((N_EXPERTS, LANES))],
        out_specs=[pl.BlockSpec((None, tm, D_MODEL), tok),
                   pl.BlockSpec((TOP_K, tm), flat), pl.BlockSpec((TOP_K, tm), flat),
                   pl.BlockSpec((sub, N_EXPERTS, LANES), lambda bi, ti: (bi * nt + ti, 0, 0)),
                   whole((N_EXPERTS, LANES))],
        out_shape=[jax.ShapeDtypeStruct(x.shape, F32),
                   jax.ShapeDtypeStruct((TOP_K, b * t), I32), jax.ShapeDtypeStruct((TOP_K, b * t), F32),
                   jax.ShapeDtypeStruct((b * t // TOKEN_BLOCK, N_EXPERTS, LANES), F32),
                   jax.ShapeDtypeStruct((N_EXPERTS, LANES), F32)],
        scratch_shapes=[pltpu.VMEM((N_EXPERTS, LANES), F32)],
        compiler_params=_params("arbitrary", "arbitrary"),
        name="merge_ln1_route",
    )(x, o_na, ret, w_out, w_out, beta_na.reshape(1, 512), gate, ln_g.reshape(1, -1), ln_b.reshape(1, -1),
      shift_f, scale_f, w_hi, w_lo, jnp.broadcast_to(b_router[:, None], (N_EXPERTS, TOKEN_BLOCK)), carry_in)


def _two_group_specs(n_p_blocks, blocks_per_batch_s, ctx_row):
    tok_p = lambda i: (jnp.minimum(i, n_p_blocks - 1), 0)
    tok_s = lambda i: (jnp.maximum(i - n_p_blocks, 0), 0)
    mod = lambda i: (jnp.where(i < n_p_blocks, ctx_row, jnp.maximum(i - n_p_blocks, 0) // blocks_per_batch_s), 0, 0)
    return tok_p, tok_s, mod


def _route_block(h, w_hi, w_lo, bias, carry):
    tm = h.shape[0]
    h_hi = h.astype(BF16)
    h_lo = (h - h_hi.astype(F32)).astype(BF16)
    lg = _dot_nt(jnp.concatenate([w_hi, w_lo], axis=0), h_hi)
    work = lg[:N_EXPERTS] + lg[N_EXPERTS:] + _dot_nt(w_hi, h_lo) + bias
    eidx = lax.broadcasted_iota(I32, (N_EXPERTS, tm), 0).astype(F32)
    vals, idxs, hots = [], [], []
    for _ in range(TOP_K):
        mx = jnp.max(work, axis=0, keepdims=True)
        idx = jnp.min(jnp.where(work == mx, eidx, float(N_EXPERTS)), axis=0, keepdims=True)
        hot = eidx == idx
        vals.append(mx)
        idxs.append(idx)
        hots.append(hot)
        work = jnp.where(hot, -jnp.inf, work)
    exps = [jnp.exp(v - vals[0]) for v in vals]
    den = exps[0] + exps[1] + exps[2] + exps[3]
    sel = jnp.zeros((N_EXPERTS, tm), F32)
    for hot in hots:
        sel = sel + hot.astype(F32)
    r_i = lax.broadcasted_iota(I32, (tm, tm), 0)
    c_i = lax.broadcasted_iota(I32, (tm, tm), 1)
    earlier = (r_i < c_i).astype(BF16)
    before = jnp.dot(sel.astype(BF16), earlier, preferred_element_type=F32)
    n_e = jnp.sum(sel, axis=1, keepdims=True)
    n8 = jnp.floor((n_e + (RUN_ALIGN - 1.0)) * (1.0 / RUN_ALIGN)) * RUN_ALIGN + jnp.zeros((1, LANES), F32)
    e_r = lax.broadcasted_iota(I32, (N_EXPERTS, N_EXPERTS), 0)
    e_c = lax.broadcasted_iota(I32, (N_EXPERTS, N_EXPERTS), 1)
    off = jnp.dot((e_c < e_r).astype(BF16), n8.astype(BF16), preferred_element_type=F32)
    local = before + jnp.concatenate([off] * (tm // LANES), axis=1)
    rows = [jnp.sum(jnp.where(hot, local, 0.0), axis=0, keepdims=True) for hot in hots]
    lane = lax.broadcasted_iota(I32, (N_EXPERTS, LANES), 1)
    table = jnp.where(lane == 0, n8, jnp.where(lane == 1, carry, jnp.where(lane == 2, off, 0.0)))
    return (jnp.concatenate(rows, axis=0).astype(I32), jnp.concatenate([e / den for e in exps], axis=0),
            table, carry + n8)


def _for_run_pieces(n_rows, visit):
    off = 0
    for size in RUN_PIECES:
        present = (n_rows & size) != 0
        visit(off, size, present)
        off = off + jnp.where(present, size, 0)


def _start_unit_copies(unit_ref, local_buf, sorted_hbm, sem, to_hbm):
    def one(u, carry):
        loc = local_buf.at[pl.ds(pl.multiple_of(u * RUN_ALIGN, RUN_ALIGN), RUN_ALIGN)]
        glob = sorted_hbm.at[pl.ds(pl.multiple_of(unit_ref[u], RUN_ALIGN), RUN_ALIGN)]
        copy = pltpu.make_async_copy(loc, glob, sem) if to_hbm else pltpu.make_async_copy(glob, loc, sem)
        copy.start()
        return carry

    lax.fori_loop(0, LOCAL_UNITS, one, 0, unroll=8)


def _wait_unit_copies(local_buf, sorted_hbm, sem):
    whole = sorted_hbm.at[pl.ds(0, LOCAL_ROWS)]
    pltpu.make_async_copy(whole, local_buf, sem).wait()


def _dispatch_body(unit_ref, zlo_ref, zn_ref, p_ref, yp_ref, ys_ref, sh_ref, sc_ref, xs_ref,
                   xbuf, zbuf, sems, zsem, *, n_p_blocks, n_blocks):
    i = pl.program_id(0)
    slot = i % 2
    y = jnp.where(i < n_p_blocks, yp_ref[...], ys_ref[...])
    h = (y * (1.0 + sc_ref[...]) + sh_ref[...]).astype(BF16)
    rows = p_ref[...]
    r_iota = lax.broadcasted_iota(I32, (LOCAL_ROWS, rows.shape[1]), 0)
    place = r_iota == rows[0:1]
    for kk in range(1, TOP_K):
        place = place | (r_iota == rows[kk:kk + 1])
    xbuf[slot] = _pack_bf16_pairs(
        jnp.dot(jnp.where(place, 1.0, 0.0).astype(BF16), h, preferred_element_type=F32))

    _start_unit_copies(unit_ref, xbuf.at[slot], xs_ref, sems.at[slot], to_hbm=True)

    @pl.when(i > 0)
    def _():
        _wait_unit_copies(xbuf.at[1 - slot], xs_ref, sems.at[1 - slot])

    @pl.when(i == n_blocks - 1)
    def _():
        _wait_unit_copies(xbuf.at[slot], xs_ref, sems.at[slot])
        zbuf[...] = jnp.zeros_like(zbuf)
        sem = zsem

        def per_expert(e, carry):
            lo = zlo_ref[e]

            def visit(off, size, present):
                copy = pltpu.make_async_copy(
                    zbuf.at[pl.ds(0, size)], xs_ref.at[pl.ds(pl.multiple_of(lo + off, RUN_ALIGN), size)], sem)

                @pl.when(present)
                def _():
                    copy.start()
                    copy.wait()

            _for_run_pieces(zn_ref[e], visit)
            return carry

        lax.fori_loop(0, N_EXPERTS, per_expert, 0)

        tail_lo = zlo_ref[N_EXPERTS]

        def tile_copy(t):
            start = pl.multiple_of(tail_lo + t * EXPERT_TILE, EXPERT_TILE)
            return pltpu.make_async_copy(zbuf, xs_ref.at[pl.ds(start, EXPERT_TILE)], sem)

        def t_issue(t, c2):
            tile_copy(t).start()
            return c2

        def t_drain(t, c2):
            tile_copy(0).wait()
            return c2

        n_tail = zn_ref[N_EXPERTS] // EXPERT_TILE
        lax.fori_loop(0, n_tail, t_issue, 0)
        lax.fori_loop(0, n_tail, t_drain, 0)


def _dispatch(yp, ys, shift, scale, ctx_row, s_blocks_per_batch, local_rows, unit_dst, zero_lo, zero_n,
              total_rows, tm):
    n_p, n_s = yp.shape[0], ys.shape[0]
    n = n_p + n_s
    npb = n_p // tm
    nb = n // tm
    tok_p, tok_s, mod = _two_group_specs(npb, s_blocks_per_batch, ctx_row)
    smem_all = pl.BlockSpec(memory_space=pltpu.SMEM)
    return pl.pallas_call(
        functools.partial(_dispatch_body, n_p_blocks=npb, n_blocks=nb),
        grid=(nb,),
        in_specs=[pl.BlockSpec((UNIT_TABLE,), lambda i: (i,), memory_space=pltpu.SMEM), smem_all, smem_all,
                  pl.BlockSpec((TOP_K, tm), lambda i: (0, i)),
                  pl.BlockSpec((tm, D_MODEL), tok_p), pl.BlockSpec((tm, D_MODEL), tok_s),
                  pl.BlockSpec((None, 1, D_MODEL), mod), pl.BlockSpec((None, 1, D_MODEL), mod)],
        out_specs=pl.BlockSpec(memory_space=pl.ANY),
        out_shape=jax.ShapeDtypeStruct((total_rows, D_MODEL // 2), U32),
        scratch_shapes=[pltpu.VMEM((2, LOCAL_ROWS, D_MODEL // 2), U32),
                        pltpu.VMEM((EXPERT_TILE, D_MODEL // 2), U32),
                        pltpu.SemaphoreType.DMA((2,)), pltpu.SemaphoreType.DMA(())],
        compiler_params=_params("arbitrary"),
        name="moe_dispatch",
    )(unit_dst, zero_lo, zero_n, local_rows, yp, ys, shift, scale)


def _expert_body(te_ref, first_ref, nu_ref, x_ref, wgu_ref, bgu_ref, wd_ref, bd_ref, y_ref, act_s, wgu_s, wd_s):
    j = pl.program_id(0)

    @pl.when(first_ref[j] == 1)
    def _():
        blk = 2 * LANES
        src = lax.broadcasted_iota(I32, (blk, blk), 0)
        dst = lax.broadcasted_iota(I32, (blk, blk), 1)
        perm = (src == jnp.where(dst < LANES, 2 * dst, 2 * (dst - LANES) + 1)).astype(F32).astype(BF16)
        for s in range(2 * D_FF // blk):
            sl = slice(s * blk, (s + 1) * blk)
            wgu_s[:, sl] = jnp.dot(wgu_ref[:, sl].astype(BF16), perm, preferred_element_type=F32).astype(BF16)
        wd_s[...] = wd_ref[...].astype(BF16)

    @pl.when(j < nu_ref[0])
    def _():
        x = _unpack_bf16_pairs(x_ref[...])
        for c in range(D_FF // LANES):
            sl = slice(c * 2 * LANES, (c + 1) * 2 * LANES)
            hu = jnp.dot(x, wgu_s[:, sl], preferred_element_type=F32) + bgu_ref[:, sl]
            x_glu = jnp.minimum(hu[:, :LANES], SWIGLU_LIMIT)
            x_lin = jnp.clip(hu[:, LANES:], -SWIGLU_LIMIT, SWIGLU_LIMIT)
            act = x_glu * (1.0 / (1.0 + jnp.exp(-SWIGLU_ALPHA * x_glu))) * (x_lin + 1.0)
            act_s[:, c * LANES:(c + 1) * LANES] = act.astype(BF16)
        y_ref[...] = _pack_bf16_pairs(jnp.dot(act_s[...], wd_s[...], preferred_element_type=F32) + bd_ref[...])

    @pl.when(j >= nu_ref[0])
    def _():
        y_ref[...] = jnp.zeros_like(y_ref)


def _experts(tile_expert, tile_first, n_used, xs, cap_rows, w_gate_up, bgu, w_down, bd):
    n_tiles = cap_rows // EXPERT_TILE
    of_expert = lambda j, te, first, nu: (te[j], 0, 0)
    grid_spec = pltpu.PrefetchScalarGridSpec(
        num_scalar_prefetch=3,
        grid=(n_tiles,),
        in_specs=[pl.BlockSpec((EXPERT_TILE, D_MODEL // 2),
                               lambda j, te, first, nu: (jnp.minimum(j, nu[0] - 1), 0)),
                  pl.BlockSpec((None, D_MODEL, 2 * D_FF), of_expert),
                  pl.BlockSpec((None, 1, 2 * D_FF), of_expert),
                  pl.BlockSpec((None, D_FF, D_MODEL), of_expert),
                  pl.BlockSpec((None, 1, D_MODEL), of_expert)],
        out_specs=pl.BlockSpec((EXPERT_TILE, D_MODEL // 2), lambda j, te, first, nu: (j, 0)),
        scratch_shapes=[pltpu.VMEM((EXPERT_TILE, D_FF), BF16),
                        pltpu.VMEM((D_MODEL, 2 * D_FF), BF16), pltpu.VMEM((D_FF, D_MODEL), BF16)],
    )
    return pl.pallas_call(
        _expert_body,
        grid_spec=grid_spec,
        out_shape=jax.ShapeDtypeStruct((cap_rows, D_MODEL // 2), U32),
        compiler_params=_params("arbitrary"),
        name="moe_experts",
    )(tile_expert, tile_first, n_used, xs, w_gate_up, bgu, w_down, bd)


def _combine_body(unit_ref, next_unit_ref, y1_ref, p_ref, gt_ref, gate_ref, lg_ref, lb_ref, ys_ref, o_ref,
                  ybuf, sems, *, n_blocks):
    i = pl.program_id(0)
    slot = i % 2
    tm = y1_ref.shape[0]

    @pl.when(i == 0)
    def _():
        _start_unit_copies(unit_ref, ybuf.at[0], ys_ref, sems.at[0], to_hbm=False)

    @pl.when(i + 1 < n_blocks)
    def _():
        _start_unit_copies(next_unit_ref, ybuf.at[1 - slot], ys_ref, sems.at[1 - slot], to_hbm=False)

    _wait_unit_copies(ybuf.at[slot], ys_ref, sems.at[slot])

    rows = p_ref[...]
    gates = gt_ref[...]
    c_iota = lax.broadcasted_iota(I32, (tm, LOCAL_ROWS), 1)
    weight = jnp.where(c_iota == rows[:, 0:1], gates[:, 0:1], 0.0)
    for kk in range(1, TOP_K):
        weight = weight + jnp.where(c_iota == rows[:, kk:kk + 1], gates[:, kk:kk + 1], 0.0)
    w_hi = weight.astype(BF16)
    w_lo = (weight - w_hi.astype(F32)).astype(BF16)
    yb = _unpack_bf16_pairs(ybuf[slot])
    f = (jnp.dot(w_hi, yb, preferred_element_type=F32) + jnp.dot(w_lo, yb, preferred_element_type=F32))
    y = DEEPNORM_ALPHA * y1_ref[...] + gate_ref[...] * f
    o_ref[...] = _standardize(y) * lg_ref[...] + lb_ref[...]


def _combine(y1, local_rows, gates, unit_src, block0, gate_mod, mod_row, blocks_per_batch,
             y_sorted, ln_g, ln_b, tm):
    n_g = y1.shape[0]
    nb = n_g // tm
    row = pl.BlockSpec((1, D_MODEL), lambda i: (0, 0))
    pair = pl.BlockSpec((tm, TOP_K), lambda i: (i + block0, 0))
    units = lambda ahead: pl.BlockSpec(
        (UNIT_TABLE,), lambda i: (jnp.minimum(i + ahead, nb - 1) + block0,), memory_space=pltpu.SMEM)
    return pl.pallas_call(
        functools.partial(_combine_body, n_blocks=nb),
        grid=(nb,),
        in_specs=[units(0), units(1),
                  pl.BlockSpec((tm, D_MODEL), lambda i: (i, 0)), pair, pair,
                  pl.BlockSpec((None, 1, D_MODEL), lambda i: (mod_row(i // blocks_per_batch), 0, 0)),
                  row, row,
                  pl.BlockSpec(memory_space=pl.ANY)],
        out_specs=pl.BlockSpec((tm, D_MODEL), lambda i: (i, 0)),
        out_shape=jax.ShapeDtypeStruct((n_g, D_MODEL), F32),
        scratch_shapes=[pltpu.VMEM((2, LOCAL_ROWS, D_MODEL // 2), U32), pltpu.SemaphoreType.DMA((2,))],
        compiler_params=_params("arbitrary"),
        name="moe_combine_ln2",
    )(unit_src, unit_src, y1, local_rows, gates, gate_mod,
      ln_g.reshape(1, -1), ln_b.reshape(1, -1), y_sorted)


def _regroup_gate_up_bias(b_gate_up):
    e = b_gate_up.shape[0]
    nblk = D_FF // LANES
    return b_gate_up.reshape(e, 1, nblk, LANES, 2).transpose(0, 1, 2, 4, 3).reshape(e, 1, 2 * D_FF)


def _routing_plan(table, totals, n_pairs):
    nb = table.shape[0]
    run_n = table[:, :, 0].astype(I32)
    rows_before = table[:, :, 1].astype(I32)
    run_local = table[:, :, 2].astype(I32)
    cnt = totals[:, 0].astype(I32)
    padded = (cnt + EXPERT_TILE - 1) // EXPERT_TILE * EXPERT_TILE
    pad_end = jnp.cumsum(padded)
    pad_start = pad_end - padded
    run_global = pad_start[None, :] + rows_before
    rows_max = n_pairs + nb * N_EXPERTS * (RUN_ALIGN - 1) + N_EXPERTS * (EXPERT_TILE - 1)
    n_tiles = -(-rows_max // EXPERT_TILE)
    n_used = pad_end[-1] // EXPERT_TILE
    tile_row = jnp.minimum(jnp.arange(n_tiles, dtype=I32), n_used - 1) * EXPERT_TILE
    tile_expert = jnp.sum((pad_end[None, :] <= tile_row[:, None]).astype(I32), axis=1)
    tile_expert = jnp.minimum(tile_expert, N_EXPERTS - 1)
    tile_id = jnp.arange(n_tiles, dtype=I32)
    changed = jnp.concatenate([jnp.ones((1,), bool), tile_expert[1:] != tile_expert[:-1]])
    tile_first = (changed & (tile_id < n_used)).astype(I32)
    cap_rows = n_tiles * EXPERT_TILE
    spare_rows = -(-LOCAL_ROWS // EXPERT_TILE) * EXPERT_TILE
    total_rows = cap_rows + 2 * spare_rows
    unit_row = jnp.arange(LOCAL_UNITS, dtype=I32) * RUN_ALIGN
    run_end = run_local + run_n
    owner = jnp.sum((run_end[:, None, :] <= unit_row[None, :, None]).astype(I32), axis=-1)
    own_hot = owner[:, :, None] == jnp.arange(N_EXPERTS, dtype=I32)
    shift = jnp.sum(jnp.where(own_hot, (run_global - run_local)[:, None, :], 0), axis=-1)
    used = owner < N_EXPERTS
    sorted_row = unit_row[None, :] + shift
    slot = (jnp.arange(nb, dtype=I32) % 2)[:, None]
    unit_dst = jnp.where(used, sorted_row, cap_rows + slot * spare_rows + unit_row[None, :])
    unit_src = jnp.where(used, sorted_row, unit_row[None, :])
    table_of = lambda a: jnp.pad(a, ((0, 0), (0, UNIT_TABLE - LOCAL_UNITS))).reshape(-1)
    zero_lo = jnp.concatenate([pad_start + cnt, pad_end[-1:]])
    zero_n = jnp.concatenate([padded - cnt, total_rows - pad_end[-1:]])
    return ((table_of(unit_dst), table_of(unit_src)), (tile_expert, tile_first, n_used.reshape(1)),
            zero_lo, zero_n, cap_rows, total_rows)


def kernel(x_prompt, x_sample, cache_na_k, cache_na_v, state_ret_fwd, state_ret_bwd, c, c_ctx, w_ada, b_ada, w_in, rpb, ret_decay_fwd, ret_decay_bwd, beta_na, beta_ret, w_out, ln1_g, ln1_b, w_router, b_router, w_gate_up, b_gate_up, w_down, b_down, ln2_g, ln2_b):
    bp, tp, _ = x_prompt.shape
    bs, ts, _ = x_sample.shape
    assert w_ada.shape[0] == DEPTH == 1
    l = 0
    ctx_row = bs
    sample_row = lambda bi: bi
    prompt_row = lambda bi: ctx_row

    cond = jnp.concatenate([c, c_ctx[None, :], jnp.zeros((16 - bs - 1, D_MODEL), F32)], axis=0)
    m = _adaln(cond, w_ada[l], b_ada[l]).reshape(16, 6, 1, D_MODEL)
    sh_a, sc_a, gt_a, sh_f, sc_f, gt_f = (m[:, i] for i in range(6))

    w_in_b = w_in[l].astype(BF16)
    w_out_b = w_out[l].astype(BF16)

    q, k_p, v_p, qr, kr, vr, g = _project(x_prompt, sh_a, sc_a, prompt_row, w_in_b, tp, F32, rope=False)
    o_na = _context_attention(q, k_p, v_p)
    zeros_state = jnp.zeros((bp, RET_HEADS, RET_DK, RET_DK), F32)
    ret, s_f, s_b = _retention(qr, kr, vr, g, ret_decay_fwd[l], ret_decay_bwd[l], beta_ret[l],
                               zeros_state, zeros_state)
    w_r = w_router[l].T
    w_r_hi = w_r.astype(BF16)
    w_r_lo = (w_r - w_r_hi.astype(F32)).astype(BF16)
    route = functools.partial(_merge_and_route, w_out=w_out_b, beta_na=beta_na[l], gate=gt_a, shift_f=sh_f,
                              scale_f=sc_f, ln_g=ln1_g[l], ln_b=ln1_b[l], w_hi=w_r_hi, w_lo=w_r_lo,
                              b_router=b_router[l])
    yp1, rows_p, gates_p, table_p, totals_p = route(
        x_prompt, o_na, ret, mod_row=prompt_row, carry_in=jnp.zeros((N_EXPERTS, LANES), F32), tm=tp)

    q, k_s, v_s, qr, kr, vr, g = _project(x_sample, sh_a, sc_a, sample_row, w_in_b, 512, BF16, rope=True)
    lc = cache_na_k.shape[2]
    o_na = _neighbourhood_attention(q, k_s, v_s, cache_na_k[:, l].reshape(bs, lc, NA_WIDTH),
                                    cache_na_v[:, l].reshape(bs, lc, NA_WIDTH), _na_bias(rpb[l]))
    ret, _, _ = _retention(qr, kr, vr, g, ret_decay_fwd[l], ret_decay_bwd[l], beta_ret[l],
                           state_ret_fwd[:, l], state_ret_bwd[:, l])
    ys1, rows_s, gates_s, table_s, totals = route(
        x_sample, o_na, ret, mod_row=sample_row, carry_in=totals_p, tm=512)

    tm = TOKEN_BLOCK
    yp1f = yp1.reshape(bp * tp, D_MODEL)
    ys1f = ys1.reshape(bs * ts, D_MODEL)
    local_rows = jnp.concatenate([rows_p, rows_s], axis=1)
    gates = jnp.concatenate([gates_p, gates_s], axis=1)
    table = jnp.concatenate([table_p, table_s], axis=0)
    (unit_dst, unit_src), tiles, zero_lo, zero_n, cap_rows, total_rows = _routing_plan(
        table, totals, (bp * tp + bs * ts) * TOP_K)
    xs = _dispatch(yp1f, ys1f, sh_f, sc_f, ctx_row, ts // tm, local_rows, unit_dst, zero_lo, zero_n,
                   total_rows, tm)
    y_sorted = _experts(*tiles, xs, cap_rows, w_gate_up[l], _regroup_gate_up_bias(b_gate_up[l]), w_down[l],
                        b_down[l][:, None, :])
    rows_t, gates_t = local_rows.T, gates.T
    yp = _combine(yp1f, rows_t, gates_t, unit_src, 0, gt_f, prompt_row, 1, y_sorted, ln2_g[l], ln2_b[l], tm)
    ys = _combine(ys1f, rows_t, gates_t, unit_src, (bp * tp) // tm, gt_f, sample_row, ts // tm, y_sorted,
                  ln2_g[l], ln2_b[l], tm)

    return (yp.reshape(bp, tp, D_MODEL), ys.reshape(bs, ts, D_MODEL),
            k_p.reshape(bp, 1, tp, NA_HEADS, NA_DIM), v_p.reshape(bp, 1, tp, NA_HEADS, NA_DIM),
            s_f[:, None], s_b[:, None])
```

```python
import functools

import jax
import jax.numpy as jnp
from jax import lax
from jax.experimental import pallas as pl
from jax.experimental.pallas import tpu as pltpu

F32 = jnp.float32
BF16 = jnp.bfloat16
I32 = jnp.int32
U32 = jnp.uint32

D_MODEL = 1024
GRID_W = 64
NA_HEADS = 8
NA_DIM = 64
NA_WIDTH = NA_HEADS * NA_DIM
NA_KH = 8
NA_KW = 16
RET_HEADS = 4
RET_DK = 128
RET_WIDTH = RET_HEADS * RET_DK
CHUNK = 128
N_EXPERTS = 32
TOP_K = 4
D_FF = 1024
SWIGLU_LIMIT = 7.0
SWIGLU_ALPHA = 1.702
ROPE_BASE = 10000.0
LN_EPS = 1e-5
DEPTH = 1
DEEPNORM_ALPHA = (2.0 * DEPTH) ** 0.25

LANES = 128
NEG_BIG = -1e30
TOKEN_BLOCK = 256
EXPERT_TILE = 512
VMEM_LIMIT = 56 * 1024 * 1024
RET_CHUNK_UNROLL = 8
SUBLANES = 8
RUN_ALIGN = SUBLANES
LOCAL_ROWS = TOKEN_BLOCK * TOP_K + N_EXPERTS * RUN_ALIGN
LOCAL_UNITS = LOCAL_ROWS // RUN_ALIGN
UNIT_TABLE = 256
RUN_PIECES = tuple(TOKEN_BLOCK >> s for s in range(TOKEN_BLOCK.bit_length())
                   if (TOKEN_BLOCK >> s) >= RUN_ALIGN)


def _params(*sem):
    return pltpu.CompilerParams(dimension_semantics=sem, vmem_limit_bytes=VMEM_LIMIT)


def _silu(x):
    return x / (1.0 + jnp.exp(-x))


def _standardize(x):
    mu = jnp.mean(x, axis=-1, keepdims=True)
    xc = x - mu
    var = jnp.mean(xc * xc, axis=-1, keepdims=True)
    return xc * lax.rsqrt(var + LN_EPS)


def _pack_bf16_pairs(x):
    bits = lax.bitcast_convert_type(x.astype(BF16).astype(F32), U32)
    half = x.shape[1] // 2
    return bits[:, :half] | lax.shift_right_logical(bits[:, half:], jnp.uint32(16))


def _unpack_bf16_pairs(p):
    hi = lax.bitcast_convert_type(p & jnp.uint32(0xFFFF0000), F32)
    lo = lax.bitcast_convert_type(lax.shift_left(p, jnp.uint32(16)), F32)
    return jnp.concatenate([hi, lo], axis=1).astype(BF16)


def _dot_nt(a, b):
    return lax.dot_general(a, b, (((1,), (1,)), ((), ())), preferred_element_type=F32)


def _dot_tn(a, b):
    return lax.dot_general(a, b, (((0,), (0,)), ((), ())), preferred_element_type=F32)


def _ada_body(c_ref, w_ref, b_ref, o_ref):
    s = _silu(c_ref[...])
    o_ref[...] = jnp.dot(s, w_ref[...], preferred_element_type=F32,
                         precision=lax.Precision.HIGHEST) + b_ref[...]


def _adaln(cond, w_ada, b_ada):
    r = cond.shape[0]
    n = w_ada.shape[1]
    tn = 1536
    return pl.pallas_call(
        _ada_body,
        grid=(n // tn,),
        in_specs=[pl.BlockSpec((r, D_MODEL), lambda j: (0, 0)),
                  pl.BlockSpec((D_MODEL, tn), lambda j: (0, j)),
                  pl.BlockSpec((1, tn), lambda j: (0, j))],
        out_specs=pl.BlockSpec((r, tn), lambda j: (0, j)),
        out_shape=jax.ShapeDtypeStruct((r, n), F32),
        compiler_params=_params("arbitrary"),
        name="adaln",
    )(cond, w_ada, b_ada.reshape(1, n))


def _proj_body(*refs, rope):
    if rope:
        (x_ref, sh_ref, sc_ref, w_ref, cos_ref, sa_ref, sb_ref,
         q_ref, k_ref, v_ref, qr_ref, kr_ref, vr_ref, g_ref) = refs
    else:
        (x_ref, sh_ref, sc_ref, w_ref,
         q_ref, k_ref, v_ref, qr_ref, kr_ref, vr_ref, g_ref) = refs
    h = (x_ref[...] * (1.0 + sc_ref[...]) + sh_ref[...]).astype(BF16)

    def cols(c):
        return jnp.dot(h, w_ref[:, c * 512:(c + 1) * 512], preferred_element_type=F32)

    q_ref[...] = cols(0).astype(q_ref.dtype)
    k_ref[...] = cols(1).astype(k_ref.dtype)
    v_ref[...] = cols(2).astype(v_ref.dtype)
    pq = cols(3)
    pk = cols(4) * (RET_DK ** -0.5)
    if rope:
        cs, sa, sb = cos_ref[...], sa_ref[...], sb_ref[...]
        for hd in range(RET_HEADS):
            sl = slice(hd * LANES, (hd + 1) * LANES)
            for p, o_ref in ((pq, qr_ref), (pk, kr_ref)):
                xs = p[:, sl]
                rot = xs * cs + pltpu.roll(xs, 96, 1) * sa + pltpu.roll(xs, 32, 1) * sb
                o_ref[:, sl] = rot.astype(o_ref.dtype)
    else:
        qr_ref[...] = pq.astype(qr_ref.dtype)
        kr_ref[...] = pk.astype(kr_ref.dtype)
    vr_ref[...] = cols(5).astype(vr_ref.dtype)
    g_ref[...] = cols(6)


def _rope_tables(t_len):
    t = jnp.arange(t_len, dtype=jnp.int32)
    pos_row = (t // GRID_W).astype(F32)[:, None]
    pos_col = (t % GRID_W).astype(F32)[:, None]
    lane = jnp.arange(LANES, dtype=jnp.int32)[None, :]
    n_freq = RET_DK // 4
    inv_freq = ROPE_BASE ** (-(lane % n_freq).astype(F32) / n_freq)
    ang = jnp.where(lane < RET_DK // 2, pos_row, pos_col) * inv_freq
    first = (lane % (2 * n_freq)) < n_freq
    cos = jnp.cos(ang)
    sin = jnp.sin(ang)
    return cos, jnp.where(first, -sin, 0.0), jnp.where(first, 0.0, sin)


def _project(x, shift, scale, mod_row, w_in, tm, kv_dtype, rope):
    b, t, _ = x.shape
    tok = lambda bi, ti: (bi, ti, 0)
    mod = lambda bi, ti: (mod_row(bi), 0, 0)
    in_specs = [pl.BlockSpec((None, tm, D_MODEL), tok),
                pl.BlockSpec((None, 1, D_MODEL), mod),
                pl.BlockSpec((None, 1, D_MODEL), mod),
                pl.BlockSpec(w_in.shape, lambda bi, ti: (0, 0))]
    args = [x, shift, scale, w_in]
    if rope:
        tab = lambda bi, ti: (ti, 0)
        in_specs += [pl.BlockSpec((tm, LANES), tab)] * 3
        args += list(_rope_tables(t))
    o512 = pl.BlockSpec((None, tm, 512), tok)
    shp = lambda dt: jax.ShapeDtypeStruct((b, t, 512), dt)
    return pl.pallas_call(
        functools.partial(_proj_body, rope=rope),
        grid=(b, t // tm),
        in_specs=in_specs,
        out_specs=[o512] * 7,
        out_shape=[shp(BF16), shp(kv_dtype), shp(kv_dtype), shp(BF16), shp(BF16), shp(BF16), shp(F32)],
        compiler_params=_params("parallel", "parallel"),
        name="in_proj",
    )(*args)


def _pair_masks(shape):
    lane = lax.broadcasted_iota(I32, shape, 1)
    return lane < NA_DIM, lane >= NA_DIM


def _ctx_attn_body(q_ref, k_ref, v_ref, o_ref):
    t = q_ref.shape[0]
    lo, hi = _pair_masks((t, LANES))
    for hp in range(NA_WIDTH // LANES):
        sl = slice(hp * LANES, (hp + 1) * LANES)
        q2 = q_ref[:, sl] * (NA_DIM ** -0.5)
        k2 = k_ref[:, sl].astype(BF16)
        v2 = v_ref[:, sl].astype(BF16)
        zero = jnp.zeros_like(q2)
        qs = jnp.concatenate([jnp.where(lo, q2, zero), jnp.where(hi, q2, zero)], axis=0)
        s = _dot_nt(qs, k2)
        e = jnp.exp(s - jnp.max(s, axis=-1, keepdims=True))
        den = jnp.sum(e, axis=-1, keepdims=True)
        o = jnp.dot(e.astype(BF16), v2, preferred_element_type=F32) / den
        o_ref[:, sl] = jnp.where(lo, o[:t], o[t:]).astype(o_ref.dtype)


def _context_attention(q, k, v):
    b, t, _ = q.shape
    spec = pl.BlockSpec((None, t, NA_WIDTH), lambda bi: (bi, 0, 0))
    return pl.pallas_call(
        _ctx_attn_body,
        grid=(b,),
        in_specs=[spec] * 3,
        out_specs=spec,
        out_shape=jax.ShapeDtypeStruct((b, t, NA_WIDTH), BF16),
        compiler_params=_params("parallel"),
        name="ctx_attn",
    )(q, k, v)


def _na_bias_body(rpb_ref, o_ref):
    h = pl.program_id(0)
    v = pl.program_id(1)
    shape = (GRID_W, LANES)
    w = lax.broadcasted_iota(I32, shape, 0)
    lane = lax.broadcasted_iota(I32, shape, 1)
    upper = lane >= GRID_W
    kc = jnp.where(upper, lane - GRID_W, lane)
    cdiff = kc - w + (NA_KW - 1)
    cstart = jnp.clip(w - NA_KW // 2, 0, GRID_W - NA_KW)
    inwin = (kc >= cstart) & (kc < cstart + NA_KW)
    n_ro = 2 * NA_KH - 1
    n_co = 2 * NA_KW - 1
    for j in range(NA_KH // 2):
        ro = 2 * j - v + (NA_KH - 1)
        base0 = (h * n_ro + ro) * n_co
        acc = jnp.zeros(shape, F32)
        for c in range(n_co):
            val = jnp.where(upper, rpb_ref[base0 + n_co + c], rpb_ref[base0 + c])
            acc = jnp.where(cdiff == c, val, acc)
        o_ref[:, j * LANES:(j + 1) * LANES] = jnp.where(inwin, acc, NEG_BIG)


def _na_bias(rpb):
    n = NA_KH * GRID_W
    return pl.pallas_call(
        _na_bias_body,
        grid=(NA_HEADS, NA_KH),
        in_specs=[pl.BlockSpec(memory_space=pltpu.SMEM)],
        out_specs=pl.BlockSpec((None, None, GRID_W, n), lambda h, v: (h, v, 0, 0)),
        out_shape=jax.ShapeDtypeStruct((NA_HEADS, NA_KH, GRID_W, n), F32),
        compiler_params=_params("parallel", "parallel"),
        name="na_bias",
    )(rpb.reshape(-1))


def _na_body(q_ref, k_ref, v_ref, ck_ref, cv_ref, bias_ref, o_ref, s_buf, e_buf, d_buf, *, rows):
    ck = ck_ref[...].astype(BF16)
    cv = cv_ref[...].astype(BF16)
    lo, hi = _pair_masks((GRID_W, LANES))
    scale = NA_DIM ** -0.5
    n_loc = NA_KH * GRID_W

    def window(r):
        r = jnp.clip(r, 0, rows - 1)
        r_start = jnp.clip(r - NA_KH // 2, 0, rows - NA_KH)
        return (r - r_start, pl.ds(pl.multiple_of(r * GRID_W, GRID_W), GRID_W),
                pl.ds(pl.multiple_of(r_start * GRID_W, GRID_W), n_loc))

    def scores(r, slot):
        vidx, qsl, ksl = window(r)
        q2 = q_ref[qsl, :]
        zero = jnp.zeros_like(q2)
        q2 = q2 * scale
        qs = jnp.concatenate([jnp.where(lo, q2, zero), jnp.where(hi, q2, zero)], axis=0)
        bias = jnp.concatenate([bias_ref[0, vidx], bias_ref[1, vidx]], axis=0)
        s_buf[slot, :, :n_loc] = _dot_nt(qs, k_ref[ksl, :]) + bias
        s_buf[slot, :, n_loc:] = _dot_nt(qs, ck)

    def numerators(slot):
        s = s_buf[slot]
        e = jnp.exp(s - jnp.max(s, axis=-1, keepdims=True))
        d_buf[slot] = jnp.broadcast_to(jnp.sum(e, axis=-1, keepdims=True), d_buf.shape[1:])
        e_buf[slot] = e.astype(BF16)

    def output(r, slot):
        _, qsl, ksl = window(r)
        o = (jnp.dot(e_buf[slot, :, :n_loc], v_ref[ksl, :], preferred_element_type=F32)
             + jnp.dot(e_buf[slot, :, n_loc:], cv, preferred_element_type=F32)) / d_buf[slot]
        o_ref[qsl, :] = jnp.where(lo, o[:GRID_W], o[GRID_W:]).astype(o_ref.dtype)

    s_buf[...] = jnp.zeros_like(s_buf)
    e_buf[...] = jnp.zeros_like(e_buf)
    d_buf[...] = jnp.ones_like(d_buf)

    def two_steps(j, carry):
        r = 2 * j
        scores(r, 0)
        numerators(1)
        output(r - 2, 0)
        scores(r + 1, 1)
        numerators(0)
        output(r - 1, 1)
        return carry

    lax.fori_loop(0, rows // 2 + 1, two_steps, 0, unroll=3)


def _neighbourhood_attention(q, k, v, ck, cv, bias):
    b, t, _ = q.shape
    lc = ck.shape[1]
    rows = t // GRID_W
    tok = pl.BlockSpec((None, t, LANES), lambda hp, bi: (bi, 0, hp))
    ctx = pl.BlockSpec((None, lc, LANES), lambda hp, bi: (bi, 0, hp))
    return pl.pallas_call(
        functools.partial(_na_body, rows=rows),
        grid=(NA_WIDTH // LANES, b),
        in_specs=[tok, tok, tok, ctx, ctx,
                  pl.BlockSpec((2, NA_KH, GRID_W, NA_KH * GRID_W), lambda hp, bi: (hp, 0, 0, 0))],
        out_specs=tok,
        out_shape=jax.ShapeDtypeStruct((b, t, NA_WIDTH), BF16),
        scratch_shapes=[pltpu.VMEM((2, 2 * GRID_W, NA_KH * GRID_W + lc), F32),
                        pltpu.VMEM((2, 2 * GRID_W, NA_KH * GRID_W + lc), BF16),
                        pltpu.VMEM((2, 2 * GRID_W, LANES), F32)],
        compiler_params=_params("parallel", "parallel"),
        name="na_attn",
    )(q, k, v, ck, cv, bias)


def _ret_body(q_ref, k_ref, v_ref, g_ref, df_ref, db_ref, beta_ref, s0f_ref, s0b_ref,
              o_ref, sf_ref, sb_ref, kvf_s, kvb_s, *, n_chunks, heads):
    for hh in range(heads):
        slab = lambda ref: ref.at[:, hh * LANES:(hh + 1) * LANES]
        _ret_head(slab(q_ref), slab(k_ref), slab(v_ref), slab(g_ref), df_ref.at[hh], db_ref.at[hh],
                  beta_ref.at[hh], s0f_ref.at[hh], s0b_ref.at[hh], slab(o_ref), sf_ref.at[hh], sb_ref.at[hh],
                  kvf_s, kvb_s, n_chunks=n_chunks)


def _ret_head(q_ref, k_ref, v_ref, g_ref, df_ref, db_ref, beta_ref, s0f_ref, s0b_ref,
              o_ref, sf_ref, sb_ref, kvf_s, kvb_s, *, n_chunks):
    c_len = CHUNK
    lgf = -jnp.log1p(jnp.exp(-df_ref[...]))
    lgb = -jnp.log1p(jnp.exp(-db_ref[...]))
    shape = (c_len, c_len)
    i = lax.broadcasted_iota(I32, shape, 0).astype(F32)
    j = lax.broadcasted_iota(I32, shape, 1).astype(F32)
    dij = i - j
    d_comb = (jnp.where(dij >= 0, jnp.exp(jnp.maximum(dij, 0.0) * lgf), 0.0)
              + jnp.where(dij <= 0, jnp.exp(jnp.maximum(-dij, 0.0) * lgb), 0.0))
    xi_f = jnp.exp((i + 1.0) * lgf)
    xi_b = jnp.exp((c_len - i) * lgb)
    zeta_f = jnp.exp((c_len - 1.0 - i) * lgf)
    zeta_b = jnp.exp(i * lgb)
    g_f = jnp.exp(c_len * lgf)
    g_b = jnp.exp(c_len * lgb)

    def chunk(c):
        return pl.ds(pl.multiple_of(c * c_len, c_len), c_len)

    def kv_pass(c, carry):
        kc = k_ref[chunk(c), :]
        vc = v_ref[chunk(c), :].astype(F32)
        kvf_s[c] = _dot_tn(kc, (vc * zeta_f).astype(BF16))
        kvb_s[c] = _dot_tn(kc, (vc * zeta_b).astype(BF16))
        return carry

    lax.fori_loop(0, n_chunks, kv_pass, 0, unroll=RET_CHUNK_UNROLL)

    def scan_f(c, s):
        kv = kvf_s[c]
        kvf_s[c] = s
        return g_f * s + kv

    def scan_b(ci, s):
        c = n_chunks - 1 - ci
        kv = kvb_s[c]
        kvb_s[c] = s
        return g_b * s + kv

    sf_ref[...] = lax.fori_loop(0, n_chunks, scan_f, s0f_ref[...])
    sb_ref[...] = lax.fori_loop(0, n_chunks, scan_b, s0b_ref[...])

    beta = beta_ref[...]

    def out_pass(c, carry):
        qc = q_ref[chunk(c), :]
        kc = k_ref[chunk(c), :]
        vc = v_ref[chunk(c), :]
        scores = _dot_nt(qc, kc) * d_comb
        y = jnp.dot(scores.astype(BF16), vc, preferred_element_type=F32)
        qf = qc.astype(F32)
        y += jnp.dot((qf * xi_f).astype(BF16), kvf_s[c].astype(BF16), preferred_element_type=F32)
        y += jnp.dot((qf * xi_b).astype(BF16), kvb_s[c].astype(BF16), preferred_element_type=F32)
        o_ref[chunk(c), :] = (_standardize(y) * beta * _silu(g_ref[chunk(c), :])).astype(o_ref.dtype)
        return carry

    lax.fori_loop(0, n_chunks, out_pass, 0, unroll=RET_CHUNK_UNROLL)


def _retention(q, k, v, g, decay_f, decay_b, beta_ret, s0f, s0b, heads_per_step):
    b, t, _ = q.shape
    n_chunks = t // CHUNK
    hs = heads_per_step
    tok = pl.BlockSpec((None, t, hs * LANES), lambda bi, h: (bi, 0, h))
    per_head = pl.BlockSpec((hs, 1, LANES), lambda bi, h: (h, 0, 0))
    state = pl.BlockSpec((None, hs, RET_DK, RET_DK), lambda bi, h: (bi, h, 0, 0))
    lanes = lambda a: jnp.broadcast_to(a.astype(F32)[:, None, None], (RET_HEADS, 1, LANES))
    st_shape = jax.ShapeDtypeStruct((b, RET_HEADS, RET_DK, RET_DK), F32)
    return pl.pallas_call(
        functools.partial(_ret_body, n_chunks=n_chunks, heads=hs),
        grid=(b, RET_HEADS // hs),
        in_specs=[tok, tok, tok, tok, per_head, per_head, per_head, state, state],
        out_specs=[tok, state, state],
        out_shape=[jax.ShapeDtypeStruct((b, t, RET_WIDTH), BF16), st_shape, st_shape],
        scratch_shapes=[pltpu.VMEM((n_chunks, RET_DK, RET_DK), F32)] * 2,
        compiler_params=_params("parallel", "parallel"),
        name="retention",
    )(q, k, v, g, lanes(decay_f), lanes(decay_b), beta_ret.reshape(RET_HEADS, 1, LANES), s0f, s0b)


def _merge_body(x_ref, ona_ref, ret_ref, wna_ref, wret_ref, bna_ref, gate_ref, lg_ref, lb_ref,
                shf_ref, scf_ref, wh_ref, wl_ref, br_ref, carry_in_ref,
                o_ref, p_ref, gt_ref, tab_ref, cnt_ref, carry_s):
    @pl.when((pl.program_id(0) == 0) & (pl.program_id(1) == 0))
    def _():
        carry_s[...] = carry_in_ref[...]

    o = ona_ref[...].astype(F32)
    na = o * lax.rsqrt(jnp.mean(o * o, axis=-1, keepdims=True) + LN_EPS) * bna_ref[...]
    mix = (jnp.dot(na.astype(BF16), wna_ref[...], preferred_element_type=F32)
           + jnp.dot(ret_ref[...], wret_ref[...], preferred_element_type=F32))
    y = DEEPNORM_ALPHA * x_ref[...] + gate_ref[...] * mix
    y = _standardize(y) * lg_ref[...] + lb_ref[...]
    o_ref[...] = y

    h = y * (1.0 + scf_ref[...]) + shf_ref[...]
    w_hi, w_lo, bias = wh_ref[...], wl_ref[...], br_ref[...]
    carry = carry_s[...]
    for s in range(y.shape[0] // TOKEN_BLOCK):
        sl = slice(s * TOKEN_BLOCK, (s + 1) * TOKEN_BLOCK)
        rows, gates, table, carry = _route_block(h[sl], w_hi, w_lo, bias, carry)
        p_ref[:, sl] = rows
        gt_ref[:, sl] = gates
        tab_ref[s] = table
    carry_s[...] = carry
    cnt_ref[...] = carry


def _merge_and_route(x, o_na, ret, w_out, beta_na, gate, shift_f, scale_f, mod_row, ln_g, ln_b,
                     w_hi, w_lo, b_router, carry_in, tm):
    b, t, _ = x.shape
    nt = t // tm
    sub = tm // TOKEN_BLOCK
    tok = lambda bi, ti: (bi, ti, 0)
    flat = lambda bi, ti: (0, bi * nt + ti)
    mod = lambda bi, ti: (mod_row(bi), 0, 0)
    whole = lambda shape: pl.BlockSpec(shape, lambda bi, ti: (0,) * len(shape))
    half = pl.BlockSpec((None, tm, 512), tok)
    mod_spec = pl.BlockSpec((None, 1, D_MODEL), mod)
    return pl.pallas_call(
        _merge_body,
        grid=(b, nt),
        in_specs=[pl.BlockSpec((None, tm, D_MODEL), tok), half, half,
                  pl.BlockSpec((512, D_MODEL), lambda bi, ti: (0, 0)),
                  pl.BlockSpec((512, D_MODEL), lambda bi, ti: (1, 0)),
                  whole((1, 512)), mod_spec, whole((1, D_MODEL)), whole((1, D_MODEL)),
                  mod_spec, mod_spec,
                  whole((N_EXPERTS, D_MODEL)), whole((N_EXPERTS, D_MODEL)), whole((N_EXPERTS, TOKEN_BLOCK)),
                  whole((N_EXPERTS, LANES))],
        out_specs=[pl.BlockSpec((None, tm, D_MODEL), tok),
                   pl.BlockSpec((TOP_K, tm), flat), pl.BlockSpec((TOP_K, tm), flat),
                   pl.BlockSpec((sub, N_EXPERTS, LANES), lambda bi, ti: (bi * nt + ti, 0, 0)),
                   whole((N_EXPERTS, LANES))],
        out_shape=[jax.ShapeDtypeStruct(x.shape, F32),
                   jax.ShapeDtypeStruct((TOP_K, b * t), I32), jax.ShapeDtypeStruct((TOP_K, b * t), F32),
                   jax.ShapeDtypeStruct((b * t // TOKEN_BLOCK, N_EXPERTS, LANES), F32),
                   jax.ShapeDtypeStruct((N_EXPERTS, LANES), F32)],
        scratch_shapes=[pltpu.VMEM((N_EXPERTS, LANES), F32)],
        compiler_params=_params("arbitrary", "arbitrary"),
        name="merge_ln1_route",
    )(x, o_na, ret, w_out, w_out, beta_na.reshape(1, 512), gate, ln_g.reshape(1, -1), ln_b.reshape(1, -1),
      shift_f, scale_f, w_hi, w_lo, jnp.broadcast_to(b_router[:, None], (N_EXPERTS, TOKEN_BLOCK)), carry_in)


def _two_group_specs(n_p_blocks, blocks_per_batch_s, ctx_row):
    tok_p = lambda i: (jnp.minimum(i, n_p_blocks - 1), 0)
    tok_s = lambda i: (jnp.maximum(i - n_p_blocks, 0), 0)
    mod = lambda i: (jnp.where(i < n_p_blocks, ctx_row, jnp.maximum(i - n_p_blocks, 0) // blocks_per_batch_s), 0, 0)
    return tok_p, tok_s, mod


def _route_block(h, w_hi, w_lo, bias, carry):
    tm = h.shape[0]
    h_hi = h.astype(BF16)
    h_lo = (h - h_hi.astype(F32)).astype(BF16)
    lg = _dot_nt(jnp.concatenate([w_hi, w_lo], axis=0), h_hi)
    work = lg[:N_EXPERTS] + lg[N_EXPERTS:] + _dot_nt(w_hi, h_lo) + bias
    eidx = lax.broadcasted_iota(I32, (N_EXPERTS, tm), 0).astype(F32)
    vals, idxs, hots = [], [], []
    for _ in range(TOP_K):
        mx = jnp.max(work, axis=0, keepdims=True)
        idx = jnp.min(jnp.where(work == mx, eidx, float(N_EXPERTS)), axis=0, keepdims=True)
        hot = eidx == idx
        vals.append(mx)
        idxs.append(idx)
        hots.append(hot)
        work = jnp.where(hot, -jnp.inf, work)
    exps = [jnp.exp(v - vals[0]) for v in vals]
    den = exps[0] + exps[1] + exps[2] + exps[3]
    sel = jnp.zeros((N_EXPERTS, tm), F32)
    for hot in hots:
        sel = sel + hot.astype(F32)
    r_i = lax.broadcasted_iota(I32, (tm, tm), 0)
    c_i = lax.broadcasted_iota(I32, (tm, tm), 1)
    earlier = (r_i < c_i).astype(BF16)
    before = jnp.dot(sel.astype(BF16), earlier, preferred_element_type=F32)
    n_e = jnp.sum(sel, axis=1, keepdims=True)
    n8 = jnp.floor((n_e + (RUN_ALIGN - 1.0)) * (1.0 / RUN_ALIGN)) * RUN_ALIGN + jnp.zeros((1, LANES), F32)
    e_r = lax.broadcasted_iota(I32, (N_EXPERTS, N_EXPERTS), 0)
    e_c = lax.broadcasted_iota(I32, (N_EXPERTS, N_EXPERTS), 1)
    off = jnp.dot((e_c < e_r).astype(BF16), n8.astype(BF16), preferred_element_type=F32)
    local = before + jnp.concatenate([off] * (tm // LANES), axis=1)
    rows = [jnp.sum(jnp.where(hot, local, 0.0), axis=0, keepdims=True) for hot in hots]
    lane = lax.broadcasted_iota(I32, (N_EXPERTS, LANES), 1)
    table = jnp.where(lane == 0, n8, jnp.where(lane == 1, carry, jnp.where(lane == 2, off, 0.0)))
    return (jnp.concatenate(rows, axis=0).astype(I32), jnp.concatenate([e / den for e in exps], axis=0),
            table, carry + n8)


def _for_run_pieces(n_rows, visit):
    off = 0
    for size in RUN_PIECES:
        present = (n_rows & size) != 0
        visit(off, size, present)
        off = off + jnp.where(present, size, 0)


def _start_unit_copies(unit_ref, local_buf, sorted_hbm, sem, to_hbm):
    def one(u, carry):
        loc = local_buf.at[pl.ds(pl.multiple_of(u * RUN_ALIGN, RUN_ALIGN), RUN_ALIGN)]
        glob = sorted_hbm.at[pl.ds(pl.multiple_of(unit_ref[u], RUN_ALIGN), RUN_ALIGN)]
        copy = pltpu.make_async_copy(loc, glob, sem) if to_hbm else pltpu.make_async_copy(glob, loc, sem)
        copy.start()
        return carry

    lax.fori_loop(0, LOCAL_UNITS, one, 0, unroll=8)


def _wait_unit_copies(local_buf, sorted_hbm, sem):
    whole = sorted_hbm.at[pl.ds(0, LOCAL_ROWS)]
    pltpu.make_async_copy(whole, local_buf, sem).wait()


def _dispatch_body(unit_ref, zlo_ref, zn_ref, p_ref, yp_ref, ys_ref, sh_ref, sc_ref, xs_ref,
                   xbuf, zbuf, sems, zsem, *, n_p_blocks, n_blocks):
    i = pl.program_id(0)
    slot = i % 2
    y = jnp.where(i < n_p_blocks, yp_ref[...], ys_ref[...])
    h = (y * (1.0 + sc_ref[...]) + sh_ref[...]).astype(BF16)
    rows = p_ref[...]
    r_iota = lax.broadcasted_iota(I32, (LOCAL_ROWS, rows.shape[1]), 0)
    place = r_iota == rows[0:1]
    for kk in range(1, TOP_K):
        place = place | (r_iota == rows[kk:kk + 1])
    xbuf[slot] = _pack_bf16_pairs(
        jnp.dot(jnp.where(place, 1.0, 0.0).astype(BF16), h, preferred_element_type=F32))

    _start_unit_copies(unit_ref, xbuf.at[slot], xs_ref, sems.at[slot], to_hbm=True)

    @pl.when(i > 0)
    def _():
        _wait_unit_copies(xbuf.at[1 - slot], xs_ref, sems.at[1 - slot])

    @pl.when(i == n_blocks - 1)
    def _():
        _wait_unit_copies(xbuf.at[slot], xs_ref, sems.at[slot])
        zbuf[...] = jnp.zeros_like(zbuf)
        sem = zsem

        def per_expert(e, carry):
            lo = zlo_ref[e]

            def visit(off, size, present):
                copy = pltpu.make_async_copy(
                    zbuf.at[pl.ds(0, size)], xs_ref.at[pl.ds(pl.multiple_of(lo + off, RUN_ALIGN), size)], sem)

                @pl.when(present)
                def _():
                    copy.start()
                    copy.wait()

            _for_run_pieces(zn_ref[e], visit)
            return carry

        lax.fori_loop(0, N_EXPERTS, per_expert, 0)

        tail_lo = zlo_ref[N_EXPERTS]

        def tile_copy(t):
            start = pl.multiple_of(tail_lo + t * EXPERT_TILE, EXPERT_TILE)
            return pltpu.make_async_copy(zbuf, xs_ref.at[pl.ds(start, EXPERT_TILE)], sem)

        def t_issue(t, c2):
            tile_copy(t).start()
            return c2

        def t_drain(t, c2):
            tile_copy(0).wait()
            return c2

        n_tail = zn_ref[N_EXPERTS] // EXPERT_TILE
        lax.fori_loop(0, n_tail, t_issue, 0)
        lax.fori_loop(0, n_tail, t_drain, 0)


def _dispatch(yp, ys, shift, scale, ctx_row, s_blocks_per_batch, local_rows, unit_dst, zero_lo, zero_n,
              total_rows, tm):
    n_p, n_s = yp.shape[0], ys.shape[0]
    n = n_p + n_s
    npb = n_p // tm
    nb = n // tm
    tok_p, tok_s, mod = _two_group_specs(npb, s_blocks_per_batch, ctx_row)
    smem_all = pl.BlockSpec(memory_space=pltpu.SMEM)
    return pl.pallas_call(
        functools.partial(_dispatch_body, n_p_blocks=npb, n_blocks=nb),
        grid=(nb,),
        in_specs=[pl.BlockSpec((UNIT_TABLE,), lambda i: (i,), memory_space=pltpu.SMEM), smem_all, smem_all,
                  pl.BlockSpec((TOP_K, tm), lambda i: (0, i)),
                  pl.BlockSpec((tm, D_MODEL), tok_p), pl.BlockSpec((tm, D_MODEL), tok_s),
                  pl.BlockSpec((None, 1, D_MODEL), mod), pl.BlockSpec((None, 1, D_MODEL), mod)],
        out_specs=pl.BlockSpec(memory_space=pl.ANY),
        out_shape=jax.ShapeDtypeStruct((total_rows, D_MODEL // 2), U32),
        scratch_shapes=[pltpu.VMEM((2, LOCAL_ROWS, D_MODEL // 2), U32),
                        pltpu.VMEM((EXPERT_TILE, D_MODEL // 2), U32),
                        pltpu.SemaphoreType.DMA((2,)), pltpu.SemaphoreType.DMA(())],
        compiler_params=_params("arbitrary"),
        name="moe_dispatch",
    )(unit_dst, zero_lo, zero_n, local_rows, yp, ys, shift, scale)


def _expert_body(te_ref, first_ref, half_ref, nu_ref, x_ref, wgu_ref, bgu_ref, wd_ref, bd_ref, y_ref,
                 act_s, wgu_s, wd_s):
    j = pl.program_id(0)

    @pl.when(first_ref[j] == 1)
    def _():
        blk = 2 * LANES
        src = lax.broadcasted_iota(I32, (blk, blk), 0)
        dst = lax.broadcasted_iota(I32, (blk, blk), 1)
        perm = (src == jnp.where(dst < LANES, 2 * dst, 2 * (dst - LANES) + 1)).astype(F32).astype(BF16)
        for s in range(2 * D_FF // blk):
            sl = slice(s * blk, (s + 1) * blk)
            wgu_s[:, sl] = jnp.dot(wgu_ref[:, sl].astype(BF16), perm, preferred_element_type=F32).astype(BF16)
        wd_s[...] = wd_ref[...].astype(BF16)

    def ffn(n_rows):
        rows = slice(0, n_rows)
        x = _unpack_bf16_pairs(x_ref[rows, :])
        for c in range(D_FF // LANES):
            sl = slice(c * 2 * LANES, (c + 1) * 2 * LANES)
            hu = jnp.dot(x, wgu_s[:, sl], preferred_element_type=F32) + bgu_ref[:, sl]
            x_glu = jnp.minimum(hu[:, :LANES], SWIGLU_LIMIT)
            x_lin = jnp.clip(hu[:, LANES:], -SWIGLU_LIMIT, SWIGLU_LIMIT)
            act = x_glu * (1.0 / (1.0 + jnp.exp(-SWIGLU_ALPHA * x_glu))) * (x_lin + 1.0)
            act_s[rows, c * LANES:(c + 1) * LANES] = act.astype(BF16)
        y_ref[rows, :] = _pack_bf16_pairs(
            jnp.dot(act_s[rows, :], wd_s[...], preferred_element_type=F32) + bd_ref[...])

    used = j < nu_ref[0]
    half = half_ref[j] == 1

    @pl.when(used & jnp.logical_not(half))
    def _():
        ffn(EXPERT_TILE)

    @pl.when(used & half)
    def _():
        ffn(EXPERT_TILE // 2)
        y_ref[EXPERT_TILE // 2:, :] = jnp.zeros((EXPERT_TILE // 2, y_ref.shape[1]), y_ref.dtype)

    @pl.when(jnp.logical_not(used))
    def _():
        y_ref[...] = jnp.zeros_like(y_ref)


def _experts(tile_expert, tile_first, tile_half, n_used, xs, cap_rows, w_gate_up, bgu, w_down, bd):
    n_tiles = cap_rows // EXPERT_TILE
    of_expert = lambda j, te, first, half, nu: (te[j], 0, 0)
    grid_spec = pltpu.PrefetchScalarGridSpec(
        num_scalar_prefetch=4,
        grid=(n_tiles,),
        in_specs=[pl.BlockSpec((EXPERT_TILE, D_MODEL // 2),
                               lambda j, te, first, half, nu: (jnp.minimum(j, nu[0] - 1), 0)),
                  pl.BlockSpec((None, D_MODEL, 2 * D_FF), of_expert),
                  pl.BlockSpec((None, 1, 2 * D_FF), of_expert),
                  pl.BlockSpec((None, D_FF, D_MODEL), of_expert),
                  pl.BlockSpec((None, 1, D_MODEL), of_expert)],
        out_specs=pl.BlockSpec((EXPERT_TILE, D_MODEL // 2), lambda j, te, first, half, nu: (j, 0)),
        scratch_shapes=[pltpu.VMEM((EXPERT_TILE, D_FF), BF16),
                        pltpu.VMEM((D_MODEL, 2 * D_FF), BF16), pltpu.VMEM((D_FF, D_MODEL), BF16)],
    )
    return pl.pallas_call(
        _expert_body,
        grid_spec=grid_spec,
        out_shape=jax.ShapeDtypeStruct((cap_rows, D_MODEL // 2), U32),
        compiler_params=_params("arbitrary"),
        name="moe_experts",
    )(tile_expert, tile_first, tile_half, n_used, xs, w_gate_up, bgu, w_down, bd)


def _combine_body(unit_ref, next_unit_ref, y1_ref, p_ref, gt_ref, gate_ref, lg_ref, lb_ref, ys_ref, o_ref,
                  ybuf, sems, *, n_blocks):
    i = pl.program_id(0)
    slot = i % 2
    tm = y1_ref.shape[0]

    @pl.when(i == 0)
    def _():
        _start_unit_copies(unit_ref, ybuf.at[0], ys_ref, sems.at[0], to_hbm=False)

    @pl.when(i + 1 < n_blocks)
    def _():
        _start_unit_copies(next_unit_ref, ybuf.at[1 - slot], ys_ref, sems.at[1 - slot], to_hbm=False)

    _wait_unit_copies(ybuf.at[slot], ys_ref, sems.at[slot])

    rows = p_ref[...]
    gates = gt_ref[...]
    c_iota = lax.broadcasted_iota(I32, (tm, LOCAL_ROWS), 1)
    weight = jnp.where(c_iota == rows[:, 0:1], gates[:, 0:1], 0.0)
    for kk in range(1, TOP_K):
        weight = weight + jnp.where(c_iota == rows[:, kk:kk + 1], gates[:, kk:kk + 1], 0.0)
    w_hi = weight.astype(BF16)
    w_lo = (weight - w_hi.astype(F32)).astype(BF16)
    yb = _unpack_bf16_pairs(ybuf[slot])
    f = (jnp.dot(w_hi, yb, preferred_element_type=F32) + jnp.dot(w_lo, yb, preferred_element_type=F32))
    y = DEEPNORM_ALPHA * y1_ref[...] + gate_ref[...] * f
    o_ref[...] = _standardize(y) * lg_ref[...] + lb_ref[...]


def _combine(y1, local_rows, gates, unit_src, block0, gate_mod, mod_row, blocks_per_batch,
             y_sorted, ln_g, ln_b, tm):
    n_g = y1.shape[0]
    nb = n_g // tm
    row = pl.BlockSpec((1, D_MODEL), lambda i: (0, 0))
    pair = pl.BlockSpec((tm, TOP_K), lambda i: (i + block0, 0))
    units = lambda ahead: pl.BlockSpec(
        (UNIT_TABLE,), lambda i: (jnp.minimum(i + ahead, nb - 1) + block0,), memory_space=pltpu.SMEM)
    return pl.pallas_call(
        functools.partial(_combine_body, n_blocks=nb),
        grid=(nb,),
        in_specs=[units(0), units(1),
                  pl.BlockSpec((tm, D_MODEL), lambda i: (i, 0)), pair, pair,
                  pl.BlockSpec((None, 1, D_MODEL), lambda i: (mod_row(i // blocks_per_batch), 0, 0)),
                  row, row,
                  pl.BlockSpec(memory_space=pl.ANY)],
        out_specs=pl.BlockSpec((tm, D_MODEL), lambda i: (i, 0)),
        out_shape=jax.ShapeDtypeStruct((n_g, D_MODEL), F32),
        scratch_shapes=[pltpu.VMEM((2, LOCAL_ROWS, D_MODEL // 2), U32), pltpu.SemaphoreType.DMA((2,))],
        compiler_params=_params("arbitrary"),
        name="moe_combine_ln2",
    )(unit_src, unit_src, y1, local_rows, gates, gate_mod,
      ln_g.reshape(1, -1), ln_b.reshape(1, -1), y_sorted)


def _regroup_gate_up_bias(b_gate_up):
    e = b_gate_up.shape[0]
    nblk = D_FF // LANES
    return b_gate_up.reshape(e, 1, nblk, LANES, 2).transpose(0, 1, 2, 4, 3).reshape(e, 1, 2 * D_FF)


def _routing_plan(table, totals, n_pairs):
    nb = table.shape[0]
    run_n = table[:, :, 0].astype(I32)
    rows_before = table[:, :, 1].astype(I32)
    run_local = table[:, :, 2].astype(I32)
    cnt = totals[:, 0].astype(I32)
    padded = (cnt + EXPERT_TILE - 1) // EXPERT_TILE * EXPERT_TILE
    pad_end = jnp.cumsum(padded)
    pad_start = pad_end - padded
    run_global = pad_start[None, :] + rows_before
    rows_max = n_pairs + nb * N_EXPERTS * (RUN_ALIGN - 1) + N_EXPERTS * (EXPERT_TILE - 1)
    n_tiles = -(-rows_max // EXPERT_TILE)
    n_used = pad_end[-1] // EXPERT_TILE
    tile_row = jnp.minimum(jnp.arange(n_tiles, dtype=I32), n_used - 1) * EXPERT_TILE
    tile_expert = jnp.sum((pad_end[None, :] <= tile_row[:, None]).astype(I32), axis=1)
    tile_expert = jnp.minimum(tile_expert, N_EXPERTS - 1)
    tile_id = jnp.arange(n_tiles, dtype=I32)
    changed = jnp.concatenate([jnp.ones((1,), bool), tile_expert[1:] != tile_expert[:-1]])
    tile_first = (changed & (tile_id < n_used)).astype(I32)
    exp_hot = tile_expert[:, None] == jnp.arange(N_EXPERTS, dtype=I32)
    rows_end = jnp.sum(jnp.where(exp_hot, (pad_start + cnt)[None, :], 0), axis=-1)
    tile_half = ((rows_end - tile_row) <= EXPERT_TILE // 2).astype(I32)
    cap_rows = n_tiles * EXPERT_TILE
    spare_rows = -(-LOCAL_ROWS // EXPERT_TILE) * EXPERT_TILE
    total_rows = cap_rows + 2 * spare_rows
    unit_row = jnp.arange(LOCAL_UNITS, dtype=I32) * RUN_ALIGN
    run_end = run_local + run_n
    owner = jnp.sum((run_end[:, None, :] <= unit_row[None, :, None]).astype(I32), axis=-1)
    own_hot = owner[:, :, None] == jnp.arange(N_EXPERTS, dtype=I32)
    shift = jnp.sum(jnp.where(own_hot, (run_global - run_local)[:, None, :], 0), axis=-1)
    used = owner < N_EXPERTS
    sorted_row = unit_row[None, :] + shift
    slot = (jnp.arange(nb, dtype=I32) % 2)[:, None]
    unit_dst = jnp.where(used, sorted_row, cap_rows + slot * spare_rows + unit_row[None, :])
    unit_src = jnp.where(used, sorted_row, unit_row[None, :])
    table_of = lambda a: jnp.pad(a, ((0, 0), (0, UNIT_TABLE - LOCAL_UNITS))).reshape(-1)
    zero_lo = jnp.concatenate([pad_start + cnt, pad_end[-1:]])
    zero_n = jnp.concatenate([padded - cnt, total_rows - pad_end[-1:]])
    return ((table_of(unit_dst), table_of(unit_src)), (tile_expert, tile_first, tile_half, n_used.reshape(1)),
            zero_lo, zero_n, cap_rows, total_rows)


def kernel(x_prompt, x_sample, cache_na_k, cache_na_v, state_ret_fwd, state_ret_bwd, c, c_ctx, w_ada, b_ada, w_in, rpb, ret_decay_fwd, ret_decay_bwd, beta_na, beta_ret, w_out, ln1_g, ln1_b, w_router, b_router, w_gate_up, b_gate_up, w_down, b_down, ln2_g, ln2_b):
    bp, tp, _ = x_prompt.shape
    bs, ts, _ = x_sample.shape
    assert w_ada.shape[0] == DEPTH == 1
    l = 0
    ctx_row = bs
    sample_row = lambda bi: bi
    prompt_row = lambda bi: ctx_row

    cond = jnp.concatenate([c, c_ctx[None, :], jnp.zeros((16 - bs - 1, D_MODEL), F32)], axis=0)
    m = _adaln(cond, w_ada[l], b_ada[l]).reshape(16, 6, 1, D_MODEL)
    sh_a, sc_a, gt_a, sh_f, sc_f, gt_f = (m[:, i] for i in range(6))

    w_in_b = w_in[l].astype(BF16)
    w_out_b = w_out[l].astype(BF16)

    q, k_p, v_p, qr, kr, vr, g = _project(x_prompt, sh_a, sc_a, prompt_row, w_in_b, tp, F32, rope=False)
    o_na = _context_attention(q, k_p, v_p)
    zeros_state = jnp.zeros((bp, RET_HEADS, RET_DK, RET_DK), F32)
    ret, s_f, s_b = _retention(qr, kr, vr, g, ret_decay_fwd[l], ret_decay_bwd[l], beta_ret[l],
                               zeros_state, zeros_state, heads_per_step=RET_HEADS)
    w_r = w_router[l].T
    w_r_hi = w_r.astype(BF16)
    w_r_lo = (w_r - w_r_hi.astype(F32)).astype(BF16)
    route = functools.partial(_merge_and_route, w_out=w_out_b, beta_na=beta_na[l], gate=gt_a, shift_f=sh_f,
                              scale_f=sc_f, ln_g=ln1_g[l], ln_b=ln1_b[l], w_hi=w_r_hi, w_lo=w_r_lo,
                              b_router=b_router[l])
    yp1, rows_p, gates_p, table_p, totals_p = route(
        x_prompt, o_na, ret, mod_row=prompt_row, carry_in=jnp.zeros((N_EXPERTS, LANES), F32), tm=tp)

    q, k_s, v_s, qr, kr, vr, g = _project(x_sample, sh_a, sc_a, sample_row, w_in_b, 512, BF16, rope=True)
    lc = cache_na_k.shape[2]
    o_na = _neighbourhood_attention(q, k_s, v_s, cache_na_k[:, l].reshape(bs, lc, NA_WIDTH),
                                    cache_na_v[:, l].reshape(bs, lc, NA_WIDTH), _na_bias(rpb[l]))
    ret, _, _ = _retention(qr, kr, vr, g, ret_decay_fwd[l], ret_decay_bwd[l], beta_ret[l],
                           state_ret_fwd[:, l], state_ret_bwd[:, l], heads_per_step=1)
    ys1, rows_s, gates_s, table_s, totals = route(
        x_sample, o_na, ret, mod_row=sample_row, carry_in=totals_p, tm=512)

    tm = TOKEN_BLOCK
    yp1f = yp1.reshape(bp * tp, D_MODEL)
    ys1f = ys1.reshape(bs * ts, D_MODEL)
    local_rows = jnp.concatenate([rows_p, rows_s], axis=1)
    gates = jnp.concatenate([gates_p, gates_s], axis=1)
    table = jnp.concatenate([table_p, table_s], axis=0)
    (unit_dst, unit_src), tiles, zero_lo, zero_n, cap_rows, total_rows = _routing_plan(
        table, totals, (bp * tp + bs * ts) * TOP_K)
    xs = _dispatch(yp1f, ys1f, sh_f, sc_f, ctx_row, ts // tm, local_rows, unit_dst, zero_lo, zero_n,
                   total_rows, tm)
    y_sorted = _experts(*tiles, xs, cap_rows, w_gate_up[l], _regroup_gate_up_bias(b_gate_up[l]), w_down[l],
                        b_down[l][:, None, :])
    rows_t, gates_t = local_rows.T, gates.T
    yp = _combine(yp1f, rows_t, gates_t, unit_src, 0, gt_f, prompt_row, 1, y_sorted, ln2_g[l], ln2_b[l], tm)
    ys = _combine(ys1f, rows_t, gates_t, unit_src, (bp * tp) // tm, gt_f, sample_row, ts // tm, y_sorted,
                  ln2_g[l], ln2_b[l], tm)

    return (yp.reshape(bp, tp, D_MODEL), ys.reshape(bs, ts, D_MODEL),
            k_p.reshape(bp, 1, tp, NA_HEADS, NA_DIM), v_p.reshape(bp, 1, tp, NA_HEADS, NA_DIM),
            s_f[:, None], s_b[:, None])
```

```python
import functools

import jax
import jax.numpy as jnp
from jax import lax
from jax.experimental import pallas as pl
from jax.experimental.pallas import tpu as pltpu

F32 = jnp.float32
BF16 = jnp.bfloat16
I32 = jnp.int32
U32 = jnp.uint32

D_MODEL = 1024
GRID_W = 64
NA_HEADS = 8
NA_DIM = 64
NA_WIDTH = NA_HEADS * NA_DIM
NA_KH = 8
NA_KW = 16
RET_HEADS = 4
RET_DK = 128
RET_WIDTH = RET_HEADS * RET_DK
CHUNK = 128
N_EXPERTS = 32
TOP_K = 4
D_FF = 1024
SWIGLU_LIMIT = 7.0
SWIGLU_ALPHA = 1.702
ROPE_BASE = 10000.0
LN_EPS = 1e-5
DEPTH = 1
DEEPNORM_ALPHA = (2.0 * DEPTH) ** 0.25

LANES = 128
NEG_BIG = -1e30
TOKEN_BLOCK = 256
EXPERT_TILE = 1024
VMEM_LIMIT = 56 * 1024 * 1024
RET_CHUNK_UNROLL = 8
SUBLANES = 8
RUN_ALIGN = SUBLANES
LOCAL_ROWS = TOKEN_BLOCK * TOP_K + N_EXPERTS * RUN_ALIGN
LOCAL_UNITS = LOCAL_ROWS // RUN_ALIGN
UNIT_TABLE = 256
PAD_PIECES = tuple(EXPERT_TILE >> s for s in range(1, EXPERT_TILE.bit_length())
                   if (EXPERT_TILE >> s) >= RUN_ALIGN)


def _params(*sem):
    return pltpu.CompilerParams(dimension_semantics=sem, vmem_limit_bytes=VMEM_LIMIT)


def _silu(x):
    return x / (1.0 + jnp.exp(-x))


def _standardize(x):
    mu = jnp.mean(x, axis=-1, keepdims=True)
    xc = x - mu
    var = jnp.mean(xc * xc, axis=-1, keepdims=True)
    return xc * lax.rsqrt(var + LN_EPS)


def _pack_bf16_pairs(x):
    bits = lax.bitcast_convert_type(x.astype(BF16).astype(F32), U32)
    half = x.shape[1] // 2
    return bits[:, :half] | lax.shift_right_logical(bits[:, half:], jnp.uint32(16))


def _unpack_bf16_pairs(p):
    hi = lax.bitcast_convert_type(p & jnp.uint32(0xFFFF0000), F32)
    lo = lax.bitcast_convert_type(lax.shift_left(p, jnp.uint32(16)), F32)
    return jnp.concatenate([hi, lo], axis=1).astype(BF16)


def _dot_nt(a, b):
    return lax.dot_general(a, b, (((1,), (1,)), ((), ())), preferred_element_type=F32)


def _dot_tn(a, b):
    return lax.dot_general(a, b, (((0,), (0,)), ((), ())), preferred_element_type=F32)


def _ada_body(c_ref, w_ref, b_ref, o_ref):
    s = _silu(c_ref[...])
    o_ref[...] = jnp.dot(s, w_ref[...], preferred_element_type=F32,
                         precision=lax.Precision.HIGHEST) + b_ref[...]


def _adaln(cond, w_ada, b_ada):
    r = cond.shape[0]
    n = w_ada.shape[1]
    tn = 1536
    return pl.pallas_call(
        _ada_body,
        grid=(n // tn,),
        in_specs=[pl.BlockSpec((r, D_MODEL), lambda j: (0, 0)),
                  pl.BlockSpec((D_MODEL, tn), lambda j: (0, j)),
                  pl.BlockSpec((1, tn), lambda j: (0, j))],
        out_specs=pl.BlockSpec((r, tn), lambda j: (0, j)),
        out_shape=jax.ShapeDtypeStruct((r, n), F32),
        compiler_params=_params("arbitrary"),
        name="adaln",
    )(cond, w_ada, b_ada.reshape(1, n))


def _proj_body(*refs, rope):
    if rope:
        (x_ref, sh_ref, sc_ref, w_ref, cos_ref, sa_ref, sb_ref,
         q_ref, k_ref, v_ref, qr_ref, kr_ref, vr_ref, g_ref) = refs
    else:
        (x_ref, sh_ref, sc_ref, w_ref,
         q_ref, k_ref, v_ref, qr_ref, kr_ref, vr_ref, g_ref) = refs
    h = (x_ref[...] * (1.0 + sc_ref[...]) + sh_ref[...]).astype(BF16)

    def cols(c):
        return jnp.dot(h, w_ref[:, c * 512:(c + 1) * 512], preferred_element_type=F32)

    q_ref[...] = cols(0).astype(q_ref.dtype)
    k_ref[...] = cols(1).astype(k_ref.dtype)
    v_ref[...] = cols(2).astype(v_ref.dtype)
    pq = cols(3)
    pk = cols(4) * (RET_DK ** -0.5)
    if rope:
        cs, sa, sb = cos_ref[...], sa_ref[...], sb_ref[...]
        for hd in range(RET_HEADS):
            sl = slice(hd * LANES, (hd + 1) * LANES)
            for p, o_ref in ((pq, qr_ref), (pk, kr_ref)):
                xs = p[:, sl]
                rot = xs * cs + pltpu.roll(xs, 96, 1) * sa + pltpu.roll(xs, 32, 1) * sb
                o_ref[:, sl] = rot.astype(o_ref.dtype)
    else:
        qr_ref[...] = pq.astype(qr_ref.dtype)
        kr_ref[...] = pk.astype(kr_ref.dtype)
    vr_ref[...] = cols(5).astype(vr_ref.dtype)
    g_ref[...] = cols(6)


def _rope_tables(t_len):
    t = jnp.arange(t_len, dtype=jnp.int32)
    pos_row = (t // GRID_W).astype(F32)[:, None]
    pos_col = (t % GRID_W).astype(F32)[:, None]
    lane = jnp.arange(LANES, dtype=jnp.int32)[None, :]
    n_freq = RET_DK // 4
    inv_freq = ROPE_BASE ** (-(lane % n_freq).astype(F32) / n_freq)
    ang = jnp.where(lane < RET_DK // 2, pos_row, pos_col) * inv_freq
    first = (lane % (2 * n_freq)) < n_freq
    cos = jnp.cos(ang)
    sin = jnp.sin(ang)
    return cos, jnp.where(first, -sin, 0.0), jnp.where(first, 0.0, sin)


def _project(x, shift, scale, mod_row, w_in, tm, kv_dtype, rope):
    b, t, _ = x.shape
    tok = lambda bi, ti: (bi, ti, 0)
    mod = lambda bi, ti: (mod_row(bi), 0, 0)
    in_specs = [pl.BlockSpec((None, tm, D_MODEL), tok),
                pl.BlockSpec((None, 1, D_MODEL), mod),
                pl.BlockSpec((None, 1, D_MODEL), mod),
                pl.BlockSpec(w_in.shape, lambda bi, ti: (0, 0))]
    args = [x, shift, scale, w_in]
    if rope:
        tab = lambda bi, ti: (ti, 0)
        in_specs += [pl.BlockSpec((tm, LANES), tab)] * 3
        args += list(_rope_tables(t))
    o512 = pl.BlockSpec((None, tm, 512), tok)
    shp = lambda dt: jax.ShapeDtypeStruct((b, t, 512), dt)
    return pl.pallas_call(
        functools.partial(_proj_body, rope=rope),
        grid=(b, t // tm),
        in_specs=in_specs,
        out_specs=[o512] * 7,
        out_shape=[shp(BF16), shp(kv_dtype), shp(kv_dtype), shp(BF16), shp(BF16), shp(BF16), shp(F32)],
        compiler_params=_params("parallel", "parallel"),
        name="in_proj",
    )(*args)


def _pair_masks(shape):
    lane = lax.broadcasted_iota(I32, shape, 1)
    return lane < NA_DIM, lane >= NA_DIM


def _ctx_attn_body(q_ref, k_ref, v_ref, o_ref):
    t = q_ref.shape[0]
    lo, hi = _pair_masks((t, LANES))
    for hp in range(NA_WIDTH // LANES):
        sl = slice(hp * LANES, (hp + 1) * LANES)
        q2 = q_ref[:, sl] * (NA_DIM ** -0.5)
        k2 = k_ref[:, sl].astype(BF16)
        v2 = v_ref[:, sl].astype(BF16)
        zero = jnp.zeros_like(q2)
        qs = jnp.concatenate([jnp.where(lo, q2, zero), jnp.where(hi, q2, zero)], axis=0)
        s = _dot_nt(qs, k2)
        e = jnp.exp(s - jnp.max(s, axis=-1, keepdims=True))
        den = jnp.sum(e, axis=-1, keepdims=True)
        o = jnp.dot(e.astype(BF16), v2, preferred_element_type=F32) / den
        o_ref[:, sl] = jnp.where(lo, o[:t], o[t:]).astype(o_ref.dtype)


def _context_attention(q, k, v):
    b, t, _ = q.shape
    spec = pl.BlockSpec((None, t, NA_WIDTH), lambda bi: (bi, 0, 0))
    return pl.pallas_call(
        _ctx_attn_body,
        grid=(b,),
        in_specs=[spec] * 3,
        out_specs=spec,
        out_shape=jax.ShapeDtypeStruct((b, t, NA_WIDTH), BF16),
        compiler_params=_params("parallel"),
        name="ctx_attn",
    )(q, k, v)


def _na_bias_body(rpb_ref, o_ref):
    h = pl.program_id(0)
    v = pl.program_id(1)
    shape = (GRID_W, LANES)
    w = lax.broadcasted_iota(I32, shape, 0)
    lane = lax.broadcasted_iota(I32, shape, 1)
    upper = lane >= GRID_W
    kc = jnp.where(upper, lane - GRID_W, lane)
    cdiff = kc - w + (NA_KW - 1)
    cstart = jnp.clip(w - NA_KW // 2, 0, GRID_W - NA_KW)
    inwin = (kc >= cstart) & (kc < cstart + NA_KW)
    n_ro = 2 * NA_KH - 1
    n_co = 2 * NA_KW - 1
    for j in range(NA_KH // 2):
        ro = 2 * j - v + (NA_KH - 1)
        base0 = (h * n_ro + ro) * n_co
        acc = jnp.zeros(shape, F32)
        for c in range(n_co):
            val = jnp.where(upper, rpb_ref[base0 + n_co + c], rpb_ref[base0 + c])
            acc = jnp.where(cdiff == c, val, acc)
        o_ref[:, j * LANES:(j + 1) * LANES] = jnp.where(inwin, acc, NEG_BIG)


def _na_bias(rpb):
    n = NA_KH * GRID_W
    return pl.pallas_call(
        _na_bias_body,
        grid=(NA_HEADS, NA_KH),
        in_specs=[pl.BlockSpec(memory_space=pltpu.SMEM)],
        out_specs=pl.BlockSpec((None, None, GRID_W, n), lambda h, v: (h, v, 0, 0)),
        out_shape=jax.ShapeDtypeStruct((NA_HEADS, NA_KH, GRID_W, n), F32),
        compiler_params=_params("parallel", "parallel"),
        name="na_bias",
    )(rpb.reshape(-1))


def _na_body(q_ref, k_ref, v_ref, ck_ref, cv_ref, bias_ref, o_ref, s_buf, e_buf, d_buf, *, rows):
    ck = ck_ref[...].astype(BF16)
    cv = cv_ref[...].astype(BF16)
    lo, hi = _pair_masks((GRID_W, LANES))
    scale = NA_DIM ** -0.5
    n_loc = NA_KH * GRID_W

    def window(r):
        r = jnp.clip(r, 0, rows - 1)
        r_start = jnp.clip(r - NA_KH // 2, 0, rows - NA_KH)
        return (r - r_start, pl.ds(pl.multiple_of(r * GRID_W, GRID_W), GRID_W),
                pl.ds(pl.multiple_of(r_start * GRID_W, GRID_W), n_loc))

    def scores(r, slot):
        vidx, qsl, ksl = window(r)
        q2 = q_ref[qsl, :]
        zero = jnp.zeros_like(q2)
        q2 = q2 * scale
        qs = jnp.concatenate([jnp.where(lo, q2, zero), jnp.where(hi, q2, zero)], axis=0)
        bias = jnp.concatenate([bias_ref[0, vidx], bias_ref[1, vidx]], axis=0)
        s_buf[slot, :, :n_loc] = _dot_nt(qs, k_ref[ksl, :]) + bias
        s_buf[slot, :, n_loc:] = _dot_nt(qs, ck)

    def numerators(slot):
        s = s_buf[slot]
        e = jnp.exp(s - jnp.max(s, axis=-1, keepdims=True))
        d_buf[slot] = jnp.broadcast_to(jnp.sum(e, axis=-1, keepdims=True), d_buf.shape[1:])
        e_buf[slot] = e.astype(BF16)

    def output(r, slot):
        _, qsl, ksl = window(r)
        o = (jnp.dot(e_buf[slot, :, :n_loc], v_ref[ksl, :], preferred_element_type=F32)
             + jnp.dot(e_buf[slot, :, n_loc:], cv, preferred_element_type=F32)) / d_buf[slot]
        o_ref[qsl, :] = jnp.where(lo, o[:GRID_W], o[GRID_W:]).astype(o_ref.dtype)

    s_buf[...] = jnp.zeros_like(s_buf)
    e_buf[...] = jnp.zeros_like(e_buf)
    d_buf[...] = jnp.ones_like(d_buf)

    def two_steps(j, carry):
        r = 2 * j
        scores(r, 0)
        numerators(1)
        output(r - 2, 0)
        scores(r + 1, 1)
        numerators(0)
        output(r - 1, 1)
        return carry

    lax.fori_loop(0, rows // 2 + 1, two_steps, 0, unroll=3)


def _neighbourhood_attention(q, k, v, ck, cv, bias):
    b, t, _ = q.shape
    lc = ck.shape[1]
    rows = t // GRID_W
    tok = pl.BlockSpec((None, t, LANES), lambda hp, bi: (bi, 0, hp))
    ctx = pl.BlockSpec((None, lc, LANES), lambda hp, bi: (bi, 0, hp))
    return pl.pallas_call(
        functools.partial(_na_body, rows=rows),
        grid=(NA_WIDTH // LANES, b),
        in_specs=[tok, tok, tok, ctx, ctx,
                  pl.BlockSpec((2, NA_KH, GRID_W, NA_KH * GRID_W), lambda hp, bi: (hp, 0, 0, 0))],
        out_specs=tok,
        out_shape=jax.ShapeDtypeStruct((b, t, NA_WIDTH), BF16),
        scratch_shapes=[pltpu.VMEM((2, 2 * GRID_W, NA_KH * GRID_W + lc), F32),
                        pltpu.VMEM((2, 2 * GRID_W, NA_KH * GRID_W + lc), BF16),
                        pltpu.VMEM((2, 2 * GRID_W, LANES), F32)],
        compiler_params=_params("parallel", "parallel"),
        name="na_attn",
    )(q, k, v, ck, cv, bias)


def _ret_body(q_ref, k_ref, v_ref, g_ref, df_ref, db_ref, beta_ref, s0f_ref, s0b_ref,
              o_ref, sf_ref, sb_ref, kvf_s, kvb_s, *, n_chunks, heads):
    for hh in range(heads):
        slab = lambda ref: ref.at[:, hh * LANES:(hh + 1) * LANES]
        _ret_head(slab(q_ref), slab(k_ref), slab(v_ref), slab(g_ref), df_ref.at[hh], db_ref.at[hh],
                  beta_ref.at[hh], s0f_ref.at[hh], s0b_ref.at[hh], slab(o_ref), sf_ref.at[hh], sb_ref.at[hh],
                  kvf_s, kvb_s, n_chunks=n_chunks)


def _ret_head(q_ref, k_ref, v_ref, g_ref, df_ref, db_ref, beta_ref, s0f_ref, s0b_ref,
              o_ref, sf_ref, sb_ref, kvf_s, kvb_s, *, n_chunks):
    c_len = CHUNK
    lgf = -jnp.log1p(jnp.exp(-df_ref[...]))
    lgb = -jnp.log1p(jnp.exp(-db_ref[...]))
    shape = (c_len, c_len)
    i = lax.broadcasted_iota(I32, shape, 0).astype(F32)
    j = lax.broadcasted_iota(I32, shape, 1).astype(F32)
    dij = i - j
    d_comb = (jnp.where(dij >= 0, jnp.exp(jnp.maximum(dij, 0.0) * lgf), 0.0)
              + jnp.where(dij <= 0, jnp.exp(jnp.maximum(-dij, 0.0) * lgb), 0.0))
    xi_f = jnp.exp((i + 1.0) * lgf)
    xi_b = jnp.exp((c_len - i) * lgb)
    zeta_f = jnp.exp((c_len - 1.0 - i) * lgf)
    zeta_b = jnp.exp(i * lgb)
    g_f = jnp.exp(c_len * lgf)
    g_b = jnp.exp(c_len * lgb)

    def chunk(c):
        return pl.ds(pl.multiple_of(c * c_len, c_len), c_len)

    def kv_pass(c, carry):
        kc = k_ref[chunk(c), :]
        vc = v_ref[chunk(c), :].astype(F32)
        kvf_s[c] = _dot_tn(kc, (vc * zeta_f).astype(BF16))
        kvb_s[c] = _dot_tn(kc, (vc * zeta_b).astype(BF16))
        return carry

    lax.fori_loop(0, n_chunks, kv_pass, 0, unroll=RET_CHUNK_UNROLL)

    def scan_f(c, s):
        kv = kvf_s[c]
        kvf_s[c] = s
        return g_f * s + kv

    def scan_b(ci, s):
        c = n_chunks - 1 - ci
        kv = kvb_s[c]
        kvb_s[c] = s
        return g_b * s + kv

    sf_ref[...] = lax.fori_loop(0, n_chunks, scan_f, s0f_ref[...])
    sb_ref[...] = lax.fori_loop(0, n_chunks, scan_b, s0b_ref[...])

    beta = beta_ref[...]

    def out_pass(c, carry):
        qc = q_ref[chunk(c), :]
        kc = k_ref[chunk(c), :]
        vc = v_ref[chunk(c), :]
        scores = _dot_nt(qc, kc) * d_comb
        y = jnp.dot(scores.astype(BF16), vc, preferred_element_type=F32)
        qf = qc.astype(F32)
        y += jnp.dot((qf * xi_f).astype(BF16), kvf_s[c].astype(BF16), preferred_element_type=F32)
        y += jnp.dot((qf * xi_b).astype(BF16), kvb_s[c].astype(BF16), preferred_element_type=F32)
        o_ref[chunk(c), :] = (_standardize(y) * beta * _silu(g_ref[chunk(c), :])).astype(o_ref.dtype)
        return carry

    lax.fori_loop(0, n_chunks, out_pass, 0, unroll=RET_CHUNK_UNROLL)


def _retention(q, k, v, g, decay_f, decay_b, beta_ret, s0f, s0b, heads_per_step):
    b, t, _ = q.shape
    n_chunks = t // CHUNK
    hs = heads_per_step
    tok = pl.BlockSpec((None, t, hs * LANES), lambda bi, h: (bi, 0, h))
    per_head = pl.BlockSpec((hs, 1, LANES), lambda bi, h: (h, 0, 0))
    state = pl.BlockSpec((None, hs, RET_DK, RET_DK), lambda bi, h: (bi, h, 0, 0))
    lanes = lambda a: jnp.broadcast_to(a.astype(F32)[:, None, None], (RET_HEADS, 1, LANES))
    st_shape = jax.ShapeDtypeStruct((b, RET_HEADS, RET_DK, RET_DK), F32)
    return pl.pallas_call(
        functools.partial(_ret_body, n_chunks=n_chunks, heads=hs),
        grid=(b, RET_HEADS // hs),
        in_specs=[tok, tok, tok, tok, per_head, per_head, per_head, state, state],
        out_specs=[tok, state, state],
        out_shape=[jax.ShapeDtypeStruct((b, t, RET_WIDTH), BF16), st_shape, st_shape],
        scratch_shapes=[pltpu.VMEM((n_chunks, RET_DK, RET_DK), F32)] * 2,
        compiler_params=_params("parallel", "parallel"),
        name="retention",
    )(q, k, v, g, lanes(decay_f), lanes(decay_b), beta_ret.reshape(RET_HEADS, 1, LANES), s0f, s0b)


def _merge_body(x_ref, ona_ref, ret_ref, wna_ref, wret_ref, bna_ref, gate_ref, lg_ref, lb_ref,
                shf_ref, scf_ref, wh_ref, wl_ref, br_ref, carry_in_ref,
                o_ref, p_ref, gt_ref, tab_ref, cnt_ref, carry_s):
    @pl.when((pl.program_id(0) == 0) & (pl.program_id(1) == 0))
    def _():
        carry_s[...] = carry_in_ref[...]

    o = ona_ref[...].astype(F32)
    na = o * lax.rsqrt(jnp.mean(o * o, axis=-1, keepdims=True) + LN_EPS) * bna_ref[...]
    mix = (jnp.dot(na.astype(BF16), wna_ref[...], preferred_element_type=F32)
           + jnp.dot(ret_ref[...], wret_ref[...], preferred_element_type=F32))
    y = DEEPNORM_ALPHA * x_ref[...] + gate_ref[...] * mix
    y = _standardize(y) * lg_ref[...] + lb_ref[...]
    o_ref[...] = y

    h = y * (1.0 + scf_ref[...]) + shf_ref[...]
    w_hi, w_lo, bias = wh_ref[...], wl_ref[...], br_ref[...]
    carry = carry_s[...]
    for s in range(y.shape[0] // TOKEN_BLOCK):
        sl = slice(s * TOKEN_BLOCK, (s + 1) * TOKEN_BLOCK)
        rows, gates, table, carry = _route_block(h[sl], w_hi, w_lo, bias, carry)
        p_ref[:, sl] = rows
        gt_ref[:, sl] = gates
        tab_ref[s] = table
    carry_s[...] = carry
    cnt_ref[...] = carry


def _merge_and_route(x, o_na, ret, w_out, beta_na, gate, shift_f, scale_f, mod_row, ln_g, ln_b,
                     w_hi, w_lo, b_router, carry_in, tm):
    b, t, _ = x.shape
    nt = t // tm
    sub = tm // TOKEN_BLOCK
    tok = lambda bi, ti: (bi, ti, 0)
    flat = lambda bi, ti: (0, bi * nt + ti)
    mod = lambda bi, ti: (mod_row(bi), 0, 0)
    whole = lambda shape: pl.BlockSpec(shape, lambda bi, ti: (0,) * len(shape))
    half = pl.BlockSpec((None, tm, 512), tok)
    mod_spec = pl.BlockSpec((None, 1, D_MODEL), mod)
    return pl.pallas_call(
        _merge_body,
        grid=(b, nt),
        in_specs=[pl.BlockSpec((None, tm, D_MODEL), tok), half, half,
                  pl.BlockSpec((512, D_MODEL), lambda bi, ti: (0, 0)),
                  pl.BlockSpec((512, D_MODEL), lambda bi, ti: (1, 0)),
                  whole((1, 512)), mod_spec, whole((1, D_MODEL)), whole((1, D_MODEL)),
                  mod_spec, mod_spec,
                  whole((N_EXPERTS, D_MODEL)), whole((N_EXPERTS, D_MODEL)), whole((N_EXPERTS, TOKEN_BLOCK)),
                  whole((N_EXPERTS, LANES))],
        out_specs=[pl.BlockSpec((None, tm, D_MODEL), tok),
                   pl.BlockSpec((TOP_K, tm), flat), pl.BlockSpec((TOP_K, tm), flat),
                   pl.BlockSpec((sub, N_EXPERTS, LANES), lambda bi, ti: (bi * nt + ti, 0, 0)),
                   whole((N_EXPERTS, LANES))],
        out_shape=[jax.ShapeDtypeStruct(x.shape, F32),
                   jax.ShapeDtypeStruct((TOP_K, b * t), I32), jax.ShapeDtypeStruct((TOP_K, b * t), F32),
                   jax.ShapeDtypeStruct((b * t // TOKEN_BLOCK, N_EXPERTS, LANES), F32),
                   jax.ShapeDtypeStruct((N_EXPERTS, LANES), F32)],
        scratch_shapes=[pltpu.VMEM((N_EXPERTS, LANES), F32)],
        compiler_params=_params("arbitrary", "arbitrary"),
        name="merge_ln1_route",
    )(x, o_na, ret, w_out, w_out, beta_na.reshape(1, 512), gate, ln_g.reshape(1, -1), ln_b.reshape(1, -1),
      shift_f, scale_f, w_hi, w_lo, jnp.broadcast_to(b_router[:, None], (N_EXPERTS, TOKEN_BLOCK)), carry_in)


def _two_group_specs(n_p_blocks, blocks_per_batch_s, ctx_row):
    tok_p = lambda i: (jnp.minimum(i, n_p_blocks - 1), 0)
    tok_s = lambda i: (jnp.maximum(i - n_p_blocks, 0), 0)
    mod = lambda i: (jnp.where(i < n_p_blocks, ctx_row, jnp.maximum(i - n_p_blocks, 0) // blocks_per_batch_s), 0, 0)
    return tok_p, tok_s, mod


def _route_block(h, w_hi, w_lo, bias, carry):
    tm = h.shape[0]
    h_hi = h.astype(BF16)
    h_lo = (h - h_hi.astype(F32)).astype(BF16)
    lg = _dot_nt(jnp.concatenate([w_hi, w_lo], axis=0), h_hi)
    work = lg[:N_EXPERTS] + lg[N_EXPERTS:] + _dot_nt(w_hi, h_lo) + bias
    eidx = lax.broadcasted_iota(I32, (N_EXPERTS, tm), 0).astype(F32)
    vals, idxs, hots = [], [], []
    for _ in range(TOP_K):
        mx = jnp.max(work, axis=0, keepdims=True)
        idx = jnp.min(jnp.where(work == mx, eidx, float(N_EXPERTS)), axis=0, keepdims=True)
        hot = eidx == idx
        vals.append(mx)
        idxs.append(idx)
        hots.append(hot)
        work = jnp.where(hot, -jnp.inf, work)
    exps = [jnp.exp(v - vals[0]) for v in vals]
    den = exps[0] + exps[1] + exps[2] + exps[3]
    sel = jnp.zeros((N_EXPERTS, tm), F32)
    for hot in hots:
        sel = sel + hot.astype(F32)
    r_i = lax.broadcasted_iota(I32, (tm, tm), 0)
    c_i = lax.broadcasted_iota(I32, (tm, tm), 1)
    earlier = (r_i < c_i).astype(BF16)
    before = jnp.dot(sel.astype(BF16), earlier, preferred_element_type=F32)
    n_e = jnp.sum(sel, axis=1, keepdims=True)
    n8 = jnp.floor((n_e + (RUN_ALIGN - 1.0)) * (1.0 / RUN_ALIGN)) * RUN_ALIGN + jnp.zeros((1, LANES), F32)
    e_r = lax.broadcasted_iota(I32, (N_EXPERTS, N_EXPERTS), 0)
    e_c = lax.broadcasted_iota(I32, (N_EXPERTS, N_EXPERTS), 1)
    off = jnp.dot((e_c < e_r).astype(BF16), n8.astype(BF16), preferred_element_type=F32)
    local = before + jnp.concatenate([off] * (tm // LANES), axis=1)
    rows = [jnp.sum(jnp.where(hot, local, 0.0), axis=0, keepdims=True) for hot in hots]
    lane = lax.broadcasted_iota(I32, (N_EXPERTS, LANES), 1)
    table = jnp.where(lane == 0, n8, jnp.where(lane == 1, carry, jnp.where(lane == 2, off, 0.0)))
    return (jnp.concatenate(rows, axis=0).astype(I32), jnp.concatenate([e / den for e in exps], axis=0),
            table, carry + n8)


def _for_pad_pieces(n_rows, visit):
    off = 0
    for size in PAD_PIECES:
        present = (n_rows & size) != 0
        visit(off, size, present)
        off = off + jnp.where(present, size, 0)


def _start_unit_copies(unit_ref, local_buf, sorted_hbm, sem, to_hbm):
    def one(u, carry):
        loc = local_buf.at[pl.ds(pl.multiple_of(u * RUN_ALIGN, RUN_ALIGN), RUN_ALIGN)]
        glob = sorted_hbm.at[pl.ds(pl.multiple_of(unit_ref[u], RUN_ALIGN), RUN_ALIGN)]
        copy = pltpu.make_async_copy(loc, glob, sem) if to_hbm else pltpu.make_async_copy(glob, loc, sem)
        copy.start()
        return carry

    lax.fori_loop(0, LOCAL_UNITS, one, 0, unroll=8)


def _wait_unit_copies(local_buf, sorted_hbm, sem):
    whole = sorted_hbm.at[pl.ds(0, LOCAL_ROWS)]
    pltpu.make_async_copy(whole, local_buf, sem).wait()


def _dispatch_body(unit_ref, zlo_ref, zn_ref, p_ref, yp_ref, ys_ref, sh_ref, sc_ref, xs_ref,
                   xbuf, zbuf, sems, zsem, *, n_p_blocks, n_blocks):
    i = pl.program_id(0)
    slot = i % 2
    y = jnp.where(i < n_p_blocks, yp_ref[...], ys_ref[...])
    h = (y * (1.0 + sc_ref[...]) + sh_ref[...]).astype(BF16)
    rows = p_ref[...]
    r_iota = lax.broadcasted_iota(I32, (LOCAL_ROWS, rows.shape[1]), 0)
    place = r_iota == rows[0:1]
    for kk in range(1, TOP_K):
        place = place | (r_iota == rows[kk:kk + 1])
    xbuf[slot] = _pack_bf16_pairs(
        jnp.dot(jnp.where(place, 1.0, 0.0).astype(BF16), h, preferred_element_type=F32))

    _start_unit_copies(unit_ref, xbuf.at[slot], xs_ref, sems.at[slot], to_hbm=True)

    @pl.when(i > 0)
    def _():
        _wait_unit_copies(xbuf.at[1 - slot], xs_ref, sems.at[1 - slot])

    @pl.when(i == n_blocks - 1)
    def _():
        _wait_unit_copies(xbuf.at[slot], xs_ref, sems.at[slot])
        zbuf[...] = jnp.zeros_like(zbuf)
        sem = zsem

        def per_expert(e, carry):
            lo = zlo_ref[e]

            def visit(off, size, present):
                copy = pltpu.make_async_copy(
                    zbuf.at[pl.ds(0, size)], xs_ref.at[pl.ds(pl.multiple_of(lo + off, RUN_ALIGN), size)], sem)

                @pl.when(present)
                def _():
                    copy.start()
                    copy.wait()

            _for_pad_pieces(zn_ref[e], visit)
            return carry

        lax.fori_loop(0, N_EXPERTS, per_expert, 0)

        tail_lo = zlo_ref[N_EXPERTS]

        def tile_copy(t):
            start = pl.multiple_of(tail_lo + t * EXPERT_TILE, EXPERT_TILE)
            return pltpu.make_async_copy(zbuf, xs_ref.at[pl.ds(start, EXPERT_TILE)], sem)

        def t_issue(t, c2):
            tile_copy(t).start()
            return c2

        def t_drain(t, c2):
            tile_copy(0).wait()
            return c2

        n_tail = zn_ref[N_EXPERTS] // EXPERT_TILE
        lax.fori_loop(0, n_tail, t_issue, 0)
        lax.fori_loop(0, n_tail, t_drain, 0)


def _dispatch(yp, ys, shift, scale, ctx_row, s_blocks_per_batch, local_rows, unit_dst, zero_lo, zero_n,
              total_rows, tm):
    n_p, n_s = yp.shape[0], ys.shape[0]
    n = n_p + n_s
    npb = n_p // tm
    nb = n // tm
    tok_p, tok_s, mod = _two_group_specs(npb, s_blocks_per_batch, ctx_row)
    smem_all = pl.BlockSpec(memory_space=pltpu.SMEM)
    return pl.pallas_call(
        functools.partial(_dispatch_body, n_p_blocks=npb, n_blocks=nb),
        grid=(nb,),
        in_specs=[pl.BlockSpec((UNIT_TABLE,), lambda i: (i,), memory_space=pltpu.SMEM), smem_all, smem_all,
                  pl.BlockSpec((TOP_K, tm), lambda i: (0, i)),
                  pl.BlockSpec((tm, D_MODEL), tok_p), pl.BlockSpec((tm, D_MODEL), tok_s),
                  pl.BlockSpec((None, 1, D_MODEL), mod), pl.BlockSpec((None, 1, D_MODEL), mod)],
        out_specs=pl.BlockSpec(memory_space=pl.ANY),
        out_shape=jax.ShapeDtypeStruct((total_rows, D_MODEL // 2), U32),
        scratch_shapes=[pltpu.VMEM((2, LOCAL_ROWS, D_MODEL // 2), U32),
                        pltpu.VMEM((EXPERT_TILE, D_MODEL // 2), U32),
                        pltpu.SemaphoreType.DMA((2,)), pltpu.SemaphoreType.DMA(())],
        compiler_params=_params("arbitrary"),
        name="moe_dispatch",
    )(unit_dst, zero_lo, zero_n, local_rows, yp, ys, shift, scale)


def _expert_body(te_ref, first_ref, half_ref, nu_ref, x_ref, wgu_ref, bgu_ref, wd_ref, bd_ref, y_ref,
                 act_s, wgu_s, wd_s):
    j = pl.program_id(0)

    @pl.when(first_ref[j] == 1)
    def _():
        blk = 2 * LANES
        src = lax.broadcasted_iota(I32, (blk, blk), 0)
        dst = lax.broadcasted_iota(I32, (blk, blk), 1)
        perm = (src == jnp.where(dst < LANES, 2 * dst, 2 * (dst - LANES) + 1)).astype(F32).astype(BF16)
        for s in range(2 * D_FF // blk):
            sl = slice(s * blk, (s + 1) * blk)
            wgu_s[:, sl] = jnp.dot(wgu_ref[:, sl].astype(BF16), perm, preferred_element_type=F32).astype(BF16)
        wd_s[...] = wd_ref[...].astype(BF16)

    def ffn(n_rows):
        rows = slice(0, n_rows)
        x = _unpack_bf16_pairs(x_ref[rows, :])
        for c in range(D_FF // LANES):
            sl = slice(c * 2 * LANES, (c + 1) * 2 * LANES)
            hu = jnp.dot(x, wgu_s[:, sl], preferred_element_type=F32) + bgu_ref[:, sl]
            x_glu = jnp.minimum(hu[:, :LANES], SWIGLU_LIMIT)
            x_lin = jnp.clip(hu[:, LANES:], -SWIGLU_LIMIT, SWIGLU_LIMIT)
            act = x_glu * (1.0 / (1.0 + jnp.exp(-SWIGLU_ALPHA * x_glu))) * (x_lin + 1.0)
            act_s[rows, c * LANES:(c + 1) * LANES] = act.astype(BF16)
        y_ref[rows, :] = _pack_bf16_pairs(
            jnp.dot(act_s[rows, :], wd_s[...], preferred_element_type=F32) + bd_ref[...])

    used = j < nu_ref[0]
    half = half_ref[j] == 1

    @pl.when(used & jnp.logical_not(half))
    def _():
        ffn(EXPERT_TILE)

    @pl.when(used & half)
    def _():
        ffn(EXPERT_TILE // 2)
        y_ref[EXPERT_TILE // 2:, :] = jnp.zeros((EXPERT_TILE // 2, y_ref.shape[1]), y_ref.dtype)

    @pl.when(jnp.logical_not(used))
    def _():
        y_ref[...] = jnp.zeros_like(y_ref)


def _experts(tile_expert, tile_first, tile_half, n_used, xs, cap_rows, w_gate_up, bgu, w_down, bd):
    n_tiles = cap_rows // EXPERT_TILE
    of_expert = lambda j, te, first, half, nu: (te[j], 0, 0)
    grid_spec = pltpu.PrefetchScalarGridSpec(
        num_scalar_prefetch=4,
        grid=(n_tiles,),
        in_specs=[pl.BlockSpec((EXPERT_TILE, D_MODEL // 2),
                               lambda j, te, first, half, nu: (jnp.minimum(j, nu[0] - 1), 0)),
                  pl.BlockSpec((None, D_MODEL, 2 * D_FF), of_expert),
                  pl.BlockSpec((None, 1, 2 * D_FF), of_expert),
                  pl.BlockSpec((None, D_FF, D_MODEL), of_expert),
                  pl.BlockSpec((None, 1, D_MODEL), of_expert)],
        out_specs=pl.BlockSpec((EXPERT_TILE, D_MODEL // 2), lambda j, te, first, half, nu: (j, 0)),
        scratch_shapes=[pltpu.VMEM((EXPERT_TILE, D_FF), BF16),
                        pltpu.VMEM((D_MODEL, 2 * D_FF), BF16), pltpu.VMEM((D_FF, D_MODEL), BF16)],
    )
    return pl.pallas_call(
        _expert_body,
        grid_spec=grid_spec,
        out_shape=jax.ShapeDtypeStruct((cap_rows, D_MODEL // 2), U32),
        compiler_params=_params("arbitrary"),
        name="moe_experts",
    )(tile_expert, tile_first, tile_half, n_used, xs, w_gate_up, bgu, w_down, bd)


def _combine_body(unit_ref, next_unit_ref, y1_ref, p_ref, gt_ref, gate_ref, lg_ref, lb_ref, ys_ref, o_ref,
                  ybuf, sems, *, n_blocks):
    i = pl.program_id(0)
    slot = i % 2
    tm = y1_ref.shape[0]

    @pl.when(i == 0)
    def _():
        _start_unit_copies(unit_ref, ybuf.at[0], ys_ref, sems.at[0], to_hbm=False)

    @pl.when(i + 1 < n_blocks)
    def _():
        _start_unit_copies(next_unit_ref, ybuf.at[1 - slot], ys_ref, sems.at[1 - slot], to_hbm=False)

    _wait_unit_copies(ybuf.at[slot], ys_ref, sems.at[slot])

    rows = p_ref[...]
    gates = gt_ref[...]
    c_iota = lax.broadcasted_iota(I32, (tm, LOCAL_ROWS), 1)
    weight = jnp.where(c_iota == rows[:, 0:1], gates[:, 0:1], 0.0)
    for kk in range(1, TOP_K):
        weight = weight + jnp.where(c_iota == rows[:, kk:kk + 1], gates[:, kk:kk + 1], 0.0)
    w_hi = weight.astype(BF16)
    w_lo = (weight - w_hi.astype(F32)).astype(BF16)
    yb = _unpack_bf16_pairs(ybuf[slot])
    f = (jnp.dot(w_hi, yb, preferred_element_type=F32) + jnp.dot(w_lo, yb, preferred_element_type=F32))
    y = DEEPNORM_ALPHA * y1_ref[...] + gate_ref[...] * f
    o_ref[...] = _standardize(y) * lg_ref[...] + lb_ref[...]


def _combine(y1, local_rows, gates, unit_src, block0, gate_mod, mod_row, blocks_per_batch,
             y_sorted, ln_g, ln_b, tm):
    n_g = y1.shape[0]
    nb = n_g // tm
    row = pl.BlockSpec((1, D_MODEL), lambda i: (0, 0))
    pair = pl.BlockSpec((tm, TOP_K), lambda i: (i + block0, 0))
    units = lambda ahead: pl.BlockSpec(
        (UNIT_TABLE,), lambda i: (jnp.minimum(i + ahead, nb - 1) + block0,), memory_space=pltpu.SMEM)
    return pl.pallas_call(
        functools.partial(_combine_body, n_blocks=nb),
        grid=(nb,),
        in_specs=[units(0), units(1),
                  pl.BlockSpec((tm, D_MODEL), lambda i: (i, 0)), pair, pair,
                  pl.BlockSpec((None, 1, D_MODEL), lambda i: (mod_row(i // blocks_per_batch), 0, 0)),
                  row, row,
                  pl.BlockSpec(memory_space=pl.ANY)],
        out_specs=pl.BlockSpec((tm, D_MODEL), lambda i: (i, 0)),
        out_shape=jax.ShapeDtypeStruct((n_g, D_MODEL), F32),
        scratch_shapes=[pltpu.VMEM((2, LOCAL_ROWS, D_MODEL // 2), U32), pltpu.SemaphoreType.DMA((2,))],
        compiler_params=_params("arbitrary"),
        name="moe_combine_ln2",
    )(unit_src, unit_src, y1, local_rows, gates, gate_mod,
      ln_g.reshape(1, -1), ln_b.reshape(1, -1), y_sorted)


def _regroup_gate_up_bias(b_gate_up):
    e = b_gate_up.shape[0]
    nblk = D_FF // LANES
    return b_gate_up.reshape(e, 1, nblk, LANES, 2).transpose(0, 1, 2, 4, 3).reshape(e, 1, 2 * D_FF)


def _routing_plan(table, totals, n_pairs):
    nb = table.shape[0]
    run_n = table[:, :, 0].astype(I32)
    rows_before = table[:, :, 1].astype(I32)
    run_local = table[:, :, 2].astype(I32)
    cnt = totals[:, 0].astype(I32)
    padded = (cnt + EXPERT_TILE - 1) // EXPERT_TILE * EXPERT_TILE
    pad_end = jnp.cumsum(padded)
    pad_start = pad_end - padded
    run_global = pad_start[None, :] + rows_before
    rows_max = n_pairs + nb * N_EXPERTS * (RUN_ALIGN - 1) + N_EXPERTS * (EXPERT_TILE - 1)
    n_tiles = -(-rows_max // EXPERT_TILE)
    n_used = pad_end[-1] // EXPERT_TILE
    tile_row = jnp.minimum(jnp.arange(n_tiles, dtype=I32), n_used - 1) * EXPERT_TILE
    tile_expert = jnp.sum((pad_end[None, :] <= tile_row[:, None]).astype(I32), axis=1)
    tile_expert = jnp.minimum(tile_expert, N_EXPERTS - 1)
    tile_id = jnp.arange(n_tiles, dtype=I32)
    changed = jnp.concatenate([jnp.ones((1,), bool), tile_expert[1:] != tile_expert[:-1]])
    tile_first = (changed & (tile_id < n_used)).astype(I32)
    exp_hot = tile_expert[:, None] == jnp.arange(N_EXPERTS, dtype=I32)
    rows_end = jnp.sum(jnp.where(exp_hot, (pad_start + cnt)[None, :], 0), axis=-1)
    tile_half = ((rows_end - tile_row) <= EXPERT_TILE // 2).astype(I32)
    cap_rows = n_tiles * EXPERT_TILE
    spare_rows = -(-LOCAL_ROWS // EXPERT_TILE) * EXPERT_TILE
    total_rows = cap_rows + 2 * spare_rows
    unit_row = jnp.arange(LOCAL_UNITS, dtype=I32) * RUN_ALIGN
    run_end = run_local + run_n
    owner = jnp.sum((run_end[:, None, :] <= unit_row[None, :, None]).astype(I32), axis=-1)
    own_hot = owner[:, :, None] == jnp.arange(N_EXPERTS, dtype=I32)
    shift = jnp.sum(jnp.where(own_hot, (run_global - run_local)[:, None, :], 0), axis=-1)
    used = owner < N_EXPERTS
    sorted_row = unit_row[None, :] + shift
    slot = (jnp.arange(nb, dtype=I32) % 2)[:, None]
    unit_dst = jnp.where(used, sorted_row, cap_rows + slot * spare_rows + unit_row[None, :])
    unit_src = jnp.where(used, sorted_row, unit_row[None, :])
    table_of = lambda a: jnp.pad(a, ((0, 0), (0, UNIT_TABLE - LOCAL_UNITS))).reshape(-1)
    zero_lo = jnp.concatenate([pad_start + cnt, pad_end[-1:]])
    zero_n = jnp.concatenate([padded - cnt, total_rows - pad_end[-1:]])
    return ((table_of(unit_dst), table_of(unit_src)), (tile_expert, tile_first, tile_half, n_used.reshape(1)),
            zero_lo, zero_n, cap_rows, total_rows)


def kernel(x_prompt, x_sample, cache_na_k, cache_na_v, state_ret_fwd, state_ret_bwd, c, c_ctx, w_ada, b_ada, w_in, rpb, ret_decay_fwd, ret_decay_bwd, beta_na, beta_ret, w_out, ln1_g, ln1_b, w_router, b_router, w_gate_up, b_gate_up, w_down, b_down, ln2_g, ln2_b):
    bp, tp, _ = x_prompt.shape
    bs, ts, _ = x_sample.shape
    assert w_ada.shape[0] == DEPTH == 1
    l = 0
    ctx_row = bs
    sample_row = lambda bi: bi
    prompt_row = lambda bi: ctx_row

    cond = jnp.concatenate([c, c_ctx[None, :], jnp.zeros((16 - bs - 1, D_MODEL), F32)], axis=0)
    m = _adaln(cond, w_ada[l], b_ada[l]).reshape(16, 6, 1, D_MODEL)
    sh_a, sc_a, gt_a, sh_f, sc_f, gt_f = (m[:, i] for i in range(6))

    w_in_b = w_in[l].astype(BF16)
    w_out_b = w_out[l].astype(BF16)

    q, k_p, v_p, qr, kr, vr, g = _project(x_prompt, sh_a, sc_a, prompt_row, w_in_b, tp, F32, rope=False)
    o_na = _context_attention(q, k_p, v_p)
    zeros_state = jnp.zeros((bp, RET_HEADS, RET_DK, RET_DK), F32)
    ret, s_f, s_b = _retention(qr, kr, vr, g, ret_decay_fwd[l], ret_decay_bwd[l], beta_ret[l],
                               zeros_state, zeros_state, heads_per_step=RET_HEADS)
    w_r = w_router[l].T
    w_r_hi = w_r.astype(BF16)
    w_r_lo = (w_r - w_r_hi.astype(F32)).astype(BF16)
    route = functools.partial(_merge_and_route, w_out=w_out_b, beta_na=beta_na[l], gate=gt_a, shift_f=sh_f,
                              scale_f=sc_f, ln_g=ln1_g[l], ln_b=ln1_b[l], w_hi=w_r_hi, w_lo=w_r_lo,
                              b_router=b_router[l])
    yp1, rows_p, gates_p, table_p, totals_p = route(
        x_prompt, o_na, ret, mod_row=prompt_row, carry_in=jnp.zeros((N_EXPERTS, LANES), F32), tm=tp)

    q, k_s, v_s, qr, kr, vr, g = _project(x_sample, sh_a, sc_a, sample_row, w_in_b, 512, BF16, rope=True)
    lc = cache_na_k.shape[2]
    o_na = _neighbourhood_attention(q, k_s, v_s, cache_na_k[:, l].reshape(bs, lc, NA_WIDTH),
                                    cache_na_v[:, l].reshape(bs, lc, NA_WIDTH), _na_bias(rpb[l]))
    ret, _, _ = _retention(qr, kr, vr, g, ret_decay_fwd[l], ret_decay_bwd[l], beta_ret[l],
                           state_ret_fwd[:, l], state_ret_bwd[:, l], heads_per_step=1)
    ys1, rows_s, gates_s, table_s, totals = route(
        x_sample, o_na, ret, mod_row=sample_row, carry_in=totals_p, tm=512)

    tm = TOKEN_BLOCK
    yp1f = yp1.reshape(bp * tp, D_MODEL)
    ys1f = ys1.reshape(bs * ts, D_MODEL)
    local_rows = jnp.concatenate([rows_p, rows_s], axis=1)
    gates = jnp.concatenate([gates_p, gates_s], axis=1)
    table = jnp.concatenate([table_p, table_s], axis=0)
    (unit_dst, unit_src), tiles, zero_lo, zero_n, cap_rows, total_rows = _routing_plan(
        table, totals, (bp * tp + bs * ts) * TOP_K)
    xs = _dispatch(yp1f, ys1f, sh_f, sc_f, ctx_row, ts // tm, local_rows, unit_dst, zero_lo, zero_n,
                   total_rows, tm)
    y_sorted = _experts(*tiles, xs, cap_rows, w_gate_up[l], _regroup_gate_up_bias(b_gate_up[l]), w_down[l],
                        b_down[l][:, None, :])
    rows_t, gates_t = local_rows.T, gates.T
    yp = _combine(yp1f, rows_t, gates_t, unit_src, 0, gt_f, prompt_row, 1, y_sorted, ln2_g[l], ln2_b[l], tm)
    ys = _combine(ys1f, rows_t, gates_t, unit_src, (bp * tp) // tm, gt_f, sample_row, ts // tm, y_sorted,
                  ln2_g[l], ln2_b[l], tm)

    return (yp.reshape(bp, tp, D_MODEL), ys.reshape(bs, ts, D_MODEL),
            k_p.reshape(bp, 1, tp, NA_HEADS, NA_DIM), v_p.reshape(bp, 1, tp, NA_HEADS, NA_DIM),
            s_f[:, None], s_b[:, None])
```

```python
import functools

import jax
import jax.numpy as jnp
from jax import lax
from jax.experimental import pallas as pl
from jax.experimental.pallas import tpu as pltpu

F32 = jnp.float32
BF16 = jnp.bfloat16
I32 = jnp.int32
U32 = jnp.uint32

D_MODEL = 1024
GRID_W = 64
NA_HEADS = 8
NA_DIM = 64
NA_WIDTH = NA_HEADS * NA_DIM
NA_KH = 8
NA_KW = 16
RET_HEADS = 4
RET_DK = 128
RET_WIDTH = RET_HEADS * RET_DK
CHUNK = 128
N_EXPERTS = 32
TOP_K = 4
D_FF = 1024
SWIGLU_LIMIT = 7.0
SWIGLU_ALPHA = 1.702
ROPE_BASE = 10000.0
LN_EPS = 1e-5
DEPTH = 1
DEEPNORM_ALPHA = (2.0 * DEPTH) ** 0.25

LANES = 128
NEG_BIG = -1e30
TOKEN_BLOCK = 256
EXPERT_TILE = 1024
VMEM_LIMIT = 56 * 1024 * 1024
RET_CHUNK_UNROLL = 16
SUBLANES = 8
RUN_ALIGN = SUBLANES
LOCAL_ROWS = TOKEN_BLOCK * TOP_K + N_EXPERTS * RUN_ALIGN
LOCAL_UNITS = LOCAL_ROWS // RUN_ALIGN
UNIT_TABLE = 256
PAD_PIECES = tuple(EXPERT_TILE >> s for s in range(1, EXPERT_TILE.bit_length())
                   if (EXPERT_TILE >> s) >= RUN_ALIGN)


def _params(*sem):
    return pltpu.CompilerParams(dimension_semantics=sem, vmem_limit_bytes=VMEM_LIMIT)


def _silu(x):
    return x / (1.0 + jnp.exp(-x))


def _standardize(x):
    mu = jnp.mean(x, axis=-1, keepdims=True)
    xc = x - mu
    var = jnp.mean(xc * xc, axis=-1, keepdims=True)
    return xc * lax.rsqrt(var + LN_EPS)


def _pack_bf16_pairs(x, is_bf16_exact=False):
    if not is_bf16_exact:
        x = x.astype(BF16).astype(F32)
    bits = lax.bitcast_convert_type(x, U32)
    half = x.shape[1] // 2
    return bits[:, :half] | lax.shift_right_logical(bits[:, half:], jnp.uint32(16))


def _unpack_bf16_pairs(p):
    hi = lax.bitcast_convert_type(p & jnp.uint32(0xFFFF0000), F32)
    lo = lax.bitcast_convert_type(lax.shift_left(p, jnp.uint32(16)), F32)
    return jnp.concatenate([hi, lo], axis=1).astype(BF16)


def _dot_nt(a, b):
    return lax.dot_general(a, b, (((1,), (1,)), ((), ())), preferred_element_type=F32)


def _dot_tn(a, b):
    return lax.dot_general(a, b, (((0,), (0,)), ((), ())), preferred_element_type=F32)


def _ada_body(c_ref, w_ref, b_ref, o_ref):
    s = _silu(c_ref[...])
    o_ref[...] = jnp.dot(s, w_ref[...], preferred_element_type=F32,
                         precision=lax.Precision.HIGHEST) + b_ref[...]


def _adaln(cond, w_ada, b_ada):
    r = cond.shape[0]
    n = w_ada.shape[1]
    tn = 1536
    return pl.pallas_call(
        _ada_body,
        grid=(n // tn,),
        in_specs=[pl.BlockSpec((r, D_MODEL), lambda j: (0, 0)),
                  pl.BlockSpec((D_MODEL, tn), lambda j: (0, j)),
                  pl.BlockSpec((1, tn), lambda j: (0, j))],
        out_specs=pl.BlockSpec((r, tn), lambda j: (0, j)),
        out_shape=jax.ShapeDtypeStruct((r, n), F32),
        compiler_params=_params("arbitrary"),
        name="adaln",
    )(cond, w_ada, b_ada.reshape(1, n))


def _proj_body(*refs, rope):
    if rope:
        (x_ref, sh_ref, sc_ref, w_ref, cos_ref, sa_ref, sb_ref,
         q_ref, k_ref, v_ref, qr_ref, kr_ref, vr_ref, g_ref) = refs
    else:
        (x_ref, sh_ref, sc_ref, w_ref,
         q_ref, k_ref, v_ref, qr_ref, kr_ref, vr_ref, g_ref) = refs
    h = (x_ref[...] * (1.0 + sc_ref[...]) + sh_ref[...]).astype(BF16)

    def cols(c):
        return jnp.dot(h, w_ref[:, c * 512:(c + 1) * 512], preferred_element_type=F32)

    q_ref[...] = cols(0).astype(q_ref.dtype)
    k_ref[...] = cols(1).astype(k_ref.dtype)
    v_ref[...] = cols(2).astype(v_ref.dtype)
    pq = cols(3)
    pk = cols(4) * (RET_DK ** -0.5)
    if rope:
        cs, sa, sb = cos_ref[...], sa_ref[...], sb_ref[...]
        for hd in range(RET_HEADS):
            sl = slice(hd * LANES, (hd + 1) * LANES)
            for p, o_ref in ((pq, qr_ref), (pk, kr_ref)):
                xs = p[:, sl]
                rot = xs * cs + pltpu.roll(xs, 96, 1) * sa + pltpu.roll(xs, 32, 1) * sb
                o_ref[:, sl] = rot.astype(o_ref.dtype)
    else:
        qr_ref[...] = pq.astype(qr_ref.dtype)
        kr_ref[...] = pk.astype(kr_ref.dtype)
    vr_ref[...] = cols(5).astype(vr_ref.dtype)
    g_ref[...] = cols(6)


def _rope_tables(t_len):
    t = jnp.arange(t_len, dtype=jnp.int32)
    pos_row = (t // GRID_W).astype(F32)[:, None]
    pos_col = (t % GRID_W).astype(F32)[:, None]
    lane = jnp.arange(LANES, dtype=jnp.int32)[None, :]
    n_freq = RET_DK // 4
    inv_freq = ROPE_BASE ** (-(lane % n_freq).astype(F32) / n_freq)
    ang = jnp.where(lane < RET_DK // 2, pos_row, pos_col) * inv_freq
    first = (lane % (2 * n_freq)) < n_freq
    cos = jnp.cos(ang)
    sin = jnp.sin(ang)
    return cos, jnp.where(first, -sin, 0.0), jnp.where(first, 0.0, sin)


def _project(x, shift, scale, mod_row, w_in, tm, kv_dtype, rope):
    b, t, _ = x.shape
    tok = lambda bi, ti: (bi, ti, 0)
    mod = lambda bi, ti: (mod_row(bi), 0, 0)
    in_specs = [pl.BlockSpec((None, tm, D_MODEL), tok),
                pl.BlockSpec((None, 1, D_MODEL), mod),
                pl.BlockSpec((None, 1, D_MODEL), mod),
                pl.BlockSpec(w_in.shape, lambda bi, ti: (0, 0))]
    args = [x, shift, scale, w_in]
    if rope:
        tab = lambda bi, ti: (ti, 0)
        in_specs += [pl.BlockSpec((tm, LANES), tab)] * 3
        args += list(_rope_tables(t))
    o512 = pl.BlockSpec((None, tm, 512), tok)
    shp = lambda dt: jax.ShapeDtypeStruct((b, t, 512), dt)
    return pl.pallas_call(
        functools.partial(_proj_body, rope=rope),
        grid=(b, t // tm),
        in_specs=in_specs,
        out_specs=[o512] * 7,
        out_shape=[shp(BF16), shp(kv_dtype), shp(kv_dtype), shp(BF16), shp(BF16), shp(BF16), shp(F32)],
        compiler_params=_params("parallel", "parallel"),
        name="in_proj",
    )(*args)


def _pair_masks(shape):
    lane = lax.broadcasted_iota(I32, shape, 1)
    return lane < NA_DIM, lane >= NA_DIM


def _ctx_attn_body(q_ref, k_ref, v_ref, o_ref):
    t = q_ref.shape[0]
    lo, hi = _pair_masks((t, LANES))
    for hp in range(NA_WIDTH // LANES):
        sl = slice(hp * LANES, (hp + 1) * LANES)
        q2 = q_ref[:, sl] * (NA_DIM ** -0.5)
        k2 = k_ref[:, sl].astype(BF16)
        v2 = v_ref[:, sl].astype(BF16)
        zero = jnp.zeros_like(q2)
        qs = jnp.concatenate([jnp.where(lo, q2, zero), jnp.where(hi, q2, zero)], axis=0)
        s = _dot_nt(qs, k2)
        e = jnp.exp(s - jnp.max(s, axis=-1, keepdims=True))
        den = jnp.sum(e, axis=-1, keepdims=True)
        o = jnp.dot(e.astype(BF16), v2, preferred_element_type=F32) / den
        o_ref[:, sl] = jnp.where(lo, o[:t], o[t:]).astype(o_ref.dtype)


def _context_attention(q, k, v):
    b, t, _ = q.shape
    spec = pl.BlockSpec((None, t, NA_WIDTH), lambda bi: (bi, 0, 0))
    return pl.pallas_call(
        _ctx_attn_body,
        grid=(b,),
        in_specs=[spec] * 3,
        out_specs=spec,
        out_shape=jax.ShapeDtypeStruct((b, t, NA_WIDTH), BF16),
        compiler_params=_params("parallel"),
        name="ctx_attn",
    )(q, k, v)


def _na_bias_body(rpb_ref, o_ref):
    h = pl.program_id(0)
    v = pl.program_id(1)
    shape = (GRID_W, LANES)
    w = lax.broadcasted_iota(I32, shape, 0)
    lane = lax.broadcasted_iota(I32, shape, 1)
    upper = lane >= GRID_W
    kc = jnp.where(upper, lane - GRID_W, lane)
    cdiff = kc - w + (NA_KW - 1)
    cstart = jnp.clip(w - NA_KW // 2, 0, GRID_W - NA_KW)
    inwin = (kc >= cstart) & (kc < cstart + NA_KW)
    n_ro = 2 * NA_KH - 1
    n_co = 2 * NA_KW - 1
    for j in range(NA_KH // 2):
        ro = 2 * j - v + (NA_KH - 1)
        base0 = (h * n_ro + ro) * n_co
        acc = jnp.zeros(shape, F32)
        for c in range(n_co):
            val = jnp.where(upper, rpb_ref[base0 + n_co + c], rpb_ref[base0 + c])
            acc = jnp.where(cdiff == c, val, acc)
        o_ref[:, j * LANES:(j + 1) * LANES] = jnp.where(inwin, acc, NEG_BIG)


def _na_bias(rpb):
    n = NA_KH * GRID_W
    return pl.pallas_call(
        _na_bias_body,
        grid=(NA_HEADS, NA_KH),
        in_specs=[pl.BlockSpec(memory_space=pltpu.SMEM)],
        out_specs=pl.BlockSpec((None, None, GRID_W, n), lambda h, v: (h, v, 0, 0)),
        out_shape=jax.ShapeDtypeStruct((NA_HEADS, NA_KH, GRID_W, n), F32),
        compiler_params=_params("parallel", "parallel"),
        name="na_bias",
    )(rpb.reshape(-1))


def _na_body(q_ref, k_ref, v_ref, ck_ref, cv_ref, bias_ref, o_ref, s_buf, e_buf, d_buf, *, rows):
    ck = ck_ref[...].astype(BF16)
    cv = cv_ref[...].astype(BF16)
    lo, hi = _pair_masks((GRID_W, LANES))
    scale = NA_DIM ** -0.5
    n_loc = NA_KH * GRID_W

    def window(r):
        r = jnp.clip(r, 0, rows - 1)
        r_start = jnp.clip(r - NA_KH // 2, 0, rows - NA_KH)
        return (r - r_start, pl.ds(pl.multiple_of(r * GRID_W, GRID_W), GRID_W),
                pl.ds(pl.multiple_of(r_start * GRID_W, GRID_W), n_loc))

    def scores(r, slot):
        vidx, qsl, ksl = window(r)
        q2 = q_ref[qsl, :]
        zero = jnp.zeros_like(q2)
        q2 = q2 * scale
        qs = jnp.concatenate([jnp.where(lo, q2, zero), jnp.where(hi, q2, zero)], axis=0)
        bias = jnp.concatenate([bias_ref[0, vidx], bias_ref[1, vidx]], axis=0)
        s_buf[slot, :, :n_loc] = _dot_nt(qs, k_ref[ksl, :]) + bias
        s_buf[slot, :, n_loc:] = _dot_nt(qs, ck)

    def numerators(slot):
        s = s_buf[slot]
        e = jnp.exp(s - jnp.max(s, axis=-1, keepdims=True))
        d_buf[slot] = jnp.broadcast_to(jnp.sum(e, axis=-1, keepdims=True), d_buf.shape[1:])
        e_buf[slot] = e.astype(BF16)

    def output(r, slot):
        _, qsl, ksl = window(r)
        o = (jnp.dot(e_buf[slot, :, :n_loc], v_ref[ksl, :], preferred_element_type=F32)
             + jnp.dot(e_buf[slot, :, n_loc:], cv, preferred_element_type=F32)) / d_buf[slot]
        o_ref[qsl, :] = jnp.where(lo, o[:GRID_W], o[GRID_W:]).astype(o_ref.dtype)

    s_buf[...] = jnp.zeros_like(s_buf)
    e_buf[...] = jnp.zeros_like(e_buf)
    d_buf[...] = jnp.ones_like(d_buf)

    def two_steps(j, carry):
        r = 2 * j
        scores(r, 0)
        numerators(1)
        output(r - 2, 0)
        scores(r + 1, 1)
        numerators(0)
        output(r - 1, 1)
        return carry

    lax.fori_loop(0, rows // 2 + 1, two_steps, 0, unroll=3)


def _neighbourhood_attention(q, k, v, ck, cv, bias):
    b, t, _ = q.shape
    lc = ck.shape[1]
    rows = t // GRID_W
    tok = pl.BlockSpec((None, t, LANES), lambda hp, bi: (bi, 0, hp))
    ctx = pl.BlockSpec((None, lc, LANES), lambda hp, bi: (bi, 0, hp))
    return pl.pallas_call(
        functools.partial(_na_body, rows=rows),
        grid=(NA_WIDTH // LANES, b),
        in_specs=[tok, tok, tok, ctx, ctx,
                  pl.BlockSpec((2, NA_KH, GRID_W, NA_KH * GRID_W), lambda hp, bi: (hp, 0, 0, 0))],
        out_specs=tok,
        out_shape=jax.ShapeDtypeStruct((b, t, NA_WIDTH), BF16),
        scratch_shapes=[pltpu.VMEM((2, 2 * GRID_W, NA_KH * GRID_W + lc), F32),
                        pltpu.VMEM((2, 2 * GRID_W, NA_KH * GRID_W + lc), BF16),
                        pltpu.VMEM((2, 2 * GRID_W, LANES), F32)],
        compiler_params=_params("parallel", "parallel"),
        name="na_attn",
    )(q, k, v, ck, cv, bias)


def _ret_body(q_ref, k_ref, v_ref, g_ref, df_ref, db_ref, beta_ref, s0f_ref, s0b_ref,
              o_ref, sf_ref, sb_ref, kvf_s, kvb_s, *, n_chunks, heads):
    for hh in range(heads):
        slab = lambda ref: ref.at[:, hh * LANES:(hh + 1) * LANES]
        _ret_head(slab(q_ref), slab(k_ref), slab(v_ref), slab(g_ref), df_ref.at[hh], db_ref.at[hh],
                  beta_ref.at[hh], s0f_ref.at[hh], s0b_ref.at[hh], slab(o_ref), sf_ref.at[hh], sb_ref.at[hh],
                  kvf_s, kvb_s, n_chunks=n_chunks)


def _ret_head(q_ref, k_ref, v_ref, g_ref, df_ref, db_ref, beta_ref, s0f_ref, s0b_ref,
              o_ref, sf_ref, sb_ref, kvf_s, kvb_s, *, n_chunks):
    c_len = CHUNK
    lgf = -jnp.log1p(jnp.exp(-df_ref[...]))
    lgb = -jnp.log1p(jnp.exp(-db_ref[...]))
    shape = (c_len, c_len)
    i = lax.broadcasted_iota(I32, shape, 0).astype(F32)
    j = lax.broadcasted_iota(I32, shape, 1).astype(F32)
    dij = i - j
    d_comb = (jnp.where(dij >= 0, jnp.exp(jnp.maximum(dij, 0.0) * lgf), 0.0)
              + jnp.where(dij <= 0, jnp.exp(jnp.maximum(-dij, 0.0) * lgb), 0.0))
    xi_f = jnp.exp((i + 1.0) * lgf)
    xi_b = jnp.exp((c_len - i) * lgb)
    zeta_f = jnp.exp((c_len - 1.0 - i) * lgf)
    zeta_b = jnp.exp(i * lgb)
    g_f = jnp.exp(c_len * lgf)
    g_b = jnp.exp(c_len * lgb)

    def chunk(c):
        return pl.ds(pl.multiple_of(c * c_len, c_len), c_len)

    def kv_pass(c, carry):
        kc = k_ref[chunk(c), :]
        vc = v_ref[chunk(c), :].astype(F32)
        kvf_s[c] = _dot_tn(kc, (vc * zeta_f).astype(BF16))
        kvb_s[c] = _dot_tn(kc, (vc * zeta_b).astype(BF16))
        return carry

    lax.fori_loop(0, n_chunks, kv_pass, 0, unroll=RET_CHUNK_UNROLL)

    def scan_f(c, s):
        kv = kvf_s[c]
        kvf_s[c] = s
        return g_f * s + kv

    def scan_b(ci, s):
        c = n_chunks - 1 - ci
        kv = kvb_s[c]
        kvb_s[c] = s
        return g_b * s + kv

    sf_ref[...] = lax.fori_loop(0, n_chunks, scan_f, s0f_ref[...])
    sb_ref[...] = lax.fori_loop(0, n_chunks, scan_b, s0b_ref[...])

    beta = beta_ref[...]

    def out_pass(c, carry):
        qc = q_ref[chunk(c), :]
        kc = k_ref[chunk(c), :]
        vc = v_ref[chunk(c), :]
        scores = _dot_nt(qc, kc) * d_comb
        y = jnp.dot(scores.astype(BF16), vc, preferred_element_type=F32)
        qf = qc.astype(F32)
        y += jnp.dot((qf * xi_f).astype(BF16), kvf_s[c].astype(BF16), preferred_element_type=F32)
        y += jnp.dot((qf * xi_b).astype(BF16), kvb_s[c].astype(BF16), preferred_element_type=F32)
        o_ref[chunk(c), :] = (_standardize(y) * beta * _silu(g_ref[chunk(c), :])).astype(o_ref.dtype)
        return carry

    lax.fori_loop(0, n_chunks, out_pass, 0, unroll=RET_CHUNK_UNROLL)


def _retention(q, k, v, g, decay_f, decay_b, beta_ret, s0f, s0b, heads_per_step):
    b, t, _ = q.shape
    n_chunks = t // CHUNK
    hs = heads_per_step
    tok = pl.BlockSpec((None, t, hs * LANES), lambda bi, h: (bi, 0, h))
    per_head = pl.BlockSpec((hs, 1, LANES), lambda bi, h: (h, 0, 0))
    state = pl.BlockSpec((None, hs, RET_DK, RET_DK), lambda bi, h: (bi, h, 0, 0))
    lanes = lambda a: jnp.broadcast_to(a.astype(F32)[:, None, None], (RET_HEADS, 1, LANES))
    st_shape = jax.ShapeDtypeStruct((b, RET_HEADS, RET_DK, RET_DK), F32)
    return pl.pallas_call(
        functools.partial(_ret_body, n_chunks=n_chunks, heads=hs),
        grid=(b, RET_HEADS // hs),
        in_specs=[tok, tok, tok, tok, per_head, per_head, per_head, state, state],
        out_specs=[tok, state, state],
        out_shape=[jax.ShapeDtypeStruct((b, t, RET_WIDTH), BF16), st_shape, st_shape],
        scratch_shapes=[pltpu.VMEM((n_chunks, RET_DK, RET_DK), F32)] * 2,
        compiler_params=_params("parallel", "parallel"),
        name="retention",
    )(q, k, v, g, lanes(decay_f), lanes(decay_b), beta_ret.reshape(RET_HEADS, 1, LANES), s0f, s0b)


def _merge_body(x_ref, ona_ref, ret_ref, wna_ref, wret_ref, bna_ref, gate_ref, lg_ref, lb_ref,
                shf_ref, scf_ref, wh_ref, wl_ref, br_ref, carry_in_ref,
                o_ref, p_ref, gt_ref, tab_ref, cnt_ref, carry_s):
    @pl.when((pl.program_id(0) == 0) & (pl.program_id(1) == 0))
    def _():
        carry_s[...] = carry_in_ref[...]

    o = ona_ref[...].astype(F32)
    na = o * lax.rsqrt(jnp.mean(o * o, axis=-1, keepdims=True) + LN_EPS) * bna_ref[...]
    mix = (jnp.dot(na.astype(BF16), wna_ref[...], preferred_element_type=F32)
           + jnp.dot(ret_ref[...], wret_ref[...], preferred_element_type=F32))
    y = DEEPNORM_ALPHA * x_ref[...] + gate_ref[...] * mix
    y = _standardize(y) * lg_ref[...] + lb_ref[...]
    o_ref[...] = y

    h = y * (1.0 + scf_ref[...]) + shf_ref[...]
    w_hi, w_lo, bias = wh_ref[...], wl_ref[...], br_ref[...]
    carry = carry_s[...]
    for s in range(y.shape[0] // TOKEN_BLOCK):
        sl = slice(s * TOKEN_BLOCK, (s + 1) * TOKEN_BLOCK)
        rows, gates, table, carry = _route_block(h[sl], w_hi, w_lo, bias, carry)
        p_ref[:, sl] = rows
        gt_ref[:, sl] = gates
        tab_ref[s] = table
    carry_s[...] = carry
    cnt_ref[...] = carry


def _merge_and_route(x, o_na, ret, w_out, beta_na, gate, shift_f, scale_f, mod_row, ln_g, ln_b,
                     w_hi, w_lo, b_router, carry_in, tm):
    b, t, _ = x.shape
    nt = t // tm
    sub = tm // TOKEN_BLOCK
    tok = lambda bi, ti: (bi, ti, 0)
    flat = lambda bi, ti: (0, bi * nt + ti)
    mod = lambda bi, ti: (mod_row(bi), 0, 0)
    whole = lambda shape: pl.BlockSpec(shape, lambda bi, ti: (0,) * len(shape))
    half = pl.BlockSpec((None, tm, 512), tok)
    mod_spec = pl.BlockSpec((None, 1, D_MODEL), mod)
    return pl.pallas_call(
        _merge_body,
        grid=(b, nt),
        in_specs=[pl.BlockSpec((None, tm, D_MODEL), tok), half, half,
                  pl.BlockSpec((512, D_MODEL), lambda bi, ti: (0, 0)),
                  pl.BlockSpec((512, D_MODEL), lambda bi, ti: (1, 0)),
                  whole((1, 512)), mod_spec, whole((1, D_MODEL)), whole((1, D_MODEL)),
                  mod_spec, mod_spec,
                  whole((N_EXPERTS, D_MODEL)), whole((N_EXPERTS, D_MODEL)), whole((N_EXPERTS, TOKEN_BLOCK)),
                  whole((N_EXPERTS, LANES))],
        out_specs=[pl.BlockSpec((None, tm, D_MODEL), tok),
                   pl.BlockSpec((TOP_K, tm), flat), pl.BlockSpec((TOP_K, tm), flat),
                   pl.BlockSpec((sub, N_EXPERTS, LANES), lambda bi, ti: (bi * nt + ti, 0, 0)),
                   whole((N_EXPERTS, LANES))],
        out_shape=[jax.ShapeDtypeStruct(x.shape, F32),
                   jax.ShapeDtypeStruct((TOP_K, b * t), I32), jax.ShapeDtypeStruct((TOP_K, b * t), F32),
                   jax.ShapeDtypeStruct((b * t // TOKEN_BLOCK, N_EXPERTS, LANES), F32),
                   jax.ShapeDtypeStruct((N_EXPERTS, LANES), F32)],
        scratch_shapes=[pltpu.VMEM((N_EXPERTS, LANES), F32)],
        compiler_params=_params("arbitrary", "arbitrary"),
        name="merge_ln1_route",
    )(x, o_na, ret, w_out, w_out, beta_na.reshape(1, 512), gate, ln_g.reshape(1, -1), ln_b.reshape(1, -1),
      shift_f, scale_f, w_hi, w_lo, jnp.broadcast_to(b_router[:, None], (N_EXPERTS, TOKEN_BLOCK)), carry_in)


def _two_group_specs(n_p_blocks, blocks_per_batch_s, ctx_row):
    tok_p = lambda i: (jnp.minimum(i, n_p_blocks - 1), 0)
    tok_s = lambda i: (jnp.maximum(i - n_p_blocks, 0), 0)
    mod = lambda i: (jnp.where(i < n_p_blocks, ctx_row, jnp.maximum(i - n_p_blocks, 0) // blocks_per_batch_s), 0, 0)
    return tok_p, tok_s, mod


def _route_block(h, w_hi, w_lo, bias, carry):
    tm = h.shape[0]
    h_hi = h.astype(BF16)
    h_lo = (h - h_hi.astype(F32)).astype(BF16)
    lg = _dot_nt(jnp.concatenate([w_hi, w_lo], axis=0), h_hi)
    work = lg[:N_EXPERTS] + lg[N_EXPERTS:] + _dot_nt(w_hi, h_lo) + bias
    eidx = lax.broadcasted_iota(I32, (N_EXPERTS, tm), 0).astype(F32)
    vals, idxs, hots = [], [], []
    for _ in range(TOP_K):
        mx = jnp.max(work, axis=0, keepdims=True)
        idx = jnp.min(jnp.where(work == mx, eidx, float(N_EXPERTS)), axis=0, keepdims=True)
        hot = eidx == idx
        vals.append(mx)
        idxs.append(idx)
        hots.append(hot)
        work = jnp.where(hot, -jnp.inf, work)
    exps = [jnp.exp(v - vals[0]) for v in vals]
    den = exps[0] + exps[1] + exps[2] + exps[3]
    sel = jnp.zeros((N_EXPERTS, tm), F32)
    for hot in hots:
        sel = sel + hot.astype(F32)
    r_i = lax.broadcasted_iota(I32, (tm, tm), 0)
    c_i = lax.broadcasted_iota(I32, (tm, tm), 1)
    earlier = (r_i < c_i).astype(BF16)
    before = jnp.dot(sel.astype(BF16), earlier, preferred_element_type=F32)
    n_e = jnp.sum(sel, axis=1, keepdims=True)
    n8 = jnp.floor((n_e + (RUN_ALIGN - 1.0)) * (1.0 / RUN_ALIGN)) * RUN_ALIGN + jnp.zeros((1, LANES), F32)
    e_r = lax.broadcasted_iota(I32, (N_EXPERTS, N_EXPERTS), 0)
    e_c = lax.broadcasted_iota(I32, (N_EXPERTS, N_EXPERTS), 1)
    off = jnp.dot((e_c < e_r).astype(BF16), n8.astype(BF16), preferred_element_type=F32)
    local = before + jnp.concatenate([off] * (tm // LANES), axis=1)
    rows = [jnp.sum(jnp.where(hot, local, 0.0), axis=0, keepdims=True) for hot in hots]
    lane = lax.broadcasted_iota(I32, (N_EXPERTS, LANES), 1)
    table = jnp.where(lane == 0, n8, jnp.where(lane == 1, carry, jnp.where(lane == 2, off, 0.0)))
    return (jnp.concatenate(rows, axis=0).astype(I32), jnp.concatenate([e / den for e in exps], axis=0),
            table, carry + n8)


def _for_pad_pieces(n_rows, visit):
    off = 0
    for size in PAD_PIECES:
        present = (n_rows & size) != 0
        visit(off, size, present)
        off = off + jnp.where(present, size, 0)


def _start_unit_copies(unit_ref, local_buf, sorted_hbm, sem, to_hbm):
    def one(u, carry):
        loc = local_buf.at[pl.ds(pl.multiple_of(u * RUN_ALIGN, RUN_ALIGN), RUN_ALIGN)]
        glob = sorted_hbm.at[pl.ds(pl.multiple_of(unit_ref[u], RUN_ALIGN), RUN_ALIGN)]
        copy = pltpu.make_async_copy(loc, glob, sem) if to_hbm else pltpu.make_async_copy(glob, loc, sem)
        copy.start()
        return carry

    lax.fori_loop(0, LOCAL_UNITS, one, 0, unroll=8)


def _wait_unit_copies(local_buf, sorted_hbm, sem):
    whole = sorted_hbm.at[pl.ds(0, LOCAL_ROWS)]
    pltpu.make_async_copy(whole, local_buf, sem).wait()


def _dispatch_body(unit_ref, zlo_ref, zn_ref, p_ref, yp_ref, ys_ref, sh_ref, sc_ref, xs_ref,
                   xbuf, zbuf, sems, zsem, *, n_p_blocks, n_blocks):
    i = pl.program_id(0)
    slot = i % 2
    y = jnp.where(i < n_p_blocks, yp_ref[...], ys_ref[...])
    h = (y * (1.0 + sc_ref[...]) + sh_ref[...]).astype(BF16)
    rows = p_ref[...]
    r_iota = lax.broadcasted_iota(I32, (LOCAL_ROWS, rows.shape[1]), 0)
    place = r_iota == rows[0:1]
    for kk in range(1, TOP_K):
        place = place | (r_iota == rows[kk:kk + 1])
    xbuf[slot] = _pack_bf16_pairs(
        jnp.dot(jnp.where(place, 1.0, 0.0).astype(BF16), h, preferred_element_type=F32), is_bf16_exact=True)

    _start_unit_copies(unit_ref, xbuf.at[slot], xs_ref, sems.at[slot], to_hbm=True)

    @pl.when(i > 0)
    def _():
        _wait_unit_copies(xbuf.at[1 - slot], xs_ref, sems.at[1 - slot])

    @pl.when(i == n_blocks - 1)
    def _():
        _wait_unit_copies(xbuf.at[slot], xs_ref, sems.at[slot])
        zbuf[...] = jnp.zeros_like(zbuf)
        sem = zsem

        def per_expert(e, carry):
            lo = zlo_ref[e]

            def visit(off, size, present):
                copy = pltpu.make_async_copy(
                    zbuf.at[pl.ds(0, size)], xs_ref.at[pl.ds(pl.multiple_of(lo + off, RUN_ALIGN), size)], sem)

                @pl.when(present)
                def _():
                    copy.start()
                    copy.wait()

            _for_pad_pieces(zn_ref[e], visit)
            return carry

        lax.fori_loop(0, N_EXPERTS, per_expert, 0)

        tail_lo = zlo_ref[N_EXPERTS]

        def tile_copy(t):
            start = pl.multiple_of(tail_lo + t * EXPERT_TILE, EXPERT_TILE)
            return pltpu.make_async_copy(zbuf, xs_ref.at[pl.ds(start, EXPERT_TILE)], sem)

        def t_issue(t, c2):
            tile_copy(t).start()
            return c2

        def t_drain(t, c2):
            tile_copy(0).wait()
            return c2

        n_tail = zn_ref[N_EXPERTS] // EXPERT_TILE
        lax.fori_loop(0, n_tail, t_issue, 0)
        lax.fori_loop(0, n_tail, t_drain, 0)


def _dispatch(yp, ys, shift, scale, ctx_row, s_blocks_per_batch, local_rows, unit_dst, zero_lo, zero_n,
              total_rows, tm):
    n_p, n_s = yp.shape[0], ys.shape[0]
    n = n_p + n_s
    npb = n_p // tm
    nb = n // tm
    tok_p, tok_s, mod = _two_group_specs(npb, s_blocks_per_batch, ctx_row)
    smem_all = pl.BlockSpec(memory_space=pltpu.SMEM)
    return pl.pallas_call(
        functools.partial(_dispatch_body, n_p_blocks=npb, n_blocks=nb),
        grid=(nb,),
        in_specs=[pl.BlockSpec((UNIT_TABLE,), lambda i: (i,), memory_space=pltpu.SMEM), smem_all, smem_all,
                  pl.BlockSpec((TOP_K, tm), lambda i: (0, i)),
                  pl.BlockSpec((tm, D_MODEL), tok_p), pl.BlockSpec((tm, D_MODEL), tok_s),
                  pl.BlockSpec((None, 1, D_MODEL), mod), pl.BlockSpec((None, 1, D_MODEL), mod)],
        out_specs=pl.BlockSpec(memory_space=pl.ANY),
        out_shape=jax.ShapeDtypeStruct((total_rows, D_MODEL // 2), U32),
        scratch_shapes=[pltpu.VMEM((2, LOCAL_ROWS, D_MODEL // 2), U32),
                        pltpu.VMEM((EXPERT_TILE, D_MODEL // 2), U32),
                        pltpu.SemaphoreType.DMA((2,)), pltpu.SemaphoreType.DMA(())],
        compiler_params=_params("arbitrary"),
        name="moe_dispatch",
    )(unit_dst, zero_lo, zero_n, local_rows, yp, ys, shift, scale)


def _expert_body(te_ref, first_ref, half_ref, nu_ref, x_ref, wgu_ref, bgu_ref, wd_ref, bd_ref, y_ref,
                 act_s, wgu_s, wd_s):
    j = pl.program_id(0)

    @pl.when(first_ref[j] == 1)
    def _():
        blk = 2 * LANES
        src = lax.broadcasted_iota(I32, (blk, blk), 0)
        dst = lax.broadcasted_iota(I32, (blk, blk), 1)
        perm = (src == jnp.where(dst < LANES, 2 * dst, 2 * (dst - LANES) + 1)).astype(F32).astype(BF16)
        for s in range(2 * D_FF // blk):
            sl = slice(s * blk, (s + 1) * blk)
            wgu_s[:, sl] = jnp.dot(wgu_ref[:, sl].astype(BF16), perm, preferred_element_type=F32).astype(BF16)
        wd_s[...] = wd_ref[...].astype(BF16)

    def ffn(n_rows):
        rows = slice(0, n_rows)
        x = _unpack_bf16_pairs(x_ref[rows, :])
        for c in range(D_FF // LANES):
            sl = slice(c * 2 * LANES, (c + 1) * 2 * LANES)
            hu = jnp.dot(x, wgu_s[:, sl], preferred_element_type=F32) + bgu_ref[:, sl]
            x_glu = jnp.minimum(hu[:, :LANES], SWIGLU_LIMIT)
            x_lin = jnp.clip(hu[:, LANES:], -SWIGLU_LIMIT, SWIGLU_LIMIT)
            act = x_glu * (1.0 / (1.0 + jnp.exp(-SWIGLU_ALPHA * x_glu))) * (x_lin + 1.0)
            act_s[rows, c * LANES:(c + 1) * LANES] = act.astype(BF16)
        y_ref[rows, :] = _pack_bf16_pairs(
            jnp.dot(act_s[rows, :], wd_s[...], preferred_element_type=F32) + bd_ref[...])

    used = j < nu_ref[0]
    half = half_ref[j] == 1

    @pl.when(used & jnp.logical_not(half))
    def _():
        ffn(EXPERT_TILE)

    @pl.when(used & half)
    def _():
        ffn(EXPERT_TILE // 2)
        y_ref[EXPERT_TILE // 2:, :] = jnp.zeros((EXPERT_TILE // 2, y_ref.shape[1]), y_ref.dtype)

    @pl.when(jnp.logical_not(used))
    def _():
        y_ref[...] = jnp.zeros_like(y_ref)


def _experts(tile_expert, tile_first, tile_half, n_used, xs, cap_rows, w_gate_up, bgu, w_down, bd):
    n_tiles = cap_rows // EXPERT_TILE
    of_expert = lambda j, te, first, half, nu: (te[j], 0, 0)
    grid_spec = pltpu.PrefetchScalarGridSpec(
        num_scalar_prefetch=4,
        grid=(n_tiles,),
        in_specs=[pl.BlockSpec((EXPERT_TILE, D_MODEL // 2),
                               lambda j, te, first, half, nu: (jnp.minimum(j, nu[0] - 1), 0)),
                  pl.BlockSpec((None, D_MODEL, 2 * D_FF), of_expert),
                  pl.BlockSpec((None, 1, 2 * D_FF), of_expert),
                  pl.BlockSpec((None, D_FF, D_MODEL), of_expert),
                  pl.BlockSpec((None, 1, D_MODEL), of_expert)],
        out_specs=pl.BlockSpec((EXPERT_TILE, D_MODEL // 2), lambda j, te, first, half, nu: (j, 0)),
        scratch_shapes=[pltpu.VMEM((EXPERT_TILE, D_FF), BF16),
                        pltpu.VMEM((D_MODEL, 2 * D_FF), BF16), pltpu.VMEM((D_FF, D_MODEL), BF16)],
    )
    return pl.pallas_call(
        _expert_body,
        grid_spec=grid_spec,
        out_shape=jax.ShapeDtypeStruct((cap_rows, D_MODEL // 2), U32),
        compiler_params=_params("arbitrary"),
        name="moe_experts",
    )(tile_expert, tile_first, tile_half, n_used, xs, w_gate_up, bgu, w_down, bd)


def _combine_body(unit_ref, next_unit_ref, y1_ref, p_ref, gt_ref, gate_ref, lg_ref, lb_ref, ys_ref, o_ref,
                  ybuf, sems, *, n_blocks):
    i = pl.program_id(0)
    slot = i % 2
    tm = y1_ref.shape[0]

    @pl.when(i == 0)
    def _():
        _start_unit_copies(unit_ref, ybuf.at[0], ys_ref, sems.at[0], to_hbm=False)

    @pl.when(i + 1 < n_blocks)
    def _():
        _start_unit_copies(next_unit_ref, ybuf.at[1 - slot], ys_ref, sems.at[1 - slot], to_hbm=False)

    _wait_unit_copies(ybuf.at[slot], ys_ref, sems.at[slot])

    rows = p_ref[...]
    gates = gt_ref[...]
    c_iota = lax.broadcasted_iota(I32, (tm, LOCAL_ROWS), 1)
    weight = jnp.zeros((tm, LOCAL_ROWS), F32)
    for kk in range(TOP_K):
        weight = jnp.where(c_iota == rows[:, kk:kk + 1], gates[:, kk:kk + 1], weight)
    w_hi = weight.astype(BF16)
    w_lo = (weight - w_hi.astype(F32)).astype(BF16)
    yb = _unpack_bf16_pairs(ybuf[slot])
    f = (jnp.dot(w_hi, yb, preferred_element_type=F32) + jnp.dot(w_lo, yb, preferred_element_type=F32))
    y = DEEPNORM_ALPHA * y1_ref[...] + gate_ref[...] * f
    o_ref[...] = _standardize(y) * lg_ref[...] + lb_ref[...]


def _combine(y1, local_rows, gates, unit_src, block0, gate_mod, mod_row, blocks_per_batch,
             y_sorted, ln_g, ln_b, tm):
    n_g = y1.shape[0]
    nb = n_g // tm
    row = pl.BlockSpec((1, D_MODEL), lambda i: (0, 0))
    pair = pl.BlockSpec((tm, TOP_K), lambda i: (i + block0, 0))
    units = lambda ahead: pl.BlockSpec(
        (UNIT_TABLE,), lambda i: (jnp.minimum(i + ahead, nb - 1) + block0,), memory_space=pltpu.SMEM)
    return pl.pallas_call(
        functools.partial(_combine_body, n_blocks=nb),
        grid=(nb,),
        in_specs=[units(0), units(1),
                  pl.BlockSpec((tm, D_MODEL), lambda i: (i, 0)), pair, pair,
                  pl.BlockSpec((None, 1, D_MODEL), lambda i: (mod_row(i // blocks_per_batch), 0, 0)),
                  row, row,
                  pl.BlockSpec(memory_space=pl.ANY)],
        out_specs=pl.BlockSpec((tm, D_MODEL), lambda i: (i, 0)),
        out_shape=jax.ShapeDtypeStruct((n_g, D_MODEL), F32),
        scratch_shapes=[pltpu.VMEM((2, LOCAL_ROWS, D_MODEL // 2), U32), pltpu.SemaphoreType.DMA((2,))],
        compiler_params=_params("arbitrary"),
        name="moe_combine_ln2",
    )(unit_src, unit_src, y1, local_rows, gates, gate_mod,
      ln_g.reshape(1, -1), ln_b.reshape(1, -1), y_sorted)


def _regroup_gate_up_bias(b_gate_up):
    e = b_gate_up.shape[0]
    nblk = D_FF // LANES
    return b_gate_up.reshape(e, 1, nblk, LANES, 2).transpose(0, 1, 2, 4, 3).reshape(e, 1, 2 * D_FF)


def _routing_plan(table, totals, n_pairs):
    nb = table.shape[0]
    run_n = table[:, :, 0].astype(I32)
    rows_before = table[:, :, 1].astype(I32)
    run_local = table[:, :, 2].astype(I32)
    cnt = totals[:, 0].astype(I32)
    padded = (cnt + EXPERT_TILE - 1) // EXPERT_TILE * EXPERT_TILE
    pad_end = jnp.cumsum(padded)
    pad_start = pad_end - padded
    run_global = pad_start[None, :] + rows_before
    rows_max = n_pairs + nb * N_EXPERTS * (RUN_ALIGN - 1) + N_EXPERTS * (EXPERT_TILE - 1)
    n_tiles = -(-rows_max // EXPERT_TILE)
    n_used = pad_end[-1] // EXPERT_TILE
    tile_row = jnp.minimum(jnp.arange(n_tiles, dtype=I32), n_used - 1) * EXPERT_TILE
    tile_expert = jnp.sum((pad_end[None, :] <= tile_row[:, None]).astype(I32), axis=1)
    tile_expert = jnp.minimum(tile_expert, N_EXPERTS - 1)
    tile_id = jnp.arange(n_tiles, dtype=I32)
    changed = jnp.concatenate([jnp.ones((1,), bool), tile_expert[1:] != tile_expert[:-1]])
    tile_first = (changed & (tile_id < n_used)).astype(I32)
    exp_hot = tile_expert[:, None] == jnp.arange(N_EXPERTS, dtype=I32)
    rows_end = jnp.sum(jnp.where(exp_hot, (pad_start + cnt)[None, :], 0), axis=-1)
    tile_half = ((rows_end - tile_row) <= EXPERT_TILE // 2).astype(I32)
    cap_rows = n_tiles * EXPERT_TILE
    spare_rows = -(-LOCAL_ROWS // EXPERT_TILE) * EXPERT_TILE
    total_rows = cap_rows + 2 * spare_rows
    unit_row = jnp.arange(LOCAL_UNITS, dtype=I32) * RUN_ALIGN
    run_end = run_local + run_n
    owner = jnp.sum((run_end[:, None, :] <= unit_row[None, :, None]).astype(I32), axis=-1)
    own_hot = owner[:, :, None] == jnp.arange(N_EXPERTS, dtype=I32)
    shift = jnp.sum(jnp.where(own_hot, (run_global - run_local)[:, None, :], 0), axis=-1)
    used = owner < N_EXPERTS
    sorted_row = unit_row[None, :] + shift
    slot = (jnp.arange(nb, dtype=I32) % 2)[:, None]
    unit_dst = jnp.where(used, sorted_row, cap_rows + slot * spare_rows + unit_row[None, :])
    unit_src = jnp.where(used, sorted_row, unit_row[None, :])
    table_of = lambda a: jnp.pad(a, ((0, 0), (0, UNIT_TABLE - LOCAL_UNITS))).reshape(-1)
    zero_lo = jnp.concatenate([pad_start + cnt, pad_end[-1:]])
    zero_n = jnp.concatenate([padded - cnt, total_rows - pad_end[-1:]])
    return ((table_of(unit_dst), table_of(unit_src)), (tile_expert, tile_first, tile_half, n_used.reshape(1)),
            zero_lo, zero_n, cap_rows, total_rows)


def kernel(x_prompt, x_sample, cache_na_k, cache_na_v, state_ret_fwd, state_ret_bwd, c, c_ctx, w_ada, b_ada, w_in, rpb, ret_decay_fwd, ret_decay_bwd, beta_na, beta_ret, w_out, ln1_g, ln1_b, w_router, b_router, w_gate_up, b_gate_up, w_down, b_down, ln2_g, ln2_b):
    bp, tp, _ = x_prompt.shape
    bs, ts, _ = x_sample.shape
    assert w_ada.shape[0] == DEPTH == 1
    l = 0
    ctx_row = bs
    sample_row = lambda bi: bi
    prompt_row = lambda bi: ctx_row

    cond = jnp.concatenate([c, c_ctx[None, :], jnp.zeros((16 - bs - 1, D_MODEL), F32)], axis=0)
    m = _adaln(cond, w_ada[l], b_ada[l]).reshape(16, 6, 1, D_MODEL)
    sh_a, sc_a, gt_a, sh_f, sc_f, gt_f = (m[:, i] for i in range(6))

    w_in_b = w_in[l].astype(BF16)
    w_out_b = w_out[l].astype(BF16)

    q, k_p, v_p, qr, kr, vr, g = _project(x_prompt, sh_a, sc_a, prompt_row, w_in_b, tp, F32, rope=False)
    o_na = _context_attention(q, k_p, v_p)
    zeros_state = jnp.zeros((bp, RET_HEADS, RET_DK, RET_DK), F32)
    ret, s_f, s_b = _retention(qr, kr, vr, g, ret_decay_fwd[l], ret_decay_bwd[l], beta_ret[l],
                               zeros_state, zeros_state, heads_per_step=RET_HEADS)
    w_r = w_router[l].T
    w_r_hi = w_r.astype(BF16)
    w_r_lo = (w_r - w_r_hi.astype(F32)).astype(BF16)
    route = functools.partial(_merge_and_route, w_out=w_out_b, beta_na=beta_na[l], gate=gt_a, shift_f=sh_f,
                              scale_f=sc_f, ln_g=ln1_g[l], ln_b=ln1_b[l], w_hi=w_r_hi, w_lo=w_r_lo,
                              b_router=b_router[l])
    yp1, rows_p, gates_p, table_p, totals_p = route(
        x_prompt, o_na, ret, mod_row=prompt_row, carry_in=jnp.zeros((N_EXPERTS, LANES), F32), tm=tp)

    q, k_s, v_s, qr, kr, vr, g = _project(x_sample, sh_a, sc_a, sample_row, w_in_b, 512, BF16, rope=True)
    lc = cache_na_k.shape[2]
    o_na = _neighbourhood_attention(q, k_s, v_s, cache_na_k[:, l].reshape(bs, lc, NA_WIDTH),
                                    cache_na_v[:, l].reshape(bs, lc, NA_WIDTH), _na_bias(rpb[l]))
    ret, _, _ = _retention(qr, kr, vr, g, ret_decay_fwd[l], ret_decay_bwd[l], beta_ret[l],
                           state_ret_fwd[:, l], state_ret_bwd[:, l], heads_per_step=1)
    ys1, rows_s, gates_s, table_s, totals = route(
        x_sample, o_na, ret, mod_row=sample_row, carry_in=totals_p, tm=512)

    tm = TOKEN_BLOCK
    yp1f = yp1.reshape(bp * tp, D_MODEL)
    ys1f = ys1.reshape(bs * ts, D_MODEL)
    local_rows = jnp.concatenate([rows_p, rows_s], axis=1)
    gates = jnp.concatenate([gates_p, gates_s], axis=1)
    table = jnp.concatenate([table_p, table_s], axis=0)
    (unit_dst, unit_src), tiles, zero_lo, zero_n, cap_rows, total_rows = _routing_plan(
        table, totals, (bp * tp + bs * ts) * TOP_K)
    xs = _dispatch(yp1f, ys1f, sh_f, sc_f, ctx_row, ts // tm, local_rows, unit_dst, zero_lo, zero_n,
                   total_rows, tm)
    y_sorted = _experts(*tiles, xs, cap_rows, w_gate_up[l], _regroup_gate_up_bias(b_gate_up[l]), w_down[l],
                        b_down[l][:, None, :])
    rows_t, gates_t = local_rows.T, gates.T
    yp = _combine(yp1f, rows_t, gates_t, unit_src, 0, gt_f, prompt_row, 1, y_sorted, ln2_g[l], ln2_b[l], tm)
    ys = _combine(ys1f, rows_t, gates_t, unit_src, (bp * tp) // tm, gt_f, sample_row, ts // tm, y_sorted,
                  ln2_g[l], ln2_b[l], tm)

    return (yp.reshape(bp, tp, D_MODEL), ys.reshape(bs, ts, D_MODEL),
            k_p.reshape(bp, 1, tp, NA_HEADS, NA_DIM), v_p.reshape(bp, 1, tp, NA_HEADS, NA_DIM),
            s_f[:, None], s_b[:, None])
```

```python
import functools

import jax
import jax.numpy as jnp
from jax import lax
from jax.experimental import pallas as pl
from jax.experimental.pallas import tpu as pltpu

F32 = jnp.float32
BF16 = jnp.bfloat16
I32 = jnp.int32
U32 = jnp.uint32

D_MODEL = 1024
GRID_W = 64
NA_HEADS = 8
NA_DIM = 64
NA_WIDTH = NA_HEADS * NA_DIM
NA_KH = 8
NA_KW = 16
RET_HEADS = 4
RET_DK = 128
RET_WIDTH = RET_HEADS * RET_DK
CHUNK = 128
N_EXPERTS = 32
TOP_K = 4
D_FF = 1024
SWIGLU_LIMIT = 7.0
SWIGLU_ALPHA = 1.702
ROPE_BASE = 10000.0
LN_EPS = 1e-5
DEPTH = 1
DEEPNORM_ALPHA = (2.0 * DEPTH) ** 0.25

LANES = 128
NEG_BIG = -1e30
TOKEN_BLOCK = 256
SAMPLE_TILE = 1024
EXPERT_TILE = 1024
VMEM_LIMIT = 56 * 1024 * 1024
RET_CHUNK_UNROLL = 16
SUBLANES = 8
RUN_ALIGN = SUBLANES
LOCAL_ROWS = TOKEN_BLOCK * TOP_K + N_EXPERTS * RUN_ALIGN
LOCAL_UNITS = LOCAL_ROWS // RUN_ALIGN
UNIT_TABLE = 256
PAD_PIECES = tuple(EXPERT_TILE >> s for s in range(1, EXPERT_TILE.bit_length())
                   if (EXPERT_TILE >> s) >= RUN_ALIGN)


def _params(*sem):
    return pltpu.CompilerParams(dimension_semantics=sem, vmem_limit_bytes=VMEM_LIMIT)


def _silu(x):
    return x / (1.0 + jnp.exp(-x))


def _standardize(x):
    mu = jnp.mean(x, axis=-1, keepdims=True)
    xc = x - mu
    var = jnp.mean(xc * xc, axis=-1, keepdims=True)
    return xc * lax.rsqrt(var + LN_EPS)


def _pack_bf16_pairs(x, is_bf16_exact=False):
    if not is_bf16_exact:
        x = x.astype(BF16).astype(F32)
    bits = lax.bitcast_convert_type(x, U32)
    half = x.shape[1] // 2
    return bits[:, :half] | lax.shift_right_logical(bits[:, half:], jnp.uint32(16))


def _unpack_bf16_pairs(p):
    hi = lax.bitcast_convert_type(p & jnp.uint32(0xFFFF0000), F32)
    lo = lax.bitcast_convert_type(lax.shift_left(p, jnp.uint32(16)), F32)
    return jnp.concatenate([hi, lo], axis=1).astype(BF16)


def _dot_nt(a, b):
    return lax.dot_general(a, b, (((1,), (1,)), ((), ())), preferred_element_type=F32)


def _dot_tn(a, b):
    return lax.dot_general(a, b, (((0,), (0,)), ((), ())), preferred_element_type=F32)


def _ada_body(c_ref, w_ref, b_ref, o_ref):
    s = _silu(c_ref[...])
    o_ref[...] = jnp.dot(s, w_ref[...], preferred_element_type=F32,
                         precision=lax.Precision.HIGHEST) + b_ref[...]


def _adaln(cond, w_ada, b_ada):
    r = cond.shape[0]
    n = w_ada.shape[1]
    tn = 1536
    return pl.pallas_call(
        _ada_body,
        grid=(n // tn,),
        in_specs=[pl.BlockSpec((r, D_MODEL), lambda j: (0, 0)),
                  pl.BlockSpec((D_MODEL, tn), lambda j: (0, j)),
                  pl.BlockSpec((1, tn), lambda j: (0, j))],
        out_specs=pl.BlockSpec((r, tn), lambda j: (0, j)),
        out_shape=jax.ShapeDtypeStruct((r, n), F32),
        compiler_params=_params("arbitrary"),
        name="adaln",
    )(cond, w_ada, b_ada.reshape(1, n))


def _proj_body(*refs, rope):
    if rope:
        (x_ref, sh_ref, sc_ref, w_ref, cos_ref, sa_ref, sb_ref,
         q_ref, k_ref, v_ref, qr_ref, kr_ref, vr_ref, g_ref) = refs
    else:
        (x_ref, sh_ref, sc_ref, w_ref,
         q_ref, k_ref, v_ref, qr_ref, kr_ref, vr_ref, g_ref) = refs
    h = (x_ref[...] * (1.0 + sc_ref[...]) + sh_ref[...]).astype(BF16)

    def cols(c):
        return jnp.dot(h, w_ref[:, c * 512:(c + 1) * 512], preferred_element_type=F32)

    q_ref[...] = cols(0).astype(q_ref.dtype)
    k_ref[...] = cols(1).astype(k_ref.dtype)
    v_ref[...] = cols(2).astype(v_ref.dtype)
    pq = cols(3)
    pk = cols(4) * (RET_DK ** -0.5)
    if rope:
        cs, sa, sb = cos_ref[...], sa_ref[...], sb_ref[...]
        for hd in range(RET_HEADS):
            sl = slice(hd * LANES, (hd + 1) * LANES)
            for p, o_ref in ((pq, qr_ref), (pk, kr_ref)):
                xs = p[:, sl]
                rot = xs * cs + pltpu.roll(xs, 96, 1) * sa + pltpu.roll(xs, 32, 1) * sb
                o_ref[:, sl] = rot.astype(o_ref.dtype)
    else:
        qr_ref[...] = pq.astype(qr_ref.dtype)
        kr_ref[...] = pk.astype(kr_ref.dtype)
    vr_ref[...] = cols(5).astype(vr_ref.dtype)
    g_ref[...] = cols(6)


def _rope_tables(t_len):
    t = jnp.arange(t_len, dtype=jnp.int32)
    pos_row = (t // GRID_W).astype(F32)[:, None]
    pos_col = (t % GRID_W).astype(F32)[:, None]
    lane = jnp.arange(LANES, dtype=jnp.int32)[None, :]
    n_freq = RET_DK // 4
    inv_freq = ROPE_BASE ** (-(lane % n_freq).astype(F32) / n_freq)
    ang = jnp.where(lane < RET_DK // 2, pos_row, pos_col) * inv_freq
    first = (lane % (2 * n_freq)) < n_freq
    cos = jnp.cos(ang)
    sin = jnp.sin(ang)
    return cos, jnp.where(first, -sin, 0.0), jnp.where(first, 0.0, sin)


def _project(x, shift, scale, mod_row, w_in, tm, kv_dtype, rope):
    b, t, _ = x.shape
    tok = lambda bi, ti: (bi, ti, 0)
    mod = lambda bi, ti: (mod_row(bi), 0, 0)
    in_specs = [pl.BlockSpec((None, tm, D_MODEL), tok),
                pl.BlockSpec((None, 1, D_MODEL), mod),
                pl.BlockSpec((None, 1, D_MODEL), mod),
                pl.BlockSpec(w_in.shape, lambda bi, ti: (0, 0))]
    args = [x, shift, scale, w_in]
    if rope:
        tab = lambda bi, ti: (ti, 0)
        in_specs += [pl.BlockSpec((tm, LANES), tab)] * 3
        args += list(_rope_tables(t))
    o512 = pl.BlockSpec((None, tm, 512), tok)
    shp = lambda dt: jax.ShapeDtypeStruct((b, t, 512), dt)
    return pl.pallas_call(
        functools.partial(_proj_body, rope=rope),
        grid=(b, t // tm),
        in_specs=in_specs,
        out_specs=[o512] * 7,
        out_shape=[shp(BF16), shp(kv_dtype), shp(kv_dtype), shp(BF16), shp(BF16), shp(BF16), shp(F32)],
        compiler_params=_params("parallel", "parallel"),
        name="in_proj",
    )(*args)


def _pair_masks(shape):
    lane = lax.broadcasted_iota(I32, shape, 1)
    return lane < NA_DIM, lane >= NA_DIM


def _ctx_attn_body(q_ref, k_ref, v_ref, o_ref):
    t = q_ref.shape[0]
    lo, hi = _pair_masks((t, LANES))
    for hp in range(NA_WIDTH // LANES):
        sl = slice(hp * LANES, (hp + 1) * LANES)
        q2 = q_ref[:, sl] * (NA_DIM ** -0.5)
        k2 = k_ref[:, sl].astype(BF16)
        v2 = v_ref[:, sl].astype(BF16)
        zero = jnp.zeros_like(q2)
        qs = jnp.concatenate([jnp.where(lo, q2, zero), jnp.where(hi, q2, zero)], axis=0)
        s = _dot_nt(qs, k2)
        e = jnp.exp(s - jnp.max(s, axis=-1, keepdims=True))
        den = jnp.sum(e, axis=-1, keepdims=True)
        o = jnp.dot(e.astype(BF16), v2, preferred_element_type=F32) / den
        o_ref[:, sl] = jnp.where(lo, o[:t], o[t:]).astype(o_ref.dtype)


def _context_attention(q, k, v):
    b, t, _ = q.shape
    spec = pl.BlockSpec((None, t, NA_WIDTH), lambda bi: (bi, 0, 0))
    return pl.pallas_call(
        _ctx_attn_body,
        grid=(b,),
        in_specs=[spec] * 3,
        out_specs=spec,
        out_shape=jax.ShapeDtypeStruct((b, t, NA_WIDTH), BF16),
        compiler_params=_params("parallel"),
        name="ctx_attn",
    )(q, k, v)


def _na_bias_body(rpb_ref, o_ref):
    h = pl.program_id(0)
    v = pl.program_id(1)
    shape = (GRID_W, LANES)
    w = lax.broadcasted_iota(I32, shape, 0)
    lane = lax.broadcasted_iota(I32, shape, 1)
    upper = lane >= GRID_W
    kc = jnp.where(upper, lane - GRID_W, lane)
    cdiff = kc - w + (NA_KW - 1)
    cstart = jnp.clip(w - NA_KW // 2, 0, GRID_W - NA_KW)
    inwin = (kc >= cstart) & (kc < cstart + NA_KW)
    n_ro = 2 * NA_KH - 1
    n_co = 2 * NA_KW - 1
    for j in range(NA_KH // 2):
        ro = 2 * j - v + (NA_KH - 1)
        base0 = (h * n_ro + ro) * n_co
        acc = jnp.zeros(shape, F32)
        for c in range(n_co):
            val = jnp.where(upper, rpb_ref[base0 + n_co + c], rpb_ref[base0 + c])
            acc = jnp.where(cdiff == c, val, acc)
        o_ref[:, j * LANES:(j + 1) * LANES] = jnp.where(inwin, acc, NEG_BIG)


def _na_bias(rpb):
    n = NA_KH * GRID_W
    return pl.pallas_call(
        _na_bias_body,
        grid=(NA_HEADS, NA_KH),
        in_specs=[pl.BlockSpec(memory_space=pltpu.SMEM)],
        out_specs=pl.BlockSpec((None, None, GRID_W, n), lambda h, v: (h, v, 0, 0)),
        out_shape=jax.ShapeDtypeStruct((NA_HEADS, NA_KH, GRID_W, n), F32),
        compiler_params=_params("parallel", "parallel"),
        name="na_bias",
    )(rpb.reshape(-1))


def _na_body(q_ref, k_ref, v_ref, ck_ref, cv_ref, bias_ref, o_ref, s_buf, e_buf, d_buf, *, rows):
    ck = ck_ref[...].astype(BF16)
    cv = cv_ref[...].astype(BF16)
    lo, hi = _pair_masks((GRID_W, LANES))
    scale = NA_DIM ** -0.5
    n_loc = NA_KH * GRID_W

    def window(r):
        r = jnp.clip(r, 0, rows - 1)
        r_start = jnp.clip(r - NA_KH // 2, 0, rows - NA_KH)
        return (r - r_start, pl.ds(pl.multiple_of(r * GRID_W, GRID_W), GRID_W),
                pl.ds(pl.multiple_of(r_start * GRID_W, GRID_W), n_loc))

    def scores(r, slot):
        vidx, qsl, ksl = window(r)
        q2 = q_ref[qsl, :]
        zero = jnp.zeros_like(q2)
        q2 = q2 * scale
        qs = jnp.concatenate([jnp.where(lo, q2, zero), jnp.where(hi, q2, zero)], axis=0)
        bias = jnp.concatenate([bias_ref[0, vidx], bias_ref[1, vidx]], axis=0)
        s_buf[slot, :, :n_loc] = _dot_nt(qs, k_ref[ksl, :]) + bias
        s_buf[slot, :, n_loc:] = _dot_nt(qs, ck)

    def numerators(slot):
        s = s_buf[slot]
        e = jnp.exp(s - jnp.max(s, axis=-1, keepdims=True))
        d_buf[slot] = jnp.broadcast_to(jnp.sum(e, axis=-1, keepdims=True), d_buf.shape[1:])
        e_buf[slot] = e.astype(BF16)

    def output(r, slot):
        _, qsl, ksl = window(r)
        o = (jnp.dot(e_buf[slot, :, :n_loc], v_ref[ksl, :], preferred_element_type=F32)
             + jnp.dot(e_buf[slot, :, n_loc:], cv, preferred_element_type=F32)) / d_buf[slot]
        o_ref[qsl, :] = jnp.where(lo, o[:GRID_W], o[GRID_W:]).astype(o_ref.dtype)

    s_buf[...] = jnp.zeros_like(s_buf)
    e_buf[...] = jnp.zeros_like(e_buf)
    d_buf[...] = jnp.ones_like(d_buf)

    def two_steps(j, carry):
        r = 2 * j
        scores(r, 0)
        numerators(1)
        output(r - 2, 0)
        scores(r + 1, 1)
        numerators(0)
        output(r - 1, 1)
        return carry

    lax.fori_loop(0, rows // 2 + 1, two_steps, 0, unroll=3)


def _neighbourhood_attention(q, k, v, ck, cv, bias):
    b, t, _ = q.shape
    lc = ck.shape[1]
    rows = t // GRID_W
    tok = pl.BlockSpec((None, t, LANES), lambda hp, bi: (bi, 0, hp))
    ctx = pl.BlockSpec((None, lc, LANES), lambda hp, bi: (bi, 0, hp))
    return pl.pallas_call(
        functools.partial(_na_body, rows=rows),
        grid=(NA_WIDTH // LANES, b),
        in_specs=[tok, tok, tok, ctx, ctx,
                  pl.BlockSpec((2, NA_KH, GRID_W, NA_KH * GRID_W), lambda hp, bi: (hp, 0, 0, 0))],
        out_specs=tok,
        out_shape=jax.ShapeDtypeStruct((b, t, NA_WIDTH), BF16),
        scratch_shapes=[pltpu.VMEM((2, 2 * GRID_W, NA_KH * GRID_W + lc), F32),
                        pltpu.VMEM((2, 2 * GRID_W, NA_KH * GRID_W + lc), BF16),
                        pltpu.VMEM((2, 2 * GRID_W, LANES), F32)],
        compiler_params=_params("parallel", "parallel"),
        name="na_attn",
    )(q, k, v, ck, cv, bias)


def _ret_body(q_ref, k_ref, v_ref, g_ref, df_ref, db_ref, beta_ref, s0f_ref, s0b_ref,
              o_ref, sf_ref, sb_ref, kvf_s, kvb_s, *, n_chunks, heads):
    for hh in range(heads):
        slab = lambda ref: ref.at[:, hh * LANES:(hh + 1) * LANES]
        _ret_head(slab(q_ref), slab(k_ref), slab(v_ref), slab(g_ref), df_ref.at[hh], db_ref.at[hh],
                  beta_ref.at[hh], s0f_ref.at[hh], s0b_ref.at[hh], slab(o_ref), sf_ref.at[hh], sb_ref.at[hh],
                  kvf_s, kvb_s, n_chunks=n_chunks)


def _ret_head(q_ref, k_ref, v_ref, g_ref, df_ref, db_ref, beta_ref, s0f_ref, s0b_ref,
              o_ref, sf_ref, sb_ref, kvf_s, kvb_s, *, n_chunks):
    c_len = CHUNK
    lgf = -jnp.log1p(jnp.exp(-df_ref[...]))
    lgb = -jnp.log1p(jnp.exp(-db_ref[...]))
    shape = (c_len, c_len)
    i = lax.broadcasted_iota(I32, shape, 0).astype(F32)
    j = lax.broadcasted_iota(I32, shape, 1).astype(F32)
    dij = i - j
    d_comb = (jnp.where(dij >= 0, jnp.exp(jnp.maximum(dij, 0.0) * lgf), 0.0)
              + jnp.where(dij <= 0, jnp.exp(jnp.maximum(-dij, 0.0) * lgb), 0.0))
    xi_f = jnp.exp((i + 1.0) * lgf)
    xi_b = jnp.exp((c_len - i) * lgb)
    zeta_f = jnp.exp((c_len - 1.0 - i) * lgf)
    zeta_b = jnp.exp(i * lgb)
    g_f = jnp.exp(c_len * lgf)
    g_b = jnp.exp(c_len * lgb)

    def chunk(c):
        return pl.ds(pl.multiple_of(c * c_len, c_len), c_len)

    def kv_pass(c, carry):
        kc = k_ref[chunk(c), :]
        vc = v_ref[chunk(c), :].astype(F32)
        kvf_s[c] = _dot_tn(kc, (vc * zeta_f).astype(BF16))
        kvb_s[c] = _dot_tn(kc, (vc * zeta_b).astype(BF16))
        return carry

    lax.fori_loop(0, n_chunks, kv_pass, 0, unroll=RET_CHUNK_UNROLL)

    def scan_f(c, s):
        kv = kvf_s[c]
        kvf_s[c] = s
        return g_f * s + kv

    def scan_b(ci, s):
        c = n_chunks - 1 - ci
        kv = kvb_s[c]
        kvb_s[c] = s
        return g_b * s + kv

    sf_ref[...] = lax.fori_loop(0, n_chunks, scan_f, s0f_ref[...])
    sb_ref[...] = lax.fori_loop(0, n_chunks, scan_b, s0b_ref[...])

    beta = beta_ref[...]

    def out_pass(c, carry):
        qc = q_ref[chunk(c), :]
        kc = k_ref[chunk(c), :]
        vc = v_ref[chunk(c), :]
        scores = _dot_nt(qc, kc) * d_comb
        y = jnp.dot(scores.astype(BF16), vc, preferred_element_type=F32)
        qf = qc.astype(F32)
        y += jnp.dot((qf * xi_f).astype(BF16), kvf_s[c].astype(BF16), preferred_element_type=F32)
        y += jnp.dot((qf * xi_b).astype(BF16), kvb_s[c].astype(BF16), preferred_element_type=F32)
        o_ref[chunk(c), :] = (_standardize(y) * beta * _silu(g_ref[chunk(c), :])).astype(o_ref.dtype)
        return carry

    lax.fori_loop(0, n_chunks, out_pass, 0, unroll=RET_CHUNK_UNROLL)


def _retention(q, k, v, g, decay_f, decay_b, beta_ret, s0f, s0b, heads_per_step):
    b, t, _ = q.shape
    n_chunks = t // CHUNK
    hs = heads_per_step
    tok = pl.BlockSpec((None, t, hs * LANES), lambda bi, h: (bi, 0, h))
    per_head = pl.BlockSpec((hs, 1, LANES), lambda bi, h: (h, 0, 0))
    state = pl.BlockSpec((None, hs, RET_DK, RET_DK), lambda bi, h: (bi, h, 0, 0))
    lanes = lambda a: jnp.broadcast_to(a.astype(F32)[:, None, None], (RET_HEADS, 1, LANES))
    st_shape = jax.ShapeDtypeStruct((b, RET_HEADS, RET_DK, RET_DK), F32)
    return pl.pallas_call(
        functools.partial(_ret_body, n_chunks=n_chunks, heads=hs),
        grid=(b, RET_HEADS // hs),
        in_specs=[tok, tok, tok, tok, per_head, per_head, per_head, state, state],
        out_specs=[tok, state, state],
        out_shape=[jax.ShapeDtypeStruct((b, t, RET_WIDTH), BF16), st_shape, st_shape],
        scratch_shapes=[pltpu.VMEM((n_chunks, RET_DK, RET_DK), F32)] * 2,
        compiler_params=_params("parallel", "parallel"),
        name="retention",
    )(q, k, v, g, lanes(decay_f), lanes(decay_b), beta_ret.reshape(RET_HEADS, 1, LANES), s0f, s0b)


def _merge_body(x_ref, ona_ref, ret_ref, wna_ref, wret_ref, bna_ref, gate_ref, lg_ref, lb_ref,
                shf_ref, scf_ref, wh_ref, wl_ref, br_ref, carry_in_ref,
                o_ref, p_ref, gt_ref, tab_ref, cnt_ref, carry_s):
    @pl.when((pl.program_id(0) == 0) & (pl.program_id(1) == 0))
    def _():
        carry_s[...] = carry_in_ref[...]

    o = ona_ref[...].astype(F32)
    na = o * lax.rsqrt(jnp.mean(o * o, axis=-1, keepdims=True) + LN_EPS) * bna_ref[...]
    mix = (jnp.dot(na.astype(BF16), wna_ref[...], preferred_element_type=F32)
           + jnp.dot(ret_ref[...], wret_ref[...], preferred_element_type=F32))
    y = DEEPNORM_ALPHA * x_ref[...] + gate_ref[...] * mix
    y = _standardize(y) * lg_ref[...] + lb_ref[...]
    o_ref[...] = y

    h = y * (1.0 + scf_ref[...]) + shf_ref[...]
    w_hi, w_lo, bias = wh_ref[...], wl_ref[...], br_ref[...]
    carry = carry_s[...]
    for s in range(y.shape[0] // TOKEN_BLOCK):
        sl = slice(s * TOKEN_BLOCK, (s + 1) * TOKEN_BLOCK)
        rows, gates, table, carry = _route_block(h[sl], w_hi, w_lo, bias, carry)
        p_ref[:, sl] = rows
        gt_ref[:, sl] = gates
        tab_ref[s] = table
    carry_s[...] = carry
    cnt_ref[...] = carry


def _merge_and_route(x, o_na, ret, w_out, beta_na, gate, shift_f, scale_f, mod_row, ln_g, ln_b,
                     w_hi, w_lo, b_router, carry_in, tm):
    b, t, _ = x.shape
    nt = t // tm
    sub = tm // TOKEN_BLOCK
    tok = lambda bi, ti: (bi, ti, 0)
    flat = lambda bi, ti: (0, bi * nt + ti)
    mod = lambda bi, ti: (mod_row(bi), 0, 0)
    whole = lambda shape: pl.BlockSpec(shape, lambda bi, ti: (0,) * len(shape))
    half = pl.BlockSpec((None, tm, 512), tok)
    mod_spec = pl.BlockSpec((None, 1, D_MODEL), mod)
    return pl.pallas_call(
        _merge_body,
        grid=(b, nt),
        in_specs=[pl.BlockSpec((None, tm, D_MODEL), tok), half, half,
                  pl.BlockSpec((512, D_MODEL), lambda bi, ti: (0, 0)),
                  pl.BlockSpec((512, D_MODEL), lambda bi, ti: (1, 0)),
                  whole((1, 512)), mod_spec, whole((1, D_MODEL)), whole((1, D_MODEL)),
                  mod_spec, mod_spec,
                  whole((N_EXPERTS, D_MODEL)), whole((N_EXPERTS, D_MODEL)), whole((N_EXPERTS, TOKEN_BLOCK)),
                  whole((N_EXPERTS, LANES))],
        out_specs=[pl.BlockSpec((None, tm, D_MODEL), tok),
                   pl.BlockSpec((TOP_K, tm), flat), pl.BlockSpec((TOP_K, tm), flat),
                   pl.BlockSpec((sub, N_EXPERTS, LANES), lambda bi, ti: (bi * nt + ti, 0, 0)),
                   whole((N_EXPERTS, LANES))],
        out_shape=[jax.ShapeDtypeStruct(x.shape, F32),
                   jax.ShapeDtypeStruct((TOP_K, b * t), I32), jax.ShapeDtypeStruct((TOP_K, b * t), F32),
                   jax.ShapeDtypeStruct((b * t // TOKEN_BLOCK, N_EXPERTS, LANES), F32),
                   jax.ShapeDtypeStruct((N_EXPERTS, LANES), F32)],
        scratch_shapes=[pltpu.VMEM((N_EXPERTS, LANES), F32)],
        compiler_params=_params("arbitrary", "arbitrary"),
        name="merge_ln1_route",
    )(x, o_na, ret, w_out, w_out, beta_na.reshape(1, 512), gate, ln_g.reshape(1, -1), ln_b.reshape(1, -1),
      shift_f, scale_f, w_hi, w_lo, jnp.broadcast_to(b_router[:, None], (N_EXPERTS, TOKEN_BLOCK)), carry_in)


def _two_group_specs(n_p_blocks, blocks_per_batch_s, ctx_row):
    tok_p = lambda i: (jnp.minimum(i, n_p_blocks - 1), 0)
    tok_s = lambda i: (jnp.maximum(i - n_p_blocks, 0), 0)
    mod = lambda i: (jnp.where(i < n_p_blocks, ctx_row, jnp.maximum(i - n_p_blocks, 0) // blocks_per_batch_s), 0, 0)
    return tok_p, tok_s, mod


def _route_block(h, w_hi, w_lo, bias, carry):
    tm = h.shape[0]
    h_hi = h.astype(BF16)
    h_lo = (h - h_hi.astype(F32)).astype(BF16)
    lg = _dot_nt(jnp.concatenate([w_hi, w_lo], axis=0), h_hi)
    work = lg[:N_EXPERTS] + lg[N_EXPERTS:] + _dot_nt(w_hi, h_lo) + bias
    eidx = lax.broadcasted_iota(I32, (N_EXPERTS, tm), 0).astype(F32)
    vals, idxs, hots = [], [], []
    for _ in range(TOP_K):
        mx = jnp.max(work, axis=0, keepdims=True)
        idx = jnp.min(jnp.where(work == mx, eidx, float(N_EXPERTS)), axis=0, keepdims=True)
        hot = eidx == idx
        vals.append(mx)
        idxs.append(idx)
        hots.append(hot)
        work = jnp.where(hot, -jnp.inf, work)
    exps = [jnp.exp(v - vals[0]) for v in vals]
    den = exps[0] + exps[1] + exps[2] + exps[3]
    sel = jnp.zeros((N_EXPERTS, tm), F32)
    for hot in hots:
        sel = sel + hot.astype(F32)
    r_i = lax.broadcasted_iota(I32, (tm, tm), 0)
    c_i = lax.broadcasted_iota(I32, (tm, tm), 1)
    earlier = (r_i < c_i).astype(BF16)
    before = jnp.dot(sel.astype(BF16), earlier, preferred_element_type=F32)
    n_e = jnp.sum(sel, axis=1, keepdims=True)
    n8 = jnp.floor((n_e + (RUN_ALIGN - 1.0)) * (1.0 / RUN_ALIGN)) * RUN_ALIGN + jnp.zeros((1, LANES), F32)
    e_r = lax.broadcasted_iota(I32, (N_EXPERTS, N_EXPERTS), 0)
    e_c = lax.broadcasted_iota(I32, (N_EXPERTS, N_EXPERTS), 1)
    off = jnp.dot((e_c < e_r).astype(BF16), n8.astype(BF16), preferred_element_type=F32)
    local = before + jnp.concatenate([off] * (tm // LANES), axis=1)
    rows = [jnp.sum(jnp.where(hot, local, 0.0), axis=0, keepdims=True) for hot in hots]
    lane = lax.broadcasted_iota(I32, (N_EXPERTS, LANES), 1)
    table = jnp.where(lane == 0, n8, jnp.where(lane == 1, carry, jnp.where(lane == 2, off, 0.0)))
    return (jnp.concatenate(rows, axis=0).astype(I32), jnp.concatenate([e / den for e in exps], axis=0),
            table, carry + n8)


def _for_pad_pieces(n_rows, visit):
    off = 0
    for size in PAD_PIECES:
        present = (n_rows & size) != 0
        visit(off, size, present)
        off = off + jnp.where(present, size, 0)


def _start_unit_copies(unit_ref, local_buf, sorted_hbm, sem, to_hbm):
    def one(u, carry):
        loc = local_buf.at[pl.ds(pl.multiple_of(u * RUN_ALIGN, RUN_ALIGN), RUN_ALIGN)]
        glob = sorted_hbm.at[pl.ds(pl.multiple_of(unit_ref[u], RUN_ALIGN), RUN_ALIGN)]
        copy = pltpu.make_async_copy(loc, glob, sem) if to_hbm else pltpu.make_async_copy(glob, loc, sem)
        copy.start()
        return carry

    lax.fori_loop(0, LOCAL_UNITS, one, 0, unroll=8)


def _wait_unit_copies(local_buf, sorted_hbm, sem):
    whole = sorted_hbm.at[pl.ds(0, LOCAL_ROWS)]
    pltpu.make_async_copy(whole, local_buf, sem).wait()


def _dispatch_body(unit_ref, zlo_ref, zn_ref, p_ref, yp_ref, ys_ref, sh_ref, sc_ref, xs_ref,
                   xbuf, zbuf, sems, zsem, *, n_p_blocks, n_blocks):
    i = pl.program_id(0)
    slot = i % 2
    y = jnp.where(i < n_p_blocks, yp_ref[...], ys_ref[...])
    h = (y * (1.0 + sc_ref[...]) + sh_ref[...]).astype(BF16)
    rows = p_ref[...]
    r_iota = lax.broadcasted_iota(I32, (LOCAL_ROWS, rows.shape[1]), 0)
    place = r_iota == rows[0:1]
    for kk in range(1, TOP_K):
        place = place | (r_iota == rows[kk:kk + 1])
    xbuf[slot] = _pack_bf16_pairs(
        jnp.dot(jnp.where(place, 1.0, 0.0).astype(BF16), h, preferred_element_type=F32), is_bf16_exact=True)

    _start_unit_copies(unit_ref, xbuf.at[slot], xs_ref, sems.at[slot], to_hbm=True)

    @pl.when(i > 0)
    def _():
        _wait_unit_copies(xbuf.at[1 - slot], xs_ref, sems.at[1 - slot])

    @pl.when(i == n_blocks - 1)
    def _():
        _wait_unit_copies(xbuf.at[slot], xs_ref, sems.at[slot])
        zbuf[...] = jnp.zeros_like(zbuf)
        sem = zsem

        def per_expert(e, carry):
            lo = zlo_ref[e]

            def visit(off, size, present):
                copy = pltpu.make_async_copy(
                    zbuf.at[pl.ds(0, size)], xs_ref.at[pl.ds(pl.multiple_of(lo + off, RUN_ALIGN), size)], sem)

                @pl.when(present)
                def _():
                    copy.start()
                    copy.wait()

            _for_pad_pieces(zn_ref[e], visit)
            return carry

        lax.fori_loop(0, N_EXPERTS, per_expert, 0)

        tail_lo = zlo_ref[N_EXPERTS]

        def tile_copy(t):
            start = pl.multiple_of(tail_lo + t * EXPERT_TILE, EXPERT_TILE)
            return pltpu.make_async_copy(zbuf, xs_ref.at[pl.ds(start, EXPERT_TILE)], sem)

        def t_issue(t, c2):
            tile_copy(t).start()
            return c2

        def t_drain(t, c2):
            tile_copy(0).wait()
            return c2

        n_tail = zn_ref[N_EXPERTS] // EXPERT_TILE
        lax.fori_loop(0, n_tail, t_issue, 0)
        lax.fori_loop(0, n_tail, t_drain, 0)


def _dispatch(yp, ys, shift, scale, ctx_row, s_blocks_per_batch, local_rows, unit_dst, zero_lo, zero_n,
              total_rows, tm):
    n_p, n_s = yp.shape[0], ys.shape[0]
    n = n_p + n_s
    npb = n_p // tm
    nb = n // tm
    tok_p, tok_s, mod = _two_group_specs(npb, s_blocks_per_batch, ctx_row)
    smem_all = pl.BlockSpec(memory_space=pltpu.SMEM)
    return pl.pallas_call(
        functools.partial(_dispatch_body, n_p_blocks=npb, n_blocks=nb),
        grid=(nb,),
        in_specs=[pl.BlockSpec((UNIT_TABLE,), lambda i: (i,), memory_space=pltpu.SMEM), smem_all, smem_all,
                  pl.BlockSpec((TOP_K, tm), lambda i: (0, i)),
                  pl.BlockSpec((tm, D_MODEL), tok_p), pl.BlockSpec((tm, D_MODEL), tok_s),
                  pl.BlockSpec((None, 1, D_MODEL), mod), pl.BlockSpec((None, 1, D_MODEL), mod)],
        out_specs=pl.BlockSpec(memory_space=pl.ANY),
        out_shape=jax.ShapeDtypeStruct((total_rows, D_MODEL // 2), U32),
        scratch_shapes=[pltpu.VMEM((2, LOCAL_ROWS, D_MODEL // 2), U32),
                        pltpu.VMEM((EXPERT_TILE, D_MODEL // 2), U32),
                        pltpu.SemaphoreType.DMA((2,)), pltpu.SemaphoreType.DMA(())],
        compiler_params=_params("arbitrary"),
        name="moe_dispatch",
    )(unit_dst, zero_lo, zero_n, local_rows, yp, ys, shift, scale)


def _expert_body(te_ref, first_ref, half_ref, nu_ref, x_ref, wgu_ref, bgu_ref, wd_ref, bd_ref, y_ref,
                 act_s, wgu_s, wd_s):
    j = pl.program_id(0)

    @pl.when(first_ref[j] == 1)
    def _():
        blk = 2 * LANES
        src = lax.broadcasted_iota(I32, (blk, blk), 0)
        dst = lax.broadcasted_iota(I32, (blk, blk), 1)
        perm = (src == jnp.where(dst < LANES, 2 * dst, 2 * (dst - LANES) + 1)).astype(F32).astype(BF16)
        for s in range(2 * D_FF // blk):
            sl = slice(s * blk, (s + 1) * blk)
            wgu_s[:, sl] = jnp.dot(wgu_ref[:, sl].astype(BF16), perm, preferred_element_type=F32).astype(BF16)
        wd_s[...] = wd_ref[...].astype(BF16)

    def ffn(n_rows):
        rows = slice(0, n_rows)
        x = _unpack_bf16_pairs(x_ref[rows, :])
        for c in range(D_FF // LANES):
            sl = slice(c * 2 * LANES, (c + 1) * 2 * LANES)
            hu = jnp.dot(x, wgu_s[:, sl], preferred_element_type=F32) + bgu_ref[:, sl]
            x_glu = jnp.minimum(hu[:, :LANES], SWIGLU_LIMIT)
            x_lin = jnp.clip(hu[:, LANES:], -SWIGLU_LIMIT, SWIGLU_LIMIT)
            act = x_glu * (1.0 / (1.0 + jnp.exp(-SWIGLU_ALPHA * x_glu))) * (x_lin + 1.0)
            act_s[rows, c * LANES:(c + 1) * LANES] = act.astype(BF16)
        y_ref[rows, :] = _pack_bf16_pairs(
            jnp.dot(act_s[rows, :], wd_s[...], preferred_element_type=F32) + bd_ref[...])

    used = j < nu_ref[0]
    half = half_ref[j] == 1

    @pl.when(used & jnp.logical_not(half))
    def _():
        ffn(EXPERT_TILE)

    @pl.when(used & half)
    def _():
        ffn(EXPERT_TILE // 2)
        y_ref[EXPERT_TILE // 2:, :] = jnp.zeros((EXPERT_TILE // 2, y_ref.shape[1]), y_ref.dtype)

    @pl.when(jnp.logical_not(used))
    def _():
        y_ref[...] = jnp.zeros_like(y_ref)


def _experts(tile_expert, tile_first, tile_half, n_used, xs, cap_rows, w_gate_up, bgu, w_down, bd):
    n_tiles = cap_rows // EXPERT_TILE
    of_expert = lambda j, te, first, half, nu: (te[j], 0, 0)
    grid_spec = pltpu.PrefetchScalarGridSpec(
        num_scalar_prefetch=4,
        grid=(n_tiles,),
        in_specs=[pl.BlockSpec((EXPERT_TILE, D_MODEL // 2),
                               lambda j, te, first, half, nu: (jnp.minimum(j, nu[0] - 1), 0)),
                  pl.BlockSpec((None, D_MODEL, 2 * D_FF), of_expert),
                  pl.BlockSpec((None, 1, 2 * D_FF), of_expert),
                  pl.BlockSpec((None, D_FF, D_MODEL), of_expert),
                  pl.BlockSpec((None, 1, D_MODEL), of_expert)],
        out_specs=pl.BlockSpec((EXPERT_TILE, D_MODEL // 2), lambda j, te, first, half, nu: (j, 0)),
        scratch_shapes=[pltpu.VMEM((EXPERT_TILE, D_FF), BF16),
                        pltpu.VMEM((D_MODEL, 2 * D_FF), BF16), pltpu.VMEM((D_FF, D_MODEL), BF16)],
    )
    return pl.pallas_call(
        _expert_body,
        grid_spec=grid_spec,
        out_shape=jax.ShapeDtypeStruct((cap_rows, D_MODEL // 2), U32),
        compiler_params=_params("arbitrary"),
        name="moe_experts",
    )(tile_expert, tile_first, tile_half, n_used, xs, w_gate_up, bgu, w_down, bd)


def _combine_body(unit_ref, next_unit_ref, y1_ref, p_ref, gt_ref, gate_ref, lg_ref, lb_ref, ys_ref, o_ref,
                  ybuf, sems, *, n_blocks):
    i = pl.program_id(0)
    slot = i % 2
    tm = y1_ref.shape[0]

    @pl.when(i == 0)
    def _():
        _start_unit_copies(unit_ref, ybuf.at[0], ys_ref, sems.at[0], to_hbm=False)

    @pl.when(i + 1 < n_blocks)
    def _():
        _start_unit_copies(next_unit_ref, ybuf.at[1 - slot], ys_ref, sems.at[1 - slot], to_hbm=False)

    _wait_unit_copies(ybuf.at[slot], ys_ref, sems.at[slot])

    rows = p_ref[...]
    gates = gt_ref[...]
    c_iota = lax.broadcasted_iota(I32, (tm, LOCAL_ROWS), 1)
    weight = jnp.zeros((tm, LOCAL_ROWS), F32)
    for kk in range(TOP_K):
        weight = jnp.where(c_iota == rows[:, kk:kk + 1], gates[:, kk:kk + 1], weight)
    w_hi = weight.astype(BF16)
    w_lo = (weight - w_hi.astype(F32)).astype(BF16)
    yb = _unpack_bf16_pairs(ybuf[slot])
    f = (jnp.dot(w_hi, yb, preferred_element_type=F32) + jnp.dot(w_lo, yb, preferred_element_type=F32))
    y = DEEPNORM_ALPHA * y1_ref[...] + gate_ref[...] * f
    o_ref[...] = _standardize(y) * lg_ref[...] + lb_ref[...]


def _combine(y1, local_rows, gates, unit_src, block0, gate_mod, mod_row, blocks_per_batch,
             y_sorted, ln_g, ln_b, tm):
    n_g = y1.shape[0]
    nb = n_g // tm
    row = pl.BlockSpec((1, D_MODEL), lambda i: (0, 0))
    pair = pl.BlockSpec((tm, TOP_K), lambda i: (i + block0, 0))
    units = lambda ahead: pl.BlockSpec(
        (UNIT_TABLE,), lambda i: (jnp.minimum(i + ahead, nb - 1) + block0,), memory_space=pltpu.SMEM)
    return pl.pallas_call(
        functools.partial(_combine_body, n_blocks=nb),
        grid=(nb,),
        in_specs=[units(0), units(1),
                  pl.BlockSpec((tm, D_MODEL), lambda i: (i, 0)), pair, pair,
                  pl.BlockSpec((None, 1, D_MODEL), lambda i: (mod_row(i // blocks_per_batch), 0, 0)),
                  row, row,
                  pl.BlockSpec(memory_space=pl.ANY)],
        out_specs=pl.BlockSpec((tm, D_MODEL), lambda i: (i, 0)),
        out_shape=jax.ShapeDtypeStruct((n_g, D_MODEL), F32),
        scratch_shapes=[pltpu.VMEM((2, LOCAL_ROWS, D_MODEL // 2), U32), pltpu.SemaphoreType.DMA((2,))],
        compiler_params=_params("arbitrary"),
        name="moe_combine_ln2",
    )(unit_src, unit_src, y1, local_rows, gates, gate_mod,
      ln_g.reshape(1, -1), ln_b.reshape(1, -1), y_sorted)


def _regroup_gate_up_bias(b_gate_up):
    e = b_gate_up.shape[0]
    nblk = D_FF // LANES
    return b_gate_up.reshape(e, 1, nblk, LANES, 2).transpose(0, 1, 2, 4, 3).reshape(e, 1, 2 * D_FF)


def _routing_plan(table, totals, n_pairs):
    nb = table.shape[0]
    run_n = table[:, :, 0].astype(I32)
    rows_before = table[:, :, 1].astype(I32)
    run_local = table[:, :, 2].astype(I32)
    cnt = totals[:, 0].astype(I32)
    padded = (cnt + EXPERT_TILE - 1) // EXPERT_TILE * EXPERT_TILE
    pad_end = jnp.cumsum(padded)
    pad_start = pad_end - padded
    run_global = pad_start[None, :] + rows_before
    rows_max = n_pairs + nb * N_EXPERTS * (RUN_ALIGN - 1) + N_EXPERTS * (EXPERT_TILE - 1)
    n_tiles = -(-rows_max // EXPERT_TILE)
    n_used = pad_end[-1] // EXPERT_TILE
    tile_row = jnp.minimum(jnp.arange(n_tiles, dtype=I32), n_used - 1) * EXPERT_TILE
    tile_expert = jnp.sum((pad_end[None, :] <= tile_row[:, None]).astype(I32), axis=1)
    tile_expert = jnp.minimum(tile_expert, N_EXPERTS - 1)
    tile_id = jnp.arange(n_tiles, dtype=I32)
    changed = jnp.concatenate([jnp.ones((1,), bool), tile_expert[1:] != tile_expert[:-1]])
    tile_first = (changed & (tile_id < n_used)).astype(I32)
    exp_hot = tile_expert[:, None] == jnp.arange(N_EXPERTS, dtype=I32)
    rows_end = jnp.sum(jnp.where(exp_hot, (pad_start + cnt)[None, :], 0), axis=-1)
    tile_half = ((rows_end - tile_row) <= EXPERT_TILE // 2).astype(I32)
    cap_rows = n_tiles * EXPERT_TILE
    spare_rows = -(-LOCAL_ROWS // EXPERT_TILE) * EXPERT_TILE
    total_rows = cap_rows + 2 * spare_rows
    unit_row = jnp.arange(LOCAL_UNITS, dtype=I32) * RUN_ALIGN
    run_end = run_local + run_n
    owner = jnp.sum((run_end[:, None, :] <= unit_row[None, :, None]).astype(I32), axis=-1)
    own_hot = owner[:, :, None] == jnp.arange(N_EXPERTS, dtype=I32)
    shift = jnp.sum(jnp.where(own_hot, (run_global - run_local)[:, None, :], 0), axis=-1)
    used = owner < N_EXPERTS
    sorted_row = unit_row[None, :] + shift
    slot = (jnp.arange(nb, dtype=I32) % 2)[:, None]
    unit_dst = jnp.where(used, sorted_row, cap_rows + slot * spare_rows + unit_row[None, :])
    unit_src = jnp.where(used, sorted_row, unit_row[None, :])
    table_of = lambda a: jnp.pad(a, ((0, 0), (0, UNIT_TABLE - LOCAL_UNITS))).reshape(-1)
    zero_lo = jnp.concatenate([pad_start + cnt, pad_end[-1:]])
    zero_n = jnp.concatenate([padded - cnt, total_rows - pad_end[-1:]])
    return ((table_of(unit_dst), table_of(unit_src)), (tile_expert, tile_first, tile_half, n_used.reshape(1)),
            zero_lo, zero_n, cap_rows, total_rows)


def kernel(x_prompt, x_sample, cache_na_k, cache_na_v, state_ret_fwd, state_ret_bwd, c, c_ctx, w_ada, b_ada, w_in, rpb, ret_decay_fwd, ret_decay_bwd, beta_na, beta_ret, w_out, ln1_g, ln1_b, w_router, b_router, w_gate_up, b_gate_up, w_down, b_down, ln2_g, ln2_b):
    bp, tp, _ = x_prompt.shape
    bs, ts, _ = x_sample.shape
    assert w_ada.shape[0] == DEPTH == 1
    l = 0
    ctx_row = bs
    sample_row = lambda bi: bi
    prompt_row = lambda bi: ctx_row

    cond = jnp.concatenate([c, c_ctx[None, :], jnp.zeros((16 - bs - 1, D_MODEL), F32)], axis=0)
    m = _adaln(cond, w_ada[l], b_ada[l]).reshape(16, 6, 1, D_MODEL)
    sh_a, sc_a, gt_a, sh_f, sc_f, gt_f = (m[:, i] for i in range(6))

    w_in_b = w_in[l].astype(BF16)
    w_out_b = w_out[l].astype(BF16)

    q, k_p, v_p, qr, kr, vr, g = _project(x_prompt, sh_a, sc_a, prompt_row, w_in_b, tp, F32, rope=False)
    o_na = _context_attention(q, k_p, v_p)
    zeros_state = jnp.zeros((bp, RET_HEADS, RET_DK, RET_DK), F32)
    ret, s_f, s_b = _retention(qr, kr, vr, g, ret_decay_fwd[l], ret_decay_bwd[l], beta_ret[l],
                               zeros_state, zeros_state, heads_per_step=RET_HEADS)
    w_r = w_router[l].T
    w_r_hi = w_r.astype(BF16)
    w_r_lo = (w_r - w_r_hi.astype(F32)).astype(BF16)
    route = functools.partial(_merge_and_route, w_out=w_out_b, beta_na=beta_na[l], gate=gt_a, shift_f=sh_f,
                              scale_f=sc_f, ln_g=ln1_g[l], ln_b=ln1_b[l], w_hi=w_r_hi, w_lo=w_r_lo,
                              b_router=b_router[l])
    yp1, rows_p, gates_p, table_p, totals_p = route(
        x_prompt, o_na, ret, mod_row=prompt_row, carry_in=jnp.zeros((N_EXPERTS, LANES), F32), tm=tp)

    q, k_s, v_s, qr, kr, vr, g = _project(x_sample, sh_a, sc_a, sample_row, w_in_b, SAMPLE_TILE, BF16, rope=True)
    lc = cache_na_k.shape[2]
    o_na = _neighbourhood_attention(q, k_s, v_s, cache_na_k[:, l].reshape(bs, lc, NA_WIDTH),
                                    cache_na_v[:, l].reshape(bs, lc, NA_WIDTH), _na_bias(rpb[l]))
    ret, _, _ = _retention(qr, kr, vr, g, ret_decay_fwd[l], ret_decay_bwd[l], beta_ret[l],
                           state_ret_fwd[:, l], state_ret_bwd[:, l], heads_per_step=1)
    ys1, rows_s, gates_s, table_s, totals = route(
        x_sample, o_na, ret, mod_row=sample_row, carry_in=totals_p, tm=SAMPLE_TILE)

    tm = TOKEN_BLOCK
    yp1f = yp1.reshape(bp * tp, D_MODEL)
    ys1f = ys1.reshape(bs * ts, D_MODEL)
    local_rows = jnp.concatenate([rows_p, rows_s], axis=1)
    gates = jnp.concatenate([gates_p, gates_s], axis=1)
    table = jnp.concatenate([table_p, table_s], axis=0)
    (unit_dst, unit_src), tiles, zero_lo, zero_n, cap_rows, total_rows = _routing_plan(
        table, totals, (bp * tp + bs * ts) * TOP_K)
    xs = _dispatch(yp1f, ys1f, sh_f, sc_f, ctx_row, ts // tm, local_rows, unit_dst, zero_lo, zero_n,
                   total_rows, tm)
    y_sorted = _experts(*tiles, xs, cap_rows, w_gate_up[l], _regroup_gate_up_bias(b_gate_up[l]), w_down[l],
                        b_down[l][:, None, :])
    rows_t, gates_t = local_rows.T, gates.T
    yp = _combine(yp1f, rows_t, gates_t, unit_src, 0, gt_f, prompt_row, 1, y_sorted, ln2_g[l], ln2_b[l], tm)
    ys = _combine(ys1f, rows_t, gates_t, unit_src, (bp * tp) // tm, gt_f, sample_row, ts // tm, y_sorted,
                  ln2_g[l], ln2_b[l], tm)

    return (yp.reshape(bp, tp, D_MODEL), ys.reshape(bs, ts, D_MODEL),
            k_p.reshape(bp, 1, tp, NA_HEADS, NA_DIM), v_p.reshape(bp, 1, tp, NA_HEADS, NA_DIM),
            s_f[:, None], s_b[:, None])
```

```python
import functools

import jax
import jax.numpy as jnp
from jax import lax
from jax.experimental import pallas as pl
from jax.experimental.pallas import tpu as pltpu

F32 = jnp.float32
BF16 = jnp.bfloat16
I32 = jnp.int32
U32 = jnp.uint32

D_MODEL = 1024
GRID_W = 64
NA_HEADS = 8
NA_DIM = 64
NA_WIDTH = NA_HEADS * NA_DIM
NA_KH = 8
NA_KW = 16
RET_HEADS = 4
RET_DK = 128
RET_WIDTH = RET_HEADS * RET_DK
CHUNK = 128
N_EXPERTS = 32
TOP_K = 4
D_FF = 1024
SWIGLU_LIMIT = 7.0
SWIGLU_ALPHA = 1.702
ROPE_BASE = 10000.0
LN_EPS = 1e-5
DEPTH = 1
DEEPNORM_ALPHA = (2.0 * DEPTH) ** 0.25

LANES = 128
NEG_BIG = -1e30
TOKEN_BLOCK = 256
SAMPLE_TILE = 1024
EXPERT_TILE = 1024
VMEM_LIMIT = 56 * 1024 * 1024
RET_CHUNK_UNROLL = 16
SUBLANES = 8
RUN_ALIGN = SUBLANES
LOCAL_ROWS = TOKEN_BLOCK * TOP_K + N_EXPERTS * RUN_ALIGN
LOCAL_UNITS = LOCAL_ROWS // RUN_ALIGN
UNIT_TABLE = 256
PAD_PIECES = tuple(EXPERT_TILE >> s for s in range(1, EXPERT_TILE.bit_length())
                   if (EXPERT_TILE >> s) >= RUN_ALIGN)


def _params(*sem):
    return pltpu.CompilerParams(dimension_semantics=sem, vmem_limit_bytes=VMEM_LIMIT)


def _silu(x):
    return x / (1.0 + jnp.exp(-x))


def _standardize(x):
    mu = jnp.mean(x, axis=-1, keepdims=True)
    xc = x - mu
    var = jnp.mean(xc * xc, axis=-1, keepdims=True)
    return xc * lax.rsqrt(var + LN_EPS)


def _pack_bf16_pairs(x, is_bf16_exact=False):
    if not is_bf16_exact:
        x = x.astype(BF16).astype(F32)
    bits = lax.bitcast_convert_type(x, U32)
    half = x.shape[1] // 2
    return bits[:, :half] | lax.shift_right_logical(bits[:, half:], jnp.uint32(16))


def _unpack_bf16_pairs(p):
    hi = lax.bitcast_convert_type(p & jnp.uint32(0xFFFF0000), F32)
    lo = lax.bitcast_convert_type(lax.shift_left(p, jnp.uint32(16)), F32)
    return jnp.concatenate([hi, lo], axis=1).astype(BF16)


def _dot_nt(a, b):
    return lax.dot_general(a, b, (((1,), (1,)), ((), ())), preferred_element_type=F32)


def _dot_tn(a, b):
    return lax.dot_general(a, b, (((0,), (0,)), ((), ())), preferred_element_type=F32)


def _ada_body(c_ref, w_ref, b_ref, o_ref):
    s = _silu(c_ref[...])
    o_ref[...] = jnp.dot(s, w_ref[...], preferred_element_type=F32,
                         precision=lax.Precision.HIGHEST) + b_ref[...]


def _adaln(cond, w_ada, b_ada):
    r = cond.shape[0]
    n = w_ada.shape[1]
    tn = 1536
    return pl.pallas_call(
        _ada_body,
        grid=(n // tn,),
        in_specs=[pl.BlockSpec((r, D_MODEL), lambda j: (0, 0)),
                  pl.BlockSpec((D_MODEL, tn), lambda j: (0, j)),
                  pl.BlockSpec((1, tn), lambda j: (0, j))],
        out_specs=pl.BlockSpec((r, tn), lambda j: (0, j)),
        out_shape=jax.ShapeDtypeStruct((r, n), F32),
        compiler_params=_params("arbitrary"),
        name="adaln",
    )(cond, w_ada, b_ada.reshape(1, n))


def _proj_body(*refs, rope):
    if rope:
        (x_ref, sh_ref, sc_ref, w_ref, cos_ref, sa_ref, sb_ref,
         q_ref, k_ref, v_ref, qr_ref, kr_ref, vr_ref, g_ref) = refs
    else:
        (x_ref, sh_ref, sc_ref, w_ref,
         q_ref, k_ref, v_ref, qr_ref, kr_ref, vr_ref, g_ref) = refs
    h = (x_ref[...] * (1.0 + sc_ref[...]) + sh_ref[...]).astype(BF16)

    def cols(c):
        return jnp.dot(h, w_ref[:, c * 512:(c + 1) * 512], preferred_element_type=F32)

    q_ref[...] = cols(0).astype(q_ref.dtype)
    k_ref[...] = cols(1).astype(k_ref.dtype)
    v_ref[...] = cols(2).astype(v_ref.dtype)
    pq = cols(3)
    pk = cols(4) * (RET_DK ** -0.5)
    if rope:
        cs, sa, sb = cos_ref[...], sa_ref[...], sb_ref[...]
        for hd in range(RET_HEADS):
            sl = slice(hd * LANES, (hd + 1) * LANES)
            for p, o_ref in ((pq, qr_ref), (pk, kr_ref)):
                xs = p[:, sl]
                rot = xs * cs + pltpu.roll(xs, 96, 1) * sa + pltpu.roll(xs, 32, 1) * sb
                o_ref[:, sl] = rot.astype(o_ref.dtype)
    else:
        qr_ref[...] = pq.astype(qr_ref.dtype)
        kr_ref[...] = pk.astype(kr_ref.dtype)
    vr_ref[...] = cols(5).astype(vr_ref.dtype)
    g_ref[...] = cols(6)


def _rope_tables(t_len):
    t = jnp.arange(t_len, dtype=jnp.int32)
    pos_row = (t // GRID_W).astype(F32)[:, None]
    pos_col = (t % GRID_W).astype(F32)[:, None]
    lane = jnp.arange(LANES, dtype=jnp.int32)[None, :]
    n_freq = RET_DK // 4
    inv_freq = ROPE_BASE ** (-(lane % n_freq).astype(F32) / n_freq)
    ang = jnp.where(lane < RET_DK // 2, pos_row, pos_col) * inv_freq
    first = (lane % (2 * n_freq)) < n_freq
    cos = jnp.cos(ang)
    sin = jnp.sin(ang)
    return cos, jnp.where(first, -sin, 0.0), jnp.where(first, 0.0, sin)


def _project(x, shift, scale, mod_row, w_in, tm, kv_dtype, rope):
    b, t, _ = x.shape
    tok = lambda bi, ti: (bi, ti, 0)
    mod = lambda bi, ti: (mod_row(bi), 0, 0)
    in_specs = [pl.BlockSpec((None, tm, D_MODEL), tok),
                pl.BlockSpec((None, 1, D_MODEL), mod),
                pl.BlockSpec((None, 1, D_MODEL), mod),
                pl.BlockSpec(w_in.shape, lambda bi, ti: (0, 0))]
    args = [x, shift, scale, w_in]
    if rope:
        tab = lambda bi, ti: (ti, 0)
        in_specs += [pl.BlockSpec((tm, LANES), tab)] * 3
        args += list(_rope_tables(t))
    o512 = pl.BlockSpec((None, tm, 512), tok)
    shp = lambda dt: jax.ShapeDtypeStruct((b, t, 512), dt)
    return pl.pallas_call(
        functools.partial(_proj_body, rope=rope),
        grid=(b, t // tm),
        in_specs=in_specs,
        out_specs=[o512] * 7,
        out_shape=[shp(BF16), shp(kv_dtype), shp(kv_dtype), shp(BF16), shp(BF16), shp(BF16), shp(F32)],
        compiler_params=_params("parallel", "parallel"),
        name="in_proj",
    )(*args)


def _pair_masks(shape):
    lane = lax.broadcasted_iota(I32, shape, 1)
    return lane < NA_DIM, lane >= NA_DIM


def _ctx_attn_body(q_ref, k_ref, v_ref, o_ref):
    t = q_ref.shape[0]
    lo, hi = _pair_masks((t, LANES))
    for hp in range(NA_WIDTH // LANES):
        sl = slice(hp * LANES, (hp + 1) * LANES)
        q2 = q_ref[:, sl] * (NA_DIM ** -0.5)
        k2 = k_ref[:, sl].astype(BF16)
        v2 = v_ref[:, sl].astype(BF16)
        zero = jnp.zeros_like(q2)
        qs = jnp.concatenate([jnp.where(lo, q2, zero), jnp.where(hi, q2, zero)], axis=0)
        s = _dot_nt(qs, k2)
        e = jnp.exp(s - jnp.max(s, axis=-1, keepdims=True))
        den = jnp.sum(e, axis=-1, keepdims=True)
        o = jnp.dot(e.astype(BF16), v2, preferred_element_type=F32) / den
        o_ref[:, sl] = jnp.where(lo, o[:t], o[t:]).astype(o_ref.dtype)


def _context_attention(q, k, v):
    b, t, _ = q.shape
    spec = pl.BlockSpec((None, t, NA_WIDTH), lambda bi: (bi, 0, 0))
    return pl.pallas_call(
        _ctx_attn_body,
        grid=(b,),
        in_specs=[spec] * 3,
        out_specs=spec,
        out_shape=jax.ShapeDtypeStruct((b, t, NA_WIDTH), BF16),
        compiler_params=_params("parallel"),
        name="ctx_attn",
    )(q, k, v)


def _na_bias_body(rpb_ref, o_ref):
    h = pl.program_id(0)
    v = pl.program_id(1)
    shape = (GRID_W, LANES)
    w = lax.broadcasted_iota(I32, shape, 0)
    lane = lax.broadcasted_iota(I32, shape, 1)
    upper = lane >= GRID_W
    kc = jnp.where(upper, lane - GRID_W, lane)
    cdiff = kc - w + (NA_KW - 1)
    cstart = jnp.clip(w - NA_KW // 2, 0, GRID_W - NA_KW)
    inwin = (kc >= cstart) & (kc < cstart + NA_KW)
    n_ro = 2 * NA_KH - 1
    n_co = 2 * NA_KW - 1
    for j in range(NA_KH // 2):
        ro = 2 * j - v + (NA_KH - 1)
        base0 = (h * n_ro + ro) * n_co
        acc = jnp.zeros(shape, F32)
        for c in range(n_co):
            val = jnp.where(upper, rpb_ref[base0 + n_co + c], rpb_ref[base0 + c])
            acc = jnp.where(cdiff == c, val, acc)
        o_ref[:, j * LANES:(j + 1) * LANES] = jnp.where(inwin, acc, NEG_BIG)


def _na_bias(rpb):
    n = NA_KH * GRID_W
    return pl.pallas_call(
        _na_bias_body,
        grid=(NA_HEADS, NA_KH),
        in_specs=[pl.BlockSpec(memory_space=pltpu.SMEM)],
        out_specs=pl.BlockSpec((None, None, GRID_W, n), lambda h, v: (h, v, 0, 0)),
        out_shape=jax.ShapeDtypeStruct((NA_HEADS, NA_KH, GRID_W, n), F32),
        compiler_params=_params("parallel", "parallel"),
        name="na_bias",
    )(rpb.reshape(-1))


def _na_body(q_ref, k_ref, v_ref, ck_ref, cv_ref, bias_ref, o_ref, s_buf, e_buf, d_buf, *, rows):
    ck = ck_ref[...].astype(BF16)
    cv = cv_ref[...].astype(BF16)
    lo, hi = _pair_masks((GRID_W, LANES))
    scale = NA_DIM ** -0.5
    n_loc = NA_KH * GRID_W

    def window(r):
        r = jnp.clip(r, 0, rows - 1)
        r_start = jnp.clip(r - NA_KH // 2, 0, rows - NA_KH)
        return (r - r_start, pl.ds(pl.multiple_of(r * GRID_W, GRID_W), GRID_W),
                pl.ds(pl.multiple_of(r_start * GRID_W, GRID_W), n_loc))

    def scores(r, slot):
        vidx, qsl, ksl = window(r)
        q2 = q_ref[qsl, :]
        zero = jnp.zeros_like(q2)
        q2 = q2 * scale
        qs = jnp.concatenate([jnp.where(lo, q2, zero), jnp.where(hi, q2, zero)], axis=0)
        bias = jnp.concatenate([bias_ref[0, vidx], bias_ref[1, vidx]], axis=0)
        s_buf[slot, :, :n_loc] = _dot_nt(qs, k_ref[ksl, :]) + bias
        s_buf[slot, :, n_loc:] = _dot_nt(qs, ck)

    def numerators(slot):
        s = s_buf[slot]
        e = jnp.exp(s - jnp.max(s, axis=-1, keepdims=True))
        d_buf[slot] = jnp.broadcast_to(jnp.sum(e, axis=-1, keepdims=True), d_buf.shape[1:])
        e_buf[slot] = e.astype(BF16)

    def output(r, slot):
        _, qsl, ksl = window(r)
        o = (jnp.dot(e_buf[slot, :, :n_loc], v_ref[ksl, :], preferred_element_type=F32)
             + jnp.dot(e_buf[slot, :, n_loc:], cv, preferred_element_type=F32)) / d_buf[slot]
        o_ref[qsl, :] = jnp.where(lo, o[:GRID_W], o[GRID_W:]).astype(o_ref.dtype)

    s_buf[...] = jnp.zeros_like(s_buf)
    e_buf[...] = jnp.zeros_like(e_buf)
    d_buf[...] = jnp.ones_like(d_buf)

    def two_steps(j, carry):
        r = 2 * j
        scores(r, 0)
        numerators(1)
        output(r - 2, 0)
        scores(r + 1, 1)
        numerators(0)
        output(r - 1, 1)
        return carry

    lax.fori_loop(0, rows // 2 + 1, two_steps, 0, unroll=3)


def _neighbourhood_attention(q, k, v, ck, cv, layer, bias):
    b, t, _ = q.shape
    lc = ck.shape[2]
    rows = t // GRID_W
    tok = pl.BlockSpec((None, t, LANES), lambda hp, bi: (bi, 0, hp))
    ctx = pl.BlockSpec((None, None, lc, LANES), lambda hp, bi: (bi, layer, 0, hp))
    return pl.pallas_call(
        functools.partial(_na_body, rows=rows),
        grid=(NA_WIDTH // LANES, b),
        in_specs=[tok, tok, tok, ctx, ctx,
                  pl.BlockSpec((2, NA_KH, GRID_W, NA_KH * GRID_W), lambda hp, bi: (hp, 0, 0, 0))],
        out_specs=tok,
        out_shape=jax.ShapeDtypeStruct((b, t, NA_WIDTH), BF16),
        scratch_shapes=[pltpu.VMEM((2, 2 * GRID_W, NA_KH * GRID_W + lc), F32),
                        pltpu.VMEM((2, 2 * GRID_W, NA_KH * GRID_W + lc), BF16),
                        pltpu.VMEM((2, 2 * GRID_W, LANES), F32)],
        compiler_params=_params("parallel", "parallel"),
        name="na_attn",
    )(q, k, v, ck, cv, bias)


def _ret_body(q_ref, k_ref, v_ref, g_ref, df_ref, db_ref, beta_ref, s0f_ref, s0b_ref,
              o_ref, sf_ref, sb_ref, kvf_s, kvb_s, *, n_chunks, heads):
    for hh in range(heads):
        slab = lambda ref: ref.at[:, hh * LANES:(hh + 1) * LANES]
        _ret_head(slab(q_ref), slab(k_ref), slab(v_ref), slab(g_ref), df_ref.at[hh], db_ref.at[hh],
                  beta_ref.at[hh], s0f_ref.at[hh], s0b_ref.at[hh], slab(o_ref), sf_ref.at[hh], sb_ref.at[hh],
                  kvf_s, kvb_s, n_chunks=n_chunks)


def _ret_head(q_ref, k_ref, v_ref, g_ref, df_ref, db_ref, beta_ref, s0f_ref, s0b_ref,
              o_ref, sf_ref, sb_ref, kvf_s, kvb_s, *, n_chunks):
    c_len = CHUNK
    lgf = -jnp.log1p(jnp.exp(-df_ref[...]))
    lgb = -jnp.log1p(jnp.exp(-db_ref[...]))
    shape = (c_len, c_len)
    i = lax.broadcasted_iota(I32, shape, 0).astype(F32)
    j = lax.broadcasted_iota(I32, shape, 1).astype(F32)
    dij = i - j
    d_comb = (jnp.where(dij >= 0, jnp.exp(jnp.maximum(dij, 0.0) * lgf), 0.0)
              + jnp.where(dij <= 0, jnp.exp(jnp.maximum(-dij, 0.0) * lgb), 0.0))
    xi_f = jnp.exp((i + 1.0) * lgf)
    xi_b = jnp.exp((c_len - i) * lgb)
    zeta_f = jnp.exp((c_len - 1.0 - i) * lgf)
    zeta_b = jnp.exp(i * lgb)
    g_f = jnp.exp(c_len * lgf)
    g_b = jnp.exp(c_len * lgb)

    def chunk(c):
        return pl.ds(pl.multiple_of(c * c_len, c_len), c_len)

    def kv_pass(c, carry):
        kc = k_ref[chunk(c), :]
        vc = v_ref[chunk(c), :].astype(F32)
        kvf_s[c] = _dot_tn(kc, (vc * zeta_f).astype(BF16))
        kvb_s[c] = _dot_tn(kc, (vc * zeta_b).astype(BF16))
        return carry

    lax.fori_loop(0, n_chunks, kv_pass, 0, unroll=RET_CHUNK_UNROLL)

    def scan_f(c, s):
        kv = kvf_s[c]
        kvf_s[c] = s
        return g_f * s + kv

    def scan_b(ci, s):
        c = n_chunks - 1 - ci
        kv = kvb_s[c]
        kvb_s[c] = s
        return g_b * s + kv

    sf_ref[...] = lax.fori_loop(0, n_chunks, scan_f, s0f_ref[...])
    sb_ref[...] = lax.fori_loop(0, n_chunks, scan_b, s0b_ref[...])

    beta = beta_ref[...]

    def out_pass(c, carry):
        qc = q_ref[chunk(c), :]
        kc = k_ref[chunk(c), :]
        vc = v_ref[chunk(c), :]
        scores = _dot_nt(qc, kc) * d_comb
        y = jnp.dot(scores.astype(BF16), vc, preferred_element_type=F32)
        qf = qc.astype(F32)
        y += jnp.dot((qf * xi_f).astype(BF16), kvf_s[c].astype(BF16), preferred_element_type=F32)
        y += jnp.dot((qf * xi_b).astype(BF16), kvb_s[c].astype(BF16), preferred_element_type=F32)
        o_ref[chunk(c), :] = (_standardize(y) * beta * _silu(g_ref[chunk(c), :])).astype(o_ref.dtype)
        return carry

    lax.fori_loop(0, n_chunks, out_pass, 0, unroll=RET_CHUNK_UNROLL)


def _retention(q, k, v, g, decay_f, decay_b, beta_ret, s0f, s0b, layer, heads_per_step):
    b, t, _ = q.shape
    n_chunks = t // CHUNK
    hs = heads_per_step
    tok = pl.BlockSpec((None, t, hs * LANES), lambda bi, h: (bi, 0, h))
    per_head = pl.BlockSpec((hs, 1, LANES), lambda bi, h: (h, 0, 0))
    state = pl.BlockSpec((None, hs, RET_DK, RET_DK), lambda bi, h: (bi, h, 0, 0))
    state0 = pl.BlockSpec((None, None, hs, RET_DK, RET_DK), lambda bi, h: (bi, layer, h, 0, 0))
    lanes = lambda a: jnp.broadcast_to(a.astype(F32)[:, None, None], (RET_HEADS, 1, LANES))
    st_shape = jax.ShapeDtypeStruct((b, RET_HEADS, RET_DK, RET_DK), F32)
    return pl.pallas_call(
        functools.partial(_ret_body, n_chunks=n_chunks, heads=hs),
        grid=(b, RET_HEADS // hs),
        in_specs=[tok, tok, tok, tok, per_head, per_head, per_head, state0, state0],
        out_specs=[tok, state, state],
        out_shape=[jax.ShapeDtypeStruct((b, t, RET_WIDTH), BF16), st_shape, st_shape],
        scratch_shapes=[pltpu.VMEM((n_chunks, RET_DK, RET_DK), F32)] * 2,
        compiler_params=_params("parallel", "parallel"),
        name="retention",
    )(q, k, v, g, lanes(decay_f), lanes(decay_b), beta_ret.reshape(RET_HEADS, 1, LANES), s0f, s0b)


def _merge_body(x_ref, ona_ref, ret_ref, wna_ref, wret_ref, bna_ref, gate_ref, lg_ref, lb_ref,
                shf_ref, scf_ref, wh_ref, wl_ref, br_ref, carry_in_ref,
                o_ref, p_ref, gt_ref, tab_ref, cnt_ref, carry_s):
    @pl.when((pl.program_id(0) == 0) & (pl.program_id(1) == 0))
    def _():
        carry_s[...] = carry_in_ref[...]

    o = ona_ref[...].astype(F32)
    na = o * lax.rsqrt(jnp.mean(o * o, axis=-1, keepdims=True) + LN_EPS) * bna_ref[...]
    mix = (jnp.dot(na.astype(BF16), wna_ref[...], preferred_element_type=F32)
           + jnp.dot(ret_ref[...], wret_ref[...], preferred_element_type=F32))
    y = DEEPNORM_ALPHA * x_ref[...] + gate_ref[...] * mix
    y = _standardize(y) * lg_ref[...] + lb_ref[...]
    o_ref[...] = y

    h = y * (1.0 + scf_ref[...]) + shf_ref[...]
    w_hi, w_lo, bias = wh_ref[...], wl_ref[...], br_ref[...]
    carry = carry_s[...]
    for s in range(y.shape[0] // TOKEN_BLOCK):
        sl = slice(s * TOKEN_BLOCK, (s + 1) * TOKEN_BLOCK)
        rows, gates, table, carry = _route_block(h[sl], w_hi, w_lo, bias, carry)
        p_ref[:, sl] = rows
        gt_ref[:, sl] = gates
        tab_ref[s] = table
    carry_s[...] = carry
    cnt_ref[...] = carry


def _merge_and_route(x, o_na, ret, w_out, beta_na, gate, shift_f, scale_f, mod_row, ln_g, ln_b,
                     w_hi, w_lo, b_router, carry_in, tm):
    b, t, _ = x.shape
    nt = t // tm
    sub = tm // TOKEN_BLOCK
    tok = lambda bi, ti: (bi, ti, 0)
    flat = lambda bi, ti: (0, bi * nt + ti)
    mod = lambda bi, ti: (mod_row(bi), 0, 0)
    whole = lambda shape: pl.BlockSpec(shape, lambda bi, ti: (0,) * len(shape))
    half = pl.BlockSpec((None, tm, 512), tok)
    mod_spec = pl.BlockSpec((None, 1, D_MODEL), mod)
    return pl.pallas_call(
        _merge_body,
        grid=(b, nt),
        in_specs=[pl.BlockSpec((None, tm, D_MODEL), tok), half, half,
                  pl.BlockSpec((512, D_MODEL), lambda bi, ti: (0, 0)),
                  pl.BlockSpec((512, D_MODEL), lambda bi, ti: (1, 0)),
                  whole((1, 512)), mod_spec, whole((1, D_MODEL)), whole((1, D_MODEL)),
                  mod_spec, mod_spec,
                  whole((N_EXPERTS, D_MODEL)), whole((N_EXPERTS, D_MODEL)), whole((N_EXPERTS, TOKEN_BLOCK)),
                  whole((N_EXPERTS, LANES))],
        out_specs=[pl.BlockSpec((None, tm, D_MODEL), tok),
                   pl.BlockSpec((TOP_K, tm), flat), pl.BlockSpec((TOP_K, tm), flat),
                   pl.BlockSpec((sub, N_EXPERTS, LANES), lambda bi, ti: (bi * nt + ti, 0, 0)),
                   whole((N_EXPERTS, LANES))],
        out_shape=[jax.ShapeDtypeStruct(x.shape, F32),
                   jax.ShapeDtypeStruct((TOP_K, b * t), I32), jax.ShapeDtypeStruct((TOP_K, b * t), F32),
                   jax.ShapeDtypeStruct((b * t // TOKEN_BLOCK, N_EXPERTS, LANES), F32),
                   jax.ShapeDtypeStruct((N_EXPERTS, LANES), F32)],
        scratch_shapes=[pltpu.VMEM((N_EXPERTS, LANES), F32)],
        compiler_params=_params("arbitrary", "arbitrary"),
        name="merge_ln1_route",
    )(x, o_na, ret, w_out, w_out, beta_na.reshape(1, 512), gate, ln_g.reshape(1, -1), ln_b.reshape(1, -1),
      shift_f, scale_f, w_hi, w_lo, jnp.broadcast_to(b_router[:, None], (N_EXPERTS, TOKEN_BLOCK)), carry_in)


def _two_group_specs(n_p_blocks, blocks_per_batch_s, ctx_row):
    tok_p = lambda i: (jnp.minimum(i, n_p_blocks - 1), 0)
    tok_s = lambda i: (jnp.maximum(i - n_p_blocks, 0), 0)
    mod = lambda i: (jnp.where(i < n_p_blocks, ctx_row, jnp.maximum(i - n_p_blocks, 0) // blocks_per_batch_s), 0, 0)
    return tok_p, tok_s, mod


def _route_block(h, w_hi, w_lo, bias, carry):
    tm = h.shape[0]
    h_hi = h.astype(BF16)
    h_lo = (h - h_hi.astype(F32)).astype(BF16)
    lg = _dot_nt(jnp.concatenate([w_hi, w_lo], axis=0), h_hi)
    work = lg[:N_EXPERTS] + lg[N_EXPERTS:] + _dot_nt(w_hi, h_lo) + bias
    eidx = lax.broadcasted_iota(I32, (N_EXPERTS, tm), 0).astype(F32)
    vals, idxs, hots = [], [], []
    for _ in range(TOP_K):
        mx = jnp.max(work, axis=0, keepdims=True)
        idx = jnp.min(jnp.where(work == mx, eidx, float(N_EXPERTS)), axis=0, keepdims=True)
        hot = eidx == idx
        vals.append(mx)
        idxs.append(idx)
        hots.append(hot)
        work = jnp.where(hot, -jnp.inf, work)
    exps = [jnp.exp(v - vals[0]) for v in vals]
    den = exps[0] + exps[1] + exps[2] + exps[3]
    sel = jnp.zeros((N_EXPERTS, tm), F32)
    for hot in hots:
        sel = sel + hot.astype(F32)
    r_i = lax.broadcasted_iota(I32, (tm, tm), 0)
    c_i = lax.broadcasted_iota(I32, (tm, tm), 1)
    earlier = (r_i < c_i).astype(BF16)
    before = jnp.dot(sel.astype(BF16), earlier, preferred_element_type=F32)
    n_e = jnp.sum(sel, axis=1, keepdims=True)
    n8 = jnp.floor((n_e + (RUN_ALIGN - 1.0)) * (1.0 / RUN_ALIGN)) * RUN_ALIGN + jnp.zeros((1, LANES), F32)
    e_r = lax.broadcasted_iota(I32, (N_EXPERTS, N_EXPERTS), 0)
    e_c = lax.broadcasted_iota(I32, (N_EXPERTS, N_EXPERTS), 1)
    off = jnp.dot((e_c < e_r).astype(BF16), n8.astype(BF16), preferred_element_type=F32)
    local = before + jnp.concatenate([off] * (tm // LANES), axis=1)
    rows = [jnp.sum(jnp.where(hot, local, 0.0), axis=0, keepdims=True) for hot in hots]
    lane = lax.broadcasted_iota(I32, (N_EXPERTS, LANES), 1)
    table = jnp.where(lane == 0, n8, jnp.where(lane == 1, carry, jnp.where(lane == 2, off, 0.0)))
    return (jnp.concatenate(rows, axis=0).astype(I32), jnp.concatenate([e / den for e in exps], axis=0),
            table, carry + n8)


def _for_pad_pieces(n_rows, visit):
    off = 0
    for size in PAD_PIECES:
        present = (n_rows & size) != 0
        visit(off, size, present)
        off = off + jnp.where(present, size, 0)


def _start_unit_copies(unit_ref, local_buf, sorted_hbm, sem, to_hbm):
    def one(u, carry):
        loc = local_buf.at[pl.ds(pl.multiple_of(u * RUN_ALIGN, RUN_ALIGN), RUN_ALIGN)]
        glob = sorted_hbm.at[pl.ds(pl.multiple_of(unit_ref[u], RUN_ALIGN), RUN_ALIGN)]
        copy = pltpu.make_async_copy(loc, glob, sem) if to_hbm else pltpu.make_async_copy(glob, loc, sem)
        copy.start()
        return carry

    lax.fori_loop(0, LOCAL_UNITS, one, 0, unroll=8)


def _wait_unit_copies(local_buf, sorted_hbm, sem):
    whole = sorted_hbm.at[pl.ds(0, LOCAL_ROWS)]
    pltpu.make_async_copy(whole, local_buf, sem).wait()


def _dispatch_body(unit_ref, zlo_ref, zn_ref, p_ref, yp_ref, ys_ref, sh_ref, sc_ref, xs_ref,
                   xbuf, zbuf, sems, zsem, *, n_p_blocks, n_blocks):
    i = pl.program_id(0)
    slot = i % 2
    y = jnp.where(i < n_p_blocks, yp_ref[...], ys_ref[...])
    h = (y * (1.0 + sc_ref[...]) + sh_ref[...]).astype(BF16)
    rows = p_ref[...]
    r_iota = lax.broadcasted_iota(I32, (LOCAL_ROWS, rows.shape[1]), 0)
    place = r_iota == rows[0:1]
    for kk in range(1, TOP_K):
        place = place | (r_iota == rows[kk:kk + 1])
    xbuf[slot] = _pack_bf16_pairs(
        jnp.dot(jnp.where(place, 1.0, 0.0).astype(BF16), h, preferred_element_type=F32), is_bf16_exact=True)

    _start_unit_copies(unit_ref, xbuf.at[slot], xs_ref, sems.at[slot], to_hbm=True)

    @pl.when(i > 0)
    def _():
        _wait_unit_copies(xbuf.at[1 - slot], xs_ref, sems.at[1 - slot])

    @pl.when(i == n_blocks - 1)
    def _():
        _wait_unit_copies(xbuf.at[slot], xs_ref, sems.at[slot])
        zbuf[...] = jnp.zeros_like(zbuf)
        sem = zsem

        def per_expert(e, carry):
            lo = zlo_ref[e]

            def visit(off, size, present):
                copy = pltpu.make_async_copy(
                    zbuf.at[pl.ds(0, size)], xs_ref.at[pl.ds(pl.multiple_of(lo + off, RUN_ALIGN), size)], sem)

                @pl.when(present)
                def _():
                    copy.start()
                    copy.wait()

            _for_pad_pieces(zn_ref[e], visit)
            return carry

        lax.fori_loop(0, N_EXPERTS, per_expert, 0)

        tail_lo = zlo_ref[N_EXPERTS]

        def tile_copy(t):
            start = pl.multiple_of(tail_lo + t * EXPERT_TILE, EXPERT_TILE)
            return pltpu.make_async_copy(zbuf, xs_ref.at[pl.ds(start, EXPERT_TILE)], sem)

        def t_issue(t, c2):
            tile_copy(t).start()
            return c2

        def t_drain(t, c2):
            tile_copy(0).wait()
            return c2

        n_tail = zn_ref[N_EXPERTS] // EXPERT_TILE
        lax.fori_loop(0, n_tail, t_issue, 0)
        lax.fori_loop(0, n_tail, t_drain, 0)


def _dispatch(yp, ys, shift, scale, ctx_row, s_blocks_per_batch, local_rows, unit_dst, zero_lo, zero_n,
              total_rows, tm):
    n_p, n_s = yp.shape[0], ys.shape[0]
    n = n_p + n_s
    npb = n_p // tm
    nb = n // tm
    tok_p, tok_s, mod = _two_group_specs(npb, s_blocks_per_batch, ctx_row)
    smem_all = pl.BlockSpec(memory_space=pltpu.SMEM)
    return pl.pallas_call(
        functools.partial(_dispatch_body, n_p_blocks=npb, n_blocks=nb),
        grid=(nb,),
        in_specs=[pl.BlockSpec((UNIT_TABLE,), lambda i: (i,), memory_space=pltpu.SMEM), smem_all, smem_all,
                  pl.BlockSpec((TOP_K, tm), lambda i: (0, i)),
                  pl.BlockSpec((tm, D_MODEL), tok_p), pl.BlockSpec((tm, D_MODEL), tok_s),
                  pl.BlockSpec((None, 1, D_MODEL), mod), pl.BlockSpec((None, 1, D_MODEL), mod)],
        out_specs=pl.BlockSpec(memory_space=pl.ANY),
        out_shape=jax.ShapeDtypeStruct((total_rows, D_MODEL // 2), U32),
        scratch_shapes=[pltpu.VMEM((2, LOCAL_ROWS, D_MODEL // 2), U32),
                        pltpu.VMEM((EXPERT_TILE, D_MODEL // 2), U32),
                        pltpu.SemaphoreType.DMA((2,)), pltpu.SemaphoreType.DMA(())],
        compiler_params=_params("arbitrary"),
        name="moe_dispatch",
    )(unit_dst, zero_lo, zero_n, local_rows, yp, ys, shift, scale)


def _expert_body(te_ref, first_ref, half_ref, nu_ref, x_ref, wgu_ref, bgu_ref, wd_ref, bd_ref, y_ref,
                 act_s, wgu_s, wd_s):
    j = pl.program_id(0)

    @pl.when(first_ref[j] == 1)
    def _():
        blk = 2 * LANES
        src = lax.broadcasted_iota(I32, (blk, blk), 0)
        dst = lax.broadcasted_iota(I32, (blk, blk), 1)
        perm = (src == jnp.where(dst < LANES, 2 * dst, 2 * (dst - LANES) + 1)).astype(F32).astype(BF16)
        for s in range(2 * D_FF // blk):
            sl = slice(s * blk, (s + 1) * blk)
            wgu_s[:, sl] = jnp.dot(wgu_ref[:, sl].astype(BF16), perm, preferred_element_type=F32).astype(BF16)
        wd_s[...] = wd_ref[...].astype(BF16)

    def ffn(n_rows):
        rows = slice(0, n_rows)
        x = _unpack_bf16_pairs(x_ref[rows, :])
        for c in range(D_FF // LANES):
            sl = slice(c * 2 * LANES, (c + 1) * 2 * LANES)
            hu = jnp.dot(x, wgu_s[:, sl], preferred_element_type=F32) + bgu_ref[:, sl]
            x_glu = jnp.minimum(hu[:, :LANES], SWIGLU_LIMIT)
            x_lin = jnp.clip(hu[:, LANES:], -SWIGLU_LIMIT, SWIGLU_LIMIT)
            act = x_glu * (1.0 / (1.0 + jnp.exp(-SWIGLU_ALPHA * x_glu))) * (x_lin + 1.0)
            act_s[rows, c * LANES:(c + 1) * LANES] = act.astype(BF16)
        y_ref[rows, :] = _pack_bf16_pairs(
            jnp.dot(act_s[rows, :], wd_s[...], preferred_element_type=F32) + bd_ref[...])

    used = j < nu_ref[0]
    half = half_ref[j] == 1

    @pl.when(used & jnp.logical_not(half))
    def _():
        ffn(EXPERT_TILE)

    @pl.when(used & half)
    def _():
        ffn(EXPERT_TILE // 2)
        y_ref[EXPERT_TILE // 2:, :] = jnp.zeros((EXPERT_TILE // 2, y_ref.shape[1]), y_ref.dtype)

    @pl.when(jnp.logical_not(used))
    def _():
        y_ref[...] = jnp.zeros_like(y_ref)


def _experts(tile_expert, tile_first, tile_half, n_used, xs, cap_rows, w_gate_up, bgu, w_down, bd):
    n_tiles = cap_rows // EXPERT_TILE
    of_expert = lambda j, te, first, half, nu: (te[j], 0, 0)
    grid_spec = pltpu.PrefetchScalarGridSpec(
        num_scalar_prefetch=4,
        grid=(n_tiles,),
        in_specs=[pl.BlockSpec((EXPERT_TILE, D_MODEL // 2),
                               lambda j, te, first, half, nu: (jnp.minimum(j, nu[0] - 1), 0)),
                  pl.BlockSpec((None, D_MODEL, 2 * D_FF), of_expert),
                  pl.BlockSpec((None, 1, 2 * D_FF), of_expert),
                  pl.BlockSpec((None, D_FF, D_MODEL), of_expert),
                  pl.BlockSpec((None, 1, D_MODEL), of_expert)],
        out_specs=pl.BlockSpec((EXPERT_TILE, D_MODEL // 2), lambda j, te, first, half, nu: (j, 0)),
        scratch_shapes=[pltpu.VMEM((EXPERT_TILE, D_FF), BF16),
                        pltpu.VMEM((D_MODEL, 2 * D_FF), BF16), pltpu.VMEM((D_FF, D_MODEL), BF16)],
    )
    return pl.pallas_call(
        _expert_body,
        grid_spec=grid_spec,
        out_shape=jax.ShapeDtypeStruct((cap_rows, D_MODEL // 2), U32),
        compiler_params=_params("arbitrary"),
        name="moe_experts",
    )(tile_expert, tile_first, tile_half, n_used, xs, w_gate_up, bgu, w_down, bd)


def _combine_body(unit_ref, next_unit_ref, y1_ref, p_ref, gt_ref, gate_ref, lg_ref, lb_ref, ys_ref, o_ref,
                  ybuf, sems, *, n_blocks):
    i = pl.program_id(0)
    slot = i % 2
    tm = y1_ref.shape[0]

    @pl.when(i == 0)
    def _():
        _start_unit_copies(unit_ref, ybuf.at[0], ys_ref, sems.at[0], to_hbm=False)

    @pl.when(i + 1 < n_blocks)
    def _():
        _start_unit_copies(next_unit_ref, ybuf.at[1 - slot], ys_ref, sems.at[1 - slot], to_hbm=False)

    _wait_unit_copies(ybuf.at[slot], ys_ref, sems.at[slot])

    rows = p_ref[...]
    gates = gt_ref[...]
    c_iota = lax.broadcasted_iota(I32, (tm, LOCAL_ROWS), 1)
    weight = jnp.zeros((tm, LOCAL_ROWS), F32)
    for kk in range(TOP_K):
        weight = jnp.where(c_iota == rows[:, kk:kk + 1], gates[:, kk:kk + 1], weight)
    w_hi = weight.astype(BF16)
    w_lo = (weight - w_hi.astype(F32)).astype(BF16)
    yb = _unpack_bf16_pairs(ybuf[slot])
    f = (jnp.dot(w_hi, yb, preferred_element_type=F32) + jnp.dot(w_lo, yb, preferred_element_type=F32))
    y = DEEPNORM_ALPHA * y1_ref[...] + gate_ref[...] * f
    o_ref[...] = _standardize(y) * lg_ref[...] + lb_ref[...]


def _combine(y1, local_rows, gates, unit_src, block0, gate_mod, mod_row, blocks_per_batch,
             y_sorted, ln_g, ln_b, tm):
    n_g = y1.shape[0]
    nb = n_g // tm
    row = pl.BlockSpec((1, D_MODEL), lambda i: (0, 0))
    pair = pl.BlockSpec((tm, TOP_K), lambda i: (i + block0, 0))
    units = lambda ahead: pl.BlockSpec(
        (UNIT_TABLE,), lambda i: (jnp.minimum(i + ahead, nb - 1) + block0,), memory_space=pltpu.SMEM)
    return pl.pallas_call(
        functools.partial(_combine_body, n_blocks=nb),
        grid=(nb,),
        in_specs=[units(0), units(1),
                  pl.BlockSpec((tm, D_MODEL), lambda i: (i, 0)), pair, pair,
                  pl.BlockSpec((None, 1, D_MODEL), lambda i: (mod_row(i // blocks_per_batch), 0, 0)),
                  row, row,
                  pl.BlockSpec(memory_space=pl.ANY)],
        out_specs=pl.BlockSpec((tm, D_MODEL), lambda i: (i, 0)),
        out_shape=jax.ShapeDtypeStruct((n_g, D_MODEL), F32),
        scratch_shapes=[pltpu.VMEM((2, LOCAL_ROWS, D_MODEL // 2), U32), pltpu.SemaphoreType.DMA((2,))],
        compiler_params=_params("arbitrary"),
        name="moe_combine_ln2",
    )(unit_src, unit_src, y1, local_rows, gates, gate_mod,
      ln_g.reshape(1, -1), ln_b.reshape(1, -1), y_sorted)


def _regroup_gate_up_bias(b_gate_up):
    e = b_gate_up.shape[0]
    nblk = D_FF // LANES
    return b_gate_up.reshape(e, 1, nblk, LANES, 2).transpose(0, 1, 2, 4, 3).reshape(e, 1, 2 * D_FF)


def _routing_plan(table, totals, n_pairs):
    nb = table.shape[0]
    run_n = table[:, :, 0].astype(I32)
    rows_before = table[:, :, 1].astype(I32)
    run_local = table[:, :, 2].astype(I32)
    cnt = totals[:, 0].astype(I32)
    padded = (cnt + EXPERT_TILE - 1) // EXPERT_TILE * EXPERT_TILE
    pad_end = jnp.cumsum(padded)
    pad_start = pad_end - padded
    run_global = pad_start[None, :] + rows_before
    rows_max = n_pairs + nb * N_EXPERTS * (RUN_ALIGN - 1) + N_EXPERTS * (EXPERT_TILE - 1)
    n_tiles = -(-rows_max // EXPERT_TILE)
    n_used = pad_end[-1] // EXPERT_TILE
    tile_row = jnp.minimum(jnp.arange(n_tiles, dtype=I32), n_used - 1) * EXPERT_TILE
    tile_expert = jnp.sum((pad_end[None, :] <= tile_row[:, None]).astype(I32), axis=1)
    tile_expert = jnp.minimum(tile_expert, N_EXPERTS - 1)
    tile_id = jnp.arange(n_tiles, dtype=I32)
    changed = jnp.concatenate([jnp.ones((1,), bool), tile_expert[1:] != tile_expert[:-1]])
    tile_first = (changed & (tile_id < n_used)).astype(I32)
    exp_hot = tile_expert[:, None] == jnp.arange(N_EXPERTS, dtype=I32)
    rows_end = jnp.sum(jnp.where(exp_hot, (pad_start + cnt)[None, :], 0), axis=-1)
    tile_half = ((rows_end - tile_row) <= EXPERT_TILE // 2).astype(I32)
    cap_rows = n_tiles * EXPERT_TILE
    spare_rows = -(-LOCAL_ROWS // EXPERT_TILE) * EXPERT_TILE
    total_rows = cap_rows + 2 * spare_rows
    unit_row = jnp.arange(LOCAL_UNITS, dtype=I32) * RUN_ALIGN
    run_end = run_local + run_n
    owner = jnp.sum((run_end[:, None, :] <= unit_row[None, :, None]).astype(I32), axis=-1)
    own_hot = owner[:, :, None] == jnp.arange(N_EXPERTS, dtype=I32)
    shift = jnp.sum(jnp.where(own_hot, (run_global - run_local)[:, None, :], 0), axis=-1)
    used = owner < N_EXPERTS
    sorted_row = unit_row[None, :] + shift
    slot = (jnp.arange(nb, dtype=I32) % 2)[:, None]
    unit_dst = jnp.where(used, sorted_row, cap_rows + slot * spare_rows + unit_row[None, :])
    unit_src = jnp.where(used, sorted_row, unit_row[None, :])
    table_of = lambda a: jnp.pad(a, ((0, 0), (0, UNIT_TABLE - LOCAL_UNITS))).reshape(-1)
    zero_lo = jnp.concatenate([pad_start + cnt, pad_end[-1:]])
    zero_n = jnp.concatenate([padded - cnt, total_rows - pad_end[-1:]])
    return ((table_of(unit_dst), table_of(unit_src)), (tile_expert, tile_first, tile_half, n_used.reshape(1)),
            zero_lo, zero_n, cap_rows, total_rows)


def kernel(x_prompt, x_sample, cache_na_k, cache_na_v, state_ret_fwd, state_ret_bwd, c, c_ctx, w_ada, b_ada, w_in, rpb, ret_decay_fwd, ret_decay_bwd, beta_na, beta_ret, w_out, ln1_g, ln1_b, w_router, b_router, w_gate_up, b_gate_up, w_down, b_down, ln2_g, ln2_b):
    bp, tp, _ = x_prompt.shape
    bs, ts, _ = x_sample.shape
    assert w_ada.shape[0] == DEPTH == 1
    l = 0
    ctx_row = bs
    sample_row = lambda bi: bi
    prompt_row = lambda bi: ctx_row

    cond = jnp.concatenate([c, c_ctx[None, :], jnp.zeros((16 - bs - 1, D_MODEL), F32)], axis=0)
    m = _adaln(cond, w_ada[l], b_ada[l]).reshape(16, 6, 1, D_MODEL)
    sh_a, sc_a, gt_a, sh_f, sc_f, gt_f = (m[:, i] for i in range(6))

    w_in_b = w_in[l].astype(BF16)
    w_out_b = w_out[l].astype(BF16)

    q, k_p, v_p, qr, kr, vr, g = _project(x_prompt, sh_a, sc_a, prompt_row, w_in_b, tp, F32, rope=False)
    o_na = _context_attention(q, k_p, v_p)
    zeros_state = jnp.zeros((bp, 1, RET_HEADS, RET_DK, RET_DK), F32)
    ret, s_f, s_b = _retention(qr, kr, vr, g, ret_decay_fwd[l], ret_decay_bwd[l], beta_ret[l],
                               zeros_state, zeros_state, layer=0, heads_per_step=RET_HEADS)
    w_r = w_router[l].T
    w_r_hi = w_r.astype(BF16)
    w_r_lo = (w_r - w_r_hi.astype(F32)).astype(BF16)
    route = functools.partial(_merge_and_route, w_out=w_out_b, beta_na=beta_na[l], gate=gt_a, shift_f=sh_f,
                              scale_f=sc_f, ln_g=ln1_g[l], ln_b=ln1_b[l], w_hi=w_r_hi, w_lo=w_r_lo,
                              b_router=b_router[l])
    yp1, rows_p, gates_p, table_p, totals_p = route(
        x_prompt, o_na, ret, mod_row=prompt_row, carry_in=jnp.zeros((N_EXPERTS, LANES), F32), tm=tp)

    q, k_s, v_s, qr, kr, vr, g = _project(x_sample, sh_a, sc_a, sample_row, w_in_b, SAMPLE_TILE, BF16, rope=True)
    lc = cache_na_k.shape[2]
    o_na = _neighbourhood_attention(q, k_s, v_s, cache_na_k.reshape(bs, DEPTH, lc, NA_WIDTH),
                                    cache_na_v.reshape(bs, DEPTH, lc, NA_WIDTH), l, _na_bias(rpb[l]))
    ret, _, _ = _retention(qr, kr, vr, g, ret_decay_fwd[l], ret_decay_bwd[l], beta_ret[l],
                           state_ret_fwd, state_ret_bwd, layer=l, heads_per_step=1)
    ys1, rows_s, gates_s, table_s, totals = route(
        x_sample, o_na, ret, mod_row=sample_row, carry_in=totals_p, tm=SAMPLE_TILE)

    tm = TOKEN_BLOCK
    yp1f = yp1.reshape(bp * tp, D_MODEL)
    ys1f = ys1.reshape(bs * ts, D_MODEL)
    local_rows = jnp.concatenate([rows_p, rows_s], axis=1)
    gates = jnp.concatenate([gates_p, gates_s], axis=1)
    table = jnp.concatenate([table_p, table_s], axis=0)
    (unit_dst, unit_src), tiles, zero_lo, zero_n, cap_rows, total_rows = _routing_plan(
        table, totals, (bp * tp + bs * ts) * TOP_K)
    xs = _dispatch(yp1f, ys1f, sh_f, sc_f, ctx_row, ts // tm, local_rows, unit_dst, zero_lo, zero_n,
                   total_rows, tm)
    y_sorted = _experts(*tiles, xs, cap_rows, w_gate_up[l], _regroup_gate_up_bias(b_gate_up[l]), w_down[l],
                        b_down[l][:, None, :])
    rows_t, gates_t = local_rows.T, gates.T
    yp = _combine(yp1f, rows_t, gates_t, unit_src, 0, gt_f, prompt_row, 1, y_sorted, ln2_g[l], ln2_b[l], tm)
    ys = _combine(ys1f, rows_t, gates_t, unit_src, (bp * tp) // tm, gt_f, sample_row, ts // tm, y_sorted,
                  ln2_g[l], ln2_b[l], tm)

    return (yp.reshape(bp, tp, D_MODEL), ys.reshape(bs, ts, D_MODEL),
            k_p.reshape(bp, 1, tp, NA_HEADS, NA_DIM), v_p.reshape(bp, 1, tp, NA_HEADS, NA_DIM),
            s_f[:, None], s_b[:, None])
```

```python
import functools

import jax
import jax.numpy as jnp
from jax import lax
from jax.experimental import pallas as pl
from jax.experimental.pallas import tpu as pltpu

F32 = jnp.float32
BF16 = jnp.bfloat16
I32 = jnp.int32
U32 = jnp.uint32

D_MODEL = 1024
GRID_W = 64
NA_HEADS = 8
NA_DIM = 64
NA_WIDTH = NA_HEADS * NA_DIM
NA_KH = 8
NA_KW = 16
RET_HEADS = 4
RET_DK = 128
RET_WIDTH = RET_HEADS * RET_DK
CHUNK = 128
N_EXPERTS = 32
TOP_K = 4
D_FF = 1024
SWIGLU_LIMIT = 7.0
SWIGLU_ALPHA = 1.702
ROPE_BASE = 10000.0
LN_EPS = 1e-5
DEPTH = 1
DEEPNORM_ALPHA = (2.0 * DEPTH) ** 0.25

LANES = 128
NEG_BIG = -1e30
TOKEN_BLOCK = 256
SAMPLE_TILE = 1024
EXPERT_TILE = 1024
VMEM_LIMIT = 56 * 1024 * 1024
RET_CHUNK_UNROLL = 16
SUBLANES = 8
RUN_ALIGN = SUBLANES
LOCAL_ROWS = TOKEN_BLOCK * TOP_K + N_EXPERTS * RUN_ALIGN
LOCAL_UNITS = LOCAL_ROWS // RUN_ALIGN
UNIT_TABLE = 256
PAD_PIECES = tuple(EXPERT_TILE >> s for s in range(1, EXPERT_TILE.bit_length())
                   if (EXPERT_TILE >> s) >= RUN_ALIGN)


def _params(*sem):
    return pltpu.CompilerParams(dimension_semantics=sem, vmem_limit_bytes=VMEM_LIMIT)


def _silu(x):
    return x / (1.0 + jnp.exp(-x))


def _standardize(x):
    mu = jnp.mean(x, axis=-1, keepdims=True)
    xc = x - mu
    var = jnp.mean(xc * xc, axis=-1, keepdims=True)
    return xc * lax.rsqrt(var + LN_EPS)


def _pack_bf16_pairs(x, is_bf16_exact=False):
    if not is_bf16_exact:
        x = x.astype(BF16).astype(F32)
    bits = lax.bitcast_convert_type(x, U32)
    half = x.shape[1] // 2
    return bits[:, :half] | lax.shift_right_logical(bits[:, half:], jnp.uint32(16))


def _unpack_bf16_pairs(p):
    hi = lax.bitcast_convert_type(p & jnp.uint32(0xFFFF0000), F32)
    lo = lax.bitcast_convert_type(lax.shift_left(p, jnp.uint32(16)), F32)
    return jnp.concatenate([hi, lo], axis=1).astype(BF16)


def _dot_nt(a, b):
    return lax.dot_general(a, b, (((1,), (1,)), ((), ())), preferred_element_type=F32)


def _dot_tn(a, b):
    return lax.dot_general(a, b, (((0,), (0,)), ((), ())), preferred_element_type=F32)


def _ada_body(c_ref, w_ref, b_ref, o_ref):
    s = _silu(c_ref[...])
    o_ref[...] = jnp.dot(s, w_ref[...], preferred_element_type=F32,
                         precision=lax.Precision.HIGHEST) + b_ref[...]


def _adaln(cond, w_ada, b_ada):
    r = cond.shape[0]
    n = w_ada.shape[1]
    tn = 1536
    return pl.pallas_call(
        _ada_body,
        grid=(n // tn,),
        in_specs=[pl.BlockSpec((r, D_MODEL), lambda j: (0, 0)),
                  pl.BlockSpec((D_MODEL, tn), lambda j: (0, j)),
                  pl.BlockSpec((1, tn), lambda j: (0, j))],
        out_specs=pl.BlockSpec((r, tn), lambda j: (0, j)),
        out_shape=jax.ShapeDtypeStruct((r, n), F32),
        compiler_params=_params("arbitrary"),
        name="adaln",
    )(cond, w_ada, b_ada.reshape(1, n))


def _proj_body(*refs, rope):
    if rope:
        (x_ref, sh_ref, sc_ref, w_ref, cos_ref, sa_ref, sb_ref,
         q_ref, k_ref, v_ref, qr_ref, kr_ref, vr_ref, g_ref) = refs
    else:
        (x_ref, sh_ref, sc_ref, w_ref,
         q_ref, k_ref, v_ref, qr_ref, kr_ref, vr_ref, g_ref) = refs
    h = (x_ref[...] * (1.0 + sc_ref[...]) + sh_ref[...]).astype(BF16)

    def cols(c):
        return jnp.dot(h, w_ref[:, c * 512:(c + 1) * 512], preferred_element_type=F32)

    q_ref[...] = cols(0).astype(q_ref.dtype)
    k_ref[...] = cols(1).astype(k_ref.dtype)
    v_ref[...] = cols(2).astype(v_ref.dtype)
    pq = cols(3)
    pk = cols(4) * (RET_DK ** -0.5)
    if rope:
        cs, sa, sb = cos_ref[...], sa_ref[...], sb_ref[...]
        for hd in range(RET_HEADS):
            sl = slice(hd * LANES, (hd + 1) * LANES)
            for p, o_ref in ((pq, qr_ref), (pk, kr_ref)):
                xs = p[:, sl]
                rot = xs * cs + pltpu.roll(xs, 96, 1) * sa + pltpu.roll(xs, 32, 1) * sb
                o_ref[:, sl] = rot.astype(o_ref.dtype)
    else:
        qr_ref[...] = pq.astype(qr_ref.dtype)
        kr_ref[...] = pk.astype(kr_ref.dtype)
    vr_ref[...] = cols(5).astype(vr_ref.dtype)
    g_ref[...] = cols(6)


def _rope_tables(t_len):
    t = jnp.arange(t_len, dtype=jnp.int32)
    pos_row = (t // GRID_W).astype(F32)[:, None]
    pos_col = (t % GRID_W).astype(F32)[:, None]
    lane = jnp.arange(LANES, dtype=jnp.int32)[None, :]
    n_freq = RET_DK // 4
    inv_freq = ROPE_BASE ** (-(lane % n_freq).astype(F32) / n_freq)
    ang = jnp.where(lane < RET_DK // 2, pos_row, pos_col) * inv_freq
    first = (lane % (2 * n_freq)) < n_freq
    cos = jnp.cos(ang)
    sin = jnp.sin(ang)
    return cos, jnp.where(first, -sin, 0.0), jnp.where(first, 0.0, sin)


def _project(x, shift, scale, mod_row, w_in, tm, kv_dtype, rope):
    b, t, _ = x.shape
    tok = lambda bi, ti: (bi, ti, 0)
    mod = lambda bi, ti: (mod_row(bi), 0, 0)
    in_specs = [pl.BlockSpec((None, tm, D_MODEL), tok),
                pl.BlockSpec((None, 1, D_MODEL), mod),
                pl.BlockSpec((None, 1, D_MODEL), mod),
                pl.BlockSpec(w_in.shape, lambda bi, ti: (0, 0))]
    args = [x, shift, scale, w_in]
    if rope:
        tab = lambda bi, ti: (ti, 0)
        in_specs += [pl.BlockSpec((tm, LANES), tab)] * 3
        args += list(_rope_tables(t))
    o512 = pl.BlockSpec((None, tm, 512), tok)
    shp = lambda dt: jax.ShapeDtypeStruct((b, t, 512), dt)
    return pl.pallas_call(
        functools.partial(_proj_body, rope=rope),
        grid=(b, t // tm),
        in_specs=in_specs,
        out_specs=[o512] * 7,
        out_shape=[shp(BF16), shp(kv_dtype), shp(kv_dtype), shp(BF16), shp(BF16), shp(BF16), shp(F32)],
        compiler_params=_params("parallel", "parallel"),
        name="in_proj",
    )(*args)


def _pair_masks(shape):
    lane = lax.broadcasted_iota(I32, shape, 1)
    return lane < NA_DIM, lane >= NA_DIM


def _ctx_attn_body(q_ref, k_ref, v_ref, o_ref):
    t = q_ref.shape[0]
    lo, hi = _pair_masks((t, LANES))
    for hp in range(NA_WIDTH // LANES):
        sl = slice(hp * LANES, (hp + 1) * LANES)
        q2 = q_ref[:, sl] * (NA_DIM ** -0.5)
        k2 = k_ref[:, sl].astype(BF16)
        v2 = v_ref[:, sl].astype(BF16)
        zero = jnp.zeros_like(q2)
        qs = jnp.concatenate([jnp.where(lo, q2, zero), jnp.where(hi, q2, zero)], axis=0)
        s = _dot_nt(qs, k2)
        e = jnp.exp(s - jnp.max(s, axis=-1, keepdims=True))
        den = jnp.sum(e, axis=-1, keepdims=True)
        o = jnp.dot(e.astype(BF16), v2, preferred_element_type=F32) / den
        o_ref[:, sl] = jnp.where(lo, o[:t], o[t:]).astype(o_ref.dtype)


def _context_attention(q, k, v):
    b, t, _ = q.shape
    spec = pl.BlockSpec((None, t, NA_WIDTH), lambda bi: (bi, 0, 0))
    return pl.pallas_call(
        _ctx_attn_body,
        grid=(b,),
        in_specs=[spec] * 3,
        out_specs=spec,
        out_shape=jax.ShapeDtypeStruct((b, t, NA_WIDTH), BF16),
        compiler_params=_params("parallel"),
        name="ctx_attn",
    )(q, k, v)


def _na_bias_body(rpb_ref, o_ref):
    h = pl.program_id(0)
    v = pl.program_id(1)
    shape = (GRID_W, LANES)
    w = lax.broadcasted_iota(I32, shape, 0)
    lane = lax.broadcasted_iota(I32, shape, 1)
    upper = lane >= GRID_W
    kc = jnp.where(upper, lane - GRID_W, lane)
    cdiff = kc - w + (NA_KW - 1)
    cstart = jnp.clip(w - NA_KW // 2, 0, GRID_W - NA_KW)
    inwin = (kc >= cstart) & (kc < cstart + NA_KW)
    n_ro = 2 * NA_KH - 1
    n_co = 2 * NA_KW - 1
    for j in range(NA_KH // 2):
        ro = 2 * j - v + (NA_KH - 1)
        base0 = (h * n_ro + ro) * n_co
        acc = jnp.zeros(shape, F32)
        for c in range(n_co):
            val = jnp.where(upper, rpb_ref[base0 + n_co + c], rpb_ref[base0 + c])
            acc = jnp.where(cdiff == c, val, acc)
        o_ref[:, j * LANES:(j + 1) * LANES] = jnp.where(inwin, acc, NEG_BIG)


def _na_bias(rpb):
    n = NA_KH * GRID_W
    return pl.pallas_call(
        _na_bias_body,
        grid=(NA_HEADS, NA_KH),
        in_specs=[pl.BlockSpec(memory_space=pltpu.SMEM)],
        out_specs=pl.BlockSpec((None, None, GRID_W, n), lambda h, v: (h, v, 0, 0)),
        out_shape=jax.ShapeDtypeStruct((NA_HEADS, NA_KH, GRID_W, n), F32),
        compiler_params=_params("parallel", "parallel"),
        name="na_bias",
    )(rpb.reshape(-1))


def _na_body(q_ref, k_ref, v_ref, ck_ref, cv_ref, bias_ref, o_ref, s_buf, e_buf, d_buf, *, rows):
    ck = ck_ref[...].astype(BF16)
    cv = cv_ref[...].astype(BF16)
    lo, hi = _pair_masks((GRID_W, LANES))
    scale = NA_DIM ** -0.5
    n_loc = NA_KH * GRID_W

    def window(r):
        r = jnp.clip(r, 0, rows - 1)
        r_start = jnp.clip(r - NA_KH // 2, 0, rows - NA_KH)
        return (r - r_start, pl.ds(pl.multiple_of(r * GRID_W, GRID_W), GRID_W),
                pl.ds(pl.multiple_of(r_start * GRID_W, GRID_W), n_loc))

    def scores(r, slot):
        vidx, qsl, ksl = window(r)
        q2 = q_ref[qsl, :]
        zero = jnp.zeros_like(q2)
        q2 = q2 * scale
        qs = jnp.concatenate([jnp.where(lo, q2, zero), jnp.where(hi, q2, zero)], axis=0)
        bias = jnp.concatenate([bias_ref[0, vidx], bias_ref[1, vidx]], axis=0)
        s_buf[slot, :, :n_loc] = _dot_nt(qs, k_ref[ksl, :]) + bias
        s_buf[slot, :, n_loc:] = _dot_nt(qs, ck)

    def numerators(slot):
        s = s_buf[slot]
        e = jnp.exp(s - jnp.max(s, axis=-1, keepdims=True))
        d_buf[slot] = jnp.broadcast_to(jnp.sum(e, axis=-1, keepdims=True), d_buf.shape[1:])
        e_buf[slot] = e.astype(BF16)

    def output(r, slot):
        _, qsl, ksl = window(r)
        o = (jnp.dot(e_buf[slot, :, :n_loc], v_ref[ksl, :], preferred_element_type=F32)
             + jnp.dot(e_buf[slot, :, n_loc:], cv, preferred_element_type=F32)) / d_buf[slot]
        o_ref[qsl, :] = jnp.where(lo, o[:GRID_W], o[GRID_W:]).astype(o_ref.dtype)

    s_buf[...] = jnp.zeros_like(s_buf)
    e_buf[...] = jnp.zeros_like(e_buf)
    d_buf[...] = jnp.ones_like(d_buf)

    def two_steps(j, carry):
        r = 2 * j
        scores(r, 0)
        numerators(1)
        output(r - 2, 0)
        scores(r + 1, 1)
        numerators(0)
        output(r - 1, 1)
        return carry

    lax.fori_loop(0, rows // 2 + 1, two_steps, 0, unroll=3)


def _neighbourhood_attention(q, k, v, ck, cv, layer, bias):
    b, t, _ = q.shape
    lc = ck.shape[2]
    rows = t // GRID_W
    tok = pl.BlockSpec((None, t, LANES), lambda hp, bi: (bi, 0, hp))
    ctx = pl.BlockSpec((None, None, lc, LANES), lambda hp, bi: (bi, layer, 0, hp))
    return pl.pallas_call(
        functools.partial(_na_body, rows=rows),
        grid=(NA_WIDTH // LANES, b),
        in_specs=[tok, tok, tok, ctx, ctx,
                  pl.BlockSpec((2, NA_KH, GRID_W, NA_KH * GRID_W), lambda hp, bi: (hp, 0, 0, 0))],
        out_specs=tok,
        out_shape=jax.ShapeDtypeStruct((b, t, NA_WIDTH), BF16),
        scratch_shapes=[pltpu.VMEM((2, 2 * GRID_W, NA_KH * GRID_W + lc), F32),
                        pltpu.VMEM((2, 2 * GRID_W, NA_KH * GRID_W + lc), BF16),
                        pltpu.VMEM((2, 2 * GRID_W, LANES), F32)],
        compiler_params=_params("parallel", "parallel"),
        name="na_attn",
    )(q, k, v, ck, cv, bias)


def _ret_body(q_ref, k_ref, v_ref, g_ref, df_ref, db_ref, beta_ref, s0f_ref, s0b_ref,
              o_ref, sf_ref, sb_ref, kvf_s, kvb_s, *, n_chunks, heads):
    for hh in range(heads):
        slab = lambda ref: ref.at[:, hh * LANES:(hh + 1) * LANES]
        _ret_head(slab(q_ref), slab(k_ref), slab(v_ref), slab(g_ref), df_ref.at[hh], db_ref.at[hh],
                  beta_ref.at[hh], s0f_ref.at[hh], s0b_ref.at[hh], slab(o_ref), sf_ref.at[hh], sb_ref.at[hh],
                  kvf_s, kvb_s, n_chunks=n_chunks)


def _ret_head(q_ref, k_ref, v_ref, g_ref, df_ref, db_ref, beta_ref, s0f_ref, s0b_ref,
              o_ref, sf_ref, sb_ref, kvf_s, kvb_s, *, n_chunks):
    c_len = CHUNK
    lgf = -jnp.log1p(jnp.exp(-df_ref[...]))
    lgb = -jnp.log1p(jnp.exp(-db_ref[...]))
    shape = (c_len, c_len)
    i = lax.broadcasted_iota(I32, shape, 0).astype(F32)
    j = lax.broadcasted_iota(I32, shape, 1).astype(F32)
    dij = i - j
    d_comb = (jnp.where(dij >= 0, jnp.exp(jnp.maximum(dij, 0.0) * lgf), 0.0)
              + jnp.where(dij <= 0, jnp.exp(jnp.maximum(-dij, 0.0) * lgb), 0.0))
    xi_f = jnp.exp((i + 1.0) * lgf)
    xi_b = jnp.exp((c_len - i) * lgb)
    zeta_f = jnp.exp((c_len - 1.0 - i) * lgf)
    zeta_b = jnp.exp(i * lgb)
    g_f = jnp.exp(c_len * lgf)
    g_b = jnp.exp(c_len * lgb)

    def chunk(c):
        return pl.ds(pl.multiple_of(c * c_len, c_len), c_len)

    def kv_pass(c, carry):
        kc = k_ref[chunk(c), :]
        vc = v_ref[chunk(c), :].astype(F32)
        kvf_s[c] = _dot_tn(kc, (vc * zeta_f).astype(BF16))
        kvb_s[c] = _dot_tn(kc, (vc * zeta_b).astype(BF16))
        return carry

    lax.fori_loop(0, n_chunks, kv_pass, 0, unroll=RET_CHUNK_UNROLL)

    def scan_f(c, s):
        kv = kvf_s[c]
        kvf_s[c] = s
        return g_f * s + kv

    def scan_b(ci, s):
        c = n_chunks - 1 - ci
        kv = kvb_s[c]
        kvb_s[c] = s
        return g_b * s + kv

    sf_ref[...] = lax.fori_loop(0, n_chunks, scan_f, s0f_ref[...])
    sb_ref[...] = lax.fori_loop(0, n_chunks, scan_b, s0b_ref[...])

    beta = beta_ref[...]

    def out_pass(c, carry):
        qc = q_ref[chunk(c), :]
        kc = k_ref[chunk(c), :]
        vc = v_ref[chunk(c), :]
        scores = _dot_nt(qc, kc) * d_comb
        y = jnp.dot(scores.astype(BF16), vc, preferred_element_type=F32)
        qf = qc.astype(F32)
        y += jnp.dot((qf * xi_f).astype(BF16), kvf_s[c].astype(BF16), preferred_element_type=F32)
        y += jnp.dot((qf * xi_b).astype(BF16), kvb_s[c].astype(BF16), preferred_element_type=F32)
        o_ref[chunk(c), :] = (_standardize(y) * beta * _silu(g_ref[chunk(c), :])).astype(o_ref.dtype)
        return carry

    lax.fori_loop(0, n_chunks, out_pass, 0, unroll=RET_CHUNK_UNROLL)


def _retention(q, k, v, g, decay_f, decay_b, beta_ret, s0f, s0b, layer, heads_per_step):
    b, t, _ = q.shape
    n_chunks = t // CHUNK
    hs = heads_per_step
    tok = pl.BlockSpec((None, t, hs * LANES), lambda bi, h: (bi, 0, h))
    per_head = pl.BlockSpec((hs, 1, LANES), lambda bi, h: (h, 0, 0))
    state = pl.BlockSpec((None, hs, RET_DK, RET_DK), lambda bi, h: (bi, h, 0, 0))
    state0 = pl.BlockSpec((None, None, hs, RET_DK, RET_DK), lambda bi, h: (bi, layer, h, 0, 0))
    lanes = lambda a: jnp.broadcast_to(a.astype(F32)[:, None, None], (RET_HEADS, 1, LANES))
    st_shape = jax.ShapeDtypeStruct((b, RET_HEADS, RET_DK, RET_DK), F32)
    return pl.pallas_call(
        functools.partial(_ret_body, n_chunks=n_chunks, heads=hs),
        grid=(b, RET_HEADS // hs),
        in_specs=[tok, tok, tok, tok, per_head, per_head, per_head, state0, state0],
        out_specs=[tok, state, state],
        out_shape=[jax.ShapeDtypeStruct((b, t, RET_WIDTH), BF16), st_shape, st_shape],
        scratch_shapes=[pltpu.VMEM((n_chunks, RET_DK, RET_DK), F32)] * 2,
        compiler_params=_params("parallel", "parallel"),
        name="retention",
    )(q, k, v, g, lanes(decay_f), lanes(decay_b), beta_ret.reshape(RET_HEADS, 1, LANES), s0f, s0b)


def _merge_body(x_ref, ona_ref, ret_ref, wna_ref, wret_ref, bna_ref, gate_ref, lg_ref, lb_ref,
                shf_ref, scf_ref, wh_ref, wl_ref, br_ref, carry_in_ref,
                o_ref, p_ref, gt_ref, tab_ref, cnt_ref, carry_s):
    @pl.when((pl.program_id(0) == 0) & (pl.program_id(1) == 0))
    def _():
        carry_s[...] = carry_in_ref[...]

    o = ona_ref[...].astype(F32)
    na = o * lax.rsqrt(jnp.mean(o * o, axis=-1, keepdims=True) + LN_EPS) * bna_ref[...]
    mix = (jnp.dot(na.astype(BF16), wna_ref[...], preferred_element_type=F32)
           + jnp.dot(ret_ref[...], wret_ref[...], preferred_element_type=F32))
    y = DEEPNORM_ALPHA * x_ref[...] + gate_ref[...] * mix
    y = _standardize(y) * lg_ref[...] + lb_ref[...]
    o_ref[...] = y

    h = y * (1.0 + scf_ref[...]) + shf_ref[...]
    w_hi, w_lo, bias = wh_ref[...], wl_ref[...], br_ref[...]
    carry = carry_s[...]
    for s in range(y.shape[0] // TOKEN_BLOCK):
        sl = slice(s * TOKEN_BLOCK, (s + 1) * TOKEN_BLOCK)
        rows, gates, table, carry = _route_block(h[sl], w_hi, w_lo, bias, carry)
        p_ref[:, sl] = rows
        gt_ref[:, sl] = gates
        tab_ref[s] = table
    carry_s[...] = carry
    cnt_ref[...] = carry


def _merge_and_route(x, o_na, ret, w_out, beta_na, gate, shift_f, scale_f, mod_row, ln_g, ln_b,
                     w_hi, w_lo, b_router, carry_in, tm):
    b, t, _ = x.shape
    nt = t // tm
    sub = tm // TOKEN_BLOCK
    tok = lambda bi, ti: (bi, ti, 0)
    flat = lambda bi, ti: (0, bi * nt + ti)
    mod = lambda bi, ti: (mod_row(bi), 0, 0)
    whole = lambda shape: pl.BlockSpec(shape, lambda bi, ti: (0,) * len(shape))
    half = pl.BlockSpec((None, tm, 512), tok)
    mod_spec = pl.BlockSpec((None, 1, D_MODEL), mod)
    return pl.pallas_call(
        _merge_body,
        grid=(b, nt),
        in_specs=[pl.BlockSpec((None, tm, D_MODEL), tok), half, half,
                  pl.BlockSpec((512, D_MODEL), lambda bi, ti: (0, 0)),
                  pl.BlockSpec((512, D_MODEL), lambda bi, ti: (1, 0)),
                  whole((1, 512)), mod_spec, whole((1, D_MODEL)), whole((1, D_MODEL)),
                  mod_spec, mod_spec,
                  whole((N_EXPERTS, D_MODEL)), whole((N_EXPERTS, D_MODEL)), whole((N_EXPERTS, TOKEN_BLOCK)),
                  whole((N_EXPERTS, LANES))],
        out_specs=[pl.BlockSpec((None, tm, D_MODEL), tok),
                   pl.BlockSpec((TOP_K, tm), flat), pl.BlockSpec((TOP_K, tm), flat),
                   pl.BlockSpec((sub, N_EXPERTS, LANES), lambda bi, ti: (bi * nt + ti, 0, 0)),
                   whole((N_EXPERTS, LANES))],
        out_shape=[jax.ShapeDtypeStruct(x.shape, F32),
                   jax.ShapeDtypeStruct((TOP_K, b * t), I32), jax.ShapeDtypeStruct((TOP_K, b * t), F32),
                   jax.ShapeDtypeStruct((b * t // TOKEN_BLOCK, N_EXPERTS, LANES), F32),
                   jax.ShapeDtypeStruct((N_EXPERTS, LANES), F32)],
        scratch_shapes=[pltpu.VMEM((N_EXPERTS, LANES), F32)],
        compiler_params=_params("arbitrary", "arbitrary"),
        name="merge_ln1_route",
    )(x, o_na, ret, w_out, w_out, beta_na.reshape(1, 512), gate, ln_g.reshape(1, -1), ln_b.reshape(1, -1),
      shift_f, scale_f, w_hi, w_lo, jnp.broadcast_to(b_router[:, None], (N_EXPERTS, TOKEN_BLOCK)), carry_in)


def _two_group_specs(n_p_blocks, blocks_per_batch_s, ctx_row):
    tok_p = lambda i: (jnp.minimum(i, n_p_blocks - 1), 0)
    tok_s = lambda i: (jnp.maximum(i - n_p_blocks, 0), 0)
    mod = lambda i: (jnp.where(i < n_p_blocks, ctx_row, jnp.maximum(i - n_p_blocks, 0) // blocks_per_batch_s), 0, 0)
    return tok_p, tok_s, mod


def _route_block(h, w_hi, w_lo, bias, carry):
    tm = h.shape[0]
    h_hi = h.astype(BF16)
    h_lo = (h - h_hi.astype(F32)).astype(BF16)
    lg = _dot_nt(jnp.concatenate([w_hi, w_lo], axis=0), h_hi)
    work = lg[:N_EXPERTS] + lg[N_EXPERTS:] + _dot_nt(w_hi, h_lo) + bias
    eidx = lax.broadcasted_iota(I32, (N_EXPERTS, tm), 0).astype(F32)
    vals, idxs, hots = [], [], []
    for _ in range(TOP_K):
        mx = jnp.max(work, axis=0, keepdims=True)
        idx = jnp.min(jnp.where(work == mx, eidx, float(N_EXPERTS)), axis=0, keepdims=True)
        hot = eidx == idx
        vals.append(mx)
        idxs.append(idx)
        hots.append(hot)
        work = jnp.where(hot, -jnp.inf, work)
    exps = [jnp.exp(v - vals[0]) for v in vals]
    den = exps[0] + exps[1] + exps[2] + exps[3]
    sel = jnp.zeros((N_EXPERTS, tm), F32)
    for hot in hots:
        sel = sel + hot.astype(F32)
    r_i = lax.broadcasted_iota(I32, (tm, tm), 0)
    c_i = lax.broadcasted_iota(I32, (tm, tm), 1)
    earlier = (r_i < c_i).astype(BF16)
    before = jnp.dot(sel.astype(BF16), earlier, preferred_element_type=F32)
    n_e = jnp.sum(sel, axis=1, keepdims=True)
    n8 = jnp.floor((n_e + (RUN_ALIGN - 1.0)) * (1.0 / RUN_ALIGN)) * RUN_ALIGN + jnp.zeros((1, LANES), F32)
    e_r = lax.broadcasted_iota(I32, (N_EXPERTS, N_EXPERTS), 0)
    e_c = lax.broadcasted_iota(I32, (N_EXPERTS, N_EXPERTS), 1)
    off = jnp.dot((e_c < e_r).astype(BF16), n8.astype(BF16), preferred_element_type=F32)
    local = before + jnp.concatenate([off] * (tm // LANES), axis=1)
    rows = [jnp.sum(jnp.where(hot, local, 0.0), axis=0, keepdims=True) for hot in hots]
    lane = lax.broadcasted_iota(I32, (N_EXPERTS, LANES), 1)
    table = jnp.where(lane == 0, n8, jnp.where(lane == 1, carry, jnp.where(lane == 2, off, 0.0)))
    return (jnp.concatenate(rows, axis=0).astype(I32), jnp.concatenate([e / den for e in exps], axis=0),
            table, carry + n8)


def _for_pad_pieces(n_rows, visit):
    off = 0
    for size in PAD_PIECES:
        present = (n_rows & size) != 0
        visit(off, size, present)
        off = off + jnp.where(present, size, 0)


def _start_unit_copies(unit_ref, local_buf, sorted_hbm, sem, to_hbm):
    def one(u, carry):
        loc = local_buf.at[pl.ds(pl.multiple_of(u * RUN_ALIGN, RUN_ALIGN), RUN_ALIGN)]
        glob = sorted_hbm.at[pl.ds(pl.multiple_of(unit_ref[u], RUN_ALIGN), RUN_ALIGN)]
        copy = pltpu.make_async_copy(loc, glob, sem) if to_hbm else pltpu.make_async_copy(glob, loc, sem)
        copy.start()
        return carry

    lax.fori_loop(0, LOCAL_UNITS, one, 0, unroll=8)


def _wait_unit_copies(local_buf, sorted_hbm, sem):
    whole = sorted_hbm.at[pl.ds(0, LOCAL_ROWS)]
    pltpu.make_async_copy(whole, local_buf, sem).wait()


def _dispatch_body(unit_ref, zlo_ref, zn_ref, p_ref, yp_ref, ys_ref, sh_ref, sc_ref, xs_ref,
                   xbuf, zbuf, sems, zsem, *, n_p_blocks, n_blocks):
    i = pl.program_id(0)
    slot = i % 2
    y = jnp.where(i < n_p_blocks, yp_ref[...], ys_ref[...])
    h = (y * (1.0 + sc_ref[...]) + sh_ref[...]).astype(BF16)
    rows = p_ref[...]
    r_iota = lax.broadcasted_iota(I32, (LOCAL_ROWS, rows.shape[1]), 0)
    place = r_iota == rows[0:1]
    for kk in range(1, TOP_K):
        place = place | (r_iota == rows[kk:kk + 1])
    xbuf[slot] = _pack_bf16_pairs(
        jnp.dot(jnp.where(place, 1.0, 0.0).astype(BF16), h, preferred_element_type=F32), is_bf16_exact=True)

    _start_unit_copies(unit_ref, xbuf.at[slot], xs_ref, sems.at[slot], to_hbm=True)

    @pl.when(i > 0)
    def _():
        _wait_unit_copies(xbuf.at[1 - slot], xs_ref, sems.at[1 - slot])

    @pl.when(i == n_blocks - 1)
    def _():
        _wait_unit_copies(xbuf.at[slot], xs_ref, sems.at[slot])
        zbuf[...] = jnp.zeros_like(zbuf)
        sem = zsem

        def pad_pieces(wait):
            def per_expert(e, carry):
                lo = zlo_ref[e]

                def visit(off, size, present):
                    copy = pltpu.make_async_copy(
                        zbuf.at[pl.ds(0, size)], xs_ref.at[pl.ds(pl.multiple_of(lo + off, RUN_ALIGN), size)], sem)

                    @pl.when(present)
                    def _():
                        if wait:
                            copy.wait()
                        else:
                            copy.start()

                _for_pad_pieces(zn_ref[e], visit)
                return carry

            lax.fori_loop(0, N_EXPERTS, per_expert, 0)

        pad_pieces(wait=False)
        pad_pieces(wait=True)

        tail_lo = zlo_ref[N_EXPERTS]

        def tile_copy(t):
            start = pl.multiple_of(tail_lo + t * EXPERT_TILE, EXPERT_TILE)
            return pltpu.make_async_copy(zbuf, xs_ref.at[pl.ds(start, EXPERT_TILE)], sem)

        def t_issue(t, c2):
            tile_copy(t).start()
            return c2

        def t_drain(t, c2):
            tile_copy(0).wait()
            return c2

        n_tail = zn_ref[N_EXPERTS] // EXPERT_TILE
        lax.fori_loop(0, n_tail, t_issue, 0)
        lax.fori_loop(0, n_tail, t_drain, 0)


def _dispatch(yp, ys, shift, scale, ctx_row, s_blocks_per_batch, local_rows, unit_dst, zero_lo, zero_n,
              total_rows, tm):
    n_p, n_s = yp.shape[0], ys.shape[0]
    n = n_p + n_s
    npb = n_p // tm
    nb = n // tm
    tok_p, tok_s, mod = _two_group_specs(npb, s_blocks_per_batch, ctx_row)
    smem_all = pl.BlockSpec(memory_space=pltpu.SMEM)
    return pl.pallas_call(
        functools.partial(_dispatch_body, n_p_blocks=npb, n_blocks=nb),
        grid=(nb,),
        in_specs=[pl.BlockSpec((UNIT_TABLE,), lambda i: (i,), memory_space=pltpu.SMEM), smem_all, smem_all,
                  pl.BlockSpec((TOP_K, tm), lambda i: (0, i)),
                  pl.BlockSpec((tm, D_MODEL), tok_p), pl.BlockSpec((tm, D_MODEL), tok_s),
                  pl.BlockSpec((None, 1, D_MODEL), mod), pl.BlockSpec((None, 1, D_MODEL), mod)],
        out_specs=pl.BlockSpec(memory_space=pl.ANY),
        out_shape=jax.ShapeDtypeStruct((total_rows, D_MODEL // 2), U32),
        scratch_shapes=[pltpu.VMEM((2, LOCAL_ROWS, D_MODEL // 2), U32),
                        pltpu.VMEM((EXPERT_TILE, D_MODEL // 2), U32),
                        pltpu.SemaphoreType.DMA((2,)), pltpu.SemaphoreType.DMA(())],
        compiler_params=_params("arbitrary"),
        name="moe_dispatch",
    )(unit_dst, zero_lo, zero_n, local_rows, yp, ys, shift, scale)


def _expert_body(te_ref, first_ref, half_ref, nu_ref, x_ref, wgu_ref, bgu_ref, wd_ref, bd_ref, y_ref,
                 act_s, wgu_s, wd_s):
    j = pl.program_id(0)

    @pl.when(first_ref[j] == 1)
    def _():
        blk = 2 * LANES
        src = lax.broadcasted_iota(I32, (blk, blk), 0)
        dst = lax.broadcasted_iota(I32, (blk, blk), 1)
        perm = (src == jnp.where(dst < LANES, 2 * dst, 2 * (dst - LANES) + 1)).astype(F32).astype(BF16)
        for s in range(2 * D_FF // blk):
            sl = slice(s * blk, (s + 1) * blk)
            wgu_s[:, sl] = jnp.dot(wgu_ref[:, sl].astype(BF16), perm, preferred_element_type=F32).astype(BF16)
        wd_s[...] = wd_ref[...].astype(BF16)

    def ffn(n_rows):
        rows = slice(0, n_rows)
        x = _unpack_bf16_pairs(x_ref[rows, :])
        for c in range(D_FF // LANES):
            sl = slice(c * 2 * LANES, (c + 1) * 2 * LANES)
            hu = jnp.dot(x, wgu_s[:, sl], preferred_element_type=F32) + bgu_ref[:, sl]
            x_glu = jnp.minimum(hu[:, :LANES], SWIGLU_LIMIT)
            x_lin = jnp.clip(hu[:, LANES:], -SWIGLU_LIMIT, SWIGLU_LIMIT)
            act = x_glu * (1.0 / (1.0 + jnp.exp(-SWIGLU_ALPHA * x_glu))) * (x_lin + 1.0)
            act_s[rows, c * LANES:(c + 1) * LANES] = act.astype(BF16)
        y_ref[rows, :] = _pack_bf16_pairs(
            jnp.dot(act_s[rows, :], wd_s[...], preferred_element_type=F32) + bd_ref[...])

    used = j < nu_ref[0]
    half = half_ref[j] == 1

    @pl.when(used & jnp.logical_not(half))
    def _():
        ffn(EXPERT_TILE)

    @pl.when(used & half)
    def _():
        ffn(EXPERT_TILE // 2)
        y_ref[EXPERT_TILE // 2:, :] = jnp.zeros((EXPERT_TILE // 2, y_ref.shape[1]), y_ref.dtype)

    @pl.when(jnp.logical_not(used))
    def _():
        y_ref[...] = jnp.zeros_like(y_ref)


def _experts(tile_expert, tile_first, tile_half, n_used, xs, cap_rows, w_gate_up, bgu, w_down, bd):
    n_tiles = cap_rows // EXPERT_TILE
    of_expert = lambda j, te, first, half, nu: (te[j], 0, 0)
    grid_spec = pltpu.PrefetchScalarGridSpec(
        num_scalar_prefetch=4,
        grid=(n_tiles,),
        in_specs=[pl.BlockSpec((EXPERT_TILE, D_MODEL // 2),
                               lambda j, te, first, half, nu: (jnp.minimum(j, nu[0] - 1), 0)),
                  pl.BlockSpec((None, D_MODEL, 2 * D_FF), of_expert),
                  pl.BlockSpec((None, 1, 2 * D_FF), of_expert),
                  pl.BlockSpec((None, D_FF, D_MODEL), of_expert),
                  pl.BlockSpec((None, 1, D_MODEL), of_expert)],
        out_specs=pl.BlockSpec((EXPERT_TILE, D_MODEL // 2), lambda j, te, first, half, nu: (j, 0)),
        scratch_shapes=[pltpu.VMEM((EXPERT_TILE, D_FF), BF16),
                        pltpu.VMEM((D_MODEL, 2 * D_FF), BF16), pltpu.VMEM((D_FF, D_MODEL), BF16)],
    )
    return pl.pallas_call(
        _expert_body,
        grid_spec=grid_spec,
        out_shape=jax.ShapeDtypeStruct((cap_rows, D_MODEL // 2), U32),
        compiler_params=_params("arbitrary"),
        name="moe_experts",
    )(tile_expert, tile_first, tile_half, n_used, xs, w_gate_up, bgu, w_down, bd)


def _combine_body(unit_ref, next_unit_ref, y1_ref, p_ref, gt_ref, gate_ref, lg_ref, lb_ref, ys_ref, o_ref,
                  ybuf, sems, *, n_blocks):
    i = pl.program_id(0)
    slot = i % 2
    tm = y1_ref.shape[0]

    @pl.when(i == 0)
    def _():
        _start_unit_copies(unit_ref, ybuf.at[0], ys_ref, sems.at[0], to_hbm=False)

    @pl.when(i + 1 < n_blocks)
    def _():
        _start_unit_copies(next_unit_ref, ybuf.at[1 - slot], ys_ref, sems.at[1 - slot], to_hbm=False)

    _wait_unit_copies(ybuf.at[slot], ys_ref, sems.at[slot])

    rows = p_ref[...]
    gates = gt_ref[...]
    c_iota = lax.broadcasted_iota(I32, (tm, LOCAL_ROWS), 1)
    weight = jnp.zeros((tm, LOCAL_ROWS), F32)
    for kk in range(TOP_K):
        weight = jnp.where(c_iota == rows[:, kk:kk + 1], gates[:, kk:kk + 1], weight)
    w_hi = weight.astype(BF16)
    w_lo = (weight - w_hi.astype(F32)).astype(BF16)
    yb = _unpack_bf16_pairs(ybuf[slot])
    f = (jnp.dot(w_hi, yb, preferred_element_type=F32) + jnp.dot(w_lo, yb, preferred_element_type=F32))
    y = DEEPNORM_ALPHA * y1_ref[...] + gate_ref[...] * f
    o_ref[...] = _standardize(y) * lg_ref[...] + lb_ref[...]


def _combine(y1, local_rows, gates, unit_src, block0, gate_mod, mod_row, blocks_per_batch,
             y_sorted, ln_g, ln_b, tm):
    n_g = y1.shape[0]
    nb = n_g // tm
    row = pl.BlockSpec((1, D_MODEL), lambda i: (0, 0))
    pair = pl.BlockSpec((tm, TOP_K), lambda i: (i + block0, 0))
    units = lambda ahead: pl.BlockSpec(
        (UNIT_TABLE,), lambda i: (jnp.minimum(i + ahead, nb - 1) + block0,), memory_space=pltpu.SMEM)
    return pl.pallas_call(
        functools.partial(_combine_body, n_blocks=nb),
        grid=(nb,),
        in_specs=[units(0), units(1),
                  pl.BlockSpec((tm, D_MODEL), lambda i: (i, 0)), pair, pair,
                  pl.BlockSpec((None, 1, D_MODEL), lambda i: (mod_row(i // blocks_per_batch), 0, 0)),
                  row, row,
                  pl.BlockSpec(memory_space=pl.ANY)],
        out_specs=pl.BlockSpec((tm, D_MODEL), lambda i: (i, 0)),
        out_shape=jax.ShapeDtypeStruct((n_g, D_MODEL), F32),
        scratch_shapes=[pltpu.VMEM((2, LOCAL_ROWS, D_MODEL // 2), U32), pltpu.SemaphoreType.DMA((2,))],
        compiler_params=_params("arbitrary"),
        name="moe_combine_ln2",
    )(unit_src, unit_src, y1, local_rows, gates, gate_mod,
      ln_g.reshape(1, -1), ln_b.reshape(1, -1), y_sorted)


def _regroup_gate_up_bias(b_gate_up):
    e = b_gate_up.shape[0]
    nblk = D_FF // LANES
    return b_gate_up.reshape(e, 1, nblk, LANES, 2).transpose(0, 1, 2, 4, 3).reshape(e, 1, 2 * D_FF)


def _routing_plan(table, totals, n_pairs):
    nb = table.shape[0]
    run_n = table[:, :, 0].astype(I32)
    rows_before = table[:, :, 1].astype(I32)
    run_local = table[:, :, 2].astype(I32)
    cnt = totals[:, 0].astype(I32)
    padded = (cnt + EXPERT_TILE - 1) // EXPERT_TILE * EXPERT_TILE
    pad_end = jnp.cumsum(padded)
    pad_start = pad_end - padded
    run_global = pad_start[None, :] + rows_before
    rows_max = n_pairs + nb * N_EXPERTS * (RUN_ALIGN - 1) + N_EXPERTS * (EXPERT_TILE - 1)
    n_tiles = -(-rows_max // EXPERT_TILE)
    n_used = pad_end[-1] // EXPERT_TILE
    tile_row = jnp.minimum(jnp.arange(n_tiles, dtype=I32), n_used - 1) * EXPERT_TILE
    tile_expert = jnp.sum((pad_end[None, :] <= tile_row[:, None]).astype(I32), axis=1)
    tile_expert = jnp.minimum(tile_expert, N_EXPERTS - 1)
    tile_id = jnp.arange(n_tiles, dtype=I32)
    changed = jnp.concatenate([jnp.ones((1,), bool), tile_expert[1:] != tile_expert[:-1]])
    tile_first = (changed & (tile_id < n_used)).astype(I32)
    exp_hot = tile_expert[:, None] == jnp.arange(N_EXPERTS, dtype=I32)
    rows_end = jnp.sum(jnp.where(exp_hot, (pad_start + cnt)[None, :], 0), axis=-1)
    tile_half = ((rows_end - tile_row) <= EXPERT_TILE // 2).astype(I32)
    cap_rows = n_tiles * EXPERT_TILE
    spare_rows = -(-LOCAL_ROWS // EXPERT_TILE) * EXPERT_TILE
    total_rows = cap_rows + 2 * spare_rows
    unit_row = jnp.arange(LOCAL_UNITS, dtype=I32) * RUN_ALIGN
    run_end = run_local + run_n
    owner = jnp.sum((run_end[:, None, :] <= unit_row[None, :, None]).astype(I32), axis=-1)
    own_hot = owner[:, :, None] == jnp.arange(N_EXPERTS, dtype=I32)
    shift = jnp.sum(jnp.where(own_hot, (run_global - run_local)[:, None, :], 0), axis=-1)
    used = owner < N_EXPERTS
    sorted_row = unit_row[None, :] + shift
    slot = (jnp.arange(nb, dtype=I32) % 2)[:, None]
    unit_dst = jnp.where(used, sorted_row, cap_rows + slot * spare_rows + unit_row[None, :])
    unit_src = jnp.where(used, sorted_row, unit_row[None, :])
    table_of = lambda a: jnp.pad(a, ((0, 0), (0, UNIT_TABLE - LOCAL_UNITS))).reshape(-1)
    zero_lo = jnp.concatenate([pad_start + cnt, pad_end[-1:]])
    zero_n = jnp.concatenate([padded - cnt, total_rows - pad_end[-1:]])
    return ((table_of(unit_dst), table_of(unit_src)), (tile_expert, tile_first, tile_half, n_used.reshape(1)),
            zero_lo, zero_n, cap_rows, total_rows)


def kernel(x_prompt, x_sample, cache_na_k, cache_na_v, state_ret_fwd, state_ret_bwd, c, c_ctx, w_ada, b_ada, w_in, rpb, ret_decay_fwd, ret_decay_bwd, beta_na, beta_ret, w_out, ln1_g, ln1_b, w_router, b_router, w_gate_up, b_gate_up, w_down, b_down, ln2_g, ln2_b):
    bp, tp, _ = x_prompt.shape
    bs, ts, _ = x_sample.shape
    assert w_ada.shape[0] == DEPTH == 1
    l = 0
    ctx_row = bs
    sample_row = lambda bi: bi
    prompt_row = lambda bi: ctx_row

    cond = jnp.concatenate([c, c_ctx[None, :], jnp.zeros((16 - bs - 1, D_MODEL), F32)], axis=0)
    m = _adaln(cond, w_ada[l], b_ada[l]).reshape(16, 6, 1, D_MODEL)
    sh_a, sc_a, gt_a, sh_f, sc_f, gt_f = (m[:, i] for i in range(6))

    w_in_b = w_in[l].astype(BF16)
    w_out_b = w_out[l].astype(BF16)

    q, k_p, v_p, qr, kr, vr, g = _project(x_prompt, sh_a, sc_a, prompt_row, w_in_b, tp, F32, rope=False)
    o_na = _context_attention(q, k_p, v_p)
    zeros_state = jnp.zeros((bp, 1, RET_HEADS, RET_DK, RET_DK), F32)
    ret, s_f, s_b = _retention(qr, kr, vr, g, ret_decay_fwd[l], ret_decay_bwd[l], beta_ret[l],
                               zeros_state, zeros_state, layer=0, heads_per_step=RET_HEADS)
    w_r = w_router[l].T
    w_r_hi = w_r.astype(BF16)
    w_r_lo = (w_r - w_r_hi.astype(F32)).astype(BF16)
    route = functools.partial(_merge_and_route, w_out=w_out_b, beta_na=beta_na[l], gate=gt_a, shift_f=sh_f,
                              scale_f=sc_f, ln_g=ln1_g[l], ln_b=ln1_b[l], w_hi=w_r_hi, w_lo=w_r_lo,
                              b_router=b_router[l])
    yp1, rows_p, gates_p, table_p, totals_p = route(
        x_prompt, o_na, ret, mod_row=prompt_row, carry_in=jnp.zeros((N_EXPERTS, LANES), F32), tm=tp)

    q, k_s, v_s, qr, kr, vr, g = _project(x_sample, sh_a, sc_a, sample_row, w_in_b, SAMPLE_TILE, BF16, rope=True)
    lc = cache_na_k.shape[2]
    o_na = _neighbourhood_attention(q, k_s, v_s, cache_na_k.reshape(bs, DEPTH, lc, NA_WIDTH),
                                    cache_na_v.reshape(bs, DEPTH, lc, NA_WIDTH), l, _na_bias(rpb[l]))
    ret, _, _ = _retention(qr, kr, vr, g, ret_decay_fwd[l], ret_decay_bwd[l], beta_ret[l],
                           state_ret_fwd, state_ret_bwd, layer=l, heads_per_step=1)
    ys1, rows_s, gates_s, table_s, totals = route(
        x_sample, o_na, ret, mod_row=sample_row, carry_in=totals_p, tm=SAMPLE_TILE)

    tm = TOKEN_BLOCK
    yp1f = yp1.reshape(bp * tp, D_MODEL)
    ys1f = ys1.reshape(bs * ts, D_MODEL)
    local_rows = jnp.concatenate([rows_p, rows_s], axis=1)
    gates = jnp.concatenate([gates_p, gates_s], axis=1)
    table = jnp.concatenate([table_p, table_s], axis=0)
    (unit_dst, unit_src), tiles, zero_lo, zero_n, cap_rows, total_rows = _routing_plan(
        table, totals, (bp * tp + bs * ts) * TOP_K)
    xs = _dispatch(yp1f, ys1f, sh_f, sc_f, ctx_row, ts // tm, local_rows, unit_dst, zero_lo, zero_n,
                   total_rows, tm)
    y_sorted = _experts(*tiles, xs, cap_rows, w_gate_up[l], _regroup_gate_up_bias(b_gate_up[l]), w_down[l],
                        b_down[l][:, None, :])
    rows_t, gates_t = local_rows.T, gates.T
    yp = _combine(yp1f, rows_t, gates_t, unit_src, 0, gt_f, prompt_row, 1, y_sorted, ln2_g[l], ln2_b[l], tm)
    ys = _combine(ys1f, rows_t, gates_t, unit_src, (bp * tp) // tm, gt_f, sample_row, ts // tm, y_sorted,
                  ln2_g[l], ln2_b[l], tm)

    return (yp.reshape(bp, tp, D_MODEL), ys.reshape(bs, ts, D_MODEL),
            k_p.reshape(bp, 1, tp, NA_HEADS, NA_DIM), v_p.reshape(bp, 1, tp, NA_HEADS, NA_DIM),
            s_f[:, None], s_b[:, None])
```
